```python
import jax, jax.numpy as jnp
from jax import lax
import numpy as np

D_MODEL = 2048
BATCH = 2
SEQ = 8192
DEPTH = 4

CTX_LEN = 256
GRID_W = 64
N_MIXERS = 3
EPS = 1e-6
N_HEADS = 16
N_KV_HEADS = 4
HEAD_DIM = D_MODEL // N_HEADS
KV_GROUP = N_HEADS // N_KV_HEADS
HQ = N_HEADS * HEAD_DIM
HKV = N_KV_HEADS * HEAD_DIM
WINDOW = 128
BLOCK = 128
ROPE_BASE = 10000.0
N_FOURIER_GROUPS = 8
FOURIER_GROUP_DIM = D_MODEL // N_FOURIER_GROUPS
SGU_CHUNK = 128
SGU_HALF = 3 * D_MODEL
N_SGU_GROUPS = 8
SGU_GROUP_DIM = SGU_HALF // N_SGU_GROUPS
D_FF = -(-8 * D_MODEL // (3 * 256)) * 256
N_ATTN_LAYERS = (DEPTH + N_MIXERS - 1) // N_MIXERS
N_FOURIER_LAYERS = (DEPTH + N_MIXERS - 2) // N_MIXERS
N_SGU_LAYERS = (DEPTH + N_MIXERS - 3) // N_MIXERS

kernel_name = 'hybrid_interleaved_swa_fourier_gmlp_dit'


def rms_norm(x, g):
    xf = x.astype(jnp.float32)
    y = xf * lax.rsqrt(jnp.mean(xf * xf, axis=-1, keepdims=True) + EPS)
    return (y * g.astype(jnp.float32)).astype(x.dtype)


def ada_params(cond, w, b):
    m = jax.nn.silu(cond) @ w + b
    return m.reshape(cond.shape[0], 6, 1, D_MODEL)


def modulated_norm(t, g, shift, scale):
    return rms_norm(t, g) * (1 + scale) + shift


def swiglu(h, w_gate, w_up, w_down):
    return (jax.nn.silu(h @ w_gate) * (h @ w_up)) @ w_down


def axial_rope_tables(n_rows, dtype):
    row, col = jnp.meshgrid(jnp.arange(n_rows), jnp.arange(GRID_W), indexing='ij')
    n_freq = HEAD_DIM // 4
    inv_freq = ROPE_BASE ** (-jnp.arange(n_freq, dtype=jnp.float32) / n_freq)
    ang = jnp.concatenate([row.reshape(-1, 1).astype(jnp.float32) * inv_freq,
                           col.reshape(-1, 1).astype(jnp.float32) * inv_freq], axis=-1)
    ang = jnp.concatenate([ang, ang], axis=-1)
    return jnp.cos(ang).astype(dtype), jnp.sin(ang).astype(dtype)


def apply_rope(t, cos, sin):
    t1, t2 = jnp.split(t, 2, axis=-1)
    rot = jnp.concatenate([-t2, t1], axis=-1)
    return t * cos[None, :, None, :] + rot * sin[None, :, None, :]


def windowed_gqa_mixer(hx, hc, w_qkv, w_o, q_g, k_g, sink, cos, sin, need_ctx):
    B, n, _ = hx.shape
    L = hc.shape[1]
    nb = n // BLOCK
    scale = HEAD_DIM ** -0.5
    neg = jnp.finfo(jnp.float32).min
    qkv = hx @ w_qkv
    q = rms_norm(qkv[..., :HQ].reshape(B, n, N_HEADS, HEAD_DIM), q_g)
    k = rms_norm(qkv[..., HQ:HQ + HKV].reshape(B, n, N_KV_HEADS, HEAD_DIM), k_g)
    v = qkv[..., HQ + HKV:].reshape(B, n, N_KV_HEADS, HEAD_DIM)
    q = apply_rope(q, cos, sin)
    k = apply_rope(k, cos, sin)
    kv_c = hc @ w_qkv[:, HQ:]
    kc = rms_norm(kv_c[..., :HKV].reshape(B, L, N_KV_HEADS, HEAD_DIM), k_g)
    vc = kv_c[..., HKV:].reshape(B, L, N_KV_HEADS, HEAD_DIM)
    sink_f = sink.astype(jnp.float32).reshape(N_KV_HEADS, KV_GROUP)

    qb = q.reshape(B, nb, BLOCK, N_KV_HEADS, KV_GROUP, HEAD_DIM)

    def band(t):
        tp = jnp.pad(t, ((0, 0), (BLOCK, BLOCK), (0, 0), (0, 0)))
        tp = tp.reshape(B, nb + 2, BLOCK, N_KV_HEADS, HEAD_DIM)
        return jnp.concatenate([tp[:, :-2], tp[:, 1:-1], tp[:, 2:]], axis=2)

    kw, vw = band(k), band(v)
    s_loc = jnp.einsum('bnqhgd,bnkhd->bhgnqk', qb, kw).astype(jnp.float32) * scale
    s_ctx = jnp.einsum('bnqhgd,bchd->bhgnqc', qb, kc).astype(jnp.float32) * scale
    blk = jnp.arange(nb)[:, None, None] * BLOCK
    qpos = blk + jnp.arange(BLOCK)[None, :, None]
    kpos = blk - BLOCK + jnp.arange(3 * BLOCK)[None, None, :]
    valid = (jnp.abs(kpos - qpos) <= WINDOW) & (kpos >= 0) & (kpos < n)
    s_loc = jnp.where(valid, s_loc, neg)
    sink_col = jnp.broadcast_to(sink_f[None, :, :, None, None, None], s_ctx.shape[:-1] + (1,))
    p = jax.nn.softmax(jnp.concatenate([sink_col, s_ctx, s_loc], axis=-1), axis=-1).astype(v.dtype)
    ox = (jnp.einsum('bhgnqc,bchd->bnqhgd', p[..., 1:1 + L], vc)
          + jnp.einsum('bhgnqk,bnkhd->bnqhgd', p[..., 1 + L:], vw))
    ox = ox.reshape(B, n, HQ) @ w_o
    if not need_ctx:
        return ox, None
    qc = rms_norm((hc @ w_qkv[:, :HQ]).reshape(B, L, N_KV_HEADS, KV_GROUP, HEAD_DIM), q_g)
    s_c = jnp.einsum('blhgd,bmhd->bhglm', qc, kc).astype(jnp.float32) * scale
    sink_c = jnp.broadcast_to(sink_f[None, :, :, None, None], s_c.shape[:-1] + (1,))
    pc = jax.nn.softmax(jnp.concatenate([sink_c, s_c], axis=-1), axis=-1).astype(vc.dtype)
    oc = jnp.einsum('bhglm,bmhd->blhgd', pc[..., 1:], vc).reshape(B, L, HQ) @ w_o
    return ox, oc


def fourier_mixer(h, w_f):
    B, n, _ = h.shape
    hg = h.astype(jnp.float32).reshape(B, n, N_FOURIER_GROUPS, FOURIER_GROUP_DIM)
    y = jnp.fft.fft2(hg, axes=(1, 3), norm='ortho').real.astype(h.dtype)
    return y.reshape(B, n, D_MODEL) @ w_f


def sgu_mixer(h, w_in, g_v, w_s, b_s, w_out):
    B, n, _ = h.shape
    z = jax.nn.gelu(h @ w_in, approximate=False)
    u, v = jnp.split(z, 2, axis=-1)
    v = rms_norm(v, g_v)
    vc = v.reshape(B, n // SGU_CHUNK, SGU_CHUNK, N_SGU_GROUPS, SGU_GROUP_DIM)
    s = jnp.einsum('gpq,bcqgd->bcpgd', w_s, vc) + b_s.T[:, :, None]
    return (u * s.reshape(B, n, SGU_HALF)) @ w_out


def setup_inputs(seed: int = 0) -> dict:
    key = jax.random.key(seed)
    ks = jax.random.split(key, 21)
    D = D_MODEL

    def nrm(k, shape, s):
        return jax.random.normal(k, shape, jnp.float32) * s

    return {
        'x': nrm(ks[0], (BATCH, SEQ, D), 1.0),
        'c': nrm(ks[1], (BATCH, D), 1.0),
        'ctx': nrm(ks[2], (BATCH, CTX_LEN, D), 1.0),
        'c_ctx': nrm(ks[3], (D,), 1.0),
        'w_ada': nrm(ks[4], (DEPTH, D, 6 * D), 0.5 * D ** -0.5),
        'b_ada': nrm(ks[5], (DEPTH, 6 * D), 0.01),
        'norm_g': 1.0 + nrm(ks[6], (DEPTH, 2, D), 0.02),
        'w_ffn_gate': nrm(ks[7], (DEPTH, D, D_FF), D ** -0.5),
        'w_ffn_up': nrm(ks[8], (DEPTH, D, D_FF), D ** -0.5),
        'w_ffn_down': nrm(ks[9], (DEPTH, D_FF, D), D_FF ** -0.5),
        'w_attn_qkv': nrm(ks[10], (N_ATTN_LAYERS, D, HQ + 2 * HKV), D ** -0.5),
        'w_attn_o': nrm(ks[11], (N_ATTN_LAYERS, HQ, D), HQ ** -0.5),
        'attn_q_g': 1.0 + nrm(ks[12], (N_ATTN_LAYERS, HEAD_DIM), 0.02),
        'attn_k_g': 1.0 + nrm(ks[13], (N_ATTN_LAYERS, HEAD_DIM), 0.02),
        'attn_sink': nrm(ks[14], (N_ATTN_LAYERS, N_HEADS), 0.5),
        'w_fourier': nrm(ks[15], (N_FOURIER_LAYERS, D, D), D ** -0.5),
        'w_sgu_in': nrm(ks[16], (N_SGU_LAYERS, D, 2 * SGU_HALF), D ** -0.5),
        'sgu_v_g': 1.0 + nrm(ks[17], (N_SGU_LAYERS, SGU_HALF), 0.02),
        'w_sgu_spatial': nrm(ks[18], (N_SGU_LAYERS, N_SGU_GROUPS, SGU_CHUNK, SGU_CHUNK), SGU_CHUNK ** -0.5),
        'b_sgu_spatial': 1.0 + nrm(ks[19], (N_SGU_LAYERS, N_SGU_GROUPS, SGU_CHUNK), 0.02),
        'w_sgu_out': nrm(ks[20], (N_SGU_LAYERS, SGU_HALF, D), SGU_HALF ** -0.5),
    }


def reference(x, c, ctx, c_ctx, w_ada, b_ada, norm_g, w_ffn_gate, w_ffn_up, w_ffn_down,
              w_attn_qkv, w_attn_o, attn_q_g, attn_k_g, attn_sink, w_fourier,
              w_sgu_in, sgu_v_g, w_sgu_spatial, b_sgu_spatial, w_sgu_out):
    n = x.shape[1]
    n_rows = n // GRID_W
    cos, sin = axial_rope_tables(n_rows, x.dtype)
    for i in range(DEPTH):
        kind = i % N_MIXERS
        j = i // N_MIXERS
        need_ctx = i < DEPTH - 1
        ctx_in = need_ctx or kind == 0
        mx = ada_params(c, w_ada[i], b_ada[i])
        hx = modulated_norm(x, norm_g[i, 0], mx[:, 0], mx[:, 1])
        if ctx_in:
            mc = ada_params(c_ctx[None], w_ada[i], b_ada[i])
            hc = modulated_norm(ctx, norm_g[i, 0], mc[:, 0], mc[:, 1])
        if kind == 0:
            ox, oc = windowed_gqa_mixer(hx, hc, w_attn_qkv[j], w_attn_o[j], attn_q_g[j],
                                        attn_k_g[j], attn_sink[j], cos, sin, need_ctx)
        elif kind == 1:
            ox = fourier_mixer(hx, w_fourier[j])
            oc = fourier_mixer(hc, w_fourier[j]) if need_ctx else None
        else:
            ox = sgu_mixer(hx, w_sgu_in[j], sgu_v_g[j], w_sgu_spatial[j], b_sgu_spatial[j], w_sgu_out[j])
            oc = (sgu_mixer(hc, w_sgu_in[j], sgu_v_g[j], w_sgu_spatial[j], b_sgu_spatial[j], w_sgu_out[j])
                  if need_ctx else None)
        x = x + mx[:, 2] * ox
        hx2 = modulated_norm(x, norm_g[i, 1], mx[:, 3], mx[:, 4])
        x = x + mx[:, 5] * swiglu(hx2, w_ffn_gate[i], w_ffn_up[i], w_ffn_down[i])
        if need_ctx:
            ctx = ctx + mc[:, 2] * oc
            hc2 = modulated_norm(ctx, norm_g[i, 1], mc[:, 3], mc[:, 4])
            ctx = ctx + mc[:, 5] * swiglu(hc2, w_ffn_gate[i], w_ffn_up[i], w_ffn_down[i])
    return x
```

```python
import functools

import numpy as np
import jax
import jax.numpy as jnp
from jax import lax
from jax.experimental import pallas as pl
from jax.experimental.pallas import tpu as pltpu

D_MODEL = 2048
DEPTH = 4
GRID_W = 64
N_MIXERS = 3
EPS = 1e-6
N_HEADS = 16
N_KV_HEADS = 4
HEAD_DIM = D_MODEL // N_HEADS
KV_GROUP = N_HEADS // N_KV_HEADS
HQ = N_HEADS * HEAD_DIM
HKV = N_KV_HEADS * HEAD_DIM
WINDOW = 128
BLOCK = 128
ROPE_BASE = 10000.0
N_FOURIER_GROUPS = 8
FOURIER_GROUP_DIM = D_MODEL // N_FOURIER_GROUPS
SGU_CHUNK = 128
SGU_HALF = 3 * D_MODEL
N_SGU_GROUPS = 8
SGU_GROUP_DIM = SGU_HALF // N_SGU_GROUPS
D_FF = 5632

BF = jnp.bfloat16
F32 = jnp.float32

VMEM_LIMIT_BYTES = 56 * 1024 * 1024
SUBLANES = 8
NEG = -1e30

DFT_N2 = 64


def _params(*sem):
    return pltpu.CompilerParams(dimension_semantics=sem, vmem_limit_bytes=VMEM_LIMIT_BYTES)


def _resident(shape, index_map):
    return pl.BlockSpec(shape, index_map, pipeline_mode=pl.Buffered(1))


def _modnorm(x, g, shift, scale):
    ms = jnp.mean(x * x, axis=-1, keepdims=True)
    y = x * lax.rsqrt(ms + EPS)
    return (y * g) * (1.0 + scale) + shift


def _silu(t):
    return t * jax.nn.sigmoid(t)


def _ada_kernel(cond_ref, w_ref, b_ref, o_ref):
    s = _silu(cond_ref[...]).astype(BF)
    o_ref[0] = jnp.dot(s, w_ref[0].astype(BF), preferred_element_type=F32) + b_ref[0]


def _ada_all(cond, w_ada, b_ada):
    tn = 1024
    return pl.pallas_call(
        _ada_kernel,
        grid=(DEPTH, 6 * D_MODEL // tn),
        in_specs=[pl.BlockSpec((SUBLANES, D_MODEL), lambda l, j: (0, 0)),
                  pl.BlockSpec((1, D_MODEL, tn), lambda l, j: (l, 0, j)),
                  pl.BlockSpec((1, 1, tn), lambda l, j: (l, 0, j))],
        out_specs=pl.BlockSpec((1, SUBLANES, tn), lambda l, j: (l, 0, j)),
        out_shape=jax.ShapeDtypeStruct((DEPTH, SUBLANES, 6 * D_MODEL), F32),
        compiler_params=_params("arbitrary", "arbitrary"),
        name="ada",
    )(cond, w_ada, b_ada.reshape(DEPTH, 1, 6 * D_MODEL))


def _modnorm_kernel(x_ref, mod_ref, g_ref, o_ref):
    o_ref[0] = _modnorm(x_ref[0], g_ref[...], mod_ref[0, 0:1, :], mod_ref[0, 1:2, :])


def _modnorm_call(x, mod, g, tm):
    B, n, _ = x.shape
    return pl.pallas_call(
        _modnorm_kernel,
        grid=(B, n // tm),
        in_specs=[pl.BlockSpec((1, tm, D_MODEL), lambda b, i: (b, i, 0)),
                  pl.BlockSpec((1, 6, D_MODEL), lambda b, i: (b, 0, 0)),
                  pl.BlockSpec((1, D_MODEL), lambda b, i: (0, 0))],
        out_specs=pl.BlockSpec((1, tm, D_MODEL), lambda b, i: (b, i, 0)),
        out_shape=jax.ShapeDtypeStruct((B, n, D_MODEL), F32),
        compiler_params=_params("parallel", "parallel"),
        name="modnorm",
    )(x, mod, g)


def _qkv_kernel(x_ref, mod_ref, g_ref, w_ref, qg_ref, kg_ref, cos_ref, sin_ref, q_ref, k_ref, v_ref):
    h = _modnorm(x_ref[0], g_ref[...], mod_ref[0, 0:1, :], mod_ref[0, 1:2, :]).astype(BF)
    cos = cos_ref[...]
    sin = sin_ref[...]

    def head(t, gain):
        ms = jnp.mean(t * t, axis=-1, keepdims=True)
        t = (t * lax.rsqrt(ms + EPS)) * gain
        return t * cos + pltpu.roll(t, HEAD_DIM // 2, 1) * sin

    cw = 4 * HEAD_DIM
    for c in range(HQ // cw):
        t = jnp.dot(h, w_ref[:, c * cw:(c + 1) * cw], preferred_element_type=F32)
        for j in range(4):
            tj = head(t[:, j * HEAD_DIM:(j + 1) * HEAD_DIM], qg_ref[...]) * (HEAD_DIM ** -0.5)
            q_ref[0, :, c * cw + j * HEAD_DIM:c * cw + (j + 1) * HEAD_DIM] = tj.astype(BF)
    t = jnp.dot(h, w_ref[:, HQ:HQ + HKV], preferred_element_type=F32)
    for j in range(N_KV_HEADS):
        tj = head(t[:, j * HEAD_DIM:(j + 1) * HEAD_DIM], kg_ref[...])
        k_ref[0, :, j * HEAD_DIM:(j + 1) * HEAD_DIM] = tj.astype(BF)
    t = jnp.dot(h, w_ref[:, HQ + HKV:], preferred_element_type=F32)
    v_ref[0] = t.astype(BF)


def _qkv_call(x, mod, g, w_bf, qg, kg, cos, sin, tm):
    B, n, _ = x.shape
    row = lambda b, i: (b, i, 0)
    return pl.pallas_call(
        _qkv_kernel,
        grid=(B, n // tm),
        in_specs=[pl.BlockSpec((1, tm, D_MODEL), row),
                  pl.BlockSpec((1, 6, D_MODEL), lambda b, i: (b, 0, 0)),
                  pl.BlockSpec((1, D_MODEL), lambda b, i: (0, 0)),
                  _resident((D_MODEL, HQ + 2 * HKV), lambda b, i: (0, 0)),
                  pl.BlockSpec((1, HEAD_DIM), lambda b, i: (0, 0)),
                  pl.BlockSpec((1, HEAD_DIM), lambda b, i: (0, 0)),
                  pl.BlockSpec((tm, HEAD_DIM), lambda b, i: (i, 0)),
                  pl.BlockSpec((tm, HEAD_DIM), lambda b, i: (i, 0))],
        out_specs=[pl.BlockSpec((1, tm, HQ), row),
                   pl.BlockSpec((1, tm, HKV), row),
                   pl.BlockSpec((1, tm, HKV), row)],
        out_shape=[jax.ShapeDtypeStruct((B, n, HQ), BF),
                   jax.ShapeDtypeStruct((B, n, HKV), BF),
                   jax.ShapeDtypeStruct((B, n, HKV), BF)],
        compiler_params=_params("parallel", "parallel"),
        name="qkv",
    )(x, mod, g, w_bf, qg, kg, cos, sin)


def _softmax_pv(s, sink_col, vcat):
    m = jnp.maximum(jnp.max(s, axis=-1, keepdims=True), sink_col)
    p = jnp.exp(s - m)
    den = jnp.sum(p, axis=-1, keepdims=True) + jnp.exp(sink_col - m)
    return jnp.dot(p.astype(BF), vcat, preferred_element_type=F32) / den


def _sink_column(sink_ref, h, rows):
    ridx = lax.broadcasted_iota(jnp.int32, (KV_GROUP * rows, 1), 0)
    col = jnp.full((KV_GROUP * rows, 1), sink_ref[h * KV_GROUP], F32)
    for g in range(1, KV_GROUP):
        col = jnp.where(ridx >= g * rows, sink_ref[h * KV_GROUP + g], col)
    return col


def _stack_heads(q):
    return jnp.concatenate([q[:, g * HEAD_DIM:(g + 1) * HEAD_DIM] for g in range(KV_GROUP)], axis=0)


def _attn_kernel(sink_ref, q_ref, kp_ref, kc_ref, kn_ref, vp_ref, vc_ref, vn_ref, kx_ref, vx_ref,
                 o_ref, *, n, n_ctx):
    h = pl.program_id(1)
    qb = pl.program_id(2)
    qs = _stack_heads(q_ref[0])
    kcat = jnp.concatenate([kx_ref[0], kp_ref[0], kc_ref[0], kn_ref[0]], axis=0)
    vcat = jnp.concatenate([vx_ref[0], vp_ref[0], vc_ref[0], vn_ref[0]], axis=0)
    nk = n_ctx + 3 * BLOCK
    s = lax.dot_general(qs, kcat, (((1,), (1,)), ((), ())), preferred_element_type=F32)
    row = lax.broadcasted_iota(jnp.int32, (KV_GROUP * BLOCK, nk), 0) & (BLOCK - 1)
    col = lax.broadcasted_iota(jnp.int32, (KV_GROUP * BLOCK, nk), 1)
    rel = col - (n_ctx + BLOCK) - row
    kpos = (qb - 1) * BLOCK + col - n_ctx
    valid = (col < n_ctx) | ((jnp.abs(rel) <= WINDOW) & (kpos >= 0) & (kpos < n))
    s = jnp.where(valid, s, NEG)
    o = _softmax_pv(s, _sink_column(sink_ref, h, BLOCK), vcat)
    for g in range(KV_GROUP):
        o_ref[0, :, g * HEAD_DIM:(g + 1) * HEAD_DIM] = o[g * BLOCK:(g + 1) * BLOCK].astype(BF)


def _attn_call(q, k, v, kx, vx, sink):
    B, n, _ = q.shape
    n_ctx = kx.shape[1]
    nb = n // BLOCK
    gw = KV_GROUP * HEAD_DIM
    prev = lambda b, h, i: (b, jnp.maximum(i - 1, 0), h)
    cur = lambda b, h, i: (b, i, h)
    nxt = lambda b, h, i: (b, jnp.minimum(i + 1, nb - 1), h)
    ctx = lambda b, h, i: (b, 0, h)
    kv = lambda m: pl.BlockSpec((1, BLOCK, HEAD_DIM), m)
    return pl.pallas_call(
        functools.partial(_attn_kernel, n=n, n_ctx=n_ctx),
        grid=(B, N_KV_HEADS, nb),
        in_specs=[pl.BlockSpec(memory_space=pltpu.SMEM),
                  pl.BlockSpec((1, BLOCK, gw), cur),
                  kv(prev), kv(cur), kv(nxt), kv(prev), kv(cur), kv(nxt),
                  pl.BlockSpec((1, n_ctx, HEAD_DIM), ctx),
                  pl.BlockSpec((1, n_ctx, HEAD_DIM), ctx)],
        out_specs=pl.BlockSpec((1, BLOCK, gw), cur),
        out_shape=jax.ShapeDtypeStruct((B, n, HQ), BF),
        compiler_params=_params("parallel", "parallel", "arbitrary"),
        name="attn",
    )(sink, q, k, k, k, v, v, v, kx, vx)


def _attn_ctx_kernel(sink_ref, q_ref, k_ref, v_ref, o_ref, *, n_ctx):
    h = pl.program_id(1)
    qs = _stack_heads(q_ref[0])
    s = lax.dot_general(qs, k_ref[0], (((1,), (1,)), ((), ())), preferred_element_type=F32)
    o = _softmax_pv(s, _sink_column(sink_ref, h, n_ctx), v_ref[0])
    for g in range(KV_GROUP):
        o_ref[0, :, g * HEAD_DIM:(g + 1) * HEAD_DIM] = o[g * n_ctx:(g + 1) * n_ctx].astype(BF)


def _attn_ctx_call(q, k, v, sink):
    B, n_ctx, _ = q.shape
    gw = KV_GROUP * HEAD_DIM
    idx = lambda b, h: (b, 0, h)
    return pl.pallas_call(
        functools.partial(_attn_ctx_kernel, n_ctx=n_ctx),
        grid=(B, N_KV_HEADS),
        in_specs=[pl.BlockSpec(memory_space=pltpu.SMEM),
                  pl.BlockSpec((1, n_ctx, gw), idx),
                  pl.BlockSpec((1, n_ctx, HEAD_DIM), idx),
                  pl.BlockSpec((1, n_ctx, HEAD_DIM), idx)],
        out_specs=pl.BlockSpec((1, n_ctx, gw), idx),
        out_shape=jax.ShapeDtypeStruct((B, n_ctx, HQ), BF),
        compiler_params=_params("parallel", "parallel"),
        name="attn_ctx",
    )(sink, q, k, v)


def _mm_res_kernel(a_ref, w_ref, x_ref, mod_ref, o_ref):
    p = jnp.dot(a_ref[0].astype(BF), w_ref[...], preferred_element_type=F32)
    o_ref[0] = x_ref[0] + mod_ref[0, 2:3, :] * p


def _mm_res_call(a, w_bf, x, mod, tm):
    B, n, K = a.shape
    row = lambda b, i: (b, i, 0)
    return pl.pallas_call(
        _mm_res_kernel,
        grid=(B, n // tm),
        in_specs=[pl.BlockSpec((1, tm, K), row),
                  _resident((K, D_MODEL), lambda b, i: (0, 0)),
                  pl.BlockSpec((1, tm, D_MODEL), row),
                  pl.BlockSpec((1, 6, D_MODEL), lambda b, i: (b, 0, 0))],
        out_specs=pl.BlockSpec((1, tm, D_MODEL), row),
        out_shape=jax.ShapeDtypeStruct((B, n, D_MODEL), F32),
        compiler_params=_params("parallel", "parallel"),
        name="mm_res",
    )(a, w_bf, x, mod)


def _ffn_kernel(x_ref, mod_ref, g_ref, wg_ref, wu_ref, wd_ref, o_ref, h_ref):
    f = pl.program_id(2)
    last = pl.num_programs(2) - 1

    @pl.when(f == 0)
    def _():
        h_ref[...] = _modnorm(x_ref[0], g_ref[...], mod_ref[0, 3:4, :], mod_ref[0, 4:5, :]).astype(BF)

    h = h_ref[...]
    gate = jnp.dot(h, wg_ref[...], preferred_element_type=F32)
    up = jnp.dot(h, wu_ref[...], preferred_element_type=F32)
    a = (_silu(gate) * up).astype(BF)
    p = jnp.dot(a, wd_ref[...], preferred_element_type=F32)

    @pl.when(f == 0)
    def _():
        o_ref[0] = p

    @pl.when(f > 0)
    def _():
        o_ref[0] += p

    @pl.when(f == last)
    def _():
        o_ref[0] = x_ref[0] + mod_ref[0, 5:6, :] * o_ref[0]


def _ffn_call(x, mod, g, wg_bf, wu_bf, wd_bf, tm, tf):
    B, n, _ = x.shape
    row = lambda b, i, f: (b, i, 0)
    return pl.pallas_call(
        _ffn_kernel,
        grid=(B, n // tm, D_FF // tf),
        in_specs=[pl.BlockSpec((1, tm, D_MODEL), row),
                  pl.BlockSpec((1, 6, D_MODEL), lambda b, i, f: (b, 0, 0)),
                  pl.BlockSpec((1, D_MODEL), lambda b, i, f: (0, 0)),
                  pl.BlockSpec((D_MODEL, tf), lambda b, i, f: (0, f)),
                  pl.BlockSpec((D_MODEL, tf), lambda b, i, f: (0, f)),
                  pl.BlockSpec((tf, D_MODEL), lambda b, i, f: (f, 0))],
        out_specs=pl.BlockSpec((1, tm, D_MODEL), row),
        out_shape=jax.ShapeDtypeStruct((B, n, D_MODEL), F32),
        scratch_shapes=[pltpu.VMEM((tm, D_MODEL), BF)],
        compiler_params=_params("parallel", "parallel", "arbitrary"),
        name="ffn",
    )(x, mod, g, wg_bf, wu_bf, wd_bf)


def _dft_tables(n):
    n1, n2, s = n // DFT_N2, DFT_N2, SUBLANES
    k1 = jnp.arange(n1, dtype=jnp.int32)
    nn1 = jnp.arange(n1, dtype=jnp.int32)
    nn2 = jnp.arange(n2, dtype=jnp.int32)
    idx = (k1[:, None, None] * nn2[None, None, :] + n2 * k1[:, None, None] * nn1[None, :, None]) % n
    ang = idx.astype(F32) * (2.0 * np.pi / n)
    m = jnp.stack([jnp.cos(ang), -jnp.sin(ang)], axis=0)
    m = m.reshape(2, n1, n1, n2 // s, s)
    eye = jnp.eye(s, dtype=F32)
    kron1 = jnp.einsum("pknjl,ml->jpkmnl", m, eye).reshape(n2 // s, 2 * n1 * s, n1 * s)
    k2 = jnp.arange(n2, dtype=jnp.int32)
    ang2 = ((k2[:, None] * nn2[None, :]) % n2).astype(F32) * (2.0 * np.pi / n2)
    c2, s2 = jnp.cos(ang2), jnp.sin(ang2)
    w2 = jnp.stack([jnp.stack([c2, s2], axis=1), jnp.stack([-s2, c2], axis=1)], axis=0)
    kron2 = jnp.einsum("qkpn,ml->qkmpln", w2, eye).reshape(2 * n2 * s, 2 * s * n2)
    return kron1.astype(BF), kron2.astype(BF)


def _channel_dft_table(n):
    c = np.arange(FOURIER_GROUP_DIM)
    ang = 2.0 * np.pi * ((c[:, None] * c[None, :]) % FOURIER_GROUP_DIM) / FOURIER_GROUP_DIM
    scale = 1.0 / np.sqrt(float(n) * FOURIER_GROUP_DIM)
    return jnp.asarray(np.concatenate([np.cos(ang), np.sin(ang)], axis=0) * scale, BF)


def _channel_mix(zr, zi, cs):
    outs = []
    for g in range(zr.shape[1] // FOURIER_GROUP_DIM):
        sl = slice(g * FOURIER_GROUP_DIM, (g + 1) * FOURIER_GROUP_DIM)
        zg = jnp.concatenate([zr[:, sl], zi[:, sl]], axis=1).astype(BF)
        outs.append(jnp.dot(zg, cs, preferred_element_type=F32))
    return jnp.concatenate(outs, axis=1)


def _dft1_kernel(x_ref, k_ref, t_ref):
    n1, s, tc = x_ref.shape[1], x_ref.shape[2], x_ref.shape[3]
    xb = x_ref[0].reshape(n1 * s, tc).astype(BF)
    t = jnp.dot(k_ref[0], xb, preferred_element_type=F32)
    t_ref[0] = t.reshape(2, n1, s, tc)


def _dft2_kernel(t_ref, k_ref, cs_ref, y_ref):
    s, tc = y_ref.shape[2], y_ref.shape[3]
    rows = s * DFT_N2
    tb = t_ref[0, :, 0].reshape(2 * rows, tc).astype(BF)
    z = jnp.dot(k_ref[...], tb, preferred_element_type=F32)
    y = _channel_mix(z[:rows], z[rows:], cs_ref[...])
    y_ref[0] = y.reshape(DFT_N2, s, tc)


def _fourier_positions(hx, kron1, kron2, cs, tc):
    B, n, _ = hx.shape
    n1, s = n // DFT_N2, SUBLANES
    nj = DFT_N2 // s
    t = pl.pallas_call(
        _dft1_kernel,
        grid=(nj, B, D_MODEL // tc),
        in_specs=[pl.BlockSpec((1, n1, s, tc), lambda j, b, c: (b, 0, j, c)),
                  pl.BlockSpec((1, 2 * n1 * s, n1 * s), lambda j, b, c: (j, 0, 0))],
        out_specs=pl.BlockSpec((1, 2, n1, s, tc), lambda j, b, c: (b, 0, 0, j, c)),
        out_shape=jax.ShapeDtypeStruct((B, 2, n1, DFT_N2, D_MODEL), F32),
        compiler_params=_params("arbitrary", "arbitrary", "arbitrary"),
        name="dft1",
    )(hx.reshape(B, n1, DFT_N2, D_MODEL), kron1)
    na = n1 // s
    y = pl.pallas_call(
        _dft2_kernel,
        grid=(B, na, D_MODEL // tc),
        in_specs=[pl.BlockSpec((1, 2, 1, s * DFT_N2, tc), lambda b, a, c: (b, 0, a, 0, c)),
                  _resident((2 * DFT_N2 * s, 2 * s * DFT_N2), lambda b, a, c: (0, 0)),
                  _resident((2 * FOURIER_GROUP_DIM, FOURIER_GROUP_DIM), lambda b, a, c: (0, 0))],
        out_specs=pl.BlockSpec((1, DFT_N2, s, tc), lambda b, a, c: (b, 0, a, c)),
        out_shape=jax.ShapeDtypeStruct((B, DFT_N2, n1, D_MODEL), F32),
        compiler_params=_params("parallel", "parallel", "parallel"),
        name="dft2",
    )(t.reshape(B, 2, na, s * DFT_N2, D_MODEL), kron2, cs)
    return y.reshape(B, n, D_MODEL)


def _dft_ctx_kernel(x_ref, f_ref, cs_ref, y_ref):
    n_ctx = x_ref.shape[1]
    z = jnp.dot(f_ref[...], x_ref[0].astype(BF), preferred_element_type=F32)
    y_ref[0] = _channel_mix(z[:n_ctx], z[n_ctx:], cs_ref[...])


def _fourier_ctx(hc, cs):
    B, n_ctx, _ = hc.shape
    p = np.arange(n_ctx)
    ang = 2.0 * np.pi * ((p[:, None] * p[None, :]) % n_ctx) / n_ctx
    fmat = jnp.asarray(np.concatenate([np.cos(ang), -np.sin(ang)], axis=0), BF)
    return pl.pallas_call(
        _dft_ctx_kernel,
        grid=(B,),
        in_specs=[pl.BlockSpec((1, n_ctx, D_MODEL), lambda b: (b, 0, 0)),
                  pl.BlockSpec((2 * n_ctx, n_ctx), lambda b: (0, 0)),
                  pl.BlockSpec((2 * FOURIER_GROUP_DIM, FOURIER_GROUP_DIM), lambda b: (0, 0))],
        out_specs=pl.BlockSpec((1, n_ctx, D_MODEL), lambda b: (b, 0, 0)),
        out_shape=jax.ShapeDtypeStruct((B, n_ctx, D_MODEL), F32),
        compiler_params=_params("parallel"),
        name="dft_ctx",
    )(hc, fmat, cs)


def _sgu_kernel(x_ref, mod_ref, g_ref, win_ref, gv_ref, ws_ref, bs_ref, wout_ref, o_ref,
                h_ref, z_ref, ssq_ref):
    s = pl.program_id(2)
    ng = N_SGU_GROUPS
    tm = h_ref.shape[0]

    @pl.when(s == 0)
    def _():
        h_ref[...] = _modnorm(x_ref[0], g_ref[...], mod_ref[0, 0:1, :], mod_ref[0, 1:2, :]).astype(BF)
        ssq_ref[...] = jnp.zeros_like(ssq_ref)

    @pl.when(s < 2 * ng)
    def _():
        z = jnp.dot(h_ref[...], win_ref[...], preferred_element_type=F32)
        z = 0.5 * z * (1.0 + lax.erf(z * (2.0 ** -0.5)))
        z_ref[s] = z.astype(BF)

        @pl.when(s >= ng)
        def _():
            ssq_ref[...] += jnp.sum(z * z, axis=-1, keepdims=True)

    @pl.when(s >= 2 * ng)
    def _():
        grp = s - 2 * ng
        r = lax.rsqrt(ssq_ref[...] * (1.0 / SGU_HALF) + EPS)
        vn = ((z_ref[grp + ng].astype(F32) * r) * gv_ref[0]).astype(BF)
        u = z_ref[grp].astype(F32)
        parts = []
        for c in range(tm // SGU_CHUNK):
            rows = slice(c * SGU_CHUNK, (c + 1) * SGU_CHUNK)
            sp = jnp.dot(ws_ref[0], vn[rows], preferred_element_type=F32) + bs_ref[0]
            parts.append((u[rows] * sp).astype(BF))
        a = jnp.concatenate(parts, axis=0)
        p = jnp.dot(a, wout_ref[...], preferred_element_type=F32)

        @pl.when(grp == 0)
        def _():
            o_ref[0] = p

        @pl.when(grp > 0)
        def _():
            o_ref[0] += p

        @pl.when(grp == ng - 1)
        def _():
            o_ref[0] = x_ref[0] + mod_ref[0, 2:3, :] * o_ref[0]


def _sgu_call(x, mod, g, win_bf, gv, ws_bf, bs, wout_bf, tm):
    B, n, _ = x.shape
    ng, gd = N_SGU_GROUPS, SGU_GROUP_DIM
    row = lambda b, i, s: (b, i, 0)
    grp = lambda b, i, s: (jnp.maximum(s - 2 * ng, 0), 0, 0)
    return pl.pallas_call(
        _sgu_kernel,
        grid=(B, n // tm, 3 * ng),
        in_specs=[pl.BlockSpec((1, tm, D_MODEL), row),
                  pl.BlockSpec((1, 6, D_MODEL), lambda b, i, s: (b, 0, 0)),
                  pl.BlockSpec((1, D_MODEL), lambda b, i, s: (0, 0)),
                  pl.BlockSpec((D_MODEL, gd), lambda b, i, s: (0, jnp.minimum(s, 2 * ng - 1))),
                  pl.BlockSpec((1, 1, gd), grp),
                  pl.BlockSpec((1, SGU_CHUNK, SGU_CHUNK), grp),
                  pl.BlockSpec((1, SGU_CHUNK, 1), grp),
                  pl.BlockSpec((gd, D_MODEL), lambda b, i, s: (jnp.maximum(s - 2 * ng, 0), 0))],
        out_specs=pl.BlockSpec((1, tm, D_MODEL), row),
        out_shape=jax.ShapeDtypeStruct((B, n, D_MODEL), F32),
        scratch_shapes=[pltpu.VMEM((tm, D_MODEL), BF),
                        pltpu.VMEM((2 * ng, tm, gd), BF),
                        pltpu.VMEM((tm, 1), F32)],
        compiler_params=_params("parallel", "parallel", "arbitrary"),
        name="sgu",
    )(x, mod, g, win_bf, gv.reshape(ng, 1, gd), ws_bf, bs.reshape(ng, SGU_CHUNK, 1), wout_bf)


def _rope_tables(n):
    row, col = jnp.meshgrid(jnp.arange(n // GRID_W), jnp.arange(GRID_W), indexing="ij")
    n_freq = HEAD_DIM // 4
    inv_freq = ROPE_BASE ** (-jnp.arange(n_freq, dtype=F32) / n_freq)
    ang = jnp.concatenate([row.reshape(-1, 1).astype(F32) * inv_freq,
                           col.reshape(-1, 1).astype(F32) * inv_freq], axis=-1)
    ang = jnp.concatenate([ang, ang], axis=-1)
    sign = jnp.where(jnp.arange(HEAD_DIM) < HEAD_DIM // 2, -1.0, 1.0).astype(F32)
    return jnp.cos(ang), jnp.sin(ang) * sign


def kernel(x, c, ctx, c_ctx, w_ada, b_ada, norm_g, w_ffn_gate, w_ffn_up, w_ffn_down, w_attn_qkv,
           w_attn_o, attn_q_g, attn_k_g, attn_sink, w_fourier, w_sgu_in, sgu_v_g, w_sgu_spatial,
           b_sgu_spatial, w_sgu_out):
    B, n, _ = x.shape
    n_ctx = ctx.shape[1]
    tm_x = 512
    tm_c = n_ctx

    cond = jnp.zeros((SUBLANES, D_MODEL), F32).at[:B].set(c).at[B].set(c_ctx)
    ada = _ada_all(cond, w_ada, b_ada)
    cos, sin = _rope_tables(n)
    cos_c = jnp.ones((n_ctx, HEAD_DIM), F32)
    sin_c = jnp.zeros((n_ctx, HEAD_DIM), F32)

    for i in range(DEPTH):
        kind = i % N_MIXERS
        j = i // N_MIXERS
        need_ctx = i < DEPTH - 1
        modx = ada[i, :B].reshape(B, 6, D_MODEL)
        modc = jnp.broadcast_to(ada[i, B].reshape(1, 6, D_MODEL), (B, 6, D_MODEL))
        g1 = norm_g[i, 0].reshape(1, D_MODEL)
        g2 = norm_g[i, 1].reshape(1, D_MODEL)
        if kind == 0:
            wqkv = w_attn_qkv[j].astype(BF)
            wo = w_attn_o[j].astype(BF)
            qg = attn_q_g[j].reshape(1, HEAD_DIM)
            kg = attn_k_g[j].reshape(1, HEAD_DIM)
            sink = attn_sink[j]
            q, k, v = _qkv_call(x, modx, g1, wqkv, qg, kg, cos, sin, tm_x)
            qc, kc, vc = _qkv_call(ctx, modc, g1, wqkv, qg, kg, cos_c, sin_c, tm_c)
            ox = _attn_call(q, k, v, kc, vc, sink)
            x = _mm_res_call(ox, wo, x, modx, tm_x)
            if need_ctx:
                oc = _attn_ctx_call(qc, kc, vc, sink)
                ctx = _mm_res_call(oc, wo, ctx, modc, tm_c)
        elif kind == 1:
            wf = w_fourier[j].astype(BF)
            kron1, kron2 = _dft_tables(n)
            hx = _modnorm_call(x, modx, g1, tm_x)
            yx = _fourier_positions(hx, kron1, kron2, _channel_dft_table(n), 512)
            x = _mm_res_call(yx, wf, x, modx, tm_x)
            if need_ctx:
                hc = _modnorm_call(ctx, modc, g1, tm_c)
                yc = _fourier_ctx(hc, _channel_dft_table(n_ctx))
                ctx = _mm_res_call(yc, wf, ctx, modc, tm_c)
        else:
            win = w_sgu_in[j].astype(BF)
            wout = w_sgu_out[j].astype(BF)
            ws = w_sgu_spatial[j].astype(BF)
            args = (win, sgu_v_g[j], ws, b_sgu_spatial[j], wout)
            x_new = _sgu_call(x, modx, g1, *args, tm_x)
            if need_ctx:
                ctx = _sgu_call(ctx, modc, g1, *args, tm_c)
            x = x_new
        wg = w_ffn_gate[i].astype(BF)
        wu = w_ffn_up[i].astype(BF)
        wd = w_ffn_down[i].astype(BF)
        x = _ffn_call(x, modx, g2, wg, wu, wd, tm_x, 512)
        if need_ctx:
            ctx = _ffn_call(ctx, modc, g2, wg, wu, wd, tm_c, 512)
    return x
```

```python
import functools

import numpy as np
import jax
import jax.numpy as jnp
from jax import lax
from jax.experimental import pallas as pl
from jax.experimental.pallas import tpu as pltpu

D_MODEL = 2048
DEPTH = 4
GRID_W = 64
N_MIXERS = 3
EPS = 1e-6
N_HEADS = 16
N_KV_HEADS = 4
HEAD_DIM = D_MODEL // N_HEADS
KV_GROUP = N_HEADS // N_KV_HEADS
HQ = N_HEADS * HEAD_DIM
HKV = N_KV_HEADS * HEAD_DIM
WINDOW = 128
BLOCK = 128
ROPE_BASE = 10000.0
N_FOURIER_GROUPS = 8
FOURIER_GROUP_DIM = D_MODEL // N_FOURIER_GROUPS
SGU_CHUNK = 128
SGU_HALF = 3 * D_MODEL
N_SGU_GROUPS = 8
SGU_GROUP_DIM = SGU_HALF // N_SGU_GROUPS
D_FF = 5632

BF = jnp.bfloat16
F32 = jnp.float32

VMEM_LIMIT_BYTES = 56 * 1024 * 1024
FFN_VMEM_LIMIT_BYTES = 60 * 1024 * 1024
SUBLANES = 8
NEG = -1e30

DFT_N2 = 64


def _params(*sem):
    return pltpu.CompilerParams(dimension_semantics=sem, vmem_limit_bytes=VMEM_LIMIT_BYTES)


def _resident(shape, index_map):
    return pl.BlockSpec(shape, index_map, pipeline_mode=pl.Buffered(1))


def _modnorm(x, g, shift, scale):
    ms = jnp.mean(x * x, axis=-1, keepdims=True)
    y = x * lax.rsqrt(ms + EPS)
    return (y * g) * (1.0 + scale) + shift


def _silu(t):
    return t * jax.nn.sigmoid(t)


def _ada_kernel(cond_ref, w_ref, b_ref, o_ref):
    s = _silu(cond_ref[...]).astype(BF)
    o_ref[0] = jnp.dot(s, w_ref[0].astype(BF), preferred_element_type=F32) + b_ref[0]


def _ada_all(cond, w_ada, b_ada):
    tn = 1024
    return pl.pallas_call(
        _ada_kernel,
        grid=(DEPTH, 6 * D_MODEL // tn),
        in_specs=[pl.BlockSpec((SUBLANES, D_MODEL), lambda l, j: (0, 0)),
                  pl.BlockSpec((1, D_MODEL, tn), lambda l, j: (l, 0, j)),
                  pl.BlockSpec((1, 1, tn), lambda l, j: (l, 0, j))],
        out_specs=pl.BlockSpec((1, SUBLANES, tn), lambda l, j: (l, 0, j)),
        out_shape=jax.ShapeDtypeStruct((DEPTH, SUBLANES, 6 * D_MODEL), F32),
        compiler_params=_params("arbitrary", "arbitrary"),
        name="ada",
    )(cond, w_ada, b_ada.reshape(DEPTH, 1, 6 * D_MODEL))


def _modnorm_kernel(x_ref, mod_ref, g_ref, o_ref):
    o_ref[0] = _modnorm(x_ref[0], g_ref[...], mod_ref[0, 0:1, :], mod_ref[0, 1:2, :])


def _modnorm_call(x, mod, g, tm):
    B, n, _ = x.shape
    return pl.pallas_call(
        _modnorm_kernel,
        grid=(B, n // tm),
        in_specs=[pl.BlockSpec((1, tm, D_MODEL), lambda b, i: (b, i, 0)),
                  pl.BlockSpec((1, 6, D_MODEL), lambda b, i: (b, 0, 0)),
                  pl.BlockSpec((1, D_MODEL), lambda b, i: (0, 0))],
        out_specs=pl.BlockSpec((1, tm, D_MODEL), lambda b, i: (b, i, 0)),
        out_shape=jax.ShapeDtypeStruct((B, n, D_MODEL), F32),
        compiler_params=_params("parallel", "parallel"),
        name="modnorm",
    )(x, mod, g)


def _qkv_kernel(x_ref, mod_ref, g_ref, w_ref, qg_ref, kg_ref, cos_ref, sin_ref, q_ref, k_ref, v_ref):
    h = _modnorm(x_ref[0], g_ref[...], mod_ref[0, 0:1, :], mod_ref[0, 1:2, :]).astype(BF)
    cos = cos_ref[...]
    sin = sin_ref[...]

    def head(t, gain):
        ms = jnp.mean(t * t, axis=-1, keepdims=True)
        t = (t * lax.rsqrt(ms + EPS)) * gain
        return t * cos + pltpu.roll(t, HEAD_DIM // 2, 1) * sin

    cw = 4 * HEAD_DIM
    for c in range(HQ // cw):
        t = jnp.dot(h, w_ref[:, c * cw:(c + 1) * cw], preferred_element_type=F32)
        for j in range(4):
            tj = head(t[:, j * HEAD_DIM:(j + 1) * HEAD_DIM], qg_ref[...]) * (HEAD_DIM ** -0.5)
            q_ref[0, :, c * cw + j * HEAD_DIM:c * cw + (j + 1) * HEAD_DIM] = tj.astype(BF)
    t = jnp.dot(h, w_ref[:, HQ:HQ + HKV], preferred_element_type=F32)
    for j in range(N_KV_HEADS):
        tj = head(t[:, j * HEAD_DIM:(j + 1) * HEAD_DIM], kg_ref[...])
        k_ref[0, :, j * HEAD_DIM:(j + 1) * HEAD_DIM] = tj.astype(BF)
    t = jnp.dot(h, w_ref[:, HQ + HKV:], preferred_element_type=F32)
    v_ref[0] = t.astype(BF)


def _qkv_call(x, mod, g, w_bf, qg, kg, cos, sin, tm):
    B, n, _ = x.shape
    row = lambda b, i: (b, i, 0)
    return pl.pallas_call(
        _qkv_kernel,
        grid=(B, n // tm),
        in_specs=[pl.BlockSpec((1, tm, D_MODEL), row),
                  pl.BlockSpec((1, 6, D_MODEL), lambda b, i: (b, 0, 0)),
                  pl.BlockSpec((1, D_MODEL), lambda b, i: (0, 0)),
                  _resident((D_MODEL, HQ + 2 * HKV), lambda b, i: (0, 0)),
                  pl.BlockSpec((1, HEAD_DIM), lambda b, i: (0, 0)),
                  pl.BlockSpec((1, HEAD_DIM), lambda b, i: (0, 0)),
                  pl.BlockSpec((tm, HEAD_DIM), lambda b, i: (i, 0)),
                  pl.BlockSpec((tm, HEAD_DIM), lambda b, i: (i, 0))],
        out_specs=[pl.BlockSpec((1, tm, HQ), row),
                   pl.BlockSpec((1, tm, HKV), row),
                   pl.BlockSpec((1, tm, HKV), row)],
        out_shape=[jax.ShapeDtypeStruct((B, n, HQ), BF),
                   jax.ShapeDtypeStruct((B, n, HKV), BF),
                   jax.ShapeDtypeStruct((B, n, HKV), BF)],
        compiler_params=_params("parallel", "parallel"),
        name="qkv",
    )(x, mod, g, w_bf, qg, kg, cos, sin)


def _softmax_pv(s, sink_col, vcat):
    m = jnp.maximum(jnp.max(s, axis=-1, keepdims=True), sink_col)
    p = jnp.exp(s - m)
    den = jnp.sum(p, axis=-1, keepdims=True) + jnp.exp(sink_col - m)
    return jnp.dot(p.astype(BF), vcat, preferred_element_type=F32) / den


def _sink_column(sink_ref, h, rows):
    ridx = lax.broadcasted_iota(jnp.int32, (KV_GROUP * rows, 1), 0)
    col = jnp.full((KV_GROUP * rows, 1), sink_ref[h * KV_GROUP], F32)
    for g in range(1, KV_GROUP):
        col = jnp.where(ridx >= g * rows, sink_ref[h * KV_GROUP + g], col)
    return col


def _stack_heads(q):
    return jnp.concatenate([q[:, g * HEAD_DIM:(g + 1) * HEAD_DIM] for g in range(KV_GROUP)], axis=0)


def _attn_kernel(sink_ref, band_ref, q_ref, kp_ref, kc_ref, kn_ref, vp_ref, vc_ref, vn_ref, kx_ref, vx_ref,
                 o_ref, *, n_ctx, qblocks):
    h = pl.program_id(1)
    i = pl.program_id(2)
    last = pl.num_programs(2) - 1
    kloc = jnp.concatenate([kp_ref[0], kc_ref[0], kn_ref[0]], axis=0)
    vloc = jnp.concatenate([vp_ref[0], vc_ref[0], vn_ref[0]], axis=0)
    sink_col = _sink_column(sink_ref, h, BLOCK)
    nk = n_ctx + 3 * BLOCK
    col = lax.broadcasted_iota(jnp.int32, (1, nk), 1)
    edge_lo = jnp.where((col >= n_ctx) & (col < n_ctx + BLOCK) & (i == 0), NEG, 0.0)
    edge_hi = jnp.where((col >= n_ctx + 2 * BLOCK) & (i == last), NEG, 0.0)
    for t in range(qblocks):
        qs = _stack_heads(q_ref[0, t * BLOCK:(t + 1) * BLOCK, :])
        kcat = jnp.concatenate([kx_ref[0], kloc[t * BLOCK:(t + 3) * BLOCK]], axis=0)
        vcat = jnp.concatenate([vx_ref[0], vloc[t * BLOCK:(t + 3) * BLOCK]], axis=0)
        s = lax.dot_general(qs, kcat, (((1,), (1,)), ((), ())), preferred_element_type=F32)
        s = s + band_ref[...]
        if t == 0:
            s = s + edge_lo
        if t == qblocks - 1:
            s = s + edge_hi
        o = _softmax_pv(s, sink_col, vcat)
        for g in range(KV_GROUP):
            o_ref[0, t * BLOCK:(t + 1) * BLOCK, g * HEAD_DIM:(g + 1) * HEAD_DIM] = (
                o[g * BLOCK:(g + 1) * BLOCK].astype(BF))


def _band_bias(n_ctx):
    row = np.arange(KV_GROUP * BLOCK)[:, None] % BLOCK
    col = np.arange(n_ctx + 3 * BLOCK)[None, :]
    rel = col - (n_ctx + BLOCK) - row
    valid = (col < n_ctx) | (np.abs(rel) <= WINDOW)
    return jnp.asarray(np.where(valid, 0.0, NEG), F32)


def _attn_call(q, k, v, kx, vx, sink, qblocks):
    B, n, _ = q.shape
    n_ctx = kx.shape[1]
    nb = n // BLOCK
    tq = qblocks * BLOCK
    gw = KV_GROUP * HEAD_DIM
    prev = lambda b, h, i: (b, jnp.maximum(i * qblocks - 1, 0), h)
    cur = lambda b, h, i: (b, i, h)
    nxt = lambda b, h, i: (b, jnp.minimum((i + 1) * qblocks, nb - 1), h)
    ctx = lambda b, h, i: (b, 0, h)
    edge = lambda m: pl.BlockSpec((1, BLOCK, HEAD_DIM), m)
    mid = pl.BlockSpec((1, tq, HEAD_DIM), cur)
    return pl.pallas_call(
        functools.partial(_attn_kernel, n_ctx=n_ctx, qblocks=qblocks),
        grid=(B, N_KV_HEADS, n // tq),
        in_specs=[pl.BlockSpec(memory_space=pltpu.SMEM),
                  pl.BlockSpec((KV_GROUP * BLOCK, n_ctx + 3 * BLOCK), lambda b, h, i: (0, 0)),
                  pl.BlockSpec((1, tq, gw), cur),
                  edge(prev), mid, edge(nxt), edge(prev), mid, edge(nxt),
                  pl.BlockSpec((1, n_ctx, HEAD_DIM), ctx),
                  pl.BlockSpec((1, n_ctx, HEAD_DIM), ctx)],
        out_specs=pl.BlockSpec((1, tq, gw), cur),
        out_shape=jax.ShapeDtypeStruct((B, n, HQ), BF),
        compiler_params=_params("parallel", "parallel", "arbitrary"),
        name="attn",
    )(sink, _band_bias(n_ctx), q, k, k, k, v, v, v, kx, vx)


def _attn_ctx_kernel(sink_ref, q_ref, k_ref, v_ref, o_ref, *, n_ctx):
    h = pl.program_id(1)
    qs = _stack_heads(q_ref[0])
    s = lax.dot_general(qs, k_ref[0], (((1,), (1,)), ((), ())), preferred_element_type=F32)
    o = _softmax_pv(s, _sink_column(sink_ref, h, n_ctx), v_ref[0])
    for g in range(KV_GROUP):
        o_ref[0, :, g * HEAD_DIM:(g + 1) * HEAD_DIM] = o[g * n_ctx:(g + 1) * n_ctx].astype(BF)


def _attn_ctx_call(q, k, v, sink):
    B, n_ctx, _ = q.shape
    gw = KV_GROUP * HEAD_DIM
    idx = lambda b, h: (b, 0, h)
    return pl.pallas_call(
        functools.partial(_attn_ctx_kernel, n_ctx=n_ctx),
        grid=(B, N_KV_HEADS),
        in_specs=[pl.BlockSpec(memory_space=pltpu.SMEM),
                  pl.BlockSpec((1, n_ctx, gw), idx),
                  pl.BlockSpec((1, n_ctx, HEAD_DIM), idx),
                  pl.BlockSpec((1, n_ctx, HEAD_DIM), idx)],
        out_specs=pl.BlockSpec((1, n_ctx, gw), idx),
        out_shape=jax.ShapeDtypeStruct((B, n_ctx, HQ), BF),
        compiler_params=_params("parallel", "parallel"),
        name="attn_ctx",
    )(sink, q, k, v)


def _mm_res_kernel(a_ref, w_ref, x_ref, mod_ref, o_ref):
    p = jnp.dot(a_ref[0].astype(BF), w_ref[...], preferred_element_type=F32)
    o_ref[0] = x_ref[0] + mod_ref[0, 2:3, :] * p


def _mm_res_call(a, w_bf, x, mod, tm):
    B, n, K = a.shape
    row = lambda b, i: (b, i, 0)
    return pl.pallas_call(
        _mm_res_kernel,
        grid=(B, n // tm),
        in_specs=[pl.BlockSpec((1, tm, K), row),
                  _resident((K, D_MODEL), lambda b, i: (0, 0)),
                  pl.BlockSpec((1, tm, D_MODEL), row),
                  pl.BlockSpec((1, 6, D_MODEL), lambda b, i: (b, 0, 0))],
        out_specs=pl.BlockSpec((1, tm, D_MODEL), row),
        out_shape=jax.ShapeDtypeStruct((B, n, D_MODEL), F32),
        compiler_params=_params("parallel", "parallel"),
        name="mm_res",
    )(a, w_bf, x, mod)


def _ffn_kernel(x_ref, mod_ref, g_ref, wg_ref, wu_ref, wd_ref, o_ref, h_ref, *, rc):
    f = pl.program_id(2)
    last = pl.num_programs(2) - 1
    tm = h_ref.shape[0]

    @pl.when(f == 0)
    def _():
        h_ref[...] = _modnorm(x_ref[0], g_ref[...], mod_ref[0, 3:4, :], mod_ref[0, 4:5, :]).astype(BF)
        o_ref[0] = jnp.zeros(o_ref.shape[1:], F32)

    for r in range(tm // rc):
        rows = slice(r * rc, (r + 1) * rc)
        h = h_ref[rows]
        gate = jnp.dot(h, wg_ref[...], preferred_element_type=F32)
        up = jnp.dot(h, wu_ref[...], preferred_element_type=F32)
        a = (_silu(gate) * up).astype(BF)
        o_ref[0, rows] += jnp.dot(a, wd_ref[...], preferred_element_type=F32)

    @pl.when(f == last)
    def _():
        o_ref[0] = x_ref[0] + mod_ref[0, 5:6, :] * o_ref[0]


def _ffn_call(x, mod, g, wg_bf, wu_bf, wd_bf, tm, tf, rc):
    B, n, _ = x.shape
    row = lambda b, i, f: (b, i, 0)
    return pl.pallas_call(
        functools.partial(_ffn_kernel, rc=rc),
        grid=(B, n // tm, D_FF // tf),
        in_specs=[pl.BlockSpec((1, tm, D_MODEL), row),
                  pl.BlockSpec((1, 6, D_MODEL), lambda b, i, f: (b, 0, 0)),
                  pl.BlockSpec((1, D_MODEL), lambda b, i, f: (0, 0)),
                  pl.BlockSpec((D_MODEL, tf), lambda b, i, f: (0, f)),
                  pl.BlockSpec((D_MODEL, tf), lambda b, i, f: (0, f)),
                  pl.BlockSpec((tf, D_MODEL), lambda b, i, f: (f, 0))],
        out_specs=pl.BlockSpec((1, tm, D_MODEL), row),
        out_shape=jax.ShapeDtypeStruct((B, n, D_MODEL), F32),
        scratch_shapes=[pltpu.VMEM((tm, D_MODEL), BF)],
        compiler_params=pltpu.CompilerParams(dimension_semantics=("parallel", "parallel", "arbitrary"),
                                             vmem_limit_bytes=FFN_VMEM_LIMIT_BYTES),
        name="ffn",
    )(x, mod, g, wg_bf, wu_bf, wd_bf)


def _dft_tables(n):
    n1, n2, s = n // DFT_N2, DFT_N2, SUBLANES
    i32 = jnp.int32
    j = jnp.arange(n2 // s, dtype=i32)[:, None, None]
    r = jnp.arange(2 * n1, dtype=i32)[None, :, None]
    c = jnp.arange(n1 * s, dtype=i32)[None, None, :]
    part, k1, nn1, l = r // n1, r % n1, c // s, c % s
    idx = (k1 * (s * j + l) + n2 * k1 * nn1 + part * (n // 4)) % n
    m1 = jnp.cos(idx.astype(F32) * (2.0 * np.pi / n))
    same_l = (jnp.arange(s, dtype=i32)[:, None] == (jnp.arange(n1 * s, dtype=i32)[None, :] % s)).astype(F32)
    kron1 = (m1[:, :, None, :] * same_l[None, None, :, :]).reshape(n2 // s, 2 * n1 * s, n1 * s)
    r = jnp.arange(2 * n2, dtype=i32)[:, None]
    c = jnp.arange(2 * s * n2, dtype=i32)[None, :]
    qpart, k2, part, nn2 = r // n2, r % n2, c // (s * n2), c % n2
    idx2 = (k2 * nn2 + (n2 // 4) * (qpart - part) + n2) % n2
    m2 = jnp.cos(idx2.astype(F32) * (2.0 * np.pi / n2))
    same_l2 = (jnp.arange(s, dtype=i32)[:, None]
               == ((jnp.arange(2 * s * n2, dtype=i32)[None, :] // n2) % s)).astype(F32)
    kron2 = (m2[:, None, :] * same_l2[None, :, :]).reshape(2 * n2 * s, 2 * s * n2)
    return kron1.astype(BF), kron2.astype(BF)


def _channel_dft_table(n):
    c = np.arange(FOURIER_GROUP_DIM)
    ang = 2.0 * np.pi * ((c[:, None] * c[None, :]) % FOURIER_GROUP_DIM) / FOURIER_GROUP_DIM
    scale = 1.0 / np.sqrt(float(n) * FOURIER_GROUP_DIM)
    return jnp.asarray(np.concatenate([np.cos(ang), np.sin(ang)], axis=0) * scale, BF)


def _channel_mix(zr, zi, cs):
    outs = []
    for g in range(zr.shape[1] // FOURIER_GROUP_DIM):
        sl = slice(g * FOURIER_GROUP_DIM, (g + 1) * FOURIER_GROUP_DIM)
        zg = jnp.concatenate([zr[:, sl], zi[:, sl]], axis=1).astype(BF)
        outs.append(jnp.dot(zg, cs, preferred_element_type=F32))
    return jnp.concatenate(outs, axis=1)


def _dft1_kernel(x_ref, k_ref, t_ref):
    n1, s, tc = x_ref.shape[1], x_ref.shape[2], x_ref.shape[3]
    xb = x_ref[0].reshape(n1 * s, tc).astype(BF)
    t = jnp.dot(k_ref[0], xb, preferred_element_type=F32)
    t_ref[0] = t.reshape(2, n1, s, tc)


def _dft2_kernel(t_ref, k_ref, cs_ref, y_ref):
    s, tc = y_ref.shape[2], y_ref.shape[3]
    rows = s * DFT_N2
    tb = t_ref[0, :, 0].reshape(2 * rows, tc).astype(BF)
    z = jnp.dot(k_ref[...], tb, preferred_element_type=F32)
    y = _channel_mix(z[:rows], z[rows:], cs_ref[...])
    y_ref[0] = y.reshape(DFT_N2, s, tc)


def _fourier_positions(hx, kron1, kron2, cs, tc):
    B, n, _ = hx.shape
    n1, s = n // DFT_N2, SUBLANES
    nj = DFT_N2 // s
    t = pl.pallas_call(
        _dft1_kernel,
        grid=(nj, B, D_MODEL // tc),
        in_specs=[pl.BlockSpec((1, n1, s, tc), lambda j, b, c: (b, 0, j, c)),
                  pl.BlockSpec((1, 2 * n1 * s, n1 * s), lambda j, b, c: (j, 0, 0))],
        out_specs=pl.BlockSpec((1, 2, n1, s, tc), lambda j, b, c: (b, 0, 0, j, c)),
        out_shape=jax.ShapeDtypeStruct((B, 2, n1, DFT_N2, D_MODEL), F32),
        compiler_params=_params("arbitrary", "arbitrary", "arbitrary"),
        name="dft1",
    )(hx.reshape(B, n1, DFT_N2, D_MODEL), kron1)
    na = n1 // s
    y = pl.pallas_call(
        _dft2_kernel,
        grid=(B, na, D_MODEL // tc),
        in_specs=[pl.BlockSpec((1, 2, 1, s * DFT_N2, tc), lambda b, a, c: (b, 0, a, 0, c)),
                  _resident((2 * DFT_N2 * s, 2 * s * DFT_N2), lambda b, a, c: (0, 0)),
                  _resident((2 * FOURIER_GROUP_DIM, FOURIER_GROUP_DIM), lambda b, a, c: (0, 0))],
        out_specs=pl.BlockSpec((1, DFT_N2, s, tc), lambda b, a, c: (b, 0, a, c)),
        out_shape=jax.ShapeDtypeStruct((B, DFT_N2, n1, D_MODEL), F32),
        compiler_params=_params("parallel", "parallel", "parallel"),
        name="dft2",
    )(t.reshape(B, 2, na, s * DFT_N2, D_MODEL), kron2, cs)
    return y.reshape(B, n, D_MODEL)


def _dft_ctx_kernel(x_ref, f_ref, cs_ref, y_ref):
    n_ctx = x_ref.shape[1]
    z = jnp.dot(f_ref[...], x_ref[0].astype(BF), preferred_element_type=F32)
    y_ref[0] = _channel_mix(z[:n_ctx], z[n_ctx:], cs_ref[...])


def _fourier_ctx(hc, cs):
    B, n_ctx, _ = hc.shape
    p = np.arange(n_ctx)
    ang = 2.0 * np.pi * ((p[:, None] * p[None, :]) % n_ctx) / n_ctx
    fmat = jnp.asarray(np.concatenate([np.cos(ang), -np.sin(ang)], axis=0), BF)
    return pl.pallas_call(
        _dft_ctx_kernel,
        grid=(B,),
        in_specs=[pl.BlockSpec((1, n_ctx, D_MODEL), lambda b: (b, 0, 0)),
                  pl.BlockSpec((2 * n_ctx, n_ctx), lambda b: (0, 0)),
                  pl.BlockSpec((2 * FOURIER_GROUP_DIM, FOURIER_GROUP_DIM), lambda b: (0, 0))],
        out_specs=pl.BlockSpec((1, n_ctx, D_MODEL), lambda b: (b, 0, 0)),
        out_shape=jax.ShapeDtypeStruct((B, n_ctx, D_MODEL), F32),
        compiler_params=_params("parallel"),
        name="dft_ctx",
    )(hc, fmat, cs)


def _sgu_kernel(x_ref, mod_ref, g_ref, win_ref, gv_ref, ws_ref, bs_ref, wout_ref, o_ref,
                h_ref, z_ref, ssq_ref):
    s = pl.program_id(2)
    ng = N_SGU_GROUPS
    tm = h_ref.shape[0]

    @pl.when(s == 0)
    def _():
        h_ref[...] = _modnorm(x_ref[0], g_ref[...], mod_ref[0, 0:1, :], mod_ref[0, 1:2, :]).astype(BF)
        ssq_ref[...] = jnp.zeros_like(ssq_ref)

    @pl.when(s < 2 * ng)
    def _():
        z = jnp.dot(h_ref[...], win_ref[...], preferred_element_type=F32)
        z = 0.5 * z * (1.0 + lax.erf(z * (2.0 ** -0.5)))
        z_ref[s] = z.astype(BF)

        @pl.when(s >= ng)
        def _():
            ssq_ref[...] += jnp.sum(z * z, axis=-1, keepdims=True)

    @pl.when(s >= 2 * ng)
    def _():
        grp = s - 2 * ng
        r = lax.rsqrt(ssq_ref[...] * (1.0 / SGU_HALF) + EPS)
        vn = ((z_ref[grp + ng].astype(F32) * r) * gv_ref[0]).astype(BF)
        u = z_ref[grp].astype(F32)
        parts = []
        for c in range(tm // SGU_CHUNK):
            rows = slice(c * SGU_CHUNK, (c + 1) * SGU_CHUNK)
            sp = jnp.dot(ws_ref[0], vn[rows], preferred_element_type=F32) + bs_ref[0]
            parts.append((u[rows] * sp).astype(BF))
        a = jnp.concatenate(parts, axis=0)
        p = jnp.dot(a, wout_ref[...], preferred_element_type=F32)

        @pl.when(grp == 0)
        def _():
            o_ref[0] = p

        @pl.when(grp > 0)
        def _():
            o_ref[0] += p

        @pl.when(grp == ng - 1)
        def _():
            o_ref[0] = x_ref[0] + mod_ref[0, 2:3, :] * o_ref[0]


def _sgu_call(x, mod, g, win_bf, gv, ws_bf, bs, wout_bf, tm):
    B, n, _ = x.shape
    ng, gd = N_SGU_GROUPS, SGU_GROUP_DIM
    row = lambda b, i, s: (b, i, 0)
    grp = lambda b, i, s: (jnp.maximum(s - 2 * ng, 0), 0, 0)
    return pl.pallas_call(
        _sgu_kernel,
        grid=(B, n // tm, 3 * ng),
        in_specs=[pl.BlockSpec((1, tm, D_MODEL), row),
                  pl.BlockSpec((1, 6, D_MODEL), lambda b, i, s: (b, 0, 0)),
                  pl.BlockSpec((1, D_MODEL), lambda b, i, s: (0, 0)),
                  pl.BlockSpec((D_MODEL, gd), lambda b, i, s: (0, jnp.minimum(s, 2 * ng - 1))),
                  pl.BlockSpec((1, 1, gd), grp),
                  pl.BlockSpec((1, SGU_CHUNK, SGU_CHUNK), grp),
                  pl.BlockSpec((1, SGU_CHUNK, 1), grp),
                  pl.BlockSpec((gd, D_MODEL), lambda b, i, s: (jnp.maximum(s - 2 * ng, 0), 0))],
        out_specs=pl.BlockSpec((1, tm, D_MODEL), row),
        out_shape=jax.ShapeDtypeStruct((B, n, D_MODEL), F32),
        scratch_shapes=[pltpu.VMEM((tm, D_MODEL), BF),
                        pltpu.VMEM((2 * ng, tm, gd), BF),
                        pltpu.VMEM((tm, 1), F32)],
        compiler_params=_params("parallel", "parallel", "arbitrary"),
        name="sgu",
    )(x, mod, g, win_bf, gv.reshape(ng, 1, gd), ws_bf, bs.reshape(ng, SGU_CHUNK, 1), wout_bf)


def _rope_tables(n):
    row, col = jnp.meshgrid(jnp.arange(n // GRID_W), jnp.arange(GRID_W), indexing="ij")
    n_freq = HEAD_DIM // 4
    inv_freq = ROPE_BASE ** (-jnp.arange(n_freq, dtype=F32) / n_freq)
    ang = jnp.concatenate([row.reshape(-1, 1).astype(F32) * inv_freq,
                           col.reshape(-1, 1).astype(F32) * inv_freq], axis=-1)
    ang = jnp.concatenate([ang, ang], axis=-1)
    sign = jnp.where(jnp.arange(HEAD_DIM) < HEAD_DIM // 2, -1.0, 1.0).astype(F32)
    return jnp.cos(ang), jnp.sin(ang) * sign


def kernel(x, c, ctx, c_ctx, w_ada, b_ada, norm_g, w_ffn_gate, w_ffn_up, w_ffn_down, w_attn_qkv,
           w_attn_o, attn_q_g, attn_k_g, attn_sink, w_fourier, w_sgu_in, sgu_v_g, w_sgu_spatial,
           b_sgu_spatial, w_sgu_out):
    B, n, _ = x.shape
    n_ctx = ctx.shape[1]
    tm_x = 512
    tm_c = n_ctx

    cond = jnp.zeros((SUBLANES, D_MODEL), F32).at[:B].set(c).at[B].set(c_ctx)
    ada = _ada_all(cond, w_ada, b_ada)
    cos, sin = _rope_tables(n)
    cos_c = jnp.ones((n_ctx, HEAD_DIM), F32)
    sin_c = jnp.zeros((n_ctx, HEAD_DIM), F32)

    for i in range(DEPTH):
        kind = i % N_MIXERS
        j = i // N_MIXERS
        need_ctx = i < DEPTH - 1
        modx = ada[i, :B].reshape(B, 6, D_MODEL)
        modc = jnp.broadcast_to(ada[i, B].reshape(1, 6, D_MODEL), (B, 6, D_MODEL))
        g1 = norm_g[i, 0].reshape(1, D_MODEL)
        g2 = norm_g[i, 1].reshape(1, D_MODEL)
        if kind == 0:
            wqkv = w_attn_qkv[j].astype(BF)
            wo = w_attn_o[j].astype(BF)
            qg = attn_q_g[j].reshape(1, HEAD_DIM)
            kg = attn_k_g[j].reshape(1, HEAD_DIM)
            sink = attn_sink[j]
            q, k, v = _qkv_call(x, modx, g1, wqkv, qg, kg, cos, sin, tm_x)
            qc, kc, vc = _qkv_call(ctx, modc, g1, wqkv, qg, kg, cos_c, sin_c, tm_c)
            ox = _attn_call(q, k, v, kc, vc, sink, 4)
            x = _mm_res_call(ox, wo, x, modx, tm_x)
            if need_ctx:
                oc = _attn_ctx_call(qc, kc, vc, sink)
                ctx = _mm_res_call(oc, wo, ctx, modc, tm_c)
        elif kind == 1:
            wf = w_fourier[j].astype(BF)
            kron1, kron2 = _dft_tables(n)
            hx = _modnorm_call(x, modx, g1, tm_x)
            yx = _fourier_positions(hx, kron1, kron2, _channel_dft_table(n), 512)
            x = _mm_res_call(yx, wf, x, modx, tm_x)
            if need_ctx:
                hc = _modnorm_call(ctx, modc, g1, tm_c)
                yc = _fourier_ctx(hc, _channel_dft_table(n_ctx))
                ctx = _mm_res_call(yc, wf, ctx, modc, tm_c)
        else:
            win = w_sgu_in[j].astype(BF)
            wout = w_sgu_out[j].astype(BF)
            ws = w_sgu_spatial[j].astype(BF)
            args = (win, sgu_v_g[j], ws, b_sgu_spatial[j], wout)
            x_new = _sgu_call(x, modx, g1, *args, tm_x)
            if need_ctx:
                ctx = _sgu_call(ctx, modc, g1, *args, tm_c)
            x = x_new
        wg = w_ffn_gate[i].astype(BF)
        wu = w_ffn_up[i].astype(BF)
        wd = w_ffn_down[i].astype(BF)
        x = _ffn_call(x, modx, g2, wg, wu, wd, 1024, 512, 512)
        if need_ctx:
            ctx = _ffn_call(ctx, modc, g2, wg, wu, wd, tm_c, 512, tm_c)
    return x
```

```python
import functools

import numpy as np
import jax
import jax.numpy as jnp
from jax import lax
from jax.experimental import pallas as pl
from jax.experimental.pallas import tpu as pltpu

D_MODEL = 2048
DEPTH = 4
GRID_W = 64
N_MIXERS = 3
EPS = 1e-6
N_HEADS = 16
N_KV_HEADS = 4
HEAD_DIM = D_MODEL // N_HEADS
KV_GROUP = N_HEADS // N_KV_HEADS
HQ = N_HEADS * HEAD_DIM
HKV = N_KV_HEADS * HEAD_DIM
WINDOW = 128
BLOCK = 128
ROPE_BASE = 10000.0
N_FOURIER_GROUPS = 8
FOURIER_GROUP_DIM = D_MODEL // N_FOURIER_GROUPS
SGU_CHUNK = 128
SGU_HALF = 3 * D_MODEL
N_SGU_GROUPS = 8
SGU_GROUP_DIM = SGU_HALF // N_SGU_GROUPS
D_FF = 5632

BF = jnp.bfloat16
F32 = jnp.float32

VMEM_LIMIT_BYTES = 56 * 1024 * 1024
FFN_VMEM_LIMIT_BYTES = 60 * 1024 * 1024
SUBLANES = 8
NEG = -1e30

DFT_N2 = 64


def _params(*sem):
    return pltpu.CompilerParams(dimension_semantics=sem, vmem_limit_bytes=VMEM_LIMIT_BYTES)


def _resident(shape, index_map):
    return pl.BlockSpec(shape, index_map, pipeline_mode=pl.Buffered(1))


def _modnorm(x, g, shift, scale):
    ms = jnp.mean(x * x, axis=-1, keepdims=True)
    y = x * lax.rsqrt(ms + EPS)
    return (y * g) * (1.0 + scale) + shift


def _silu(t):
    return t * jax.nn.sigmoid(t)


def _ada_kernel(cond_ref, w_ref, b_ref, o_ref):
    s = _silu(cond_ref[...]).astype(BF)
    o_ref[0] = jnp.dot(s, w_ref[0].astype(BF), preferred_element_type=F32) + b_ref[0]


def _ada_all(cond, w_ada, b_ada):
    tn = 1024
    return pl.pallas_call(
        _ada_kernel,
        grid=(DEPTH, 6 * D_MODEL // tn),
        in_specs=[pl.BlockSpec((SUBLANES, D_MODEL), lambda l, j: (0, 0)),
                  pl.BlockSpec((1, D_MODEL, tn), lambda l, j: (l, 0, j)),
                  pl.BlockSpec((1, 1, tn), lambda l, j: (l, 0, j))],
        out_specs=pl.BlockSpec((1, SUBLANES, tn), lambda l, j: (l, 0, j)),
        out_shape=jax.ShapeDtypeStruct((DEPTH, SUBLANES, 6 * D_MODEL), F32),
        compiler_params=_params("arbitrary", "arbitrary"),
        name="ada",
    )(cond, w_ada, b_ada.reshape(DEPTH, 1, 6 * D_MODEL))


def _modnorm_kernel(x_ref, mod_ref, g_ref, o_ref):
    o_ref[0] = _modnorm(x_ref[0], g_ref[...], mod_ref[0, 0:1, :], mod_ref[0, 1:2, :])


def _modnorm_call(x, mod, g, tm):
    B, n, _ = x.shape
    return pl.pallas_call(
        _modnorm_kernel,
        grid=(B, n // tm),
        in_specs=[pl.BlockSpec((1, tm, D_MODEL), lambda b, i: (b, i, 0)),
                  pl.BlockSpec((1, 6, D_MODEL), lambda b, i: (b, 0, 0)),
                  pl.BlockSpec((1, D_MODEL), lambda b, i: (0, 0))],
        out_specs=pl.BlockSpec((1, tm, D_MODEL), lambda b, i: (b, i, 0)),
        out_shape=jax.ShapeDtypeStruct((B, n, D_MODEL), F32),
        compiler_params=_params("parallel", "parallel"),
        name="modnorm",
    )(x, mod, g)


def _qkv_kernel(x_ref, mod_ref, g_ref, w_ref, qg_ref, kg_ref, cos_ref, sin_ref, q_ref, k_ref, v_ref):
    h = _modnorm(x_ref[0], g_ref[...], mod_ref[0, 0:1, :], mod_ref[0, 1:2, :]).astype(BF)
    cos = cos_ref[...]
    sin = sin_ref[...]

    def head(t, gain):
        ms = jnp.mean(t * t, axis=-1, keepdims=True)
        t = (t * lax.rsqrt(ms + EPS)) * gain
        return t * cos + pltpu.roll(t, HEAD_DIM // 2, 1) * sin

    t = jnp.dot(h, w_ref[:, HQ:HQ + HKV], preferred_element_type=F32)
    for j in range(N_KV_HEADS):
        tj = head(t[:, j * HEAD_DIM:(j + 1) * HEAD_DIM], kg_ref[...])
        k_ref[0, :, j * HEAD_DIM:(j + 1) * HEAD_DIM] = tj.astype(BF)
    cw = 4 * HEAD_DIM
    for c in range(HQ // cw):
        t = jnp.dot(h, w_ref[:, c * cw:(c + 1) * cw], preferred_element_type=F32)
        for j in range(4):
            tj = head(t[:, j * HEAD_DIM:(j + 1) * HEAD_DIM], qg_ref[...]) * (HEAD_DIM ** -0.5)
            q_ref[0, :, c * cw + j * HEAD_DIM:c * cw + (j + 1) * HEAD_DIM] = tj.astype(BF)
    t = jnp.dot(h, w_ref[:, HQ + HKV:], preferred_element_type=F32)
    v_ref[0] = t.astype(BF)


def _qkv_call(x, mod, g, w_bf, qg, kg, cos, sin, tm):
    B, n, _ = x.shape
    row = lambda b, i: (b, i, 0)
    return pl.pallas_call(
        _qkv_kernel,
        grid=(B, n // tm),
        in_specs=[pl.BlockSpec((1, tm, D_MODEL), row),
                  pl.BlockSpec((1, 6, D_MODEL), lambda b, i: (b, 0, 0)),
                  pl.BlockSpec((1, D_MODEL), lambda b, i: (0, 0)),
                  _resident((D_MODEL, HQ + 2 * HKV), lambda b, i: (0, 0)),
                  pl.BlockSpec((1, HEAD_DIM), lambda b, i: (0, 0)),
                  pl.BlockSpec((1, HEAD_DIM), lambda b, i: (0, 0)),
                  pl.BlockSpec((tm, HEAD_DIM), lambda b, i: (i, 0)),
                  pl.BlockSpec((tm, HEAD_DIM), lambda b, i: (i, 0))],
        out_specs=[pl.BlockSpec((1, tm, HQ), row),
                   pl.BlockSpec((1, tm, HKV), row),
                   pl.BlockSpec((1, tm, HKV), row)],
        out_shape=[jax.ShapeDtypeStruct((B, n, HQ), BF),
                   jax.ShapeDtypeStruct((B, n, HKV), BF),
                   jax.ShapeDtypeStruct((B, n, HKV), BF)],
        compiler_params=_params("parallel", "parallel"),
        name="qkv",
    )(x, mod, g, w_bf, qg, kg, cos, sin)


def _softmax_pv(s, sink_col, vcat):
    m = jnp.maximum(jnp.max(s, axis=-1, keepdims=True), sink_col)
    p = jnp.exp(s - m)
    den = jnp.sum(p, axis=-1, keepdims=True) + jnp.exp(sink_col - m)
    return jnp.dot(p.astype(BF), vcat, preferred_element_type=F32) / den


def _sink_column(sink_ref, h, rows):
    ridx = lax.broadcasted_iota(jnp.int32, (KV_GROUP * rows, 1), 0)
    col = jnp.full((KV_GROUP * rows, 1), sink_ref[h * KV_GROUP], F32)
    for g in range(1, KV_GROUP):
        col = jnp.where(ridx >= g * rows, sink_ref[h * KV_GROUP + g], col)
    return col


def _stack_heads(q):
    return jnp.concatenate([q[:, g * HEAD_DIM:(g + 1) * HEAD_DIM] for g in range(KV_GROUP)], axis=0)


def _attn_kernel(sink_ref, band_ref, q_ref, kp_ref, kc_ref, kn_ref, vp_ref, vc_ref, vn_ref, kx_ref, vx_ref,
                 o_ref, *, n_ctx, qblocks):
    h = pl.program_id(1)
    i = pl.program_id(2)
    last = pl.num_programs(2) - 1
    kloc = jnp.concatenate([kp_ref[0], kc_ref[0], kn_ref[0]], axis=0)
    vloc = jnp.concatenate([vp_ref[0], vc_ref[0], vn_ref[0]], axis=0)
    sink_col = _sink_column(sink_ref, h, BLOCK)
    nk = n_ctx + 3 * BLOCK
    col = lax.broadcasted_iota(jnp.int32, (1, nk), 1)
    edge_lo = jnp.where((col >= n_ctx) & (col < n_ctx + BLOCK) & (i == 0), NEG, 0.0)
    edge_hi = jnp.where((col >= n_ctx + 2 * BLOCK) & (i == last), NEG, 0.0)
    for t in range(qblocks):
        qs = _stack_heads(q_ref[0, t * BLOCK:(t + 1) * BLOCK, :])
        kcat = jnp.concatenate([kx_ref[0], kloc[t * BLOCK:(t + 3) * BLOCK]], axis=0)
        vcat = jnp.concatenate([vx_ref[0], vloc[t * BLOCK:(t + 3) * BLOCK]], axis=0)
        s = lax.dot_general(qs, kcat, (((1,), (1,)), ((), ())), preferred_element_type=F32)
        s = s + band_ref[...]
        if t == 0:
            s = s + edge_lo
        if t == qblocks - 1:
            s = s + edge_hi
        o = _softmax_pv(s, sink_col, vcat)
        for g in range(KV_GROUP):
            o_ref[0, t * BLOCK:(t + 1) * BLOCK, g * HEAD_DIM:(g + 1) * HEAD_DIM] = (
                o[g * BLOCK:(g + 1) * BLOCK].astype(BF))


def _band_bias(n_ctx):
    row = np.arange(KV_GROUP * BLOCK)[:, None] % BLOCK
    col = np.arange(n_ctx + 3 * BLOCK)[None, :]
    rel = col - (n_ctx + BLOCK) - row
    valid = (col < n_ctx) | (np.abs(rel) <= WINDOW)
    return jnp.asarray(np.where(valid, 0.0, NEG), F32)


def _attn_call(q, k, v, kx, vx, sink, qblocks):
    B, n, _ = q.shape
    n_ctx = kx.shape[1]
    nb = n // BLOCK
    tq = qblocks * BLOCK
    gw = KV_GROUP * HEAD_DIM
    prev = lambda b, h, i: (b, jnp.maximum(i * qblocks - 1, 0), h)
    cur = lambda b, h, i: (b, i, h)
    nxt = lambda b, h, i: (b, jnp.minimum((i + 1) * qblocks, nb - 1), h)
    ctx = lambda b, h, i: (b, 0, h)
    edge = lambda m: pl.BlockSpec((1, BLOCK, HEAD_DIM), m)
    mid = pl.BlockSpec((1, tq, HEAD_DIM), cur)
    return pl.pallas_call(
        functools.partial(_attn_kernel, n_ctx=n_ctx, qblocks=qblocks),
        grid=(B, N_KV_HEADS, n // tq),
        in_specs=[pl.BlockSpec(memory_space=pltpu.SMEM),
                  pl.BlockSpec((KV_GROUP * BLOCK, n_ctx + 3 * BLOCK), lambda b, h, i: (0, 0)),
                  pl.BlockSpec((1, tq, gw), cur),
                  edge(prev), mid, edge(nxt), edge(prev), mid, edge(nxt),
                  pl.BlockSpec((1, n_ctx, HEAD_DIM), ctx),
                  pl.BlockSpec((1, n_ctx, HEAD_DIM), ctx)],
        out_specs=pl.BlockSpec((1, tq, gw), cur),
        out_shape=jax.ShapeDtypeStruct((B, n, HQ), BF),
        compiler_params=_params("parallel", "parallel", "arbitrary"),
        name="attn",
    )(sink, _band_bias(n_ctx), q, k, k, k, v, v, v, kx, vx)


def _attn_ctx_kernel(sink_ref, q_ref, k_ref, v_ref, o_ref, *, n_ctx):
    h = pl.program_id(1)
    qs = _stack_heads(q_ref[0])
    s = lax.dot_general(qs, k_ref[0], (((1,), (1,)), ((), ())), preferred_element_type=F32)
    o = _softmax_pv(s, _sink_column(sink_ref, h, n_ctx), v_ref[0])
    for g in range(KV_GROUP):
        o_ref[0, :, g * HEAD_DIM:(g + 1) * HEAD_DIM] = o[g * n_ctx:(g + 1) * n_ctx].astype(BF)


def _attn_ctx_call(q, k, v, sink):
    B, n_ctx, _ = q.shape
    gw = KV_GROUP * HEAD_DIM
    idx = lambda b, h: (b, 0, h)
    return pl.pallas_call(
        functools.partial(_attn_ctx_kernel, n_ctx=n_ctx),
        grid=(B, N_KV_HEADS),
        in_specs=[pl.BlockSpec(memory_space=pltpu.SMEM),
                  pl.BlockSpec((1, n_ctx, gw), idx),
                  pl.BlockSpec((1, n_ctx, HEAD_DIM), idx),
                  pl.BlockSpec((1, n_ctx, HEAD_DIM), idx)],
        out_specs=pl.BlockSpec((1, n_ctx, gw), idx),
        out_shape=jax.ShapeDtypeStruct((B, n_ctx, HQ), BF),
        compiler_params=_params("parallel", "parallel"),
        name="attn_ctx",
    )(sink, q, k, v)


def _mm_res_kernel(a_ref, w_ref, x_ref, mod_ref, g2_ref, o_ref, h_ref, *, rc):
    tm = x_ref.shape[1]
    for r in range(tm // rc):
        rows = slice(r * rc, (r + 1) * rc)
        p = jnp.dot(a_ref[0, rows].astype(BF), w_ref[...], preferred_element_type=F32)
        xn = x_ref[0, rows] + mod_ref[0, 2:3, :] * p
        o_ref[0, rows] = xn
        h_ref[0, rows] = _modnorm(xn, g2_ref[...], mod_ref[0, 3:4, :], mod_ref[0, 4:5, :]).astype(BF)


def _mm_res_call(a, w_bf, x, mod, g2, tm, rc):
    B, n, K = a.shape
    row = lambda b, i: (b, i, 0)
    return pl.pallas_call(
        functools.partial(_mm_res_kernel, rc=rc),
        grid=(B, n // tm),
        in_specs=[pl.BlockSpec((1, tm, K), row),
                  _resident((K, D_MODEL), lambda b, i: (0, 0)),
                  pl.BlockSpec((1, tm, D_MODEL), row),
                  pl.BlockSpec((1, 6, D_MODEL), lambda b, i: (b, 0, 0)),
                  pl.BlockSpec((1, D_MODEL), lambda b, i: (0, 0))],
        out_specs=[pl.BlockSpec((1, tm, D_MODEL), row),
                   pl.BlockSpec((1, tm, D_MODEL), row)],
        out_shape=[jax.ShapeDtypeStruct((B, n, D_MODEL), F32),
                   jax.ShapeDtypeStruct((B, n, D_MODEL), BF)],
        compiler_params=_params("parallel", "parallel"),
        name="mm_res",
    )(a, w_bf, x, mod, g2)


def _ffn_kernel(h_ref, x_ref, mod_ref, wg_ref, wu_ref, wd_ref, o_ref, *, rc, nc):
    f = pl.program_id(2)
    last = pl.num_programs(2) - 1
    tm = h_ref.shape[1]

    @pl.when(f == 0)
    def _():
        o_ref[0] = jnp.zeros(o_ref.shape[1:], F32)

    for r in range(tm // rc):
        rows = slice(r * rc, (r + 1) * rc)
        h = h_ref[0, rows]
        gate = jnp.dot(h, wg_ref[...], preferred_element_type=F32)
        up = jnp.dot(h, wu_ref[...], preferred_element_type=F32)
        a = (_silu(gate) * up).astype(BF)
        for c in range(D_MODEL // nc):
            cols = slice(c * nc, (c + 1) * nc)
            o_ref[0, rows, cols] += jnp.dot(a, wd_ref[:, cols], preferred_element_type=F32)

    @pl.when(f == last)
    def _():
        o_ref[0] = x_ref[0] + mod_ref[0, 5:6, :] * o_ref[0]


def _ffn_call(h, x, mod, wg_bf, wu_bf, wd_bf, tm, tf, rc, nc):
    B, n, _ = x.shape
    row = lambda b, i, f: (b, i, 0)
    return pl.pallas_call(
        functools.partial(_ffn_kernel, rc=rc, nc=nc),
        grid=(B, n // tm, D_FF // tf),
        in_specs=[pl.BlockSpec((1, tm, D_MODEL), row),
                  pl.BlockSpec((1, tm, D_MODEL), row),
                  pl.BlockSpec((1, 6, D_MODEL), lambda b, i, f: (b, 0, 0)),
                  pl.BlockSpec((D_MODEL, tf), lambda b, i, f: (0, f)),
                  pl.BlockSpec((D_MODEL, tf), lambda b, i, f: (0, f)),
                  pl.BlockSpec((tf, D_MODEL), lambda b, i, f: (f, 0))],
        out_specs=pl.BlockSpec((1, tm, D_MODEL), row),
        out_shape=jax.ShapeDtypeStruct((B, n, D_MODEL), F32),
        compiler_params=pltpu.CompilerParams(dimension_semantics=("parallel", "parallel", "arbitrary"),
                                             vmem_limit_bytes=FFN_VMEM_LIMIT_BYTES),
        name="ffn",
    )(h, x, mod, wg_bf, wu_bf, wd_bf)


def _dft_tables(n):
    n1, n2, s = n // DFT_N2, DFT_N2, SUBLANES
    i32 = jnp.int32
    j = jnp.arange(n2 // s, dtype=i32)[:, None, None]
    r = jnp.arange(2 * n1, dtype=i32)[None, :, None]
    c = jnp.arange(n1 * s, dtype=i32)[None, None, :]
    part, k1, nn1, l = r // n1, r % n1, c // s, c % s
    idx = (k1 * (s * j + l) + n2 * k1 * nn1 + part * (n // 4)) % n
    m1 = jnp.cos(idx.astype(F32) * (2.0 * np.pi / n))
    same_l = (jnp.arange(s, dtype=i32)[:, None] == (jnp.arange(n1 * s, dtype=i32)[None, :] % s)).astype(F32)
    kron1 = (m1[:, :, None, :] * same_l[None, None, :, :]).reshape(n2 // s, 2 * n1 * s, n1 * s)
    r = jnp.arange(2 * n2, dtype=i32)[:, None]
    c = jnp.arange(2 * s * n2, dtype=i32)[None, :]
    qpart, k2, part, nn2 = r // n2, r % n2, c // (s * n2), c % n2
    idx2 = (k2 * nn2 + (n2 // 4) * (qpart - part) + n2) % n2
    m2 = jnp.cos(idx2.astype(F32) * (2.0 * np.pi / n2))
    same_l2 = (jnp.arange(s, dtype=i32)[:, None]
               == ((jnp.arange(2 * s * n2, dtype=i32)[None, :] // n2) % s)).astype(F32)
    kron2 = (m2[:, None, :] * same_l2[None, :, :]).reshape(2 * n2 * s, 2 * s * n2)
    return kron1.astype(BF), kron2.astype(BF)


def _channel_dft_table(n):
    c = np.arange(FOURIER_GROUP_DIM)
    ang = 2.0 * np.pi * ((c[:, None] * c[None, :]) % FOURIER_GROUP_DIM) / FOURIER_GROUP_DIM
    scale = 1.0 / np.sqrt(float(n) * FOURIER_GROUP_DIM)
    return jnp.asarray(np.concatenate([np.cos(ang), np.sin(ang)], axis=0) * scale, BF)


def _channel_mix(zr, zi, cs):
    outs = []
    for g in range(zr.shape[1] // FOURIER_GROUP_DIM):
        sl = slice(g * FOURIER_GROUP_DIM, (g + 1) * FOURIER_GROUP_DIM)
        zg = jnp.concatenate([zr[:, sl], zi[:, sl]], axis=1).astype(BF)
        outs.append(jnp.dot(zg, cs, preferred_element_type=F32))
    return jnp.concatenate(outs, axis=1)


def _dft1_kernel(x_ref, k_ref, t_ref):
    n1, s, tc = x_ref.shape[1], x_ref.shape[2], x_ref.shape[3]
    xb = x_ref[0].reshape(n1 * s, tc).astype(BF)
    t = jnp.dot(k_ref[0], xb, preferred_element_type=F32)
    t_ref[0] = t.reshape(2, n1, s, tc)


def _dft2_kernel(t_ref, k_ref, cs_ref, y_ref):
    s, tc = y_ref.shape[2], y_ref.shape[3]
    rows = s * DFT_N2
    tb = t_ref[0, :, 0].reshape(2 * rows, tc).astype(BF)
    z = jnp.dot(k_ref[...], tb, preferred_element_type=F32)
    y = _channel_mix(z[:rows], z[rows:], cs_ref[...])
    y_ref[0] = y.reshape(DFT_N2, s, tc)


def _fourier_positions(hx, kron1, kron2, cs, tc):
    B, n, _ = hx.shape
    n1, s = n // DFT_N2, SUBLANES
    nj = DFT_N2 // s
    t = pl.pallas_call(
        _dft1_kernel,
        grid=(nj, B, D_MODEL // tc),
        in_specs=[pl.BlockSpec((1, n1, s, tc), lambda j, b, c: (b, 0, j, c)),
                  pl.BlockSpec((1, 2 * n1 * s, n1 * s), lambda j, b, c: (j, 0, 0))],
        out_specs=pl.BlockSpec((1, 2, n1, s, tc), lambda j, b, c: (b, 0, 0, j, c)),
        out_shape=jax.ShapeDtypeStruct((B, 2, n1, DFT_N2, D_MODEL), F32),
        compiler_params=_params("arbitrary", "arbitrary", "arbitrary"),
        name="dft1",
    )(hx.reshape(B, n1, DFT_N2, D_MODEL), kron1)
    na = n1 // s
    y = pl.pallas_call(
        _dft2_kernel,
        grid=(B, na, D_MODEL // tc),
        in_specs=[pl.BlockSpec((1, 2, 1, s * DFT_N2, tc), lambda b, a, c: (b, 0, a, 0, c)),
                  _resident((2 * DFT_N2 * s, 2 * s * DFT_N2), lambda b, a, c: (0, 0)),
                  _resident((2 * FOURIER_GROUP_DIM, FOURIER_GROUP_DIM), lambda b, a, c: (0, 0))],
        out_specs=pl.BlockSpec((1, DFT_N2, s, tc), lambda b, a, c: (b, 0, a, c)),
        out_shape=jax.ShapeDtypeStruct((B, DFT_N2, n1, D_MODEL), F32),
        compiler_params=_params("parallel", "parallel", "parallel"),
        name="dft2",
    )(t.reshape(B, 2, na, s * DFT_N2, D_MODEL), kron2, cs)
    return y.reshape(B, n, D_MODEL)


def _dft_ctx_kernel(x_ref, f_ref, cs_ref, y_ref):
    n_ctx = x_ref.shape[1]
    z = jnp.dot(f_ref[...], x_ref[0].astype(BF), preferred_element_type=F32)
    y_ref[0] = _channel_mix(z[:n_ctx], z[n_ctx:], cs_ref[...])


def _fourier_ctx(hc, cs):
    B, n_ctx, _ = hc.shape
    p = np.arange(n_ctx)
    ang = 2.0 * np.pi * ((p[:, None] * p[None, :]) % n_ctx) / n_ctx
    fmat = jnp.asarray(np.concatenate([np.cos(ang), -np.sin(ang)], axis=0), BF)
    return pl.pallas_call(
        _dft_ctx_kernel,
        grid=(B,),
        in_specs=[pl.BlockSpec((1, n_ctx, D_MODEL), lambda b: (b, 0, 0)),
                  pl.BlockSpec((2 * n_ctx, n_ctx), lambda b: (0, 0)),
                  pl.BlockSpec((2 * FOURIER_GROUP_DIM, FOURIER_GROUP_DIM), lambda b: (0, 0))],
        out_specs=pl.BlockSpec((1, n_ctx, D_MODEL), lambda b: (b, 0, 0)),
        out_shape=jax.ShapeDtypeStruct((B, n_ctx, D_MODEL), F32),
        compiler_params=_params("parallel"),
        name="dft_ctx",
    )(hc, fmat, cs)


def _sgu_kernel(x_ref, mod_ref, g_ref, g2_ref, win_ref, gv_ref, ws_ref, bs_ref, wout_ref, o_ref, h2_ref,
                h_ref, z_ref, ssq_ref, a0_ref, a1_ref, *, cw, nc):
    s = pl.program_id(2)
    ng = N_SGU_GROUPS
    gd = SGU_GROUP_DIM
    tm = h_ref.shape[0]
    t = s - 2 * ng

    @pl.when(s == 0)
    def _():
        h_ref[...] = _modnorm(x_ref[0], g_ref[...], mod_ref[0, 0:1, :], mod_ref[0, 1:2, :]).astype(BF)
        ssq_ref[...] = jnp.zeros_like(ssq_ref)
        o_ref[0] = jnp.zeros(o_ref.shape[1:], F32)

    def in_proj(with_ssq):
        sq = None
        for c in range(gd // cw):
            cols = slice(c * cw, (c + 1) * cw)
            z = jnp.dot(h_ref[...], win_ref[:, cols], preferred_element_type=F32)
            z = 0.5 * z * (1.0 + lax.erf(z * (2.0 ** -0.5)))
            z_ref[s, :, cols] = z.astype(BF)
            if with_ssq:
                part = jnp.sum(z * z, axis=-1, keepdims=True)
                sq = part if sq is None else sq + part
        if with_ssq:
            ssq_ref[...] += sq

    @pl.when(s < ng)
    def _():
        in_proj(False)

    @pl.when((s >= ng) & (s < 2 * ng))
    def _():
        in_proj(True)

    def out_proj(a_ref):
        a = a_ref[...]
        for c in range(D_MODEL // nc):
            cols = slice(c * nc, (c + 1) * nc)
            o_ref[0, :, cols] += jnp.dot(a, wout_ref[:, cols], preferred_element_type=F32)

    def prep(a_ref, grp):
        r = lax.rsqrt(ssq_ref[...] * (1.0 / SGU_HALF) + EPS)
        for c in range(tm // SGU_CHUNK):
            rows = slice(c * SGU_CHUNK, (c + 1) * SGU_CHUNK)
            vn = ((z_ref[grp + ng, rows, :].astype(F32) * r[rows]) * gv_ref[0]).astype(BF)
            sp = jnp.dot(ws_ref[0], vn, preferred_element_type=F32) + bs_ref[0]
            a_ref[rows, :] = (z_ref[grp, rows, :].astype(F32) * sp).astype(BF)

    @pl.when(t == 0)
    def _():
        prep(a0_ref, 0)

    @pl.when((t >= 1) & (t < ng) & ((t & 1) == 1))
    def _():
        out_proj(a0_ref)
        prep(a1_ref, t)

    @pl.when((t >= 1) & (t < ng) & ((t & 1) == 0))
    def _():
        out_proj(a1_ref)
        prep(a0_ref, t)

    @pl.when(t == ng)
    def _():
        out_proj(a1_ref if (ng - 1) % 2 == 1 else a0_ref)
        xn = x_ref[0] + mod_ref[0, 2:3, :] * o_ref[0]
        o_ref[0] = xn
        h2_ref[0] = _modnorm(xn, g2_ref[...], mod_ref[0, 3:4, :], mod_ref[0, 4:5, :]).astype(BF)


def _sgu_call(x, mod, g, g2, win_bf, gv, ws_bf, bs, wout_bf, tm):
    B, n, _ = x.shape
    ng, gd = N_SGU_GROUPS, SGU_GROUP_DIM
    row = lambda b, i, s: (b, i, 0)
    grp = lambda b, i, s: (jnp.clip(s - 2 * ng, 0, ng - 1), 0, 0)
    return pl.pallas_call(
        functools.partial(_sgu_kernel, cw=256, nc=512),
        grid=(B, n // tm, 3 * ng + 1),
        in_specs=[pl.BlockSpec((1, tm, D_MODEL), row),
                  pl.BlockSpec((1, 6, D_MODEL), lambda b, i, s: (b, 0, 0)),
                  pl.BlockSpec((1, D_MODEL), lambda b, i, s: (0, 0)),
                  pl.BlockSpec((1, D_MODEL), lambda b, i, s: (0, 0)),
                  pl.BlockSpec((D_MODEL, gd), lambda b, i, s: (0, jnp.minimum(s, 2 * ng - 1))),
                  pl.BlockSpec((1, 1, gd), grp),
                  pl.BlockSpec((1, SGU_CHUNK, SGU_CHUNK), grp),
                  pl.BlockSpec((1, SGU_CHUNK, 1), grp),
                  pl.BlockSpec((gd, D_MODEL), lambda b, i, s: (jnp.clip(s - 2 * ng - 1, 0, ng - 1), 0))],
        out_specs=[pl.BlockSpec((1, tm, D_MODEL), row),
                   pl.BlockSpec((1, tm, D_MODEL), row)],
        out_shape=[jax.ShapeDtypeStruct((B, n, D_MODEL), F32),
                   jax.ShapeDtypeStruct((B, n, D_MODEL), BF)],
        scratch_shapes=[pltpu.VMEM((tm, D_MODEL), BF),
                        pltpu.VMEM((2 * ng, tm, gd), BF),
                        pltpu.VMEM((tm, 1), F32),
                        pltpu.VMEM((tm, gd), BF),
                        pltpu.VMEM((tm, gd), BF)],
        compiler_params=_params("parallel", "parallel", "arbitrary"),
        name="sgu",
    )(x, mod, g, g2, win_bf, gv.reshape(ng, 1, gd), ws_bf, bs.reshape(ng, SGU_CHUNK, 1), wout_bf)


def _rope_tables(n):
    row, col = jnp.meshgrid(jnp.arange(n // GRID_W), jnp.arange(GRID_W), indexing="ij")
    n_freq = HEAD_DIM // 4
    inv_freq = ROPE_BASE ** (-jnp.arange(n_freq, dtype=F32) / n_freq)
    ang = jnp.concatenate([row.reshape(-1, 1).astype(F32) * inv_freq,
                           col.reshape(-1, 1).astype(F32) * inv_freq], axis=-1)
    ang = jnp.concatenate([ang, ang], axis=-1)
    sign = jnp.where(jnp.arange(HEAD_DIM) < HEAD_DIM // 2, -1.0, 1.0).astype(F32)
    return jnp.cos(ang), jnp.sin(ang) * sign


def kernel(x, c, ctx, c_ctx, w_ada, b_ada, norm_g, w_ffn_gate, w_ffn_up, w_ffn_down, w_attn_qkv,
           w_attn_o, attn_q_g, attn_k_g, attn_sink, w_fourier, w_sgu_in, sgu_v_g, w_sgu_spatial,
           b_sgu_spatial, w_sgu_out):
    B, n, _ = x.shape
    n_ctx = ctx.shape[1]
    tm_x = 512
    tm_c = B * n_ctx
    ctx = ctx.reshape(1, tm_c, D_MODEL)

    cond = jnp.zeros((SUBLANES, D_MODEL), F32).at[:B].set(c).at[B].set(c_ctx)
    ada = _ada_all(cond, w_ada, b_ada)
    cos, sin = _rope_tables(n)
    cos_c = jnp.ones((tm_c, HEAD_DIM), F32)
    sin_c = jnp.zeros((tm_c, HEAD_DIM), F32)

    for i in range(DEPTH):
        kind = i % N_MIXERS
        j = i // N_MIXERS
        need_ctx = i < DEPTH - 1
        modx = ada[i, :B].reshape(B, 6, D_MODEL)
        modc = ada[i, B].reshape(1, 6, D_MODEL)
        g1 = norm_g[i, 0].reshape(1, D_MODEL)
        g2 = norm_g[i, 1].reshape(1, D_MODEL)
        if kind == 0:
            wqkv = w_attn_qkv[j].astype(BF)
            wo = w_attn_o[j].astype(BF)
            qg = attn_q_g[j].reshape(1, HEAD_DIM)
            kg = attn_k_g[j].reshape(1, HEAD_DIM)
            sink = attn_sink[j]
            q, k, v = _qkv_call(x, modx, g1, wqkv, qg, kg, cos, sin, tm_x)
            qc, kc, vc = _qkv_call(ctx, modc, g1, wqkv, qg, kg, cos_c, sin_c, tm_c)
            kc = kc.reshape(B, n_ctx, HKV)
            vc = vc.reshape(B, n_ctx, HKV)
            ox = _attn_call(q, k, v, kc, vc, sink, 4)
            x, hx2 = _mm_res_call(ox, wo, x, modx, g2, tm_x, 256)
            if need_ctx:
                oc = _attn_ctx_call(qc.reshape(B, n_ctx, HQ), kc, vc, sink).reshape(1, tm_c, HQ)
                ctx, hc2 = _mm_res_call(oc, wo, ctx, modc, g2, tm_c, 256)
        elif kind == 1:
            wf = w_fourier[j].astype(BF)
            kron1, kron2 = _dft_tables(n)
            hx = _modnorm_call(x, modx, g1, tm_x)
            yx = _fourier_positions(hx, kron1, kron2, _channel_dft_table(n), 512)
            x, hx2 = _mm_res_call(yx, wf, x, modx, g2, tm_x, 256)
            if need_ctx:
                hc = _modnorm_call(ctx, modc, g1, tm_c).reshape(B, n_ctx, D_MODEL)
                yc = _fourier_ctx(hc, _channel_dft_table(n_ctx)).reshape(1, tm_c, D_MODEL)
                ctx, hc2 = _mm_res_call(yc, wf, ctx, modc, g2, tm_c, 256)
        else:
            win = w_sgu_in[j].astype(BF)
            wout = w_sgu_out[j].astype(BF)
            ws = w_sgu_spatial[j].astype(BF)
            args = (win, sgu_v_g[j], ws, b_sgu_spatial[j], wout)
            x, hx2 = _sgu_call(x, modx, g1, g2, *args, tm_x)
            if need_ctx:
                ctx, hc2 = _sgu_call(ctx, modc, g1, g2, *args, tm_c)
        wg = w_ffn_gate[i].astype(BF)
        wu = w_ffn_up[i].astype(BF)
        wd = w_ffn_down[i].astype(BF)
        x = _ffn_call(hx2, x, modx, wg, wu, wd, 1024, 512, 512, 512)
        if need_ctx:
            ctx = _ffn_call(hc2, ctx, modc, wg, wu, wd, tm_c, 512, tm_c, 512)
    return x
```

```python
import functools

import numpy as np
import jax
import jax.numpy as jnp
from jax import lax
from jax.experimental import pallas as pl
from jax.experimental.pallas import tpu as pltpu

D_MODEL = 2048
DEPTH = 4
GRID_W = 64
N_MIXERS = 3
EPS = 1e-6
N_HEADS = 16
N_KV_HEADS = 4
HEAD_DIM = D_MODEL // N_HEADS
KV_GROUP = N_HEADS // N_KV_HEADS
HQ = N_HEADS * HEAD_DIM
HKV = N_KV_HEADS * HEAD_DIM
WINDOW = 128
BLOCK = 128
ROPE_BASE = 10000.0
N_FOURIER_GROUPS = 8
FOURIER_GROUP_DIM = D_MODEL // N_FOURIER_GROUPS
SGU_CHUNK = 128
SGU_HALF = 3 * D_MODEL
N_SGU_GROUPS = 8
SGU_GROUP_DIM = SGU_HALF // N_SGU_GROUPS
D_FF = 5632

BF = jnp.bfloat16
F32 = jnp.float32

VMEM_LIMIT_BYTES = 56 * 1024 * 1024
FFN_VMEM_LIMIT_BYTES = 60 * 1024 * 1024
SUBLANES = 8
NEG = -1e30
LOG2E = float(np.log2(np.e))

DFT_N2 = 64


def _params(*sem):
    return pltpu.CompilerParams(dimension_semantics=sem, vmem_limit_bytes=VMEM_LIMIT_BYTES)


def _resident(shape, index_map):
    return pl.BlockSpec(shape, index_map, pipeline_mode=pl.Buffered(1))


def _modnorm(x, g, shift, scale):
    ms = jnp.mean(x * x, axis=-1, keepdims=True)
    y = x * lax.rsqrt(ms + EPS)
    return (y * g) * (1.0 + scale) + shift


def _silu(t):
    return t * jax.nn.sigmoid(t)


def _ada_kernel(cond_ref, w_ref, b_ref, o_ref):
    s = _silu(cond_ref[...]).astype(BF)
    o_ref[0] = jnp.dot(s, w_ref[0].astype(BF), preferred_element_type=F32) + b_ref[0]


def _ada_all(cond, w_ada, b_ada):
    tn = 1024
    return pl.pallas_call(
        _ada_kernel,
        grid=(DEPTH, 6 * D_MODEL // tn),
        in_specs=[pl.BlockSpec((SUBLANES, D_MODEL), lambda l, j: (0, 0)),
                  pl.BlockSpec((1, D_MODEL, tn), lambda l, j: (l, 0, j)),
                  pl.BlockSpec((1, 1, tn), lambda l, j: (l, 0, j))],
        out_specs=pl.BlockSpec((1, SUBLANES, tn), lambda l, j: (l, 0, j)),
        out_shape=jax.ShapeDtypeStruct((DEPTH, SUBLANES, 6 * D_MODEL), F32),
        compiler_params=_params("arbitrary", "arbitrary"),
        name="ada",
    )(cond, w_ada, b_ada.reshape(DEPTH, 1, 6 * D_MODEL))


def _modnorm_kernel(x_ref, mod_ref, g_ref, o_ref):
    o_ref[0] = _modnorm(x_ref[0], g_ref[...], mod_ref[0, 0:1, :], mod_ref[0, 1:2, :])


def _modnorm_call(x, mod, g, tm):
    B, n, _ = x.shape
    return pl.pallas_call(
        _modnorm_kernel,
        grid=(B, n // tm),
        in_specs=[pl.BlockSpec((1, tm, D_MODEL), lambda b, i: (b, i, 0)),
                  pl.BlockSpec((1, 6, D_MODEL), lambda b, i: (b, 0, 0)),
                  pl.BlockSpec((1, D_MODEL), lambda b, i: (0, 0))],
        out_specs=pl.BlockSpec((1, tm, D_MODEL), lambda b, i: (b, i, 0)),
        out_shape=jax.ShapeDtypeStruct((B, n, D_MODEL), F32),
        compiler_params=_params("parallel", "parallel"),
        name="modnorm",
    )(x, mod, g)


def _qkv_kernel(x_ref, mod_ref, g_ref, w_ref, qg_ref, kg_ref, cos_ref, sin_ref, q_ref, k_ref, v_ref, *, rc):
    tm = x_ref.shape[1]
    qscale = (HEAD_DIM ** -0.5) * LOG2E
    cw = 4 * HEAD_DIM

    for r in range(tm // rc):
        rows = slice(r * rc, (r + 1) * rc)
        h = _modnorm(x_ref[0, rows], g_ref[...], mod_ref[0, 0:1, :], mod_ref[0, 1:2, :]).astype(BF)
        cos = cos_ref[rows]
        sin = sin_ref[rows]

        def head(t, gain):
            ms = jnp.mean(t * t, axis=-1, keepdims=True)
            t = (t * lax.rsqrt(ms + EPS)) * gain
            return t * cos + pltpu.roll(t, HEAD_DIM // 2, 1) * sin

        t = jnp.dot(h, w_ref[:, HQ:HQ + HKV], preferred_element_type=F32)
        for j in range(N_KV_HEADS):
            tj = head(t[:, j * HEAD_DIM:(j + 1) * HEAD_DIM], kg_ref[...])
            k_ref[0, rows, j * HEAD_DIM:(j + 1) * HEAD_DIM] = tj.astype(BF)
        for c in range(HQ // cw):
            t = jnp.dot(h, w_ref[:, c * cw:(c + 1) * cw], preferred_element_type=F32)
            for j in range(4):
                tj = head(t[:, j * HEAD_DIM:(j + 1) * HEAD_DIM], qg_ref[...]) * qscale
                q_ref[0, rows, c * cw + j * HEAD_DIM:c * cw + (j + 1) * HEAD_DIM] = tj.astype(BF)
        t = jnp.dot(h, w_ref[:, HQ + HKV:], preferred_element_type=F32)
        v_ref[0, rows] = t.astype(BF)


def _qkv_call(x, mod, g, w_bf, qg, kg, cos, sin, tm, rc):
    B, n, _ = x.shape
    row = lambda b, i: (b, i, 0)
    return pl.pallas_call(
        functools.partial(_qkv_kernel, rc=rc),
        grid=(B, n // tm),
        in_specs=[pl.BlockSpec((1, tm, D_MODEL), row),
                  pl.BlockSpec((1, 6, D_MODEL), lambda b, i: (b, 0, 0)),
                  pl.BlockSpec((1, D_MODEL), lambda b, i: (0, 0)),
                  _resident((D_MODEL, HQ + 2 * HKV), lambda b, i: (0, 0)),
                  pl.BlockSpec((1, HEAD_DIM), lambda b, i: (0, 0)),
                  pl.BlockSpec((1, HEAD_DIM), lambda b, i: (0, 0)),
                  pl.BlockSpec((tm, HEAD_DIM), lambda b, i: (i, 0)),
                  pl.BlockSpec((tm, HEAD_DIM), lambda b, i: (i, 0))],
        out_specs=[pl.BlockSpec((1, tm, HQ), row),
                   pl.BlockSpec((1, tm, HKV), row),
                   pl.BlockSpec((1, tm, HKV), row)],
        out_shape=[jax.ShapeDtypeStruct((B, n, HQ), BF),
                   jax.ShapeDtypeStruct((B, n, HKV), BF),
                   jax.ShapeDtypeStruct((B, n, HKV), BF)],
        compiler_params=_params("parallel", "parallel"),
        name="qkv",
    )(x, mod, g, w_bf, qg, kg, cos, sin)


def _softmax_pv(s, sink_col, vcat):
    m = jnp.maximum(jnp.max(s, axis=-1, keepdims=True), sink_col)
    p = jnp.exp2(s - m)
    den = jnp.sum(p, axis=-1, keepdims=True) + jnp.exp2(sink_col - m)
    return jnp.dot(p.astype(BF), vcat, preferred_element_type=F32) / den


def _sink_column(sink_ref, h, rows):
    ridx = lax.broadcasted_iota(jnp.int32, (KV_GROUP * rows, 1), 0)
    col = jnp.full((KV_GROUP * rows, 1), sink_ref[h * KV_GROUP] * LOG2E, F32)
    for g in range(1, KV_GROUP):
        col = jnp.where(ridx >= g * rows, sink_ref[h * KV_GROUP + g] * LOG2E, col)
    return col


def _stack_heads(q):
    return jnp.concatenate([q[:, g * HEAD_DIM:(g + 1) * HEAD_DIM] for g in range(KV_GROUP)], axis=0)


def _attn_kernel(sink_ref, band_ref, q_ref, kp_ref, kc_ref, kn_ref, vp_ref, vc_ref, vn_ref, kx_ref, vx_ref,
                 o_ref, *, n_ctx, qblocks):
    h = pl.program_id(1)
    i = pl.program_id(2)
    last = pl.num_programs(2) - 1
    kloc = jnp.concatenate([kp_ref[0], kc_ref[0], kn_ref[0]], axis=0)
    vloc = jnp.concatenate([vp_ref[0], vc_ref[0], vn_ref[0]], axis=0)
    sink_col = _sink_column(sink_ref, h, BLOCK)
    for t in range(qblocks):
        qs = _stack_heads(q_ref[0, t * BLOCK:(t + 1) * BLOCK, :])
        kcat = jnp.concatenate([kx_ref[0], kloc[t * BLOCK:(t + 3) * BLOCK]], axis=0)
        vcat = jnp.concatenate([vx_ref[0], vloc[t * BLOCK:(t + 3) * BLOCK]], axis=0)
        s = lax.dot_general(qs, kcat, (((1,), (1,)), ((), ())), preferred_element_type=F32)
        lo = s[:, n_ctx:n_ctx + BLOCK] + band_ref[:, :BLOCK]
        hi = s[:, n_ctx + 2 * BLOCK:] + band_ref[:, BLOCK:]
        if t == 0:
            lo = lo + jnp.where(i == 0, NEG, 0.0)
        if t == qblocks - 1:
            hi = hi + jnp.where(i == last, NEG, 0.0)
        s = jnp.concatenate([s[:, :n_ctx], lo, s[:, n_ctx + BLOCK:n_ctx + 2 * BLOCK], hi], axis=1)
        o = _softmax_pv(s, sink_col, vcat)
        for g in range(KV_GROUP):
            o_ref[0, t * BLOCK:(t + 1) * BLOCK, g * HEAD_DIM:(g + 1) * HEAD_DIM] = (
                o[g * BLOCK:(g + 1) * BLOCK].astype(BF))


def _band_bias():
    row = np.arange(KV_GROUP * BLOCK)[:, None] % BLOCK
    col = np.arange(BLOCK)[None, :]
    prev_ok = (row - (col - BLOCK)) <= WINDOW
    next_ok = ((col + BLOCK) - row) <= WINDOW
    return jnp.asarray(np.where(np.concatenate([prev_ok, next_ok], axis=1), 0.0, NEG), F32)


def _attn_call(q, k, v, kx, vx, sink, qblocks):
    B, n, _ = q.shape
    n_ctx = kx.shape[1]
    nb = n // BLOCK
    tq = qblocks * BLOCK
    gw = KV_GROUP * HEAD_DIM
    prev = lambda b, h, i: (b, jnp.maximum(i * qblocks - 1, 0), h)
    cur = lambda b, h, i: (b, i, h)
    nxt = lambda b, h, i: (b, jnp.minimum((i + 1) * qblocks, nb - 1), h)
    ctx = lambda b, h, i: (b, 0, h)
    edge = lambda m: pl.BlockSpec((1, BLOCK, HEAD_DIM), m)
    mid = pl.BlockSpec((1, tq, HEAD_DIM), cur)
    return pl.pallas_call(
        functools.partial(_attn_kernel, n_ctx=n_ctx, qblocks=qblocks),
        grid=(B, N_KV_HEADS, n // tq),
        in_specs=[pl.BlockSpec(memory_space=pltpu.SMEM),
                  pl.BlockSpec((KV_GROUP * BLOCK, 2 * BLOCK), lambda b, h, i: (0, 0)),
                  pl.BlockSpec((1, tq, gw), cur),
                  edge(prev), mid, edge(nxt), edge(prev), mid, edge(nxt),
                  pl.BlockSpec((1, n_ctx, HEAD_DIM), ctx),
                  pl.BlockSpec((1, n_ctx, HEAD_DIM), ctx)],
        out_specs=pl.BlockSpec((1, tq, gw), cur),
        out_shape=jax.ShapeDtypeStruct((B, n, HQ), BF),
        compiler_params=_params("parallel", "parallel", "arbitrary"),
        name="attn",
    )(sink, _band_bias(), q, k, k, k, v, v, v, kx, vx)


def _attn_ctx_kernel(sink_ref, q_ref, k_ref, v_ref, o_ref, *, n_ctx):
    h = pl.program_id(1)
    qs = _stack_heads(q_ref[0])
    s = lax.dot_general(qs, k_ref[0], (((1,), (1,)), ((), ())), preferred_element_type=F32)
    o = _softmax_pv(s, _sink_column(sink_ref, h, n_ctx), v_ref[0])
    for g in range(KV_GROUP):
        o_ref[0, :, g * HEAD_DIM:(g + 1) * HEAD_DIM] = o[g * n_ctx:(g + 1) * n_ctx].astype(BF)


def _attn_ctx_call(q, k, v, sink):
    B, n_ctx, _ = q.shape
    gw = KV_GROUP * HEAD_DIM
    idx = lambda b, h: (b, 0, h)
    return pl.pallas_call(
        functools.partial(_attn_ctx_kernel, n_ctx=n_ctx),
        grid=(B, N_KV_HEADS),
        in_specs=[pl.BlockSpec(memory_space=pltpu.SMEM),
                  pl.BlockSpec((1, n_ctx, gw), idx),
                  pl.BlockSpec((1, n_ctx, HEAD_DIM), idx),
                  pl.BlockSpec((1, n_ctx, HEAD_DIM), idx)],
        out_specs=pl.BlockSpec((1, n_ctx, gw), idx),
        out_shape=jax.ShapeDtypeStruct((B, n_ctx, HQ), BF),
        compiler_params=_params("parallel", "parallel"),
        name="attn_ctx",
    )(sink, q, k, v)


def _mm_res_kernel(a_ref, w_ref, x_ref, mod_ref, g2_ref, o_ref, h_ref, *, rc):
    tm = x_ref.shape[1]
    for r in range(tm // rc):
        rows = slice(r * rc, (r + 1) * rc)
        p = jnp.dot(a_ref[0, rows].astype(BF), w_ref[...], preferred_element_type=F32)
        xn = x_ref[0, rows] + mod_ref[0, 2:3, :] * p
        o_ref[0, rows] = xn
        h_ref[0, rows] = _modnorm(xn, g2_ref[...], mod_ref[0, 3:4, :], mod_ref[0, 4:5, :]).astype(BF)


def _mm_res_call(a, w_bf, x, mod, g2, tm, rc):
    B, n, K = a.shape
    row = lambda b, i: (b, i, 0)
    return pl.pallas_call(
        functools.partial(_mm_res_kernel, rc=rc),
        grid=(B, n // tm),
        in_specs=[pl.BlockSpec((1, tm, K), row),
                  _resident((K, D_MODEL), lambda b, i: (0, 0)),
                  pl.BlockSpec((1, tm, D_MODEL), row),
                  pl.BlockSpec((1, 6, D_MODEL), lambda b, i: (b, 0, 0)),
                  pl.BlockSpec((1, D_MODEL), lambda b, i: (0, 0))],
        out_specs=[pl.BlockSpec((1, tm, D_MODEL), row),
                   pl.BlockSpec((1, tm, D_MODEL), row)],
        out_shape=[jax.ShapeDtypeStruct((B, n, D_MODEL), F32),
                   jax.ShapeDtypeStruct((B, n, D_MODEL), BF)],
        compiler_params=_params("parallel", "parallel"),
        name="mm_res",
    )(a, w_bf, x, mod, g2)


def _ffn_accumulate(h_ref, wg, wu, wd, o_ref, rc, nc):
    tm = h_ref.shape[1]
    for r in range(tm // rc):
        rows = slice(r * rc, (r + 1) * rc)
        h = h_ref[0, rows]
        gate = jnp.dot(h, wg[...], preferred_element_type=F32)
        up = jnp.dot(h, wu[...], preferred_element_type=F32)
        a = (_silu(gate) * up).astype(BF)
        for c in range(D_MODEL // nc):
            cols = slice(c * nc, (c + 1) * nc)
            o_ref[0, rows, cols] += jnp.dot(a, wd[:, cols], preferred_element_type=F32)


def _ffn_kernel(h_ref, x_ref, mod_ref, wg_ref, wu_ref, wd_ref, o_ref, *, rc, nc):
    f = pl.program_id(2)

    @pl.when(f == 0)
    def _():
        o_ref[0] = jnp.zeros(o_ref.shape[1:], F32)

    _ffn_accumulate(h_ref, wg_ref, wu_ref, wd_ref, o_ref, rc, nc)

    @pl.when(f == pl.num_programs(2) - 1)
    def _():
        o_ref[0] = x_ref[0] + mod_ref[0, 5:6, :] * o_ref[0]


def _ffn_cast_kernel(h_ref, x_ref, mod_ref, wg_ref, wu_ref, wd_ref, o_ref, wgb_ref, wub_ref, wdb_ref, *, rc, nc):
    f = pl.program_id(2)

    @pl.when(f == 0)
    def _():
        o_ref[0] = jnp.zeros(o_ref.shape[1:], F32)

    wg = wg_ref[0].astype(BF)
    wu = wu_ref[0].astype(BF)
    wd = wd_ref[0].astype(BF)
    wgb_ref[...] = wg
    wub_ref[...] = wu
    wdb_ref[...] = wd
    _ffn_accumulate(h_ref, wg, wu, wd, o_ref, rc, nc)

    @pl.when(f == pl.num_programs(2) - 1)
    def _():
        o_ref[0] = x_ref[0] + mod_ref[0, 5:6, :] * o_ref[0]


def _ffn_specs(tm, tf):
    row = lambda b, i, f: (b, i, 0)
    return [pl.BlockSpec((1, tm, D_MODEL), row),
            pl.BlockSpec((1, tm, D_MODEL), row),
            pl.BlockSpec((1, 6, D_MODEL), lambda b, i, f: (b, 0, 0))], pl.BlockSpec((1, tm, D_MODEL), row)


def _ffn_params():
    return pltpu.CompilerParams(dimension_semantics=("parallel", "parallel", "arbitrary"),
                                vmem_limit_bytes=FFN_VMEM_LIMIT_BYTES)


def _ffn_call(h, x, mod, wg_bf, wu_bf, wd_bf, tm, tf, rc, nc):
    B, n, _ = x.shape
    act_specs, out_spec = _ffn_specs(tm, tf)
    return pl.pallas_call(
        functools.partial(_ffn_kernel, rc=rc, nc=nc),
        grid=(B, n // tm, D_FF // tf),
        in_specs=act_specs + [pl.BlockSpec((D_MODEL, tf), lambda b, i, f: (0, f)),
                              pl.BlockSpec((D_MODEL, tf), lambda b, i, f: (0, f)),
                              pl.BlockSpec((tf, D_MODEL), lambda b, i, f: (f, 0))],
        out_specs=out_spec,
        out_shape=jax.ShapeDtypeStruct((B, n, D_MODEL), F32),
        compiler_params=_ffn_params(),
        name="ffn",
    )(h, x, mod, wg_bf, wu_bf, wd_bf)


def _ffn_cast_call(h, x, mod, wg_all, wu_all, wd_all, layer, tm, tf, rc, nc):
    B, n, _ = x.shape
    assert B == 1 and n == tm
    act_specs, out_spec = _ffn_specs(tm, tf)
    up_in = pl.BlockSpec((1, D_MODEL, tf), lambda b, i, f: (layer, 0, f))
    down_in = pl.BlockSpec((1, tf, D_MODEL), lambda b, i, f: (layer, f, 0))
    up_out = pl.BlockSpec((D_MODEL, tf), lambda b, i, f: (0, f))
    down_out = pl.BlockSpec((tf, D_MODEL), lambda b, i, f: (f, 0))
    return pl.pallas_call(
        functools.partial(_ffn_cast_kernel, rc=rc, nc=nc),
        grid=(1, 1, D_FF // tf),
        in_specs=act_specs + [up_in, up_in, down_in],
        out_specs=[out_spec, up_out, up_out, down_out],
        out_shape=[jax.ShapeDtypeStruct((1, n, D_MODEL), F32),
                   jax.ShapeDtypeStruct((D_MODEL, D_FF), BF),
                   jax.ShapeDtypeStruct((D_MODEL, D_FF), BF),
                   jax.ShapeDtypeStruct((D_FF, D_MODEL), BF)],
        compiler_params=_ffn_params(),
        name="ffn_cast",
    )(h, x, mod, wg_all, wu_all, wd_all)


def _dft_tables(n):
    n1, n2, s = n // DFT_N2, DFT_N2, SUBLANES
    i32 = jnp.int32
    j = jnp.arange(n2 // s, dtype=i32)[:, None, None]
    r = jnp.arange(2 * n1, dtype=i32)[None, :, None]
    c = jnp.arange(n1 * s, dtype=i32)[None, None, :]
    part, k1, nn1, l = r // n1, r % n1, c // s, c % s
    idx = (k1 * (s * j + l) + n2 * k1 * nn1 + part * (n // 4)) % n
    m1 = jnp.cos(idx.astype(F32) * (2.0 * np.pi / n))
    same_l = (jnp.arange(s, dtype=i32)[:, None] == (jnp.arange(n1 * s, dtype=i32)[None, :] % s)).astype(F32)
    kron1 = (m1[:, :, None, :] * same_l[None, None, :, :]).reshape(n2 // s, 2 * n1 * s, n1 * s)
    r = jnp.arange(2 * n2, dtype=i32)[:, None]
    c = jnp.arange(2 * s * n2, dtype=i32)[None, :]
    qpart, k2, part, nn2 = r // n2, r % n2, c // (s * n2), c % n2
    idx2 = (k2 * nn2 + (n2 // 4) * (qpart - part) + n2) % n2
    m2 = jnp.cos(idx2.astype(F32) * (2.0 * np.pi / n2))
    same_l2 = (jnp.arange(s, dtype=i32)[:, None]
               == ((jnp.arange(2 * s * n2, dtype=i32)[None, :] // n2) % s)).astype(F32)
    kron2 = (m2[:, None, :] * same_l2[None, :, :]).reshape(2 * n2 * s, 2 * s * n2)
    return kron1.astype(BF), kron2.astype(BF)


def _channel_dft_table(n):
    c = np.arange(FOURIER_GROUP_DIM)
    ang = 2.0 * np.pi * ((c[:, None] * c[None, :]) % FOURIER_GROUP_DIM) / FOURIER_GROUP_DIM
    scale = 1.0 / np.sqrt(float(n) * FOURIER_GROUP_DIM)
    return jnp.asarray(np.concatenate([np.cos(ang), np.sin(ang)], axis=0) * scale, BF)


def _channel_mix(zr, zi, cs):
    outs = []
    for g in range(zr.shape[1] // FOURIER_GROUP_DIM):
        sl = slice(g * FOURIER_GROUP_DIM, (g + 1) * FOURIER_GROUP_DIM)
        zg = jnp.concatenate([zr[:, sl], zi[:, sl]], axis=1).astype(BF)
        outs.append(jnp.dot(zg, cs, preferred_element_type=F32))
    return jnp.concatenate(outs, axis=1)


def _dft1_kernel(x_ref, k_ref, t_ref):
    n1, s, tc = x_ref.shape[1], x_ref.shape[2], x_ref.shape[3]
    xb = x_ref[0].reshape(n1 * s, tc).astype(BF)
    t = jnp.dot(k_ref[0], xb, preferred_element_type=F32)
    t_ref[0] = t.reshape(2, n1, s, tc)


def _dft2_kernel(t_ref, k_ref, cs_ref, y_ref):
    s, tc = y_ref.shape[2], y_ref.shape[3]
    rows = s * DFT_N2
    tb = t_ref[0, :, 0].reshape(2 * rows, tc).astype(BF)
    z = jnp.dot(k_ref[...], tb, preferred_element_type=F32)
    y = _channel_mix(z[:rows], z[rows:], cs_ref[...])
    y_ref[0] = y.reshape(DFT_N2, s, tc)


def _fourier_positions(hx, kron1, kron2, cs, tc):
    B, n, _ = hx.shape
    n1, s = n // DFT_N2, SUBLANES
    nj = DFT_N2 // s
    t = pl.pallas_call(
        _dft1_kernel,
        grid=(nj, B, D_MODEL // tc),
        in_specs=[pl.BlockSpec((1, n1, s, tc), lambda j, b, c: (b, 0, j, c)),
                  pl.BlockSpec((1, 2 * n1 * s, n1 * s), lambda j, b, c: (j, 0, 0))],
        out_specs=pl.BlockSpec((1, 2, n1, s, tc), lambda j, b, c: (b, 0, 0, j, c)),
        out_shape=jax.ShapeDtypeStruct((B, 2, n1, DFT_N2, D_MODEL), F32),
        compiler_params=_params("arbitrary", "arbitrary", "arbitrary"),
        name="dft1",
    )(hx.reshape(B, n1, DFT_N2, D_MODEL), kron1)
    na = n1 // s
    y = pl.pallas_call(
        _dft2_kernel,
        grid=(B, na, D_MODEL // tc),
        in_specs=[pl.BlockSpec((1, 2, 1, s * DFT_N2, tc), lambda b, a, c: (b, 0, a, 0, c)),
                  _resident((2 * DFT_N2 * s, 2 * s * DFT_N2), lambda b, a, c: (0, 0)),
                  _resident((2 * FOURIER_GROUP_DIM, FOURIER_GROUP_DIM), lambda b, a, c: (0, 0))],
        out_specs=pl.BlockSpec((1, DFT_N2, s, tc), lambda b, a, c: (b, 0, a, c)),
        out_shape=jax.ShapeDtypeStruct((B, DFT_N2, n1, D_MODEL), F32),
        compiler_params=_params("parallel", "parallel", "parallel"),
        name="dft2",
    )(t.reshape(B, 2, na, s * DFT_N2, D_MODEL), kron2, cs)
    return y.reshape(B, n, D_MODEL)


def _dft_ctx_kernel(x_ref, f_ref, cs_ref, y_ref):
    n_ctx = x_ref.shape[1]
    z = jnp.dot(f_ref[...], x_ref[0].astype(BF), preferred_element_type=F32)
    y_ref[0] = _channel_mix(z[:n_ctx], z[n_ctx:], cs_ref[...])


def _fourier_ctx(hc, cs):
    B, n_ctx, _ = hc.shape
    p = np.arange(n_ctx)
    ang = 2.0 * np.pi * ((p[:, None] * p[None, :]) % n_ctx) / n_ctx
    fmat = jnp.asarray(np.concatenate([np.cos(ang), -np.sin(ang)], axis=0), BF)
    return pl.pallas_call(
        _dft_ctx_kernel,
        grid=(B,),
        in_specs=[pl.BlockSpec((1, n_ctx, D_MODEL), lambda b: (b, 0, 0)),
                  pl.BlockSpec((2 * n_ctx, n_ctx), lambda b: (0, 0)),
                  pl.BlockSpec((2 * FOURIER_GROUP_DIM, FOURIER_GROUP_DIM), lambda b: (0, 0))],
        out_specs=pl.BlockSpec((1, n_ctx, D_MODEL), lambda b: (b, 0, 0)),
        out_shape=jax.ShapeDtypeStruct((B, n_ctx, D_MODEL), F32),
        compiler_params=_params("parallel"),
        name="dft_ctx",
    )(hc, fmat, cs)


def _sgu_kernel(x_ref, mod_ref, g_ref, g2_ref, win_ref, gv_ref, ws_ref, bs_ref, wout_ref, o_ref, h2_ref,
                h_ref, z_ref, ssq_ref, a0_ref, a1_ref, *, cw, nc):
    s = pl.program_id(2)
    ng = N_SGU_GROUPS
    gd = SGU_GROUP_DIM
    tm = h_ref.shape[0]
    t = s - 2 * ng

    @pl.when(s == 0)
    def _():
        h_ref[...] = _modnorm(x_ref[0], g_ref[...], mod_ref[0, 0:1, :], mod_ref[0, 1:2, :]).astype(BF)
        ssq_ref[...] = jnp.zeros_like(ssq_ref)
        o_ref[0] = jnp.zeros(o_ref.shape[1:], F32)

    def in_proj(with_ssq):
        sq = None
        for c in range(gd // cw):
            cols = slice(c * cw, (c + 1) * cw)
            z = jnp.dot(h_ref[...], win_ref[:, cols], preferred_element_type=F32)
            z = 0.5 * z * (1.0 + lax.erf(z * (2.0 ** -0.5)))
            z_ref[s, :, cols] = z.astype(BF)
            if with_ssq:
                part = jnp.sum(z * z, axis=-1, keepdims=True)
                sq = part if sq is None else sq + part
        if with_ssq:
            ssq_ref[...] += sq

    @pl.when(s < ng)
    def _():
        in_proj(False)

    @pl.when((s >= ng) & (s < 2 * ng))
    def _():
        in_proj(True)

    def out_proj(a_ref):
        a = a_ref[...]
        for c in range(D_MODEL // nc):
            cols = slice(c * nc, (c + 1) * nc)
            o_ref[0, :, cols] += jnp.dot(a, wout_ref[:, cols], preferred_element_type=F32)

    def prep(a_ref, grp):
        r = lax.rsqrt(ssq_ref[...] * (1.0 / SGU_HALF) + EPS)
        for c in range(tm // SGU_CHUNK):
            rows = slice(c * SGU_CHUNK, (c + 1) * SGU_CHUNK)
            vn = ((z_ref[grp + ng, rows, :].astype(F32) * r[rows]) * gv_ref[0]).astype(BF)
            sp = jnp.dot(ws_ref[0], vn, preferred_element_type=F32) + bs_ref[0]
            a_ref[rows, :] = (z_ref[grp, rows, :].astype(F32) * sp).astype(BF)

    @pl.when(t == 0)
    def _():
        prep(a0_ref, 0)

    @pl.when((t >= 1) & (t < ng) & ((t & 1) == 1))
    def _():
        out_proj(a0_ref)
        prep(a1_ref, t)

    @pl.when((t >= 1) & (t < ng) & ((t & 1) == 0))
    def _():
        out_proj(a1_ref)
        prep(a0_ref, t)

    @pl.when(t == ng)
    def _():
        out_proj(a1_ref if (ng - 1) % 2 == 1 else a0_ref)
        xn = x_ref[0] + mod_ref[0, 2:3, :] * o_ref[0]
        o_ref[0] = xn
        h2_ref[0] = _modnorm(xn, g2_ref[...], mod_ref[0, 3:4, :], mod_ref[0, 4:5, :]).astype(BF)


def _sgu_call(x, mod, g, g2, win_bf, gv, ws_bf, bs, wout_bf, tm):
    B, n, _ = x.shape
    ng, gd = N_SGU_GROUPS, SGU_GROUP_DIM
    row = lambda b, i, s: (b, i, 0)
    grp = lambda b, i, s: (jnp.clip(s - 2 * ng, 0, ng - 1), 0, 0)
    return pl.pallas_call(
        functools.partial(_sgu_kernel, cw=256, nc=512),
        grid=(B, n // tm, 3 * ng + 1),
        in_specs=[pl.BlockSpec((1, tm, D_MODEL), row),
                  pl.BlockSpec((1, 6, D_MODEL), lambda b, i, s: (b, 0, 0)),
                  pl.BlockSpec((1, D_MODEL), lambda b, i, s: (0, 0)),
                  pl.BlockSpec((1, D_MODEL), lambda b, i, s: (0, 0)),
                  pl.BlockSpec((D_MODEL, gd), lambda b, i, s: (0, jnp.minimum(s, 2 * ng - 1))),
                  pl.BlockSpec((1, 1, gd), grp),
                  pl.BlockSpec((1, SGU_CHUNK, SGU_CHUNK), grp),
                  pl.BlockSpec((1, SGU_CHUNK, 1), grp),
                  pl.BlockSpec((gd, D_MODEL), lambda b, i, s: (jnp.clip(s - 2 * ng - 1, 0, ng - 1), 0))],
        out_specs=[pl.BlockSpec((1, tm, D_MODEL), row),
                   pl.BlockSpec((1, tm, D_MODEL), row)],
        out_shape=[jax.ShapeDtypeStruct((B, n, D_MODEL), F32),
                   jax.ShapeDtypeStruct((B, n, D_MODEL), BF)],
        scratch_shapes=[pltpu.VMEM((tm, D_MODEL), BF),
                        pltpu.VMEM((2 * ng, tm, gd), BF),
                        pltpu.VMEM((tm, 1), F32),
                        pltpu.VMEM((tm, gd), BF),
                        pltpu.VMEM((tm, gd), BF)],
        compiler_params=_params("parallel", "parallel", "arbitrary"),
        name="sgu",
    )(x, mod, g, g2, win_bf, gv.reshape(ng, 1, gd), ws_bf, bs.reshape(ng, SGU_CHUNK, 1), wout_bf)


def _rope_tables(n):
    row, col = jnp.meshgrid(jnp.arange(n // GRID_W), jnp.arange(GRID_W), indexing="ij")
    n_freq = HEAD_DIM // 4
    inv_freq = ROPE_BASE ** (-jnp.arange(n_freq, dtype=F32) / n_freq)
    ang = jnp.concatenate([row.reshape(-1, 1).astype(F32) * inv_freq,
                           col.reshape(-1, 1).astype(F32) * inv_freq], axis=-1)
    ang = jnp.concatenate([ang, ang], axis=-1)
    sign = jnp.where(jnp.arange(HEAD_DIM) < HEAD_DIM // 2, -1.0, 1.0).astype(F32)
    return jnp.cos(ang), jnp.sin(ang) * sign


def kernel(x, c, ctx, c_ctx, w_ada, b_ada, norm_g, w_ffn_gate, w_ffn_up, w_ffn_down, w_attn_qkv,
           w_attn_o, attn_q_g, attn_k_g, attn_sink, w_fourier, w_sgu_in, sgu_v_g, w_sgu_spatial,
           b_sgu_spatial, w_sgu_out):
    B, n, _ = x.shape
    n_ctx = ctx.shape[1]
    tm_x = 512
    tm_c = B * n_ctx
    ctx = ctx.reshape(1, tm_c, D_MODEL)

    cond = jnp.zeros((SUBLANES, D_MODEL), F32).at[:B].set(c).at[B].set(c_ctx)
    ada = _ada_all(cond, w_ada, b_ada)
    cos, sin = _rope_tables(n)
    cos_c = jnp.ones((tm_c, HEAD_DIM), F32)
    sin_c = jnp.zeros((tm_c, HEAD_DIM), F32)

    for i in range(DEPTH):
        kind = i % N_MIXERS
        j = i // N_MIXERS
        need_ctx = i < DEPTH - 1
        modx = ada[i, :B].reshape(B, 6, D_MODEL)
        modc = ada[i, B].reshape(1, 6, D_MODEL)
        g1 = norm_g[i, 0].reshape(1, D_MODEL)
        g2 = norm_g[i, 1].reshape(1, D_MODEL)
        if kind == 0:
            wqkv = w_attn_qkv[j].astype(BF)
            wo = w_attn_o[j].astype(BF)
            qg = attn_q_g[j].reshape(1, HEAD_DIM)
            kg = attn_k_g[j].reshape(1, HEAD_DIM)
            sink = attn_sink[j]
            q, k, v = _qkv_call(x, modx, g1, wqkv, qg, kg, cos, sin, 1024, 256)
            qc, kc, vc = _qkv_call(ctx, modc, g1, wqkv, qg, kg, cos_c, sin_c, tm_c, 256)
            kc = kc.reshape(B, n_ctx, HKV)
            vc = vc.reshape(B, n_ctx, HKV)
            ox = _attn_call(q, k, v, kc, vc, sink, 4)
            x, hx2 = _mm_res_call(ox, wo, x, modx, g2, tm_x, 256)
            if need_ctx:
                oc = _attn_ctx_call(qc.reshape(B, n_ctx, HQ), kc, vc, sink).reshape(1, tm_c, HQ)
                ctx, hc2 = _mm_res_call(oc, wo, ctx, modc, g2, tm_c, 256)
        elif kind == 1:
            wf = w_fourier[j].astype(BF)
            kron1, kron2 = _dft_tables(n)
            hx = _modnorm_call(x, modx, g1, tm_x)
            yx = _fourier_positions(hx, kron1, kron2, _channel_dft_table(n), 512)
            x, hx2 = _mm_res_call(yx, wf, x, modx, g2, tm_x, 256)
            if need_ctx:
                hc = _modnorm_call(ctx, modc, g1, tm_c).reshape(B, n_ctx, D_MODEL)
                yc = _fourier_ctx(hc, _channel_dft_table(n_ctx)).reshape(1, tm_c, D_MODEL)
                ctx, hc2 = _mm_res_call(yc, wf, ctx, modc, g2, tm_c, 256)
        else:
            win = w_sgu_in[j].astype(BF)
            wout = w_sgu_out[j].astype(BF)
            ws = w_sgu_spatial[j].astype(BF)
            args = (win, sgu_v_g[j], ws, b_sgu_spatial[j], wout)
            x, hx2 = _sgu_call(x, modx, g1, g2, *args, tm_x)
            if need_ctx:
                ctx, hc2 = _sgu_call(ctx, modc, g1, g2, *args, tm_c)
        if need_ctx:
            ctx, wg, wu, wd = _ffn_cast_call(hc2, ctx, modc, w_ffn_gate, w_ffn_up, w_ffn_down, i,
                                             tm_c, 256, tm_c, 512)
        else:
            wg = w_ffn_gate[i].astype(BF)
            wu = w_ffn_up[i].astype(BF)
            wd = w_ffn_down[i].astype(BF)
        x = _ffn_call(hx2, x, modx, wg, wu, wd, 1024, 512, 512, 512)
    return x
```

```python
import functools

import numpy as np
import jax
import jax.numpy as jnp
from jax import lax
from jax.experimental import pallas as pl
from jax.experimental.pallas import tpu as pltpu

D_MODEL = 2048
DEPTH = 4
GRID_W = 64
N_MIXERS = 3
EPS = 1e-6
N_HEADS = 16
N_KV_HEADS = 4
HEAD_DIM = D_MODEL // N_HEADS
KV_GROUP = N_HEADS // N_KV_HEADS
HQ = N_HEADS * HEAD_DIM
HKV = N_KV_HEADS * HEAD_DIM
WINDOW = 128
BLOCK = 128
ROPE_BASE = 10000.0
N_FOURIER_GROUPS = 8
FOURIER_GROUP_DIM = D_MODEL // N_FOURIER_GROUPS
SGU_CHUNK = 128
SGU_HALF = 3 * D_MODEL
N_SGU_GROUPS = 8
SGU_GROUP_DIM = SGU_HALF // N_SGU_GROUPS
D_FF = 5632

BF = jnp.bfloat16
F32 = jnp.float32

VMEM_LIMIT_BYTES = 56 * 1024 * 1024
BIG_VMEM_LIMIT_BYTES = 60 * 1024 * 1024
SUBLANES = 8
NEG = -1e30
LOG2E = float(np.log2(np.e))

DFT_N2 = 64


def _params(*sem):
    return pltpu.CompilerParams(dimension_semantics=sem, vmem_limit_bytes=VMEM_LIMIT_BYTES)


def _resident(shape, index_map):
    return pl.BlockSpec(shape, index_map, pipeline_mode=pl.Buffered(1))


def _modnorm(x, g, shift, scale):
    ms = jnp.mean(x * x, axis=-1, keepdims=True)
    y = x * lax.rsqrt(ms + EPS)
    return (y * g) * (1.0 + scale) + shift


def _silu(t):
    return t * jax.nn.sigmoid(t)


def _ada_kernel(cond_ref, w_ref, b_ref, o_ref):
    s = _silu(cond_ref[...]).astype(BF)
    o_ref[0] = jnp.dot(s, w_ref[0].astype(BF), preferred_element_type=F32) + b_ref[0]


def _ada_all(cond, w_ada, b_ada):
    tn = 1024
    return pl.pallas_call(
        _ada_kernel,
        grid=(DEPTH, 6 * D_MODEL // tn),
        in_specs=[pl.BlockSpec((SUBLANES, D_MODEL), lambda l, j: (0, 0)),
                  pl.BlockSpec((1, D_MODEL, tn), lambda l, j: (l, 0, j)),
                  pl.BlockSpec((1, 1, tn), lambda l, j: (l, 0, j))],
        out_specs=pl.BlockSpec((1, SUBLANES, tn), lambda l, j: (l, 0, j)),
        out_shape=jax.ShapeDtypeStruct((DEPTH, SUBLANES, 6 * D_MODEL), F32),
        compiler_params=_params("arbitrary", "arbitrary"),
        name="ada",
    )(cond, w_ada, b_ada.reshape(DEPTH, 1, 6 * D_MODEL))


def _modnorm_kernel(x_ref, mod_ref, g_ref, o_ref):
    o_ref[0] = _modnorm(x_ref[0], g_ref[...], mod_ref[0, 0:1, :], mod_ref[0, 1:2, :])


def _modnorm_call(x, mod, g, tm):
    B, n, _ = x.shape
    return pl.pallas_call(
        _modnorm_kernel,
        grid=(B, n // tm),
        in_specs=[pl.BlockSpec((1, tm, D_MODEL), lambda b, i: (b, i, 0)),
                  pl.BlockSpec((1, 6, D_MODEL), lambda b, i: (b, 0, 0)),
                  pl.BlockSpec((1, D_MODEL), lambda b, i: (0, 0))],
        out_specs=pl.BlockSpec((1, tm, D_MODEL), lambda b, i: (b, i, 0)),
        out_shape=jax.ShapeDtypeStruct((B, n, D_MODEL), F32),
        compiler_params=_params("parallel", "parallel"),
        name="modnorm",
    )(x, mod, g)


def _qkv_kernel(x_ref, mod_ref, g_ref, w_ref, qg_ref, kg_ref, cos_ref, sin_ref, q_ref, k_ref, v_ref, *, rc):
    tm = x_ref.shape[1]
    qscale = (HEAD_DIM ** -0.5) * LOG2E
    cw = 4 * HEAD_DIM

    for r in range(tm // rc):
        rows = slice(r * rc, (r + 1) * rc)
        h = _modnorm(x_ref[0, rows], g_ref[...], mod_ref[0, 0:1, :], mod_ref[0, 1:2, :]).astype(BF)
        cos = cos_ref[rows]
        sin = sin_ref[rows]

        def head(t, gain):
            ms = jnp.mean(t * t, axis=-1, keepdims=True)
            t = (t * lax.rsqrt(ms + EPS)) * gain
            return t * cos + pltpu.roll(t, HEAD_DIM // 2, 1) * sin

        t = jnp.dot(h, w_ref[:, HQ:HQ + HKV], preferred_element_type=F32)
        for j in range(N_KV_HEADS):
            tj = head(t[:, j * HEAD_DIM:(j + 1) * HEAD_DIM], kg_ref[...])
            k_ref[0, rows, j * HEAD_DIM:(j + 1) * HEAD_DIM] = tj.astype(BF)
        for c in range(HQ // cw):
            t = jnp.dot(h, w_ref[:, c * cw:(c + 1) * cw], preferred_element_type=F32)
            for j in range(4):
                tj = head(t[:, j * HEAD_DIM:(j + 1) * HEAD_DIM], qg_ref[...]) * qscale
                q_ref[0, rows, c * cw + j * HEAD_DIM:c * cw + (j + 1) * HEAD_DIM] = tj.astype(BF)
        t = jnp.dot(h, w_ref[:, HQ + HKV:], preferred_element_type=F32)
        v_ref[0, rows] = t.astype(BF)


def _qkv_call(x, mod, g, w_bf, qg, kg, cos, sin, tm, rc):
    B, n, _ = x.shape
    row = lambda b, i: (b, i, 0)
    return pl.pallas_call(
        functools.partial(_qkv_kernel, rc=rc),
        grid=(B, n // tm),
        in_specs=[pl.BlockSpec((1, tm, D_MODEL), row),
                  pl.BlockSpec((1, 6, D_MODEL), lambda b, i: (b, 0, 0)),
                  pl.BlockSpec((1, D_MODEL), lambda b, i: (0, 0)),
                  _resident((D_MODEL, HQ + 2 * HKV), lambda b, i: (0, 0)),
                  pl.BlockSpec((1, HEAD_DIM), lambda b, i: (0, 0)),
                  pl.BlockSpec((1, HEAD_DIM), lambda b, i: (0, 0)),
                  pl.BlockSpec((tm, HEAD_DIM), lambda b, i: (i, 0)),
                  pl.BlockSpec((tm, HEAD_DIM), lambda b, i: (i, 0))],
        out_specs=[pl.BlockSpec((1, tm, HQ), row),
                   pl.BlockSpec((1, tm, HKV), row),
                   pl.BlockSpec((1, tm, HKV), row)],
        out_shape=[jax.ShapeDtypeStruct((B, n, HQ), BF),
                   jax.ShapeDtypeStruct((B, n, HKV), BF),
                   jax.ShapeDtypeStruct((B, n, HKV), BF)],
        compiler_params=_params("parallel", "parallel"),
        name="qkv",
    )(x, mod, g, w_bf, qg, kg, cos, sin)


def _softmax_pv(s, sink_col, vcat):
    m = jnp.maximum(jnp.max(s, axis=-1, keepdims=True), sink_col)
    p = jnp.exp2(s - m)
    den = jnp.sum(p, axis=-1, keepdims=True) + jnp.exp2(sink_col - m)
    return jnp.dot(p.astype(BF), vcat, preferred_element_type=F32) / den


def _sink_column(sink_ref, h, rows):
    ridx = lax.broadcasted_iota(jnp.int32, (KV_GROUP * rows, 1), 0)
    col = jnp.full((KV_GROUP * rows, 1), sink_ref[h * KV_GROUP] * LOG2E, F32)
    for g in range(1, KV_GROUP):
        col = jnp.where(ridx >= g * rows, sink_ref[h * KV_GROUP + g] * LOG2E, col)
    return col


def _stack_heads(q):
    return jnp.concatenate([q[:, g * HEAD_DIM:(g + 1) * HEAD_DIM] for g in range(KV_GROUP)], axis=0)


def _attn_kernel(sink_ref, band_ref, q_ref, kp_ref, kc_ref, kn_ref, vp_ref, vc_ref, vn_ref, kx_ref, vx_ref,
                 o_ref, *, n_ctx, qblocks):
    h = pl.program_id(1)
    i = pl.program_id(2)
    last = pl.num_programs(2) - 1
    kloc = jnp.concatenate([kp_ref[0], kc_ref[0], kn_ref[0]], axis=0)
    vloc = jnp.concatenate([vp_ref[0], vc_ref[0], vn_ref[0]], axis=0)
    sink_col = _sink_column(sink_ref, h, BLOCK)
    for t in range(qblocks):
        qs = _stack_heads(q_ref[0, t * BLOCK:(t + 1) * BLOCK, :])
        kcat = jnp.concatenate([kx_ref[0], kloc[t * BLOCK:(t + 3) * BLOCK]], axis=0)
        vcat = jnp.concatenate([vx_ref[0], vloc[t * BLOCK:(t + 3) * BLOCK]], axis=0)
        s = lax.dot_general(qs, kcat, (((1,), (1,)), ((), ())), preferred_element_type=F32)
        lo = s[:, n_ctx:n_ctx + BLOCK] + band_ref[:, :BLOCK]
        hi = s[:, n_ctx + 2 * BLOCK:] + band_ref[:, BLOCK:]
        if t == 0:
            lo = lo + jnp.where(i == 0, NEG, 0.0)
        if t == qblocks - 1:
            hi = hi + jnp.where(i == last, NEG, 0.0)
        s = jnp.concatenate([s[:, :n_ctx], lo, s[:, n_ctx + BLOCK:n_ctx + 2 * BLOCK], hi], axis=1)
        o = _softmax_pv(s, sink_col, vcat)
        for g in range(KV_GROUP):
            o_ref[0, t * BLOCK:(t + 1) * BLOCK, g * HEAD_DIM:(g + 1) * HEAD_DIM] = (
                o[g * BLOCK:(g + 1) * BLOCK].astype(BF))


def _band_bias():
    row = np.arange(KV_GROUP * BLOCK)[:, None] % BLOCK
    col = np.arange(BLOCK)[None, :]
    prev_ok = (row - (col - BLOCK)) <= WINDOW
    next_ok = ((col + BLOCK) - row) <= WINDOW
    return jnp.asarray(np.where(np.concatenate([prev_ok, next_ok], axis=1), 0.0, NEG), F32)


def _attn_call(q, k, v, kx, vx, sink, qblocks):
    B, n, _ = q.shape
    n_ctx = kx.shape[1]
    nb = n // BLOCK
    tq = qblocks * BLOCK
    gw = KV_GROUP * HEAD_DIM
    prev = lambda b, h, i: (b, jnp.maximum(i * qblocks - 1, 0), h)
    cur = lambda b, h, i: (b, i, h)
    nxt = lambda b, h, i: (b, jnp.minimum((i + 1) * qblocks, nb - 1), h)
    ctx = lambda b, h, i: (b, 0, h)
    edge = lambda m: pl.BlockSpec((1, BLOCK, HEAD_DIM), m)
    mid = pl.BlockSpec((1, tq, HEAD_DIM), cur)
    return pl.pallas_call(
        functools.partial(_attn_kernel, n_ctx=n_ctx, qblocks=qblocks),
        grid=(B, N_KV_HEADS, n // tq),
        in_specs=[pl.BlockSpec(memory_space=pltpu.SMEM),
                  pl.BlockSpec((KV_GROUP * BLOCK, 2 * BLOCK), lambda b, h, i: (0, 0)),
                  pl.BlockSpec((1, tq, gw), cur),
                  edge(prev), mid, edge(nxt), edge(prev), mid, edge(nxt),
                  pl.BlockSpec((1, n_ctx, HEAD_DIM), ctx),
                  pl.BlockSpec((1, n_ctx, HEAD_DIM), ctx)],
        out_specs=pl.BlockSpec((1, tq, gw), cur),
        out_shape=jax.ShapeDtypeStruct((B, n, HQ), BF),
        compiler_params=_params("parallel", "parallel", "arbitrary"),
        name="attn",
    )(sink, _band_bias(), q, k, k, k, v, v, v, kx, vx)


def _attn_ctx_kernel(sink_ref, q_ref, k_ref, v_ref, o_ref, *, n_ctx):
    h = pl.program_id(1)
    qs = _stack_heads(q_ref[0])
    s = lax.dot_general(qs, k_ref[0], (((1,), (1,)), ((), ())), preferred_element_type=F32)
    o = _softmax_pv(s, _sink_column(sink_ref, h, n_ctx), v_ref[0])
    for g in range(KV_GROUP):
        o_ref[0, :, g * HEAD_DIM:(g + 1) * HEAD_DIM] = o[g * n_ctx:(g + 1) * n_ctx].astype(BF)


def _attn_ctx_call(q, k, v, sink):
    B, n_ctx, _ = q.shape
    gw = KV_GROUP * HEAD_DIM
    idx = lambda b, h: (b, 0, h)
    return pl.pallas_call(
        functools.partial(_attn_ctx_kernel, n_ctx=n_ctx),
        grid=(B, N_KV_HEADS),
        in_specs=[pl.BlockSpec(memory_space=pltpu.SMEM),
                  pl.BlockSpec((1, n_ctx, gw), idx),
                  pl.BlockSpec((1, n_ctx, HEAD_DIM), idx),
                  pl.BlockSpec((1, n_ctx, HEAD_DIM), idx)],
        out_specs=pl.BlockSpec((1, n_ctx, gw), idx),
        out_shape=jax.ShapeDtypeStruct((B, n_ctx, HQ), BF),
        compiler_params=_params("parallel", "parallel"),
        name="attn_ctx",
    )(sink, q, k, v)


def _mm_res_kernel(a_ref, w_ref, x_ref, mod_ref, g2_ref, o_ref, h_ref, *, rc):
    tm = x_ref.shape[1]
    for r in range(tm // rc):
        rows = slice(r * rc, (r + 1) * rc)
        p = jnp.dot(a_ref[0, rows].astype(BF), w_ref[...], preferred_element_type=F32)
        xn = x_ref[0, rows] + mod_ref[0, 2:3, :] * p
        o_ref[0, rows] = xn
        h_ref[0, rows] = _modnorm(xn, g2_ref[...], mod_ref[0, 3:4, :], mod_ref[0, 4:5, :]).astype(BF)


def _mm_res_call(a, w_bf, x, mod, g2, tm, rc):
    B, n, K = a.shape
    row = lambda b, i: (b, i, 0)
    return pl.pallas_call(
        functools.partial(_mm_res_kernel, rc=rc),
        grid=(B, n // tm),
        in_specs=[pl.BlockSpec((1, tm, K), row),
                  _resident((K, D_MODEL), lambda b, i: (0, 0)),
                  pl.BlockSpec((1, tm, D_MODEL), row),
                  pl.BlockSpec((1, 6, D_MODEL), lambda b, i: (b, 0, 0)),
                  pl.BlockSpec((1, D_MODEL), lambda b, i: (0, 0))],
        out_specs=[pl.BlockSpec((1, tm, D_MODEL), row),
                   pl.BlockSpec((1, tm, D_MODEL), row)],
        out_shape=[jax.ShapeDtypeStruct((B, n, D_MODEL), F32),
                   jax.ShapeDtypeStruct((B, n, D_MODEL), BF)],
        compiler_params=_params("parallel", "parallel"),
        name="mm_res",
    )(a, w_bf, x, mod, g2)


def _ffn_step(h_ref, x_ref, mod_ref, wg, wu, wd, o_ref, ssq_ref, rc, nc, first, last):
    tm = h_ref.shape[1]
    for r in range(tm // rc):
        rows = slice(r * rc, (r + 1) * rc)
        h = h_ref[0, rows]
        gate = jnp.dot(h, wg[...], preferred_element_type=F32)
        up = jnp.dot(h, wu[...], preferred_element_type=F32)
        a = (_silu(gate) * up).astype(BF)
        sq = None
        for c in range(D_MODEL // nc):
            cols = slice(c * nc, (c + 1) * nc)
            acc = jnp.dot(a, wd[:, cols], preferred_element_type=F32)
            if not first:
                acc = o_ref[0, rows, cols] + acc
            if last:
                acc = x_ref[0, rows, cols] + mod_ref[0, 5:6, cols] * acc
                if ssq_ref is not None:
                    part = jnp.sum(acc * acc, axis=-1, keepdims=True)
                    sq = part if sq is None else sq + part
            o_ref[0, rows, cols] = acc
        if sq is not None:
            ssq_ref[0, rows] = sq


def _ffn_steps(h_ref, x_ref, mod_ref, wg, wu, wd, o_ref, ssq_ref, rc, nc):
    f = pl.program_id(2)
    nf = pl.num_programs(2)

    @pl.when(f == 0)
    def _():
        _ffn_step(h_ref, x_ref, mod_ref, wg, wu, wd, o_ref, ssq_ref, rc, nc, True, False)

    @pl.when((f > 0) & (f < nf - 1))
    def _():
        _ffn_step(h_ref, x_ref, mod_ref, wg, wu, wd, o_ref, ssq_ref, rc, nc, False, False)

    @pl.when(f == nf - 1)
    def _():
        _ffn_step(h_ref, x_ref, mod_ref, wg, wu, wd, o_ref, ssq_ref, rc, nc, False, True)


def _ffn_kernel(h_ref, x_ref, mod_ref, wg_ref, wu_ref, wd_ref, o_ref, *maybe_ssq_ref, rc, nc):
    ssq_ref = maybe_ssq_ref[0] if maybe_ssq_ref else None
    _ffn_steps(h_ref, x_ref, mod_ref, wg_ref, wu_ref, wd_ref, o_ref, ssq_ref, rc, nc)


def _ffn_cast_kernel(h_ref, x_ref, mod_ref, wg_ref, wu_ref, wd_ref, o_ref, wgb_ref, wub_ref, wdb_ref, *, rc, nc):
    wg = wg_ref[0].astype(BF)
    wu = wu_ref[0].astype(BF)
    wd = wd_ref[0].astype(BF)
    wgb_ref[...] = wg
    wub_ref[...] = wu
    wdb_ref[...] = wd
    _ffn_steps(h_ref, x_ref, mod_ref, wg, wu, wd, o_ref, None, rc, nc)


def _ffn_specs(tm, tf):
    row = lambda b, i, f: (b, i, 0)
    return [pl.BlockSpec((1, tm, D_MODEL), row),
            pl.BlockSpec((1, tm, D_MODEL), row),
            pl.BlockSpec((1, 6, D_MODEL), lambda b, i, f: (b, 0, 0))], pl.BlockSpec((1, tm, D_MODEL), row)


def _ffn_params():
    return pltpu.CompilerParams(dimension_semantics=("parallel", "parallel", "arbitrary"),
                                vmem_limit_bytes=BIG_VMEM_LIMIT_BYTES)


def _ffn_call(h, x, mod, wg_bf, wu_bf, wd_bf, tm, tf, rc, nc, emit_ssq):
    B, n, _ = x.shape
    act_specs, out_spec = _ffn_specs(tm, tf)
    out_specs, out_shape = out_spec, jax.ShapeDtypeStruct((B, n, D_MODEL), F32)
    if emit_ssq:
        out_specs = [out_spec, pl.BlockSpec((1, tm, 1), lambda b, i, f: (b, i, 0))]
        out_shape = [out_shape, jax.ShapeDtypeStruct((B, n, 1), F32)]
    return pl.pallas_call(
        functools.partial(_ffn_kernel, rc=rc, nc=nc),
        grid=(B, n // tm, D_FF // tf),
        in_specs=act_specs + [pl.BlockSpec((D_MODEL, tf), lambda b, i, f: (0, f)),
                              pl.BlockSpec((D_MODEL, tf), lambda b, i, f: (0, f)),
                              pl.BlockSpec((tf, D_MODEL), lambda b, i, f: (f, 0))],
        out_specs=out_specs,
        out_shape=out_shape,
        compiler_params=_ffn_params(),
        name="ffn",
    )(h, x, mod, wg_bf, wu_bf, wd_bf)


def _ffn_cast_call(h, x, mod, wg_all, wu_all, wd_all, layer, tm, tf, rc, nc):
    B, n, _ = x.shape
    assert B == 1 and n == tm
    act_specs, out_spec = _ffn_specs(tm, tf)
    up_in = pl.BlockSpec((1, D_MODEL, tf), lambda b, i, f: (layer, 0, f))
    down_in = pl.BlockSpec((1, tf, D_MODEL), lambda b, i, f: (layer, f, 0))
    up_out = pl.BlockSpec((D_MODEL, tf), lambda b, i, f: (0, f))
    down_out = pl.BlockSpec((tf, D_MODEL), lambda b, i, f: (f, 0))
    return pl.pallas_call(
        functools.partial(_ffn_cast_kernel, rc=rc, nc=nc),
        grid=(1, 1, D_FF // tf),
        in_specs=act_specs + [up_in, up_in, down_in],
        out_specs=[out_spec, up_out, up_out, down_out],
        out_shape=[jax.ShapeDtypeStruct((1, n, D_MODEL), F32),
                   jax.ShapeDtypeStruct((D_MODEL, D_FF), BF),
                   jax.ShapeDtypeStruct((D_MODEL, D_FF), BF),
                   jax.ShapeDtypeStruct((D_FF, D_MODEL), BF)],
        compiler_params=_ffn_params(),
        name="ffn_cast",
    )(h, x, mod, wg_all, wu_all, wd_all)


def _dft_tables(n):
    n1, n2, s = n // DFT_N2, DFT_N2, SUBLANES
    i32 = jnp.int32
    j = jnp.arange(n2 // s, dtype=i32)[:, None, None]
    r = jnp.arange(2 * n1, dtype=i32)[None, :, None]
    c = jnp.arange(n1 * s, dtype=i32)[None, None, :]
    part, k1, nn1, l = r // n1, r % n1, c // s, c % s
    idx = (k1 * (s * j + l) + n2 * k1 * nn1 + part * (n // 4)) % n
    m1 = jnp.cos(idx.astype(F32) * (2.0 * np.pi / n))
    same_l = (jnp.arange(s, dtype=i32)[:, None] == (jnp.arange(n1 * s, dtype=i32)[None, :] % s)).astype(F32)
    kron1 = (m1[:, :, None, :] * same_l[None, None, :, :]).reshape(n2 // s, 2 * n1 * s, n1 * s)
    r = jnp.arange(2 * n2, dtype=i32)[:, None]
    c = jnp.arange(2 * s * n2, dtype=i32)[None, :]
    qpart, k2, part, nn2 = r // n2, r % n2, c // (s * n2), c % n2
    idx2 = (k2 * nn2 + (n2 // 4) * (qpart - part) + n2) % n2
    m2 = jnp.cos(idx2.astype(F32) * (2.0 * np.pi / n2))
    same_l2 = (jnp.arange(s, dtype=i32)[:, None]
               == ((jnp.arange(2 * s * n2, dtype=i32)[None, :] // n2) % s)).astype(F32)
    kron2 = (m2[:, None, :] * same_l2[None, :, :]).reshape(2 * n2 * s, 2 * s * n2)
    return kron1.astype(BF), kron2.astype(BF)


def _channel_dft_table(n):
    c = np.arange(FOURIER_GROUP_DIM)
    ang = 2.0 * np.pi * ((c[:, None] * c[None, :]) % FOURIER_GROUP_DIM) / FOURIER_GROUP_DIM
    scale = 1.0 / np.sqrt(float(n) * FOURIER_GROUP_DIM)
    return jnp.asarray(np.concatenate([np.cos(ang), np.sin(ang)], axis=0) * scale, BF)


def _channel_mix(zr, zi, cs):
    outs = []
    for g in range(zr.shape[1] // FOURIER_GROUP_DIM):
        sl = slice(g * FOURIER_GROUP_DIM, (g + 1) * FOURIER_GROUP_DIM)
        zg = jnp.concatenate([zr[:, sl], zi[:, sl]], axis=1).astype(BF)
        outs.append(jnp.dot(zg, cs, preferred_element_type=F32))
    return jnp.concatenate(outs, axis=1)


def _dft1_kernel(x_ref, ssq_ref, mod_ref, g_ref, k_ref, t_ref):
    n1, s, tc = x_ref.shape[1], x_ref.shape[2], x_ref.shape[3]
    rstd = lax.rsqrt(ssq_ref[0].reshape(n1 * s, 1) * (1.0 / D_MODEL) + EPS)
    y = x_ref[0].reshape(n1 * s, tc) * rstd
    xb = ((y * g_ref[...]) * (1.0 + mod_ref[0, 1:2, :]) + mod_ref[0, 0:1, :]).astype(BF)
    t = jnp.dot(k_ref[0], xb, preferred_element_type=F32)
    t_ref[0] = t.reshape(2, n1, s, tc)


def _dft2_kernel(t_ref, k_ref, cs_ref, y_ref):
    s, tc = y_ref.shape[2], y_ref.shape[3]
    rows = s * DFT_N2
    tb = t_ref[0, :, 0].reshape(2 * rows, tc).astype(BF)
    z = jnp.dot(k_ref[...], tb, preferred_element_type=F32)
    y = _channel_mix(z[:rows], z[rows:], cs_ref[...])
    y_ref[0] = y.reshape(DFT_N2, s, tc)


def _fourier_positions(x, ssq, mod, g, kron1, kron2, cs, tc):
    B, n, _ = x.shape
    n1, s = n // DFT_N2, SUBLANES
    nj = DFT_N2 // s
    t = pl.pallas_call(
        _dft1_kernel,
        grid=(nj, B, D_MODEL // tc),
        in_specs=[pl.BlockSpec((1, n1, s, tc), lambda j, b, c: (b, 0, j, c)),
                  pl.BlockSpec((1, n1, s, 1), lambda j, b, c: (b, 0, j, 0)),
                  pl.BlockSpec((1, 6, tc), lambda j, b, c: (b, 0, c)),
                  pl.BlockSpec((1, tc), lambda j, b, c: (0, c)),
                  pl.BlockSpec((1, 2 * n1 * s, n1 * s), lambda j, b, c: (j, 0, 0))],
        out_specs=pl.BlockSpec((1, 2, n1, s, tc), lambda j, b, c: (b, 0, 0, j, c)),
        out_shape=jax.ShapeDtypeStruct((B, 2, n1, DFT_N2, D_MODEL), F32),
        compiler_params=_params("arbitrary", "arbitrary", "arbitrary"),
        name="dft1",
    )(x.reshape(B, n1, DFT_N2, D_MODEL), ssq.reshape(B, n1, DFT_N2, 1), mod, g, kron1)
    na = n1 // s
    y = pl.pallas_call(
        _dft2_kernel,
        grid=(B, na, D_MODEL // tc),
        in_specs=[pl.BlockSpec((1, 2, 1, s * DFT_N2, tc), lambda b, a, c: (b, 0, a, 0, c)),
                  _resident((2 * DFT_N2 * s, 2 * s * DFT_N2), lambda b, a, c: (0, 0)),
                  _resident((2 * FOURIER_GROUP_DIM, FOURIER_GROUP_DIM), lambda b, a, c: (0, 0))],
        out_specs=pl.BlockSpec((1, DFT_N2, s, tc), lambda b, a, c: (b, 0, a, c)),
        out_shape=jax.ShapeDtypeStruct((B, DFT_N2, n1, D_MODEL), F32),
        compiler_params=_params("parallel", "parallel", "parallel"),
        name="dft2",
    )(t.reshape(B, 2, na, s * DFT_N2, D_MODEL), kron2, cs)
    return y.reshape(B, n, D_MODEL)


def _dft_ctx_kernel(x_ref, f_ref, cs_ref, y_ref):
    n_ctx = x_ref.shape[1]
    z = jnp.dot(f_ref[...], x_ref[0].astype(BF), preferred_element_type=F32)
    y_ref[0] = _channel_mix(z[:n_ctx], z[n_ctx:], cs_ref[...])


def _fourier_ctx(hc, cs):
    B, n_ctx, _ = hc.shape
    p = np.arange(n_ctx)
    ang = 2.0 * np.pi * ((p[:, None] * p[None, :]) % n_ctx) / n_ctx
    fmat = jnp.asarray(np.concatenate([np.cos(ang), -np.sin(ang)], axis=0), BF)
    return pl.pallas_call(
        _dft_ctx_kernel,
        grid=(B,),
        in_specs=[pl.BlockSpec((1, n_ctx, D_MODEL), lambda b: (b, 0, 0)),
                  pl.BlockSpec((2 * n_ctx, n_ctx), lambda b: (0, 0)),
                  pl.BlockSpec((2 * FOURIER_GROUP_DIM, FOURIER_GROUP_DIM), lambda b: (0, 0))],
        out_specs=pl.BlockSpec((1, n_ctx, D_MODEL), lambda b: (b, 0, 0)),
        out_shape=jax.ShapeDtypeStruct((B, n_ctx, D_MODEL), F32),
        compiler_params=_params("parallel"),
        name="dft_ctx",
    )(hc, fmat, cs)


def _sgu_kernel(x_ref, mod_ref, g_ref, g2_ref, win_ref, gv_ref, ws_ref, bs_ref, wout_ref, o_ref, h2_ref,
                h_ref, z_ref, ssq_ref, a0_ref, a1_ref, *, cw, nc, gpt):
    s = pl.program_id(2)
    ng = N_SGU_GROUPS
    gd = SGU_GROUP_DIM
    tm = h_ref.shape[0]
    n_in = 2 * ng // gpt
    t = s - n_in

    @pl.when(s == 0)
    def _():
        h_ref[...] = _modnorm(x_ref[0], g_ref[...], mod_ref[0, 0:1, :], mod_ref[0, 1:2, :]).astype(BF)
        ssq_ref[...] = jnp.zeros_like(ssq_ref)

    def in_proj(with_ssq):
        sq = None
        per_group = gd // cw
        for c in range(gpt * per_group):
            z = jnp.dot(h_ref[...], win_ref[:, c * cw:(c + 1) * cw], preferred_element_type=F32)
            z = 0.5 * z * (1.0 + lax.erf(z * (2.0 ** -0.5)))
            zc = (c % per_group) * cw
            z_ref[s * gpt + c // per_group, :, zc:zc + cw] = z.astype(BF)
            if with_ssq:
                part = jnp.sum(z * z, axis=-1, keepdims=True)
                sq = part if sq is None else sq + part
        if with_ssq:
            ssq_ref[...] += sq

    @pl.when(s < n_in // 2)
    def _():
        in_proj(False)

    @pl.when((s >= n_in // 2) & (s < n_in))
    def _():
        in_proj(True)

    def out_proj(a_ref):
        a = a_ref[...]
        for c in range(D_MODEL // nc):
            cols = slice(c * nc, (c + 1) * nc)
            o_ref[0, :, cols] += jnp.dot(a, wout_ref[:, cols], preferred_element_type=F32)

    def prep(a_ref, grp):
        r = lax.rsqrt(ssq_ref[...] * (1.0 / SGU_HALF) + EPS)
        for c in range(tm // SGU_CHUNK):
            rows = slice(c * SGU_CHUNK, (c + 1) * SGU_CHUNK)
            vn = ((z_ref[grp + ng, rows, :].astype(F32) * r[rows]) * gv_ref[0]).astype(BF)
            sp = jnp.dot(ws_ref[0], vn, preferred_element_type=F32) + bs_ref[0]
            a_ref[rows, :] = (z_ref[grp, rows, :].astype(F32) * sp).astype(BF)

    @pl.when(t == 0)
    def _():
        o_ref[0] = jnp.zeros(o_ref.shape[1:], F32)
        prep(a0_ref, 0)

    @pl.when((t >= 1) & (t < ng) & ((t & 1) == 1))
    def _():
        out_proj(a0_ref)
        prep(a1_ref, t)

    @pl.when((t >= 1) & (t < ng) & ((t & 1) == 0))
    def _():
        out_proj(a1_ref)
        prep(a0_ref, t)

    @pl.when(t == ng)
    def _():
        out_proj(a1_ref if (ng - 1) % 2 == 1 else a0_ref)
        xn = x_ref[0] + mod_ref[0, 2:3, :] * o_ref[0]
        o_ref[0] = xn
        h2_ref[0] = _modnorm(xn, g2_ref[...], mod_ref[0, 3:4, :], mod_ref[0, 4:5, :]).astype(BF)


def _sgu_call(x, mod, g, g2, win_bf, gv, ws_bf, bs, wout_bf, tm):
    B, n, _ = x.shape
    ng, gd = N_SGU_GROUPS, SGU_GROUP_DIM
    gpt = 2
    n_in = 2 * ng // gpt
    row = lambda b, i, s: (b, i, 0)
    grp = lambda b, i, s: (jnp.clip(s - n_in, 0, ng - 1), 0, 0)
    return pl.pallas_call(
        functools.partial(_sgu_kernel, cw=256, nc=512, gpt=gpt),
        grid=(B, n // tm, n_in + ng + 1),
        in_specs=[pl.BlockSpec((1, tm, D_MODEL), row),
                  pl.BlockSpec((1, 6, D_MODEL), lambda b, i, s: (b, 0, 0)),
                  pl.BlockSpec((1, D_MODEL), lambda b, i, s: (0, 0)),
                  pl.BlockSpec((1, D_MODEL), lambda b, i, s: (0, 0)),
                  pl.BlockSpec((D_MODEL, gpt * gd), lambda b, i, s: (0, jnp.minimum(s, n_in - 1))),
                  pl.BlockSpec((1, 1, gd), grp),
                  pl.BlockSpec((1, SGU_CHUNK, SGU_CHUNK), grp),
                  pl.BlockSpec((1, SGU_CHUNK, 1), grp),
                  pl.BlockSpec((gd, D_MODEL), lambda b, i, s: (jnp.clip(s - n_in - 1, 0, ng - 1), 0))],
        out_specs=[pl.BlockSpec((1, tm, D_MODEL), row),
                   pl.BlockSpec((1, tm, D_MODEL), row)],
        out_shape=[jax.ShapeDtypeStruct((B, n, D_MODEL), F32),
                   jax.ShapeDtypeStruct((B, n, D_MODEL), BF)],
        scratch_shapes=[pltpu.VMEM((tm, D_MODEL), BF),
                        pltpu.VMEM((2 * ng, tm, gd), BF),
                        pltpu.VMEM((tm, 1), F32),
                        pltpu.VMEM((tm, gd), BF),
                        pltpu.VMEM((tm, gd), BF)],
        compiler_params=pltpu.CompilerParams(dimension_semantics=("parallel", "parallel", "arbitrary"),
                                             vmem_limit_bytes=BIG_VMEM_LIMIT_BYTES),
        name="sgu",
    )(x, mod, g, g2, win_bf, gv.reshape(ng, 1, gd), ws_bf, bs.reshape(ng, SGU_CHUNK, 1), wout_bf)


def _rope_tables(n):
    row, col = jnp.meshgrid(jnp.arange(n // GRID_W), jnp.arange(GRID_W), indexing="ij")
    n_freq = HEAD_DIM // 4
    inv_freq = ROPE_BASE ** (-jnp.arange(n_freq, dtype=F32) / n_freq)
    ang = jnp.concatenate([row.reshape(-1, 1).astype(F32) * inv_freq,
                           col.reshape(-1, 1).astype(F32) * inv_freq], axis=-1)
    ang = jnp.concatenate([ang, ang], axis=-1)
    sign = jnp.where(jnp.arange(HEAD_DIM) < HEAD_DIM // 2, -1.0, 1.0).astype(F32)
    return jnp.cos(ang), jnp.sin(ang) * sign


def kernel(x, c, ctx, c_ctx, w_ada, b_ada, norm_g, w_ffn_gate, w_ffn_up, w_ffn_down, w_attn_qkv,
           w_attn_o, attn_q_g, attn_k_g, attn_sink, w_fourier, w_sgu_in, sgu_v_g, w_sgu_spatial,
           b_sgu_spatial, w_sgu_out):
    B, n, _ = x.shape
    n_ctx = ctx.shape[1]
    tm_x = 512
    tm_c = B * n_ctx
    ctx = ctx.reshape(1, tm_c, D_MODEL)

    cond = jnp.zeros((SUBLANES, D_MODEL), F32).at[:B].set(c).at[B].set(c_ctx)
    ada = _ada_all(cond, w_ada, b_ada)
    cos, sin = _rope_tables(n)
    cos_c = jnp.ones((tm_c, HEAD_DIM), F32)
    sin_c = jnp.zeros((tm_c, HEAD_DIM), F32)

    for i in range(DEPTH):
        kind = i % N_MIXERS
        j = i // N_MIXERS
        need_ctx = i < DEPTH - 1
        modx = ada[i, :B].reshape(B, 6, D_MODEL)
        modc = ada[i, B].reshape(1, 6, D_MODEL)
        g1 = norm_g[i, 0].reshape(1, D_MODEL)
        g2 = norm_g[i, 1].reshape(1, D_MODEL)
        if kind == 0:
            wqkv = w_attn_qkv[j].astype(BF)
            wo = w_attn_o[j].astype(BF)
            qg = attn_q_g[j].reshape(1, HEAD_DIM)
            kg = attn_k_g[j].reshape(1, HEAD_DIM)
            sink = attn_sink[j]
            q, k, v = _qkv_call(x, modx, g1, wqkv, qg, kg, cos, sin, 1024, 256)
            qc, kc, vc = _qkv_call(ctx, modc, g1, wqkv, qg, kg, cos_c, sin_c, tm_c, 256)
            kc = kc.reshape(B, n_ctx, HKV)
            vc = vc.reshape(B, n_ctx, HKV)
            ox = _attn_call(q, k, v, kc, vc, sink, 4)
            x, hx2 = _mm_res_call(ox, wo, x, modx, g2, tm_x, 256)
            if need_ctx:
                oc = _attn_ctx_call(qc.reshape(B, n_ctx, HQ), kc, vc, sink).reshape(1, tm_c, HQ)
                ctx, hc2 = _mm_res_call(oc, wo, ctx, modc, g2, tm_c, 256)
        elif kind == 1:
            wf = w_fourier[j].astype(BF)
            kron1, kron2 = _dft_tables(n)
            yx = _fourier_positions(x, ssq_x, modx, g1, kron1, kron2, _channel_dft_table(n), 1024)
            x, hx2 = _mm_res_call(yx, wf, x, modx, g2, tm_x, 256)
            if need_ctx:
                hc = _modnorm_call(ctx, modc, g1, tm_c).reshape(B, n_ctx, D_MODEL)
                yc = _fourier_ctx(hc, _channel_dft_table(n_ctx)).reshape(1, tm_c, D_MODEL)
                ctx, hc2 = _mm_res_call(yc, wf, ctx, modc, g2, tm_c, 256)
        else:
            win = w_sgu_in[j].astype(BF)
            wout = w_sgu_out[j].astype(BF)
            ws = w_sgu_spatial[j].astype(BF)
            args = (win, sgu_v_g[j], ws, b_sgu_spatial[j], wout)
            x, hx2 = _sgu_call(x, modx, g1, g2, *args, tm_x)
            if need_ctx:
                ctx, hc2 = _sgu_call(ctx, modc, g1, g2, *args, tm_c)
        if need_ctx:
            ctx, wg, wu, wd = _ffn_cast_call(hc2, ctx, modc, w_ffn_gate, w_ffn_up, w_ffn_down, i,
                                             tm_c, 256, tm_c, 512)
        else:
            wg = w_ffn_gate[i].astype(BF)
            wu = w_ffn_up[i].astype(BF)
            wd = w_ffn_down[i].astype(BF)
        if (i + 1) % N_MIXERS == 1 and i + 1 < DEPTH:
            x, ssq_x = _ffn_call(hx2, x, modx, wg, wu, wd, 1024, 512, 512, 512, True)
        else:
            x = _ffn_call(hx2, x, modx, wg, wu, wd, 1024, 512, 512, 512, False)
    return x
```

```python
import functools

import numpy as np
import jax
import jax.numpy as jnp
from jax import lax
from jax.experimental import pallas as pl
from jax.experimental.pallas import tpu as pltpu

D_MODEL = 2048
DEPTH = 4
GRID_W = 64
N_MIXERS = 3
EPS = 1e-6
N_HEADS = 16
N_KV_HEADS = 4
HEAD_DIM = D_MODEL // N_HEADS
KV_GROUP = N_HEADS // N_KV_HEADS
HQ = N_HEADS * HEAD_DIM
HKV = N_KV_HEADS * HEAD_DIM
WINDOW = 128
BLOCK = 128
ROPE_BASE = 10000.0
N_FOURIER_GROUPS = 8
FOURIER_GROUP_DIM = D_MODEL // N_FOURIER_GROUPS
SGU_CHUNK = 128
SGU_HALF = 3 * D_MODEL
N_SGU_GROUPS = 8
SGU_GROUP_DIM = SGU_HALF // N_SGU_GROUPS
D_FF = 5632

BF = jnp.bfloat16
F32 = jnp.float32

V7X_VMEM_BYTES = 64 * 1024 * 1024
VMEM_LIMIT_BYTES = V7X_VMEM_BYTES - 8 * 1024 * 1024
BIG_VMEM_LIMIT_BYTES = V7X_VMEM_BYTES - 4 * 1024 * 1024
SUBLANES = 8
MXU_DIM = 256

ROW_TILE = 512
ROW_CHUNK = MXU_DIM
QKV_TM = 1024
FFN_TM = 1024
FFN_TF = 512
FFN_RC = 512
FFN_NC = 512
FFN_CAST_TF = MXU_DIM
ATTN_QBLOCKS = 8
DFT_TC = 1024
ADA_TN = 1024
NEG = -1e30
NO_CAP = float(np.finfo(np.float32).max)
LOG2E = float(np.log2(np.e))

DFT_N2 = 64


def _params(*sem):
    return pltpu.CompilerParams(dimension_semantics=sem, vmem_limit_bytes=VMEM_LIMIT_BYTES)


def _resident(shape, index_map):
    return pl.BlockSpec(shape, index_map, pipeline_mode=pl.Buffered(1))


def _modnorm(x, g, shift, scale):
    ms = jnp.mean(x * x, axis=-1, keepdims=True)
    y = x * lax.rsqrt(ms + EPS)
    return (y * g) * (1.0 + scale) + shift


def _silu(t):
    return t * jax.nn.sigmoid(t)


def _ada_kernel(cond_ref, w_ref, b_ref, o_ref):
    s = _silu(cond_ref[...]).astype(BF)
    o_ref[0] = jnp.dot(s, w_ref[0].astype(BF), preferred_element_type=F32) + b_ref[0]


def _ada_all(cond, w_ada, b_ada):
    tn = ADA_TN
    return pl.pallas_call(
        _ada_kernel,
        grid=(DEPTH, 6 * D_MODEL // tn),
        in_specs=[pl.BlockSpec((SUBLANES, D_MODEL), lambda l, j: (0, 0)),
                  pl.BlockSpec((1, D_MODEL, tn), lambda l, j: (l, 0, j)),
                  pl.BlockSpec((1, 1, tn), lambda l, j: (l, 0, j))],
        out_specs=pl.BlockSpec((1, SUBLANES, tn), lambda l, j: (l, 0, j)),
        out_shape=jax.ShapeDtypeStruct((DEPTH, SUBLANES, 6 * D_MODEL), F32),
        compiler_params=_params("arbitrary", "arbitrary"),
        name="ada",
    )(cond, w_ada, b_ada.reshape(DEPTH, 1, 6 * D_MODEL))


def _modnorm_kernel(x_ref, mod_ref, g_ref, o_ref):
    o_ref[0] = _modnorm(x_ref[0], g_ref[...], mod_ref[0, 0:1, :], mod_ref[0, 1:2, :])


def _modnorm_call(x, mod, g, tm):
    B, n, _ = x.shape
    return pl.pallas_call(
        _modnorm_kernel,
        grid=(B, n // tm),
        in_specs=[pl.BlockSpec((1, tm, D_MODEL), lambda b, i: (b, i, 0)),
                  pl.BlockSpec((1, 6, D_MODEL), lambda b, i: (b, 0, 0)),
                  pl.BlockSpec((1, D_MODEL), lambda b, i: (0, 0))],
        out_specs=pl.BlockSpec((1, tm, D_MODEL), lambda b, i: (b, i, 0)),
        out_shape=jax.ShapeDtypeStruct((B, n, D_MODEL), F32),
        compiler_params=_params("parallel", "parallel"),
        name="modnorm",
    )(x, mod, g)


def _qkv_kernel(x_ref, mod_ref, g_ref, w_ref, qg_ref, kg_ref, cos_ref, sin_ref, q_ref, k_ref, v_ref, *, rc):
    tm = x_ref.shape[1]
    qscale = (HEAD_DIM ** -0.5) * LOG2E
    cw = 4 * HEAD_DIM

    for r in range(tm // rc):
        rows = slice(r * rc, (r + 1) * rc)
        h = _modnorm(x_ref[0, rows], g_ref[...], mod_ref[0, 0:1, :], mod_ref[0, 1:2, :]).astype(BF)
        cos = cos_ref[rows]
        sin = sin_ref[rows]

        def head(t, gain):
            ms = jnp.mean(t * t, axis=-1, keepdims=True)
            t = (t * lax.rsqrt(ms + EPS)) * gain
            return t * cos + pltpu.roll(t, HEAD_DIM // 2, 1) * sin

        t = jnp.dot(h, w_ref[:, HQ:HQ + HKV], preferred_element_type=F32)
        for j in range(N_KV_HEADS):
            tj = head(t[:, j * HEAD_DIM:(j + 1) * HEAD_DIM], kg_ref[...])
            k_ref[0, rows, j * HEAD_DIM:(j + 1) * HEAD_DIM] = tj.astype(BF)
        for c in range(HQ // cw):
            t = jnp.dot(h, w_ref[:, c * cw:(c + 1) * cw], preferred_element_type=F32)
            for j in range(4):
                tj = head(t[:, j * HEAD_DIM:(j + 1) * HEAD_DIM], qg_ref[...]) * qscale
                q_ref[0, rows, c * cw + j * HEAD_DIM:c * cw + (j + 1) * HEAD_DIM] = tj.astype(BF)
        t = jnp.dot(h, w_ref[:, HQ + HKV:], preferred_element_type=F32)
        v_ref[0, rows] = t.astype(BF)


def _qkv_call(x, mod, g, w_bf, qg, kg, cos, sin, tm, rc):
    B, n, _ = x.shape
    row = lambda b, i: (b, i, 0)
    return pl.pallas_call(
        functools.partial(_qkv_kernel, rc=rc),
        grid=(B, n // tm),
        in_specs=[pl.BlockSpec((1, tm, D_MODEL), row),
                  pl.BlockSpec((1, 6, D_MODEL), lambda b, i: (b, 0, 0)),
                  pl.BlockSpec((1, D_MODEL), lambda b, i: (0, 0)),
                  _resident((D_MODEL, HQ + 2 * HKV), lambda b, i: (0, 0)),
                  pl.BlockSpec((1, HEAD_DIM), lambda b, i: (0, 0)),
                  pl.BlockSpec((1, HEAD_DIM), lambda b, i: (0, 0)),
                  pl.BlockSpec((tm, HEAD_DIM), lambda b, i: (i, 0)),
                  pl.BlockSpec((tm, HEAD_DIM), lambda b, i: (i, 0))],
        out_specs=[pl.BlockSpec((1, tm, HQ), row),
                   pl.BlockSpec((1, tm, HKV), row),
                   pl.BlockSpec((1, tm, HKV), row)],
        out_shape=[jax.ShapeDtypeStruct((B, n, HQ), BF),
                   jax.ShapeDtypeStruct((B, n, HKV), BF),
                   jax.ShapeDtypeStruct((B, n, HKV), BF)],
        compiler_params=_params("parallel", "parallel"),
        name="qkv",
    )(x, mod, g, w_bf, qg, kg, cos, sin)


def _softmax_pv(s, sink_col, vcat):
    m = jnp.maximum(jnp.max(s, axis=-1, keepdims=True), sink_col)
    p = jnp.exp2(s - m)
    den = jnp.sum(p, axis=-1, keepdims=True) + jnp.exp2(sink_col - m)
    return jnp.dot(p.astype(BF), vcat, preferred_element_type=F32) / den


def _sink_column(sink_ref, h, rows):
    ridx = lax.broadcasted_iota(jnp.int32, (KV_GROUP * rows, 1), 0)
    col = jnp.full((KV_GROUP * rows, 1), sink_ref[h * KV_GROUP] * LOG2E, F32)
    for g in range(1, KV_GROUP):
        col = jnp.where(ridx >= g * rows, sink_ref[h * KV_GROUP + g] * LOG2E, col)
    return col


def _stack_heads(q):
    return jnp.concatenate([q[:, g * HEAD_DIM:(g + 1) * HEAD_DIM] for g in range(KV_GROUP)], axis=0)


def _attn_kernel(sink_ref, band_ref, q_ref, kp_ref, kc_ref, kn_ref, vp_ref, vc_ref, vn_ref, kx_ref, vx_ref,
                 o_ref, *, n_ctx, qblocks):
    h = pl.program_id(1)
    i = pl.program_id(2)
    last = pl.num_programs(2) - 1
    kloc = jnp.concatenate([kp_ref[0], kc_ref[0], kn_ref[0]], axis=0)
    vloc = jnp.concatenate([vp_ref[0], vc_ref[0], vn_ref[0]], axis=0)
    sink_col = _sink_column(sink_ref, h, BLOCK)
    for t in range(qblocks):
        qs = _stack_heads(q_ref[0, t * BLOCK:(t + 1) * BLOCK, :])
        kcat = jnp.concatenate([kx_ref[0], kloc[t * BLOCK:(t + 3) * BLOCK]], axis=0)
        vcat = jnp.concatenate([vx_ref[0], vloc[t * BLOCK:(t + 3) * BLOCK]], axis=0)
        s = lax.dot_general(qs, kcat, (((1,), (1,)), ((), ())), preferred_element_type=F32)
        lo = jnp.minimum(s[:, n_ctx:n_ctx + BLOCK], band_ref[:, :BLOCK])
        hi = jnp.minimum(s[:, n_ctx + 2 * BLOCK:], band_ref[:, BLOCK:])
        if t == 0:
            lo = jnp.minimum(lo, jnp.where(i == 0, NEG, NO_CAP))
        if t == qblocks - 1:
            hi = jnp.minimum(hi, jnp.where(i == last, NEG, NO_CAP))
        s = jnp.concatenate([s[:, :n_ctx], lo, s[:, n_ctx + BLOCK:n_ctx + 2 * BLOCK], hi], axis=1)
        o = _softmax_pv(s, sink_col, vcat)
        for g in range(KV_GROUP):
            o_ref[0, t * BLOCK:(t + 1) * BLOCK, g * HEAD_DIM:(g + 1) * HEAD_DIM] = (
                o[g * BLOCK:(g + 1) * BLOCK].astype(BF))


def _band_bias():
    row = np.arange(KV_GROUP * BLOCK)[:, None] % BLOCK
    col = np.arange(BLOCK)[None, :]
    prev_ok = (row - (col - BLOCK)) <= WINDOW
    next_ok = ((col + BLOCK) - row) <= WINDOW
    return jnp.asarray(np.where(np.concatenate([prev_ok, next_ok], axis=1), NO_CAP, NEG), F32)


def _attn_call(q, k, v, kx, vx, sink, qblocks):
    B, n, _ = q.shape
    n_ctx = kx.shape[1]
    nb = n // BLOCK
    tq = qblocks * BLOCK
    gw = KV_GROUP * HEAD_DIM
    prev = lambda b, h, i: (b, jnp.maximum(i * qblocks - 1, 0), h)
    cur = lambda b, h, i: (b, i, h)
    nxt = lambda b, h, i: (b, jnp.minimum((i + 1) * qblocks, nb - 1), h)
    ctx = lambda b, h, i: (b, 0, h)
    edge = lambda m: pl.BlockSpec((1, BLOCK, HEAD_DIM), m)
    mid = pl.BlockSpec((1, tq, HEAD_DIM), cur)
    return pl.pallas_call(
        functools.partial(_attn_kernel, n_ctx=n_ctx, qblocks=qblocks),
        grid=(B, N_KV_HEADS, n // tq),
        in_specs=[pl.BlockSpec(memory_space=pltpu.SMEM),
                  pl.BlockSpec((KV_GROUP * BLOCK, 2 * BLOCK), lambda b, h, i: (0, 0)),
                  pl.BlockSpec((1, tq, gw), cur),
                  edge(prev), mid, edge(nxt), edge(prev), mid, edge(nxt),
                  pl.BlockSpec((1, n_ctx, HEAD_DIM), ctx),
                  pl.BlockSpec((1, n_ctx, HEAD_DIM), ctx)],
        out_specs=pl.BlockSpec((1, tq, gw), cur),
        out_shape=jax.ShapeDtypeStruct((B, n, HQ), BF),
        compiler_params=_params("parallel", "parallel", "arbitrary"),
        name="attn",
    )(sink, _band_bias(), q, k, k, k, v, v, v, kx, vx)


def _attn_ctx_kernel(sink_ref, q_ref, k_ref, v_ref, o_ref, *, n_ctx):
    h = pl.program_id(1)
    qs = _stack_heads(q_ref[0])
    s = lax.dot_general(qs, k_ref[0], (((1,), (1,)), ((), ())), preferred_element_type=F32)
    o = _softmax_pv(s, _sink_column(sink_ref, h, n_ctx), v_ref[0])
    for g in range(KV_GROUP):
        o_ref[0, :, g * HEAD_DIM:(g + 1) * HEAD_DIM] = o[g * n_ctx:(g + 1) * n_ctx].astype(BF)


def _attn_ctx_call(q, k, v, sink):
    B, n_ctx, _ = q.shape
    gw = KV_GROUP * HEAD_DIM
    idx = lambda b, h: (b, 0, h)
    return pl.pallas_call(
        functools.partial(_attn_ctx_kernel, n_ctx=n_ctx),
        grid=(B, N_KV_HEADS),
        in_specs=[pl.BlockSpec(memory_space=pltpu.SMEM),
                  pl.BlockSpec((1, n_ctx, gw), idx),
                  pl.BlockSpec((1, n_ctx, HEAD_DIM), idx),
                  pl.BlockSpec((1, n_ctx, HEAD_DIM), idx)],
        out_specs=pl.BlockSpec((1, n_ctx, gw), idx),
        out_shape=jax.ShapeDtypeStruct((B, n_ctx, HQ), BF),
        compiler_params=_params("parallel", "parallel"),
        name="attn_ctx",
    )(sink, q, k, v)


def _mm_res_kernel(a_ref, w_ref, x_ref, mod_ref, g2_ref, o_ref, h_ref, *, rc):
    tm = x_ref.shape[1]
    for r in range(tm // rc):
        rows = slice(r * rc, (r + 1) * rc)
        p = jnp.dot(a_ref[0, rows].astype(BF), w_ref[...], preferred_element_type=F32)
        xn = x_ref[0, rows] + mod_ref[0, 2:3, :] * p
        o_ref[0, rows] = xn
        h_ref[0, rows] = _modnorm(xn, g2_ref[...], mod_ref[0, 3:4, :], mod_ref[0, 4:5, :]).astype(BF)


def _mm_res_call(a, w_bf, x, mod, g2, tm, rc):
    B, n, K = a.shape
    row = lambda b, i: (b, i, 0)
    return pl.pallas_call(
        functools.partial(_mm_res_kernel, rc=rc),
        grid=(B, n // tm),
        in_specs=[pl.BlockSpec((1, tm, K), row),
                  _resident((K, D_MODEL), lambda b, i: (0, 0)),
                  pl.BlockSpec((1, tm, D_MODEL), row),
                  pl.BlockSpec((1, 6, D_MODEL), lambda b, i: (b, 0, 0)),
                  pl.BlockSpec((1, D_MODEL), lambda b, i: (0, 0))],
        out_specs=[pl.BlockSpec((1, tm, D_MODEL), row),
                   pl.BlockSpec((1, tm, D_MODEL), row)],
        out_shape=[jax.ShapeDtypeStruct((B, n, D_MODEL), F32),
                   jax.ShapeDtypeStruct((B, n, D_MODEL), BF)],
        compiler_params=_params("parallel", "parallel"),
        name="mm_res",
    )(a, w_bf, x, mod, g2)


def _ffn_step(h_ref, x_ref, mod_ref, wg, wu, wd, o_ref, ssq_ref, rc, nc, first, last):
    tm = h_ref.shape[1]
    for r in range(tm // rc):
        rows = slice(r * rc, (r + 1) * rc)
        h = h_ref[0, rows]
        gate = jnp.dot(h, wg[...], preferred_element_type=F32)
        up = jnp.dot(h, wu[...], preferred_element_type=F32)
        a = (_silu(gate) * up).astype(BF)
        sq = None
        for c in range(D_MODEL // nc):
            cols = slice(c * nc, (c + 1) * nc)
            acc = jnp.dot(a, wd[:, cols], preferred_element_type=F32)
            if not first:
                acc = o_ref[0, rows, cols] + acc
            if last:
                acc = x_ref[0, rows, cols] + mod_ref[0, 5:6, cols] * acc
                if ssq_ref is not None:
                    part = jnp.sum(acc * acc, axis=-1, keepdims=True)
                    sq = part if sq is None else sq + part
            o_ref[0, rows, cols] = acc
        if sq is not None:
            ssq_ref[0, rows] = sq


def _ffn_steps(h_ref, x_ref, mod_ref, wg, wu, wd, o_ref, ssq_ref, rc, nc):
    f = pl.program_id(2)
    nf = pl.num_programs(2)

    @pl.when(f == 0)
    def _():
        _ffn_step(h_ref, x_ref, mod_ref, wg, wu, wd, o_ref, ssq_ref, rc, nc, True, False)

    @pl.when((f > 0) & (f < nf - 1))
    def _():
        _ffn_step(h_ref, x_ref, mod_ref, wg, wu, wd, o_ref, ssq_ref, rc, nc, False, False)

    @pl.when(f == nf - 1)
    def _():
        _ffn_step(h_ref, x_ref, mod_ref, wg, wu, wd, o_ref, ssq_ref, rc, nc, False, True)


def _ffn_kernel(h_ref, x_ref, mod_ref, wg_ref, wu_ref, wd_ref, o_ref, *maybe_ssq_ref, rc, nc):
    ssq_ref = maybe_ssq_ref[0] if maybe_ssq_ref else None
    _ffn_steps(h_ref, x_ref, mod_ref, wg_ref, wu_ref, wd_ref, o_ref, ssq_ref, rc, nc)


def _ffn_cast_kernel(h_ref, x_ref, mod_ref, wg_ref, wu_ref, wd_ref, o_ref, wgb_ref, wub_ref, wdb_ref, *, rc, nc):
    wg = wg_ref[0].astype(BF)
    wu = wu_ref[0].astype(BF)
    wd = wd_ref[0].astype(BF)
    wgb_ref[...] = wg
    wub_ref[...] = wu
    wdb_ref[...] = wd
    _ffn_steps(h_ref, x_ref, mod_ref, wg, wu, wd, o_ref, None, rc, nc)


def _ffn_specs(tm):
    row = lambda b, i, f: (b, i, 0)
    return [pl.BlockSpec((1, tm, D_MODEL), row),
            pl.BlockSpec((1, tm, D_MODEL), row),
            pl.BlockSpec((1, 6, D_MODEL), lambda b, i, f: (b, 0, 0))], pl.BlockSpec((1, tm, D_MODEL), row)


def _ffn_params():
    return pltpu.CompilerParams(dimension_semantics=("parallel", "parallel", "arbitrary"),
                                vmem_limit_bytes=BIG_VMEM_LIMIT_BYTES)


def _ffn_call(h, x, mod, wg_bf, wu_bf, wd_bf, tm, tf, rc, nc, emit_ssq):
    B, n, _ = x.shape
    act_specs, out_spec = _ffn_specs(tm)
    out_specs, out_shape = out_spec, jax.ShapeDtypeStruct((B, n, D_MODEL), F32)
    if emit_ssq:
        out_specs = [out_spec, pl.BlockSpec((1, tm, 1), lambda b, i, f: (b, i, 0))]
        out_shape = [out_shape, jax.ShapeDtypeStruct((B, n, 1), F32)]
    return pl.pallas_call(
        functools.partial(_ffn_kernel, rc=rc, nc=nc),
        grid=(B, n // tm, D_FF // tf),
        in_specs=act_specs + [pl.BlockSpec((D_MODEL, tf), lambda b, i, f: (0, f)),
                              pl.BlockSpec((D_MODEL, tf), lambda b, i, f: (0, f)),
                              pl.BlockSpec((tf, D_MODEL), lambda b, i, f: (f, 0))],
        out_specs=out_specs,
        out_shape=out_shape,
        compiler_params=_ffn_params(),
        name="ffn",
    )(h, x, mod, wg_bf, wu_bf, wd_bf)


def _ffn_cast_call(h, x, mod, wg_all, wu_all, wd_all, layer, tm, tf, rc, nc):
    B, n, _ = x.shape
    assert B == 1 and n == tm
    act_specs, out_spec = _ffn_specs(tm)
    up_in = pl.BlockSpec((1, D_MODEL, tf), lambda b, i, f: (layer, 0, f))
    down_in = pl.BlockSpec((1, tf, D_MODEL), lambda b, i, f: (layer, f, 0))
    up_out = pl.BlockSpec((D_MODEL, tf), lambda b, i, f: (0, f))
    down_out = pl.BlockSpec((tf, D_MODEL), lambda b, i, f: (f, 0))
    return pl.pallas_call(
        functools.partial(_ffn_cast_kernel, rc=rc, nc=nc),
        grid=(1, 1, D_FF // tf),
        in_specs=act_specs + [up_in, up_in, down_in],
        out_specs=[out_spec, up_out, up_out, down_out],
        out_shape=[jax.ShapeDtypeStruct((1, n, D_MODEL), F32),
                   jax.ShapeDtypeStruct((D_MODEL, D_FF), BF),
                   jax.ShapeDtypeStruct((D_MODEL, D_FF), BF),
                   jax.ShapeDtypeStruct((D_FF, D_MODEL), BF)],
        compiler_params=_ffn_params(),
        name="ffn_cast",
    )(h, x, mod, wg_all, wu_all, wd_all)


def _dft_tables(n):
    n1, n2, s = n // DFT_N2, DFT_N2, SUBLANES
    i32 = jnp.int32
    j = jnp.arange(n2 // s, dtype=i32)[:, None, None]
    r = jnp.arange(2 * n1, dtype=i32)[None, :, None]
    c = jnp.arange(n1 * s, dtype=i32)[None, None, :]
    part, k1, nn1, l = r // n1, r % n1, c // s, c % s
    idx = (k1 * (s * j + l) + n2 * k1 * nn1 + part * (n // 4)) % n
    m1 = jnp.cos(idx.astype(F32) * (2.0 * np.pi / n))
    same_l = (jnp.arange(s, dtype=i32)[:, None] == (jnp.arange(n1 * s, dtype=i32)[None, :] % s)).astype(F32)
    kron1 = (m1[:, :, None, :] * same_l[None, None, :, :]).reshape(n2 // s, 2 * n1 * s, n1 * s)
    r = jnp.arange(2 * n2, dtype=i32)[:, None]
    c = jnp.arange(2 * s * n2, dtype=i32)[None, :]
    qpart, k2, part, nn2 = r // n2, r % n2, c // (s * n2), c % n2
    idx2 = (k2 * nn2 + (n2 // 4) * (qpart - part) + n2) % n2
    m2 = jnp.cos(idx2.astype(F32) * (2.0 * np.pi / n2))
    same_l2 = (jnp.arange(s, dtype=i32)[:, None]
               == ((jnp.arange(2 * s * n2, dtype=i32)[None, :] // n2) % s)).astype(F32)
    kron2 = (m2[:, None, :] * same_l2[None, :, :]).reshape(2 * n2 * s, 2 * s * n2)
    return kron1.astype(BF), kron2.astype(BF)


def _channel_dft_table(n):
    c = np.arange(FOURIER_GROUP_DIM)
    ang = 2.0 * np.pi * ((c[:, None] * c[None, :]) % FOURIER_GROUP_DIM) / FOURIER_GROUP_DIM
    scale = 1.0 / np.sqrt(float(n) * FOURIER_GROUP_DIM)
    return jnp.asarray(np.concatenate([np.cos(ang), np.sin(ang)], axis=0) * scale, BF)


def _channel_mix(zr, zi, cs):
    outs = []
    for g in range(zr.shape[1] // FOURIER_GROUP_DIM):
        sl = slice(g * FOURIER_GROUP_DIM, (g + 1) * FOURIER_GROUP_DIM)
        zg = jnp.concatenate([zr[:, sl], zi[:, sl]], axis=1).astype(BF)
        outs.append(jnp.dot(zg, cs, preferred_element_type=F32))
    return jnp.concatenate(outs, axis=1)


def _dft1_kernel(x_ref, ssq_ref, mod_ref, g_ref, k_ref, t_ref):
    n1, s, tc = x_ref.shape[1], x_ref.shape[2], x_ref.shape[3]
    rstd = lax.rsqrt(ssq_ref[0].reshape(n1 * s, 1) * (1.0 / D_MODEL) + EPS)
    y = x_ref[0].reshape(n1 * s, tc) * rstd
    xb = ((y * g_ref[...]) * (1.0 + mod_ref[0, 1:2, :]) + mod_ref[0, 0:1, :]).astype(BF)
    t = jnp.dot(k_ref[0], xb, preferred_element_type=F32)
    t_ref[0] = t.reshape(2, n1, s, tc)


def _dft2_kernel(t_ref, k_ref, cs_ref, y_ref):
    s, tc = y_ref.shape[2], y_ref.shape[3]
    rows = s * DFT_N2
    tb = t_ref[0, :, 0].reshape(2 * rows, tc).astype(BF)
    z = jnp.dot(k_ref[...], tb, preferred_element_type=F32)
    y = _channel_mix(z[:rows], z[rows:], cs_ref[...])
    y_ref[0] = y.reshape(DFT_N2, s, tc)


def _fourier_positions(x, ssq, mod, g, kron1, kron2, cs, tc):
    B, n, _ = x.shape
    n1, s = n // DFT_N2, SUBLANES
    nj = DFT_N2 // s
    t = pl.pallas_call(
        _dft1_kernel,
        grid=(nj, B, D_MODEL // tc),
        in_specs=[pl.BlockSpec((1, n1, s, tc), lambda j, b, c: (b, 0, j, c)),
                  pl.BlockSpec((1, n1, s, 1), lambda j, b, c: (b, 0, j, 0)),
                  pl.BlockSpec((1, 6, tc), lambda j, b, c: (b, 0, c)),
                  pl.BlockSpec((1, tc), lambda j, b, c: (0, c)),
                  pl.BlockSpec((1, 2 * n1 * s, n1 * s), lambda j, b, c: (j, 0, 0))],
        out_specs=pl.BlockSpec((1, 2, n1, s, tc), lambda j, b, c: (b, 0, 0, j, c)),
        out_shape=jax.ShapeDtypeStruct((B, 2, n1, DFT_N2, D_MODEL), F32),
        compiler_params=_params("arbitrary", "arbitrary", "arbitrary"),
        name="dft1",
    )(x.reshape(B, n1, DFT_N2, D_MODEL), ssq.reshape(B, n1, DFT_N2, 1), mod, g, kron1)
    na = n1 // s
    y = pl.pallas_call(
        _dft2_kernel,
        grid=(B, na, D_MODEL // tc),
        in_specs=[pl.BlockSpec((1, 2, 1, s * DFT_N2, tc), lambda b, a, c: (b, 0, a, 0, c)),
                  _resident((2 * DFT_N2 * s, 2 * s * DFT_N2), lambda b, a, c: (0, 0)),
                  _resident((2 * FOURIER_GROUP_DIM, FOURIER_GROUP_DIM), lambda b, a, c: (0, 0))],
        out_specs=pl.BlockSpec((1, DFT_N2, s, tc), lambda b, a, c: (b, 0, a, c)),
        out_shape=jax.ShapeDtypeStruct((B, DFT_N2, n1, D_MODEL), F32),
        compiler_params=_params("parallel", "parallel", "parallel"),
        name="dft2",
    )(t.reshape(B, 2, na, s * DFT_N2, D_MODEL), kron2, cs)
    return y.reshape(B, n, D_MODEL)


def _dft_ctx_kernel(x_ref, f_ref, cs_ref, y_ref):
    n_ctx = x_ref.shape[1]
    z = jnp.dot(f_ref[...], x_ref[0].astype(BF), preferred_element_type=F32)
    y_ref[0] = _channel_mix(z[:n_ctx], z[n_ctx:], cs_ref[...])


def _fourier_ctx(hc, cs):
    B, n_ctx, _ = hc.shape
    p = np.arange(n_ctx)
    ang = 2.0 * np.pi * ((p[:, None] * p[None, :]) % n_ctx) / n_ctx
    fmat = jnp.asarray(np.concatenate([np.cos(ang), -np.sin(ang)], axis=0), BF)
    return pl.pallas_call(
        _dft_ctx_kernel,
        grid=(B,),
        in_specs=[pl.BlockSpec((1, n_ctx, D_MODEL), lambda b: (b, 0, 0)),
                  pl.BlockSpec((2 * n_ctx, n_ctx), lambda b: (0, 0)),
                  pl.BlockSpec((2 * FOURIER_GROUP_DIM, FOURIER_GROUP_DIM), lambda b: (0, 0))],
        out_specs=pl.BlockSpec((1, n_ctx, D_MODEL), lambda b: (b, 0, 0)),
        out_shape=jax.ShapeDtypeStruct((B, n_ctx, D_MODEL), F32),
        compiler_params=_params("parallel"),
        name="dft_ctx",
    )(hc, fmat, cs)


def _sgu_kernel(x_ref, mod_ref, g_ref, g2_ref, win_ref, gv_ref, ws_ref, bs_ref, wout_ref, o_ref, h2_ref,
                h_ref, z_ref, ssq_ref, a0_ref, a1_ref, *, cw, nc, gpt):
    s = pl.program_id(2)
    ng = N_SGU_GROUPS
    gd = SGU_GROUP_DIM
    tm = h_ref.shape[0]
    n_in = 2 * ng // gpt
    t = s - n_in

    @pl.when(s == 0)
    def _():
        h_ref[...] = _modnorm(x_ref[0], g_ref[...], mod_ref[0, 0:1, :], mod_ref[0, 1:2, :]).astype(BF)
        ssq_ref[...] = jnp.zeros_like(ssq_ref)

    def in_proj(with_ssq):
        sq = None
        per_group = gd // cw
        for c in range(gpt * per_group):
            z = jnp.dot(h_ref[...], win_ref[:, c * cw:(c + 1) * cw], preferred_element_type=F32)
            z = 0.5 * z * (1.0 + lax.erf(z * (2.0 ** -0.5)))
            zc = (c % per_group) * cw
            z_ref[s * gpt + c // per_group, :, zc:zc + cw] = z.astype(BF)
            if with_ssq:
                part = jnp.sum(z * z, axis=-1, keepdims=True)
                sq = part if sq is None else sq + part
        if with_ssq:
            ssq_ref[...] += sq

    @pl.when(s < n_in // 2)
    def _():
        in_proj(False)

    @pl.when((s >= n_in // 2) & (s < n_in))
    def _():
        in_proj(True)

    def out_proj(a_ref):
        a = a_ref[...]
        for c in range(D_MODEL // nc):
            cols = slice(c * nc, (c + 1) * nc)
            o_ref[0, :, cols] += jnp.dot(a, wout_ref[:, cols], preferred_element_type=F32)

    def prep(a_ref, grp):
        r = lax.rsqrt(ssq_ref[...] * (1.0 / SGU_HALF) + EPS)
        for c in range(tm // SGU_CHUNK):
            rows = slice(c * SGU_CHUNK, (c + 1) * SGU_CHUNK)
            vn = ((z_ref[grp + ng, rows, :].astype(F32) * r[rows]) * gv_ref[0]).astype(BF)
            sp = jnp.dot(ws_ref[0], vn, preferred_element_type=F32) + bs_ref[0]
            a_ref[rows, :] = (z_ref[grp, rows, :].astype(F32) * sp).astype(BF)

    @pl.when(t == 0)
    def _():
        o_ref[0] = jnp.zeros(o_ref.shape[1:], F32)
        prep(a0_ref, 0)

    @pl.when((t >= 1) & (t < ng) & ((t & 1) == 1))
    def _():
        out_proj(a0_ref)
        prep(a1_ref, t)

    @pl.when((t >= 1) & (t < ng) & ((t & 1) == 0))
    def _():
        out_proj(a1_ref)
        prep(a0_ref, t)

    @pl.when(t == ng)
    def _():
        out_proj(a1_ref if (ng - 1) % 2 == 1 else a0_ref)
        xn = x_ref[0] + mod_ref[0, 2:3, :] * o_ref[0]
        o_ref[0] = xn
        h2_ref[0] = _modnorm(xn, g2_ref[...], mod_ref[0, 3:4, :], mod_ref[0, 4:5, :]).astype(BF)


def _sgu_call(x, mod, g, g2, win_bf, gv, ws_bf, bs, wout_bf, tm):
    B, n, _ = x.shape
    ng, gd = N_SGU_GROUPS, SGU_GROUP_DIM
    gpt = 2
    n_in = 2 * ng // gpt
    row = lambda b, i, s: (b, i, 0)
    grp = lambda b, i, s: (jnp.clip(s - n_in, 0, ng - 1), 0, 0)
    return pl.pallas_call(
        functools.partial(_sgu_kernel, cw=MXU_DIM, nc=FFN_NC, gpt=gpt),
        grid=(B, n // tm, n_in + ng + 1),
        in_specs=[pl.BlockSpec((1, tm, D_MODEL), row),
                  pl.BlockSpec((1, 6, D_MODEL), lambda b, i, s: (b, 0, 0)),
                  pl.BlockSpec((1, D_MODEL), lambda b, i, s: (0, 0)),
                  pl.BlockSpec((1, D_MODEL), lambda b, i, s: (0, 0)),
                  pl.BlockSpec((D_MODEL, gpt * gd), lambda b, i, s: (0, jnp.minimum(s, n_in - 1))),
                  pl.BlockSpec((1, 1, gd), grp),
                  pl.BlockSpec((1, SGU_CHUNK, SGU_CHUNK), grp),
                  pl.BlockSpec((1, SGU_CHUNK, 1), grp),
                  pl.BlockSpec((gd, D_MODEL), lambda b, i, s: (jnp.clip(s - n_in - 1, 0, ng - 1), 0))],
        out_specs=[pl.BlockSpec((1, tm, D_MODEL), row),
                   pl.BlockSpec((1, tm, D_MODEL), row)],
        out_shape=[jax.ShapeDtypeStruct((B, n, D_MODEL), F32),
                   jax.ShapeDtypeStruct((B, n, D_MODEL), BF)],
        scratch_shapes=[pltpu.VMEM((tm, D_MODEL), BF),
                        pltpu.VMEM((2 * ng, tm, gd), BF),
                        pltpu.VMEM((tm, 1), F32),
                        pltpu.VMEM((tm, gd), BF),
                        pltpu.VMEM((tm, gd), BF)],
        compiler_params=pltpu.CompilerParams(dimension_semantics=("parallel", "parallel", "arbitrary"),
                                             vmem_limit_bytes=BIG_VMEM_LIMIT_BYTES),
        name="sgu",
    )(x, mod, g, g2, win_bf, gv.reshape(ng, 1, gd), ws_bf, bs.reshape(ng, SGU_CHUNK, 1), wout_bf)


def _rope_tables(n):
    row, col = jnp.meshgrid(jnp.arange(n // GRID_W), jnp.arange(GRID_W), indexing="ij")
    n_freq = HEAD_DIM // 4
    inv_freq = ROPE_BASE ** (-jnp.arange(n_freq, dtype=F32) / n_freq)
    ang = jnp.concatenate([row.reshape(-1, 1).astype(F32) * inv_freq,
                           col.reshape(-1, 1).astype(F32) * inv_freq], axis=-1)
    ang = jnp.concatenate([ang, ang], axis=-1)
    sign = jnp.where(jnp.arange(HEAD_DIM) < HEAD_DIM // 2, -1.0, 1.0).astype(F32)
    return jnp.cos(ang), jnp.sin(ang) * sign


def kernel(x, c, ctx, c_ctx, w_ada, b_ada, norm_g, w_ffn_gate, w_ffn_up, w_ffn_down, w_attn_qkv,
           w_attn_o, attn_q_g, attn_k_g, attn_sink, w_fourier, w_sgu_in, sgu_v_g, w_sgu_spatial,
           b_sgu_spatial, w_sgu_out):
    B, n, _ = x.shape
    n_ctx = ctx.shape[1]
    tm_x = ROW_TILE
    tm_c = B * n_ctx
    ctx = ctx.reshape(1, tm_c, D_MODEL)

    cond = jnp.zeros((SUBLANES, D_MODEL), F32).at[:B].set(c).at[B].set(c_ctx)
    ada = _ada_all(cond, w_ada, b_ada)
    cos, sin = _rope_tables(n)
    cos_c = jnp.ones((tm_c, HEAD_DIM), F32)
    sin_c = jnp.zeros((tm_c, HEAD_DIM), F32)

    for i in range(DEPTH):
        kind = i % N_MIXERS
        j = i // N_MIXERS
        need_ctx = i < DEPTH - 1
        modx = ada[i, :B].reshape(B, 6, D_MODEL)
        modc = ada[i, B].reshape(1, 6, D_MODEL)
        g1 = norm_g[i, 0].reshape(1, D_MODEL)
        g2 = norm_g[i, 1].reshape(1, D_MODEL)
        if kind == 0:
            wqkv = w_attn_qkv[j].astype(BF)
            wo = w_attn_o[j].astype(BF)
            qg = attn_q_g[j].reshape(1, HEAD_DIM)
            kg = attn_k_g[j].reshape(1, HEAD_DIM)
            sink = attn_sink[j]
            q, k, v = _qkv_call(x, modx, g1, wqkv, qg, kg, cos, sin, QKV_TM, ROW_CHUNK)
            qc, kc, vc = _qkv_call(ctx, modc, g1, wqkv, qg, kg, cos_c, sin_c, tm_c, ROW_CHUNK)
            kc = kc.reshape(B, n_ctx, HKV)
            vc = vc.reshape(B, n_ctx, HKV)
            ox = _attn_call(q, k, v, kc, vc, sink, ATTN_QBLOCKS)
            x, hx2 = _mm_res_call(ox, wo, x, modx, g2, tm_x, ROW_CHUNK)
            if need_ctx:
                oc = _attn_ctx_call(qc.reshape(B, n_ctx, HQ), kc, vc, sink).reshape(1, tm_c, HQ)
                ctx, hc2 = _mm_res_call(oc, wo, ctx, modc, g2, tm_c, ROW_CHUNK)
        elif kind == 1:
            wf = w_fourier[j].astype(BF)
            kron1, kron2 = _dft_tables(n)
            yx = _fourier_positions(x, ssq_x, modx, g1, kron1, kron2, _channel_dft_table(n), DFT_TC)
            x, hx2 = _mm_res_call(yx, wf, x, modx, g2, tm_x, ROW_CHUNK)
            if need_ctx:
                hc = _modnorm_call(ctx, modc, g1, tm_c).reshape(B, n_ctx, D_MODEL)
                yc = _fourier_ctx(hc, _channel_dft_table(n_ctx)).reshape(1, tm_c, D_MODEL)
                ctx, hc2 = _mm_res_call(yc, wf, ctx, modc, g2, tm_c, ROW_CHUNK)
        else:
            win = w_sgu_in[j].astype(BF)
            wout = w_sgu_out[j].astype(BF)
            ws = w_sgu_spatial[j].astype(BF)
            args = (win, sgu_v_g[j], ws, b_sgu_spatial[j], wout)
            x, hx2 = _sgu_call(x, modx, g1, g2, *args, tm_x)
            if need_ctx:
                ctx, hc2 = _sgu_call(ctx, modc, g1, g2, *args, tm_c)
        if need_ctx:
            ctx, wg, wu, wd = _ffn_cast_call(hc2, ctx, modc, w_ffn_gate, w_ffn_up, w_ffn_down, i,
                                             tm_c, FFN_CAST_TF, tm_c, FFN_NC)
        else:
            wg = w_ffn_gate[i].astype(BF)
            wu = w_ffn_up[i].astype(BF)
            wd = w_ffn_down[i].astype(BF)
        if (i + 1) % N_MIXERS == 1 and i + 1 < DEPTH:
            x, ssq_x = _ffn_call(hx2, x, modx, wg, wu, wd, FFN_TM, FFN_TF, FFN_RC, FFN_NC, True)
        else:
            x = _ffn_call(hx2, x, modx, wg, wu, wd, FFN_TM, FFN_TF, FFN_RC, FFN_NC, False)
    return x
```

```python
import functools

import numpy as np
import jax
import jax.numpy as jnp
from jax import lax
from jax.experimental import pallas as pl
from jax.experimental.pallas import tpu as pltpu

D_MODEL = 2048
DEPTH = 4
GRID_W = 64
N_MIXERS = 3
EPS = 1e-6
N_HEADS = 16
N_KV_HEADS = 4
HEAD_DIM = D_MODEL // N_HEADS
KV_GROUP = N_HEADS // N_KV_HEADS
HQ = N_HEADS * HEAD_DIM
HKV = N_KV_HEADS * HEAD_DIM
WINDOW = 128
BLOCK = 128
ROPE_BASE = 10000.0
N_FOURIER_GROUPS = 8
FOURIER_GROUP_DIM = D_MODEL // N_FOURIER_GROUPS
SGU_CHUNK = 128
SGU_HALF = 3 * D_MODEL
N_SGU_GROUPS = 8
SGU_GROUP_DIM = SGU_HALF // N_SGU_GROUPS
D_FF = 5632

BF = jnp.bfloat16
F32 = jnp.float32

V7X_VMEM_BYTES = 64 * 1024 * 1024
VMEM_LIMIT_BYTES = V7X_VMEM_BYTES - 8 * 1024 * 1024
BIG_VMEM_LIMIT_BYTES = V7X_VMEM_BYTES - 4 * 1024 * 1024
SUBLANES = 8
MXU_DIM = 256

ROW_TILE = 512
SGU_TM = 1024
ROW_CHUNK = MXU_DIM
QKV_TM = 1024
FFN_TM = 1024
FFN_TF = 512
FFN_RC = 512
FFN_NC = 512
FFN_CAST_TF = MXU_DIM
ATTN_QBLOCKS = 8
DFT_TC = 1024
ADA_TN = 1024
NEG = -1e30
NO_CAP = float(np.finfo(np.float32).max)
LOG2E = float(np.log2(np.e))

DFT_N2 = 64


def _params(*sem):
    return pltpu.CompilerParams(dimension_semantics=sem, vmem_limit_bytes=VMEM_LIMIT_BYTES)


def _resident(shape, index_map):
    return pl.BlockSpec(shape, index_map, pipeline_mode=pl.Buffered(1))


def _modnorm(x, g, shift, scale):
    ms = jnp.mean(x * x, axis=-1, keepdims=True)
    y = x * lax.rsqrt(ms + EPS)
    return (y * g) * (1.0 + scale) + shift


def _silu(t):
    return t * jax.nn.sigmoid(t)


def _ada_kernel(cond_ref, w_ref, b_ref, o_ref):
    s = _silu(cond_ref[...]).astype(BF)
    o_ref[0] = jnp.dot(s, w_ref[0].astype(BF), preferred_element_type=F32) + b_ref[0]


def _ada_all(cond, w_ada, b_ada):
    tn = ADA_TN
    return pl.pallas_call(
        _ada_kernel,
        grid=(DEPTH, 6 * D_MODEL // tn),
        in_specs=[pl.BlockSpec((SUBLANES, D_MODEL), lambda l, j: (0, 0)),
                  pl.BlockSpec((1, D_MODEL, tn), lambda l, j: (l, 0, j)),
                  pl.BlockSpec((1, 1, tn), lambda l, j: (l, 0, j))],
        out_specs=pl.BlockSpec((1, SUBLANES, tn), lambda l, j: (l, 0, j)),
        out_shape=jax.ShapeDtypeStruct((DEPTH, SUBLANES, 6 * D_MODEL), F32),
        compiler_params=_params("arbitrary", "arbitrary"),
        name="ada",
    )(cond, w_ada, b_ada.reshape(DEPTH, 1, 6 * D_MODEL))


def _modnorm_kernel(x_ref, mod_ref, g_ref, o_ref):
    o_ref[0] = _modnorm(x_ref[0], g_ref[...], mod_ref[0, 0:1, :], mod_ref[0, 1:2, :])


def _modnorm_call(x, mod, g, tm):
    B, n, _ = x.shape
    return pl.pallas_call(
        _modnorm_kernel,
        grid=(B, n // tm),
        in_specs=[pl.BlockSpec((1, tm, D_MODEL), lambda b, i: (b, i, 0)),
                  pl.BlockSpec((1, 6, D_MODEL), lambda b, i: (b, 0, 0)),
                  pl.BlockSpec((1, D_MODEL), lambda b, i: (0, 0))],
        out_specs=pl.BlockSpec((1, tm, D_MODEL), lambda b, i: (b, i, 0)),
        out_shape=jax.ShapeDtypeStruct((B, n, D_MODEL), F32),
        compiler_params=_params("parallel", "parallel"),
        name="modnorm",
    )(x, mod, g)


def _qkv_kernel(x_ref, mod_ref, g_ref, w_ref, qg_ref, kg_ref, cos_ref, sin_ref, q_ref, k_ref, v_ref, *, rc):
    tm = x_ref.shape[1]
    qscale = (HEAD_DIM ** -0.5) * LOG2E
    cw = 4 * HEAD_DIM

    for r in range(tm // rc):
        rows = slice(r * rc, (r + 1) * rc)
        h = _modnorm(x_ref[0, rows], g_ref[...], mod_ref[0, 0:1, :], mod_ref[0, 1:2, :]).astype(BF)
        cos = cos_ref[rows]
        sin = sin_ref[rows]

        def head(t, gain):
            ms = jnp.mean(t * t, axis=-1, keepdims=True)
            t = (t * lax.rsqrt(ms + EPS)) * gain
            return t * cos + pltpu.roll(t, HEAD_DIM // 2, 1) * sin

        t = jnp.dot(h, w_ref[:, HQ:HQ + HKV], preferred_element_type=F32)
        for j in range(N_KV_HEADS):
            tj = head(t[:, j * HEAD_DIM:(j + 1) * HEAD_DIM], kg_ref[...])
            k_ref[0, rows, j * HEAD_DIM:(j + 1) * HEAD_DIM] = tj.astype(BF)
        for c in range(HQ // cw):
            t = jnp.dot(h, w_ref[:, c * cw:(c + 1) * cw], preferred_element_type=F32)
            for j in range(4):
                tj = head(t[:, j * HEAD_DIM:(j + 1) * HEAD_DIM], qg_ref[...]) * qscale
                q_ref[0, rows, c * cw + j * HEAD_DIM:c * cw + (j + 1) * HEAD_DIM] = tj.astype(BF)
        t = jnp.dot(h, w_ref[:, HQ + HKV:], preferred_element_type=F32)
        v_ref[0, rows] = t.astype(BF)


def _qkv_call(x, mod, g, w_bf, qg, kg, cos, sin, tm, rc):
    B, n, _ = x.shape
    row = lambda b, i: (b, i, 0)
    return pl.pallas_call(
        functools.partial(_qkv_kernel, rc=rc),
        grid=(B, n // tm),
        in_specs=[pl.BlockSpec((1, tm, D_MODEL), row),
                  pl.BlockSpec((1, 6, D_MODEL), lambda b, i: (b, 0, 0)),
                  pl.BlockSpec((1, D_MODEL), lambda b, i: (0, 0)),
                  _resident((D_MODEL, HQ + 2 * HKV), lambda b, i: (0, 0)),
                  pl.BlockSpec((1, HEAD_DIM), lambda b, i: (0, 0)),
                  pl.BlockSpec((1, HEAD_DIM), lambda b, i: (0, 0)),
                  pl.BlockSpec((tm, HEAD_DIM), lambda b, i: (i, 0)),
                  pl.BlockSpec((tm, HEAD_DIM), lambda b, i: (i, 0))],
        out_specs=[pl.BlockSpec((1, tm, HQ), row),
                   pl.BlockSpec((1, tm, HKV), row),
                   pl.BlockSpec((1, tm, HKV), row)],
        out_shape=[jax.ShapeDtypeStruct((B, n, HQ), BF),
                   jax.ShapeDtypeStruct((B, n, HKV), BF),
                   jax.ShapeDtypeStruct((B, n, HKV), BF)],
        compiler_params=_params("parallel", "parallel"),
        name="qkv",
    )(x, mod, g, w_bf, qg, kg, cos, sin)


def _softmax_pv(s, sink_col, vcat):
    m = jnp.maximum(jnp.max(s, axis=-1, keepdims=True), sink_col)
    p = jnp.exp2(s - m)
    den = jnp.sum(p, axis=-1, keepdims=True) + jnp.exp2(sink_col - m)
    return jnp.dot(p.astype(BF), vcat, preferred_element_type=F32) / den


def _sink_column(sink_ref, h, rows):
    ridx = lax.broadcasted_iota(jnp.int32, (KV_GROUP * rows, 1), 0)
    col = jnp.full((KV_GROUP * rows, 1), sink_ref[h * KV_GROUP] * LOG2E, F32)
    for g in range(1, KV_GROUP):
        col = jnp.where(ridx >= g * rows, sink_ref[h * KV_GROUP + g] * LOG2E, col)
    return col


def _stack_heads(q):
    return jnp.concatenate([q[:, g * HEAD_DIM:(g + 1) * HEAD_DIM] for g in range(KV_GROUP)], axis=0)


def _attn_kernel(sink_ref, band_ref, q_ref, kp_ref, kc_ref, kn_ref, vp_ref, vc_ref, vn_ref, kx_ref, vx_ref,
                 o_ref, *, n_ctx, qblocks):
    h = pl.program_id(1)
    i = pl.program_id(2)
    last = pl.num_programs(2) - 1
    kloc = jnp.concatenate([kp_ref[0], kc_ref[0], kn_ref[0]], axis=0)
    vloc = jnp.concatenate([vp_ref[0], vc_ref[0], vn_ref[0]], axis=0)
    sink_col = _sink_column(sink_ref, h, BLOCK)
    for t in range(qblocks):
        qs = _stack_heads(q_ref[0, t * BLOCK:(t + 1) * BLOCK, :])
        kcat = jnp.concatenate([kx_ref[0], kloc[t * BLOCK:(t + 3) * BLOCK]], axis=0)
        vcat = jnp.concatenate([vx_ref[0], vloc[t * BLOCK:(t + 3) * BLOCK]], axis=0)
        s = lax.dot_general(qs, kcat, (((1,), (1,)), ((), ())), preferred_element_type=F32)
        lo = jnp.minimum(s[:, n_ctx:n_ctx + BLOCK], band_ref[:, :BLOCK])
        hi = jnp.minimum(s[:, n_ctx + 2 * BLOCK:], band_ref[:, BLOCK:])
        if t == 0:
            lo = jnp.minimum(lo, jnp.where(i == 0, NEG, NO_CAP))
        if t == qblocks - 1:
            hi = jnp.minimum(hi, jnp.where(i == last, NEG, NO_CAP))
        s = jnp.concatenate([s[:, :n_ctx], lo, s[:, n_ctx + BLOCK:n_ctx + 2 * BLOCK], hi], axis=1)
        o = _softmax_pv(s, sink_col, vcat)
        for g in range(KV_GROUP):
            o_ref[0, t * BLOCK:(t + 1) * BLOCK, g * HEAD_DIM:(g + 1) * HEAD_DIM] = (
                o[g * BLOCK:(g + 1) * BLOCK].astype(BF))


def _band_bias():
    row = np.arange(KV_GROUP * BLOCK)[:, None] % BLOCK
    col = np.arange(BLOCK)[None, :]
    prev_ok = (row - (col - BLOCK)) <= WINDOW
    next_ok = ((col + BLOCK) - row) <= WINDOW
    return jnp.asarray(np.where(np.concatenate([prev_ok, next_ok], axis=1), NO_CAP, NEG), F32)


def _attn_call(q, k, v, kx, vx, sink, qblocks):
    B, n, _ = q.shape
    n_ctx = kx.shape[1]
    nb = n // BLOCK
    tq = qblocks * BLOCK
    gw = KV_GROUP * HEAD_DIM
    prev = lambda b, h, i: (b, jnp.maximum(i * qblocks - 1, 0), h)
    cur = lambda b, h, i: (b, i, h)
    nxt = lambda b, h, i: (b, jnp.minimum((i + 1) * qblocks, nb - 1), h)
    ctx = lambda b, h, i: (b, 0, h)
    edge = lambda m: pl.BlockSpec((1, BLOCK, HEAD_DIM), m)
    mid = pl.BlockSpec((1, tq, HEAD_DIM), cur)
    return pl.pallas_call(
        functools.partial(_attn_kernel, n_ctx=n_ctx, qblocks=qblocks),
        grid=(B, N_KV_HEADS, n // tq),
        in_specs=[pl.BlockSpec(memory_space=pltpu.SMEM),
                  pl.BlockSpec((KV_GROUP * BLOCK, 2 * BLOCK), lambda b, h, i: (0, 0)),
                  pl.BlockSpec((1, tq, gw), cur),
                  edge(prev), mid, edge(nxt), edge(prev), mid, edge(nxt),
                  pl.BlockSpec((1, n_ctx, HEAD_DIM), ctx),
                  pl.BlockSpec((1, n_ctx, HEAD_DIM), ctx)],
        out_specs=pl.BlockSpec((1, tq, gw), cur),
        out_shape=jax.ShapeDtypeStruct((B, n, HQ), BF),
        compiler_params=_params("parallel", "parallel", "arbitrary"),
        name="attn",
    )(sink, _band_bias(), q, k, k, k, v, v, v, kx, vx)


def _attn_ctx_kernel(sink_ref, q_ref, k_ref, v_ref, o_ref, *, n_ctx):
    h = pl.program_id(1)
    qs = _stack_heads(q_ref[0])
    s = lax.dot_general(qs, k_ref[0], (((1,), (1,)), ((), ())), preferred_element_type=F32)
    o = _softmax_pv(s, _sink_column(sink_ref, h, n_ctx), v_ref[0])
    for g in range(KV_GROUP):
        o_ref[0, :, g * HEAD_DIM:(g + 1) * HEAD_DIM] = o[g * n_ctx:(g + 1) * n_ctx].astype(BF)


def _attn_ctx_call(q, k, v, sink):
    B, n_ctx, _ = q.shape
    gw = KV_GROUP * HEAD_DIM
    idx = lambda b, h: (b, 0, h)
    return pl.pallas_call(
        functools.partial(_attn_ctx_kernel, n_ctx=n_ctx),
        grid=(B, N_KV_HEADS),
        in_specs=[pl.BlockSpec(memory_space=pltpu.SMEM),
                  pl.BlockSpec((1, n_ctx, gw), idx),
                  pl.BlockSpec((1, n_ctx, HEAD_DIM), idx),
                  pl.BlockSpec((1, n_ctx, HEAD_DIM), idx)],
        out_specs=pl.BlockSpec((1, n_ctx, gw), idx),
        out_shape=jax.ShapeDtypeStruct((B, n_ctx, HQ), BF),
        compiler_params=_params("parallel", "parallel"),
        name="attn_ctx",
    )(sink, q, k, v)


def _mm_res_kernel(a_ref, w_ref, x_ref, mod_ref, g2_ref, o_ref, h_ref, *, rc):
    tm = x_ref.shape[1]
    for r in range(tm // rc):
        rows = slice(r * rc, (r + 1) * rc)
        p = jnp.dot(a_ref[0, rows].astype(BF), w_ref[...], preferred_element_type=F32)
        xn = x_ref[0, rows] + mod_ref[0, 2:3, :] * p
        o_ref[0, rows] = xn
        h_ref[0, rows] = _modnorm(xn, g2_ref[...], mod_ref[0, 3:4, :], mod_ref[0, 4:5, :]).astype(BF)


def _mm_res_call(a, w_bf, x, mod, g2, tm, rc):
    B, n, K = a.shape
    row = lambda b, i: (b, i, 0)
    return pl.pallas_call(
        functools.partial(_mm_res_kernel, rc=rc),
        grid=(B, n // tm),
        in_specs=[pl.BlockSpec((1, tm, K), row),
                  _resident((K, D_MODEL), lambda b, i: (0, 0)),
                  pl.BlockSpec((1, tm, D_MODEL), row),
                  pl.BlockSpec((1, 6, D_MODEL), lambda b, i: (b, 0, 0)),
                  pl.BlockSpec((1, D_MODEL), lambda b, i: (0, 0))],
        out_specs=[pl.BlockSpec((1, tm, D_MODEL), row),
                   pl.BlockSpec((1, tm, D_MODEL), row)],
        out_shape=[jax.ShapeDtypeStruct((B, n, D_MODEL), F32),
                   jax.ShapeDtypeStruct((B, n, D_MODEL), BF)],
        compiler_params=_params("parallel", "parallel"),
        name="mm_res",
    )(a, w_bf, x, mod, g2)


def _ffn_step(h_ref, x_ref, mod_ref, wg, wu, wd, o_ref, ssq_ref, rc, nc, first, last):
    tm = h_ref.shape[1]
    for r in range(tm // rc):
        rows = slice(r * rc, (r + 1) * rc)
        h = h_ref[0, rows]
        gate = jnp.dot(h, wg[...], preferred_element_type=F32)
        up = jnp.dot(h, wu[...], preferred_element_type=F32)
        a = (_silu(gate) * up).astype(BF)
        sq = None
        for c in range(D_MODEL // nc):
            cols = slice(c * nc, (c + 1) * nc)
            acc = jnp.dot(a, wd[:, cols], preferred_element_type=F32)
            if not first:
                acc = o_ref[0, rows, cols] + acc
            if last:
                acc = x_ref[0, rows, cols] + mod_ref[0, 5:6, cols] * acc
                if ssq_ref is not None:
                    part = jnp.sum(acc * acc, axis=-1, keepdims=True)
                    sq = part if sq is None else sq + part
            o_ref[0, rows, cols] = acc
        if sq is not None:
            ssq_ref[0, rows] = sq


def _ffn_steps(h_ref, x_ref, mod_ref, wg, wu, wd, o_ref, ssq_ref, rc, nc):
    f = pl.program_id(2)
    nf = pl.num_programs(2)

    @pl.when(f == 0)
    def _():
        _ffn_step(h_ref, x_ref, mod_ref, wg, wu, wd, o_ref, ssq_ref, rc, nc, True, False)

    @pl.when((f > 0) & (f < nf - 1))
    def _():
        _ffn_step(h_ref, x_ref, mod_ref, wg, wu, wd, o_ref, ssq_ref, rc, nc, False, False)

    @pl.when(f == nf - 1)
    def _():
        _ffn_step(h_ref, x_ref, mod_ref, wg, wu, wd, o_ref, ssq_ref, rc, nc, False, True)


def _ffn_kernel(h_ref, x_ref, mod_ref, wg_ref, wu_ref, wd_ref, o_ref, *maybe_ssq_ref, rc, nc):
    ssq_ref = maybe_ssq_ref[0] if maybe_ssq_ref else None
    _ffn_steps(h_ref, x_ref, mod_ref, wg_ref, wu_ref, wd_ref, o_ref, ssq_ref, rc, nc)


def _ffn_cast_kernel(h_ref, x_ref, mod_ref, wg_ref, wu_ref, wd_ref, o_ref, wgb_ref, wub_ref, wdb_ref, *, rc, nc):
    wg = wg_ref[0].astype(BF)
    wu = wu_ref[0].astype(BF)
    wd = wd_ref[0].astype(BF)
    wgb_ref[...] = wg
    wub_ref[...] = wu
    wdb_ref[...] = wd
    _ffn_steps(h_ref, x_ref, mod_ref, wg, wu, wd, o_ref, None, rc, nc)


def _ffn_specs(tm):
    row = lambda b, i, f: (b, i, 0)
    return [pl.BlockSpec((1, tm, D_MODEL), row),
            pl.BlockSpec((1, tm, D_MODEL), row),
            pl.BlockSpec((1, 6, D_MODEL), lambda b, i, f: (b, 0, 0))], pl.BlockSpec((1, tm, D_MODEL), row)


def _ffn_params():
    return pltpu.CompilerParams(dimension_semantics=("parallel", "parallel", "arbitrary"),
                                vmem_limit_bytes=BIG_VMEM_LIMIT_BYTES)


def _ffn_call(h, x, mod, wg_bf, wu_bf, wd_bf, tm, tf, rc, nc, emit_ssq):
    B, n, _ = x.shape
    act_specs, out_spec = _ffn_specs(tm)
    out_specs, out_shape = out_spec, jax.ShapeDtypeStruct((B, n, D_MODEL), F32)
    if emit_ssq:
        out_specs = [out_spec, pl.BlockSpec((1, tm, 1), lambda b, i, f: (b, i, 0))]
        out_shape = [out_shape, jax.ShapeDtypeStruct((B, n, 1), F32)]
    return pl.pallas_call(
        functools.partial(_ffn_kernel, rc=rc, nc=nc),
        grid=(B, n // tm, D_FF // tf),
        in_specs=act_specs + [pl.BlockSpec((D_MODEL, tf), lambda b, i, f: (0, f)),
                              pl.BlockSpec((D_MODEL, tf), lambda b, i, f: (0, f)),
                              pl.BlockSpec((tf, D_MODEL), lambda b, i, f: (f, 0))],
        out_specs=out_specs,
        out_shape=out_shape,
        compiler_params=_ffn_params(),
        name="ffn",
    )(h, x, mod, wg_bf, wu_bf, wd_bf)


def _ffn_cast_call(h, x, mod, wg_all, wu_all, wd_all, layer, tm, tf, rc, nc):
    B, n, _ = x.shape
    assert B == 1 and n == tm
    act_specs, out_spec = _ffn_specs(tm)
    up_in = pl.BlockSpec((1, D_MODEL, tf), lambda b, i, f: (layer, 0, f))
    down_in = pl.BlockSpec((1, tf, D_MODEL), lambda b, i, f: (layer, f, 0))
    up_out = pl.BlockSpec((D_MODEL, tf), lambda b, i, f: (0, f))
    down_out = pl.BlockSpec((tf, D_MODEL), lambda b, i, f: (f, 0))
    return pl.pallas_call(
        functools.partial(_ffn_cast_kernel, rc=rc, nc=nc),
        grid=(1, 1, D_FF // tf),
        in_specs=act_specs + [up_in, up_in, down_in],
        out_specs=[out_spec, up_out, up_out, down_out],
        out_shape=[jax.ShapeDtypeStruct((1, n, D_MODEL), F32),
                   jax.ShapeDtypeStruct((D_MODEL, D_FF), BF),
                   jax.ShapeDtypeStruct((D_MODEL, D_FF), BF),
                   jax.ShapeDtypeStruct((D_FF, D_MODEL), BF)],
        compiler_params=_ffn_params(),
        name="ffn_cast",
    )(h, x, mod, wg_all, wu_all, wd_all)


def _dft_tables(n):
    n1, n2, s = n // DFT_N2, DFT_N2, SUBLANES
    i32 = jnp.int32
    j = jnp.arange(n2 // s, dtype=i32)[:, None, None]
    r = jnp.arange(2 * n1, dtype=i32)[None, :, None]
    c = jnp.arange(n1 * s, dtype=i32)[None, None, :]
    part, k1, nn1, l = r // n1, r % n1, c // s, c % s
    idx = (k1 * (s * j + l) + n2 * k1 * nn1 + part * (n // 4)) % n
    m1 = jnp.cos(idx.astype(F32) * (2.0 * np.pi / n))
    same_l = (jnp.arange(s, dtype=i32)[:, None] == (jnp.arange(n1 * s, dtype=i32)[None, :] % s)).astype(F32)
    kron1 = (m1[:, :, None, :] * same_l[None, None, :, :]).reshape(n2 // s, 2 * n1 * s, n1 * s)
    r = jnp.arange(2 * n2, dtype=i32)[:, None]
    c = jnp.arange(2 * s * n2, dtype=i32)[None, :]
    qpart, k2, part, nn2 = r // n2, r % n2, c // (s * n2), c % n2
    idx2 = (k2 * nn2 + (n2 // 4) * (qpart - part) + n2) % n2
    m2 = jnp.cos(idx2.astype(F32) * (2.0 * np.pi / n2))
    same_l2 = (jnp.arange(s, dtype=i32)[:, None]
               == ((jnp.arange(2 * s * n2, dtype=i32)[None, :] // n2) % s)).astype(F32)
    kron2 = (m2[:, None, :] * same_l2[None, :, :]).reshape(2 * n2 * s, 2 * s * n2)
    return kron1.astype(BF), kron2.astype(BF)


def _channel_dft_table(n):
    c = np.arange(FOURIER_GROUP_DIM)
    ang = 2.0 * np.pi * ((c[:, None] * c[None, :]) % FOURIER_GROUP_DIM) / FOURIER_GROUP_DIM
    scale = 1.0 / np.sqrt(float(n) * FOURIER_GROUP_DIM)
    return jnp.asarray(np.concatenate([np.cos(ang), np.sin(ang)], axis=0) * scale, BF)


def _channel_mix(zr, zi, cs):
    outs = []
    for g in range(zr.shape[1] // FOURIER_GROUP_DIM):
        sl = slice(g * FOURIER_GROUP_DIM, (g + 1) * FOURIER_GROUP_DIM)
        zg = jnp.concatenate([zr[:, sl], zi[:, sl]], axis=1).astype(BF)
        outs.append(jnp.dot(zg, cs, preferred_element_type=F32))
    return jnp.concatenate(outs, axis=1)


def _dft1_kernel(x_ref, ssq_ref, mod_ref, g_ref, k_ref, t_ref):
    n1, s, tc = x_ref.shape[1], x_ref.shape[2], x_ref.shape[3]
    rstd = lax.rsqrt(ssq_ref[0].reshape(n1 * s, 1) * (1.0 / D_MODEL) + EPS)
    y = x_ref[0].reshape(n1 * s, tc) * rstd
    xb = ((y * g_ref[...]) * (1.0 + mod_ref[0, 1:2, :]) + mod_ref[0, 0:1, :]).astype(BF)
    t = jnp.dot(k_ref[0], xb, preferred_element_type=F32)
    t_ref[0] = t.reshape(2, n1, s, tc)


def _dft2_kernel(t_ref, k_ref, cs_ref, y_ref):
    s, tc = y_ref.shape[2], y_ref.shape[3]
    rows = s * DFT_N2
    tb = t_ref[0, :, 0].reshape(2 * rows, tc).astype(BF)
    z = jnp.dot(k_ref[...], tb, preferred_element_type=F32)
    y = _channel_mix(z[:rows], z[rows:], cs_ref[...])
    y_ref[0] = y.reshape(DFT_N2, s, tc)


def _fourier_positions(x, ssq, mod, g, kron1, kron2, cs, tc):
    B, n, _ = x.shape
    n1, s = n // DFT_N2, SUBLANES
    nj = DFT_N2 // s
    t = pl.pallas_call(
        _dft1_kernel,
        grid=(nj, B, D_MODEL // tc),
        in_specs=[pl.BlockSpec((1, n1, s, tc), lambda j, b, c: (b, 0, j, c)),
                  pl.BlockSpec((1, n1, s, 1), lambda j, b, c: (b, 0, j, 0)),
                  pl.BlockSpec((1, 6, tc), lambda j, b, c: (b, 0, c)),
                  pl.BlockSpec((1, tc), lambda j, b, c: (0, c)),
                  pl.BlockSpec((1, 2 * n1 * s, n1 * s), lambda j, b, c: (j, 0, 0))],
        out_specs=pl.BlockSpec((1, 2, n1, s, tc), lambda j, b, c: (b, 0, 0, j, c)),
        out_shape=jax.ShapeDtypeStruct((B, 2, n1, DFT_N2, D_MODEL), F32),
        compiler_params=_params("arbitrary", "arbitrary", "arbitrary"),
        name="dft1",
    )(x.reshape(B, n1, DFT_N2, D_MODEL), ssq.reshape(B, n1, DFT_N2, 1), mod, g, kron1)
    na = n1 // s
    y = pl.pallas_call(
        _dft2_kernel,
        grid=(B, na, D_MODEL // tc),
        in_specs=[pl.BlockSpec((1, 2, 1, s * DFT_N2, tc), lambda b, a, c: (b, 0, a, 0, c)),
                  _resident((2 * DFT_N2 * s, 2 * s * DFT_N2), lambda b, a, c: (0, 0)),
                  _resident((2 * FOURIER_GROUP_DIM, FOURIER_GROUP_DIM), lambda b, a, c: (0, 0))],
        out_specs=pl.BlockSpec((1, DFT_N2, s, tc), lambda b, a, c: (b, 0, a, c)),
        out_shape=jax.ShapeDtypeStruct((B, DFT_N2, n1, D_MODEL), F32),
        compiler_params=_params("parallel", "parallel", "parallel"),
        name="dft2",
    )(t.reshape(B, 2, na, s * DFT_N2, D_MODEL), kron2, cs)
    return y.reshape(B, n, D_MODEL)


def _dft_ctx_kernel(x_ref, f_ref, cs_ref, y_ref):
    n_ctx = x_ref.shape[1]
    z = jnp.dot(f_ref[...], x_ref[0].astype(BF), preferred_element_type=F32)
    y_ref[0] = _channel_mix(z[:n_ctx], z[n_ctx:], cs_ref[...])


def _fourier_ctx(hc, cs):
    B, n_ctx, _ = hc.shape
    p = np.arange(n_ctx)
    ang = 2.0 * np.pi * ((p[:, None] * p[None, :]) % n_ctx) / n_ctx
    fmat = jnp.asarray(np.concatenate([np.cos(ang), -np.sin(ang)], axis=0), BF)
    return pl.pallas_call(
        _dft_ctx_kernel,
        grid=(B,),
        in_specs=[pl.BlockSpec((1, n_ctx, D_MODEL), lambda b: (b, 0, 0)),
                  pl.BlockSpec((2 * n_ctx, n_ctx), lambda b: (0, 0)),
                  pl.BlockSpec((2 * FOURIER_GROUP_DIM, FOURIER_GROUP_DIM), lambda b: (0, 0))],
        out_specs=pl.BlockSpec((1, n_ctx, D_MODEL), lambda b: (b, 0, 0)),
        out_shape=jax.ShapeDtypeStruct((B, n_ctx, D_MODEL), F32),
        compiler_params=_params("parallel"),
        name="dft_ctx",
    )(hc, fmat, cs)


def _gelu(z):
    return 0.5 * z * (1.0 + lax.erf(z * (2.0 ** -0.5)))


def _sgu_in_kernel(x_ref, mod_ref, g_ref, win_ref, h_ref, v_ref, ssq_ref, *, cw, rc):
    s = pl.program_id(2)
    tm = x_ref.shape[1]

    def tile(first):
        for r in range(tm // rc):
            rows = slice(r * rc, (r + 1) * rc)
            if first:
                h = _modnorm(x_ref[0, rows], g_ref[...], mod_ref[0, 0:1, :], mod_ref[0, 1:2, :]).astype(BF)
                h_ref[0, rows] = h
            else:
                h = h_ref[0, rows]
            sq = None
            for c in range(win_ref.shape[1] // cw):
                cols = slice(c * cw, (c + 1) * cw)
                z = _gelu(jnp.dot(h, win_ref[:, cols], preferred_element_type=F32))
                v_ref[0, rows, cols] = z.astype(BF)
                part = jnp.sum(z * z, axis=-1, keepdims=True)
                sq = part if sq is None else sq + part
            ssq_ref[0, rows] = sq if first else ssq_ref[0, rows] + sq

    @pl.when(s == 0)
    def _():
        tile(True)

    @pl.when(s > 0)
    def _():
        tile(False)


def _sgu_in_call(x, mod, g, win_bf, tm):
    B, n, _ = x.shape
    gpt = 2
    tw = gpt * SGU_GROUP_DIM
    n_v = SGU_HALF // tw
    row = lambda b, i, s: (b, i, 0)
    return pl.pallas_call(
        functools.partial(_sgu_in_kernel, cw=MXU_DIM, rc=min(tm, FFN_RC)),
        grid=(B, n // tm, n_v),
        in_specs=[pl.BlockSpec((1, tm, D_MODEL), row),
                  pl.BlockSpec((1, 6, D_MODEL), lambda b, i, s: (b, 0, 0)),
                  pl.BlockSpec((1, D_MODEL), lambda b, i, s: (0, 0)),
                  pl.BlockSpec((D_MODEL, tw), lambda b, i, s: (0, n_v + s))],
        out_specs=[pl.BlockSpec((1, tm, D_MODEL), row),
                   pl.BlockSpec((1, tm, tw), lambda b, i, s: (b, i, s)),
                   pl.BlockSpec((1, tm, 1), row)],
        out_shape=[jax.ShapeDtypeStruct((B, n, D_MODEL), BF),
                   jax.ShapeDtypeStruct((B, n, SGU_HALF), BF),
                   jax.ShapeDtypeStruct((B, n, 1), F32)],
        compiler_params=_params("parallel", "parallel", "arbitrary"),
        name="sgu_in",
    )(x, mod, g, win_bf)


def _sgu_out_kernel(h_ref, v_ref, ssq_ref, x_ref, mod_ref, g2_ref, win_ref, gv_ref, ws_ref, bs_ref, wout_ref,
                    o_ref, h2_ref, a0_ref, a1_ref, *, cw, nc, rc, fr):
    t = pl.program_id(2)
    ng = N_SGU_GROUPS
    gd = SGU_GROUP_DIM
    tm = h_ref.shape[1]

    def out_proj(a_ref):
        for r in range(tm // rc):
            rows = slice(r * rc, (r + 1) * rc)
            a = a_ref[rows, :]
            for c in range(D_MODEL // nc):
                cols = slice(c * nc, (c + 1) * nc)
                o_ref[0, rows, cols] += jnp.dot(a, wout_ref[:, cols], preferred_element_type=F32)

    def prep(a_ref):
        for r in range(tm // rc):
            rows = slice(r * rc, (r + 1) * rc)
            h = h_ref[0, rows]
            for c in range(gd // cw):
                cols = slice(c * cw, (c + 1) * cw)
                u = _gelu(jnp.dot(h, win_ref[:, cols], preferred_element_type=F32))
                a_ref[rows, cols] = u.astype(BF)
        rstd = lax.rsqrt(ssq_ref[0] * (1.0 / SGU_HALF) + EPS)
        for c in range(tm // SGU_CHUNK):
            rows = slice(c * SGU_CHUNK, (c + 1) * SGU_CHUNK)
            vn = ((v_ref[0, rows, :].astype(F32) * rstd[rows]) * gv_ref[0]).astype(BF)
            sp = jnp.dot(ws_ref[0], vn, preferred_element_type=F32) + bs_ref[0]
            a_ref[rows, :] = (a_ref[rows, :].astype(F32) * sp).astype(BF)

    @pl.when(t == 0)
    def _():
        o_ref[0] = jnp.zeros(o_ref.shape[1:], F32)
        prep(a0_ref)

    @pl.when((t >= 1) & (t < ng) & ((t & 1) == 1))
    def _():
        out_proj(a0_ref)
        prep(a1_ref)

    @pl.when((t >= 1) & (t < ng) & ((t & 1) == 0))
    def _():
        out_proj(a1_ref)
        prep(a0_ref)

    @pl.when(t == ng)
    def _():
        out_proj(a1_ref if (ng - 1) % 2 == 1 else a0_ref)

    @pl.when(t > ng)
    def _():
        rows = pl.ds(pl.multiple_of((t - ng - 1) * fr, fr), fr)
        xn = x_ref[0] + mod_ref[0, 2:3, :] * o_ref[0, rows, :]
        o_ref[0, rows, :] = xn
        h2_ref[0, rows, :] = _modnorm(xn, g2_ref[...], mod_ref[0, 3:4, :], mod_ref[0, 4:5, :]).astype(BF)


def _sgu_out_call(h, v, ssq, x, mod, g2, win_bf, gv, ws_bf, bs, wout_bf, tm):
    B, n, _ = x.shape
    ng, gd = N_SGU_GROUPS, SGU_GROUP_DIM
    fr = ROW_CHUNK
    nfin = tm // fr
    row = lambda b, i, t: (b, i, 0)
    grp = lambda b, i, t: (jnp.clip(t, 0, ng - 1), 0, 0)
    return pl.pallas_call(
        functools.partial(_sgu_out_kernel, cw=MXU_DIM, nc=FFN_NC, rc=min(tm, FFN_RC), fr=fr),
        grid=(B, n // tm, ng + 1 + nfin),
        in_specs=[pl.BlockSpec((1, tm, D_MODEL), row),
                  pl.BlockSpec((1, tm, gd), lambda b, i, t: (b, i, jnp.clip(t, 0, ng - 1))),
                  pl.BlockSpec((1, tm, 1), row),
                  pl.BlockSpec((1, fr, D_MODEL), lambda b, i, t: (b, i * nfin + jnp.clip(t - ng - 1, 0, nfin - 1), 0)),
                  pl.BlockSpec((1, 6, D_MODEL), lambda b, i, t: (b, 0, 0)),
                  pl.BlockSpec((1, D_MODEL), lambda b, i, t: (0, 0)),
                  pl.BlockSpec((D_MODEL, gd), lambda b, i, t: (0, jnp.clip(t, 0, ng - 1))),
                  pl.BlockSpec((1, 1, gd), grp),
                  pl.BlockSpec((1, SGU_CHUNK, SGU_CHUNK), grp),
                  pl.BlockSpec((1, SGU_CHUNK, 1), grp),
                  pl.BlockSpec((gd, D_MODEL), lambda b, i, t: (jnp.clip(t - 1, 0, ng - 1), 0))],
        out_specs=[pl.BlockSpec((1, tm, D_MODEL), row),
                   pl.BlockSpec((1, tm, D_MODEL), row)],
        out_shape=[jax.ShapeDtypeStruct((B, n, D_MODEL), F32),
                   jax.ShapeDtypeStruct((B, n, D_MODEL), BF)],
        scratch_shapes=[pltpu.VMEM((tm, gd), BF),
                        pltpu.VMEM((tm, gd), BF)],
        compiler_params=pltpu.CompilerParams(dimension_semantics=("parallel", "parallel", "arbitrary"),
                                             vmem_limit_bytes=BIG_VMEM_LIMIT_BYTES),
        name="sgu_out",
    )(h, v, ssq, x, mod, g2, win_bf, gv.reshape(ng, 1, gd), ws_bf, bs.reshape(ng, SGU_CHUNK, 1), wout_bf)


def _sgu_call(x, mod, g, g2, win_bf, gv, ws_bf, bs, wout_bf, tm):
    h, v, ssq = _sgu_in_call(x, mod, g, win_bf, tm)
    return _sgu_out_call(h, v, ssq, x, mod, g2, win_bf, gv, ws_bf, bs, wout_bf, tm)


def _rope_tables(n):
    row, col = jnp.meshgrid(jnp.arange(n // GRID_W), jnp.arange(GRID_W), indexing="ij")
    n_freq = HEAD_DIM // 4
    inv_freq = ROPE_BASE ** (-jnp.arange(n_freq, dtype=F32) / n_freq)
    ang = jnp.concatenate([row.reshape(-1, 1).astype(F32) * inv_freq,
                           col.reshape(-1, 1).astype(F32) * inv_freq], axis=-1)
    ang = jnp.concatenate([ang, ang], axis=-1)
    sign = jnp.where(jnp.arange(HEAD_DIM) < HEAD_DIM // 2, -1.0, 1.0).astype(F32)
    return jnp.cos(ang), jnp.sin(ang) * sign


def kernel(x, c, ctx, c_ctx, w_ada, b_ada, norm_g, w_ffn_gate, w_ffn_up, w_ffn_down, w_attn_qkv,
           w_attn_o, attn_q_g, attn_k_g, attn_sink, w_fourier, w_sgu_in, sgu_v_g, w_sgu_spatial,
           b_sgu_spatial, w_sgu_out):
    B, n, _ = x.shape
    n_ctx = ctx.shape[1]
    tm_x = ROW_TILE
    tm_c = B * n_ctx
    ctx = ctx.reshape(1, tm_c, D_MODEL)

    cond = jnp.zeros((SUBLANES, D_MODEL), F32).at[:B].set(c).at[B].set(c_ctx)
    ada = _ada_all(cond, w_ada, b_ada)
    cos, sin = _rope_tables(n)
    cos_c = jnp.ones((tm_c, HEAD_DIM), F32)
    sin_c = jnp.zeros((tm_c, HEAD_DIM), F32)

    for i in range(DEPTH):
        kind = i % N_MIXERS
        j = i // N_MIXERS
        need_ctx = i < DEPTH - 1
        modx = ada[i, :B].reshape(B, 6, D_MODEL)
        modc = ada[i, B].reshape(1, 6, D_MODEL)
        g1 = norm_g[i, 0].reshape(1, D_MODEL)
        g2 = norm_g[i, 1].reshape(1, D_MODEL)
        if kind == 0:
            wqkv = w_attn_qkv[j].astype(BF)
            wo = w_attn_o[j].astype(BF)
            qg = attn_q_g[j].reshape(1, HEAD_DIM)
            kg = attn_k_g[j].reshape(1, HEAD_DIM)
            sink = attn_sink[j]
            q, k, v = _qkv_call(x, modx, g1, wqkv, qg, kg, cos, sin, QKV_TM, ROW_CHUNK)
            qc, kc, vc = _qkv_call(ctx, modc, g1, wqkv, qg, kg, cos_c, sin_c, tm_c, ROW_CHUNK)
            kc = kc.reshape(B, n_ctx, HKV)
            vc = vc.reshape(B, n_ctx, HKV)
            ox = _attn_call(q, k, v, kc, vc, sink, ATTN_QBLOCKS)
            x, hx2 = _mm_res_call(ox, wo, x, modx, g2, tm_x, ROW_CHUNK)
            if need_ctx:
                oc = _attn_ctx_call(qc.reshape(B, n_ctx, HQ), kc, vc, sink).reshape(1, tm_c, HQ)
                ctx, hc2 = _mm_res_call(oc, wo, ctx, modc, g2, tm_c, ROW_CHUNK)
        elif kind == 1:
            wf = w_fourier[j].astype(BF)
            kron1, kron2 = _dft_tables(n)
            yx = _fourier_positions(x, ssq_x, modx, g1, kron1, kron2, _channel_dft_table(n), DFT_TC)
            x, hx2 = _mm_res_call(yx, wf, x, modx, g2, tm_x, ROW_CHUNK)
            if need_ctx:
                hc = _modnorm_call(ctx, modc, g1, tm_c).reshape(B, n_ctx, D_MODEL)
                yc = _fourier_ctx(hc, _channel_dft_table(n_ctx)).reshape(1, tm_c, D_MODEL)
                ctx, hc2 = _mm_res_call(yc, wf, ctx, modc, g2, tm_c, ROW_CHUNK)
        else:
            win = w_sgu_in[j].astype(BF)
            wout = w_sgu_out[j].astype(BF)
            ws = w_sgu_spatial[j].astype(BF)
            args = (win, sgu_v_g[j], ws, b_sgu_spatial[j], wout)
            x, hx2 = _sgu_call(x, modx, g1, g2, *args, SGU_TM)
            if need_ctx:
                ctx, hc2 = _sgu_call(ctx, modc, g1, g2, *args, tm_c)
        if need_ctx:
            ctx, wg, wu, wd = _ffn_cast_call(hc2, ctx, modc, w_ffn_gate, w_ffn_up, w_ffn_down, i,
                                             tm_c, FFN_CAST_TF, tm_c, FFN_NC)
        else:
            wg = w_ffn_gate[i].astype(BF)
            wu = w_ffn_up[i].astype(BF)
            wd = w_ffn_down[i].astype(BF)
        if (i + 1) % N_MIXERS == 1 and i + 1 < DEPTH:
            x, ssq_x = _ffn_call(hx2, x, modx, wg, wu, wd, FFN_TM, FFN_TF, FFN_RC, FFN_NC, True)
        else:
            x = _ffn_call(hx2, x, modx, wg, wu, wd, FFN_TM, FFN_TF, FFN_RC, FFN_NC, False)
    return x
```

```python
import functools

import numpy as np
import jax
import jax.numpy as jnp
from jax import lax
from jax.experimental import pallas as pl
from jax.experimental.pallas import tpu as pltpu

D_MODEL = 2048
DEPTH = 4
GRID_W = 64
N_MIXERS = 3
EPS = 1e-6
N_HEADS = 16
N_KV_HEADS = 4
HEAD_DIM = D_MODEL // N_HEADS
KV_GROUP = N_HEADS // N_KV_HEADS
HQ = N_HEADS * HEAD_DIM
HKV = N_KV_HEADS * HEAD_DIM
WINDOW = 128
BLOCK = 128
ROPE_BASE = 10000.0
N_FOURIER_GROUPS = 8
FOURIER_GROUP_DIM = D_MODEL // N_FOURIER_GROUPS
SGU_CHUNK = 128
SGU_HALF = 3 * D_MODEL
N_SGU_GROUPS = 8
SGU_GROUP_DIM = SGU_HALF // N_SGU_GROUPS
D_FF = 5632

BF = jnp.bfloat16
F32 = jnp.float32

V7X_VMEM_BYTES = 64 * 1024 * 1024
VMEM_LIMIT_BYTES = V7X_VMEM_BYTES - 8 * 1024 * 1024
BIG_VMEM_LIMIT_BYTES = V7X_VMEM_BYTES - 4 * 1024 * 1024
SUBLANES = 8
MXU_DIM = 256

ROW_TILE = 512
SGU_TM = 1024
ROW_CHUNK = MXU_DIM
QKV_TM = 1024
FFN_TM = 1024
FFN_TF = 512
FFN_RC = 512
FFN_NC = 512
FFN_CAST_TF = MXU_DIM
ATTN_QBLOCKS = 8
DFT_TC = 1024
ADA_TN = 1024
NEG = -1e30
NO_CAP = float(np.finfo(np.float32).max)
LOG2E = float(np.log2(np.e))

DFT_N2 = 64


def _params(*sem):
    return pltpu.CompilerParams(dimension_semantics=sem, vmem_limit_bytes=VMEM_LIMIT_BYTES)


def _resident(shape, index_map):
    return pl.BlockSpec(shape, index_map, pipeline_mode=pl.Buffered(1))


def _modnorm(x, g, shift, scale):
    ms = jnp.mean(x * x, axis=-1, keepdims=True)
    y = x * lax.rsqrt(ms + EPS)
    return (y * g) * (1.0 + scale) + shift


def _silu(t):
    return t * jax.nn.sigmoid(t)


def _ada_kernel(cond_ref, w_ref, b_ref, o_ref):
    s = _silu(cond_ref[...]).astype(BF)
    o_ref[0] = jnp.dot(s, w_ref[0].astype(BF), preferred_element_type=F32) + b_ref[0]


def _ada_all(cond, w_ada, b_ada):
    tn = ADA_TN
    return pl.pallas_call(
        _ada_kernel,
        grid=(DEPTH, 6 * D_MODEL // tn),
        in_specs=[pl.BlockSpec((SUBLANES, D_MODEL), lambda l, j: (0, 0)),
                  pl.BlockSpec((1, D_MODEL, tn), lambda l, j: (l, 0, j)),
                  pl.BlockSpec((1, 1, tn), lambda l, j: (l, 0, j))],
        out_specs=pl.BlockSpec((1, SUBLANES, tn), lambda l, j: (l, 0, j)),
        out_shape=jax.ShapeDtypeStruct((DEPTH, SUBLANES, 6 * D_MODEL), F32),
        compiler_params=_params("arbitrary", "arbitrary"),
        name="ada",
    )(cond, w_ada, b_ada.reshape(DEPTH, 1, 6 * D_MODEL))


def _modnorm_kernel(x_ref, mod_ref, g_ref, o_ref):
    o_ref[0] = _modnorm(x_ref[0], g_ref[...], mod_ref[0, 0:1, :], mod_ref[0, 1:2, :])


def _modnorm_call(x, mod, g, tm):
    B, n, _ = x.shape
    return pl.pallas_call(
        _modnorm_kernel,
        grid=(B, n // tm),
        in_specs=[pl.BlockSpec((1, tm, D_MODEL), lambda b, i: (b, i, 0)),
                  pl.BlockSpec((1, 6, D_MODEL), lambda b, i: (b, 0, 0)),
                  pl.BlockSpec((1, D_MODEL), lambda b, i: (0, 0))],
        out_specs=pl.BlockSpec((1, tm, D_MODEL), lambda b, i: (b, i, 0)),
        out_shape=jax.ShapeDtypeStruct((B, n, D_MODEL), F32),
        compiler_params=_params("parallel", "parallel"),
        name="modnorm",
    )(x, mod, g)


def _qkv_kernel(x_ref, mod_ref, g_ref, w_ref, qg_ref, kg_ref, cos_ref, sin_ref, q_ref, k_ref, v_ref, *, rc):
    tm = x_ref.shape[1]
    qscale = (HEAD_DIM ** -0.5) * LOG2E
    cw = 4 * HEAD_DIM

    for r in range(tm // rc):
        rows = slice(r * rc, (r + 1) * rc)
        h = _modnorm(x_ref[0, rows], g_ref[...], mod_ref[0, 0:1, :], mod_ref[0, 1:2, :]).astype(BF)
        cos = cos_ref[rows]
        sin = sin_ref[rows]

        def head(t, gain):
            ms = jnp.mean(t * t, axis=-1, keepdims=True)
            t = (t * lax.rsqrt(ms + EPS)) * gain
            return t * cos + pltpu.roll(t, HEAD_DIM // 2, 1) * sin

        t = jnp.dot(h, w_ref[:, HQ:HQ + HKV], preferred_element_type=F32)
        for j in range(N_KV_HEADS):
            tj = head(t[:, j * HEAD_DIM:(j + 1) * HEAD_DIM], kg_ref[...])
            k_ref[0, rows, j * HEAD_DIM:(j + 1) * HEAD_DIM] = tj.astype(BF)
        for c in range(HQ // cw):
            t = jnp.dot(h, w_ref[:, c * cw:(c + 1) * cw], preferred_element_type=F32)
            for j in range(4):
                tj = head(t[:, j * HEAD_DIM:(j + 1) * HEAD_DIM], qg_ref[...]) * qscale
                q_ref[0, rows, c * cw + j * HEAD_DIM:c * cw + (j + 1) * HEAD_DIM] = tj.astype(BF)
        t = jnp.dot(h, w_ref[:, HQ + HKV:], preferred_element_type=F32)
        v_ref[0, rows] = t.astype(BF)


def _qkv_call(x, mod, g, w_bf, qg, kg, cos, sin, tm, rc):
    B, n, _ = x.shape
    row = lambda b, i: (b, i, 0)
    return pl.pallas_call(
        functools.partial(_qkv_kernel, rc=rc),
        grid=(B, n // tm),
        in_specs=[pl.BlockSpec((1, tm, D_MODEL), row),
                  pl.BlockSpec((1, 6, D_MODEL), lambda b, i: (b, 0, 0)),
                  pl.BlockSpec((1, D_MODEL), lambda b, i: (0, 0)),
                  _resident((D_MODEL, HQ + 2 * HKV), lambda b, i: (0, 0)),
                  pl.BlockSpec((1, HEAD_DIM), lambda b, i: (0, 0)),
                  pl.BlockSpec((1, HEAD_DIM), lambda b, i: (0, 0)),
                  pl.BlockSpec((tm, HEAD_DIM), lambda b, i: (i, 0)),
                  pl.BlockSpec((tm, HEAD_DIM), lambda b, i: (i, 0))],
        out_specs=[pl.BlockSpec((1, tm, HQ), row),
                   pl.BlockSpec((1, tm, HKV), row),
                   pl.BlockSpec((1, tm, HKV), row)],
        out_shape=[jax.ShapeDtypeStruct((B, n, HQ), BF),
                   jax.ShapeDtypeStruct((B, n, HKV), BF),
                   jax.ShapeDtypeStruct((B, n, HKV), BF)],
        compiler_params=_params("parallel", "parallel"),
        name="qkv",
    )(x, mod, g, w_bf, qg, kg, cos, sin)


def _softmax_pv(s, sink_col, vcat):
    m = jnp.maximum(jnp.max(s, axis=-1, keepdims=True), sink_col)
    p = jnp.exp2(s - m)
    den = jnp.sum(p, axis=-1, keepdims=True) + jnp.exp2(sink_col - m)
    return jnp.dot(p.astype(BF), vcat, preferred_element_type=F32) / den


def _sink_column(sink_ref, h, rows):
    ridx = lax.broadcasted_iota(jnp.int32, (KV_GROUP * rows, 1), 0)
    col = jnp.full((KV_GROUP * rows, 1), sink_ref[h * KV_GROUP] * LOG2E, F32)
    for g in range(1, KV_GROUP):
        col = jnp.where(ridx >= g * rows, sink_ref[h * KV_GROUP + g] * LOG2E, col)
    return col


def _stack_heads(q):
    return jnp.concatenate([q[:, g * HEAD_DIM:(g + 1) * HEAD_DIM] for g in range(KV_GROUP)], axis=0)


def _attn_kernel(sink_ref, band_ref, q_ref, kp_ref, kc_ref, kn_ref, vp_ref, vc_ref, vn_ref, kx_ref, vx_ref,
                 o_ref, *, n_ctx, qblocks):
    h = pl.program_id(1)
    i = pl.program_id(2)
    last = pl.num_programs(2) - 1
    kloc = jnp.concatenate([kp_ref[0], kc_ref[0], kn_ref[0]], axis=0)
    vloc = jnp.concatenate([vp_ref[0], vc_ref[0], vn_ref[0]], axis=0)
    sink_col = _sink_column(sink_ref, h, BLOCK)
    for t in range(qblocks):
        qs = _stack_heads(q_ref[0, t * BLOCK:(t + 1) * BLOCK, :])
        kcat = jnp.concatenate([kx_ref[0], kloc[t * BLOCK:(t + 3) * BLOCK]], axis=0)
        vcat = jnp.concatenate([vx_ref[0], vloc[t * BLOCK:(t + 3) * BLOCK]], axis=0)
        s = lax.dot_general(qs, kcat, (((1,), (1,)), ((), ())), preferred_element_type=F32)
        lo = jnp.minimum(s[:, n_ctx:n_ctx + BLOCK], band_ref[:, :BLOCK])
        hi = jnp.minimum(s[:, n_ctx + 2 * BLOCK:], band_ref[:, BLOCK:])
        if t == 0:
            lo = jnp.minimum(lo, jnp.where(i == 0, NEG, NO_CAP))
        if t == qblocks - 1:
            hi = jnp.minimum(hi, jnp.where(i == last, NEG, NO_CAP))
        s = jnp.concatenate([s[:, :n_ctx], lo, s[:, n_ctx + BLOCK:n_ctx + 2 * BLOCK], hi], axis=1)
        o = _softmax_pv(s, sink_col, vcat)
        for g in range(KV_GROUP):
            o_ref[0, t * BLOCK:(t + 1) * BLOCK, g * HEAD_DIM:(g + 1) * HEAD_DIM] = (
                o[g * BLOCK:(g + 1) * BLOCK].astype(BF))


def _band_bias():
    row = np.arange(KV_GROUP * BLOCK)[:, None] % BLOCK
    col = np.arange(BLOCK)[None, :]
    prev_ok = (row - (col - BLOCK)) <= WINDOW
    next_ok = ((col + BLOCK) - row) <= WINDOW
    return jnp.asarray(np.where(np.concatenate([prev_ok, next_ok], axis=1), NO_CAP, NEG), F32)


def _attn_call(q, k, v, kx, vx, sink, qblocks):
    B, n, _ = q.shape
    n_ctx = kx.shape[1]
    nb = n // BLOCK
    tq = qblocks * BLOCK
    gw = KV_GROUP * HEAD_DIM
    prev = lambda b, h, i: (b, jnp.maximum(i * qblocks - 1, 0), h)
    cur = lambda b, h, i: (b, i, h)
    nxt = lambda b, h, i: (b, jnp.minimum((i + 1) * qblocks, nb - 1), h)
    ctx = lambda b, h, i: (b, 0, h)
    edge = lambda m: pl.BlockSpec((1, BLOCK, HEAD_DIM), m)
    mid = pl.BlockSpec((1, tq, HEAD_DIM), cur)
    return pl.pallas_call(
        functools.partial(_attn_kernel, n_ctx=n_ctx, qblocks=qblocks),
        grid=(B, N_KV_HEADS, n // tq),
        in_specs=[pl.BlockSpec(memory_space=pltpu.SMEM),
                  pl.BlockSpec((KV_GROUP * BLOCK, 2 * BLOCK), lambda b, h, i: (0, 0)),
                  pl.BlockSpec((1, tq, gw), cur),
                  edge(prev), mid, edge(nxt), edge(prev), mid, edge(nxt),
                  pl.BlockSpec((1, n_ctx, HEAD_DIM), ctx),
                  pl.BlockSpec((1, n_ctx, HEAD_DIM), ctx)],
        out_specs=pl.BlockSpec((1, tq, gw), cur),
        out_shape=jax.ShapeDtypeStruct((B, n, HQ), BF),
        compiler_params=_params("parallel", "parallel", "arbitrary"),
        name="attn",
    )(sink, _band_bias(), q, k, k, k, v, v, v, kx, vx)


def _attn_ctx_kernel(sink_ref, q_ref, k_ref, v_ref, o_ref, *, n_ctx):
    h = pl.program_id(1)
    qs = _stack_heads(q_ref[0])
    s = lax.dot_general(qs, k_ref[0], (((1,), (1,)), ((), ())), preferred_element_type=F32)
    o = _softmax_pv(s, _sink_column(sink_ref, h, n_ctx), v_ref[0])
    for g in range(KV_GROUP):
        o_ref[0, :, g * HEAD_DIM:(g + 1) * HEAD_DIM] = o[g * n_ctx:(g + 1) * n_ctx].astype(BF)


def _attn_ctx_call(q, k, v, sink):
    B, n_ctx, _ = q.shape
    gw = KV_GROUP * HEAD_DIM
    idx = lambda b, h: (b, 0, h)
    return pl.pallas_call(
        functools.partial(_attn_ctx_kernel, n_ctx=n_ctx),
        grid=(B, N_KV_HEADS),
        in_specs=[pl.BlockSpec(memory_space=pltpu.SMEM),
                  pl.BlockSpec((1, n_ctx, gw), idx),
                  pl.BlockSpec((1, n_ctx, HEAD_DIM), idx),
                  pl.BlockSpec((1, n_ctx, HEAD_DIM), idx)],
        out_specs=pl.BlockSpec((1, n_ctx, gw), idx),
        out_shape=jax.ShapeDtypeStruct((B, n_ctx, HQ), BF),
        compiler_params=_params("parallel", "parallel"),
        name="attn_ctx",
    )(sink, q, k, v)


def _mm_res_kernel(a_ref, w_ref, x_ref, mod_ref, g2_ref, o_ref, h_ref, *, rc, packed_half):
    tm = x_ref.shape[1]
    for r in range(tm // rc):
        rows = slice(r * rc, (r + 1) * rc)
        if packed_half:
            hi, lo = _unpack_bf16_pair(a_ref[0, rows])
            pieces = []
            for c in range(hi.shape[1] // packed_half):
                cols = slice(c * packed_half, (c + 1) * packed_half)
                pieces += [hi[:, cols], lo[:, cols]]
            a = jnp.concatenate(pieces, axis=1)
        else:
            a = a_ref[0, rows].astype(BF)
        p = jnp.dot(a, w_ref[...], preferred_element_type=F32)
        xn = x_ref[0, rows] + mod_ref[0, 2:3, :] * p
        o_ref[0, rows] = xn
        h_ref[0, rows] = _modnorm(xn, g2_ref[...], mod_ref[0, 3:4, :], mod_ref[0, 4:5, :]).astype(BF)


def _mm_res_call(a, w_bf, x, mod, g2, tm, rc, packed_half=0):
    B, n, ka = a.shape
    K = w_bf.shape[0]
    row = lambda b, i: (b, i, 0)
    return pl.pallas_call(
        functools.partial(_mm_res_kernel, rc=rc, packed_half=packed_half),
        grid=(B, n // tm),
        in_specs=[pl.BlockSpec((1, tm, ka), row),
                  _resident((K, D_MODEL), lambda b, i: (0, 0)),
                  pl.BlockSpec((1, tm, D_MODEL), row),
                  pl.BlockSpec((1, 6, D_MODEL), lambda b, i: (b, 0, 0)),
                  pl.BlockSpec((1, D_MODEL), lambda b, i: (0, 0))],
        out_specs=[pl.BlockSpec((1, tm, D_MODEL), row),
                   pl.BlockSpec((1, tm, D_MODEL), row)],
        out_shape=[jax.ShapeDtypeStruct((B, n, D_MODEL), F32),
                   jax.ShapeDtypeStruct((B, n, D_MODEL), BF)],
        compiler_params=_params("parallel", "parallel"),
        name="mm_res",
    )(a, w_bf, x, mod, g2)


def _ffn_step(h_ref, x_ref, mod_ref, wg, wu, wd, o_ref, ssq_ref, rc, nc, first, last):
    tm = h_ref.shape[1]
    for r in range(tm // rc):
        rows = slice(r * rc, (r + 1) * rc)
        h = h_ref[0, rows]
        gate = jnp.dot(h, wg[...], preferred_element_type=F32)
        up = jnp.dot(h, wu[...], preferred_element_type=F32)
        a = (_silu(gate) * up).astype(BF)
        sq = None
        for c in range(D_MODEL // nc):
            cols = slice(c * nc, (c + 1) * nc)
            acc = jnp.dot(a, wd[:, cols], preferred_element_type=F32)
            if not first:
                acc = o_ref[0, rows, cols] + acc
            if last:
                acc = x_ref[0, rows, cols] + mod_ref[0, 5:6, cols] * acc
                if ssq_ref is not None:
                    part = jnp.sum(acc * acc, axis=-1, keepdims=True)
                    sq = part if sq is None else sq + part
            o_ref[0, rows, cols] = acc
        if sq is not None:
            ssq_ref[0, rows] = sq


def _ffn_steps(h_ref, x_ref, mod_ref, wg, wu, wd, o_ref, ssq_ref, rc, nc):
    f = pl.program_id(2)
    nf = pl.num_programs(2)

    @pl.when(f == 0)
    def _():
        _ffn_step(h_ref, x_ref, mod_ref, wg, wu, wd, o_ref, ssq_ref, rc, nc, True, False)

    @pl.when((f > 0) & (f < nf - 1))
    def _():
        _ffn_step(h_ref, x_ref, mod_ref, wg, wu, wd, o_ref, ssq_ref, rc, nc, False, False)

    @pl.when(f == nf - 1)
    def _():
        _ffn_step(h_ref, x_ref, mod_ref, wg, wu, wd, o_ref, ssq_ref, rc, nc, False, True)


def _ffn_kernel(h_ref, x_ref, mod_ref, wg_ref, wu_ref, wd_ref, o_ref, *maybe_ssq_ref, rc, nc):
    ssq_ref = maybe_ssq_ref[0] if maybe_ssq_ref else None
    _ffn_steps(h_ref, x_ref, mod_ref, wg_ref, wu_ref, wd_ref, o_ref, ssq_ref, rc, nc)


def _ffn_cast_kernel(h_ref, x_ref, mod_ref, wg_ref, wu_ref, wd_ref, o_ref, wgb_ref, wub_ref, wdb_ref, *, rc, nc):
    wg = wg_ref[0].astype(BF)
    wu = wu_ref[0].astype(BF)
    wd = wd_ref[0].astype(BF)
    wgb_ref[...] = wg
    wub_ref[...] = wu
    wdb_ref[...] = wd
    _ffn_steps(h_ref, x_ref, mod_ref, wg, wu, wd, o_ref, None, rc, nc)


def _ffn_specs(tm):
    row = lambda b, i, f: (b, i, 0)
    return [pl.BlockSpec((1, tm, D_MODEL), row),
            pl.BlockSpec((1, tm, D_MODEL), row),
            pl.BlockSpec((1, 6, D_MODEL), lambda b, i, f: (b, 0, 0))], pl.BlockSpec((1, tm, D_MODEL), row)


def _ffn_params():
    return pltpu.CompilerParams(dimension_semantics=("parallel", "parallel", "arbitrary"),
                                vmem_limit_bytes=BIG_VMEM_LIMIT_BYTES)


def _ffn_call(h, x, mod, wg_bf, wu_bf, wd_bf, tm, tf, rc, nc, emit_ssq):
    B, n, _ = x.shape
    act_specs, out_spec = _ffn_specs(tm)
    out_specs, out_shape = out_spec, jax.ShapeDtypeStruct((B, n, D_MODEL), F32)
    if emit_ssq:
        out_specs = [out_spec, pl.BlockSpec((1, tm, 1), lambda b, i, f: (b, i, 0))]
        out_shape = [out_shape, jax.ShapeDtypeStruct((B, n, 1), F32)]
    return pl.pallas_call(
        functools.partial(_ffn_kernel, rc=rc, nc=nc),
        grid=(B, n // tm, D_FF // tf),
        in_specs=act_specs + [pl.BlockSpec((D_MODEL, tf), lambda b, i, f: (0, f)),
                              pl.BlockSpec((D_MODEL, tf), lambda b, i, f: (0, f)),
                              pl.BlockSpec((tf, D_MODEL), lambda b, i, f: (f, 0))],
        out_specs=out_specs,
        out_shape=out_shape,
        compiler_params=_ffn_params(),
        name="ffn",
    )(h, x, mod, wg_bf, wu_bf, wd_bf)


def _ffn_cast_call(h, x, mod, wg_all, wu_all, wd_all, layer, tm, tf, rc, nc):
    B, n, _ = x.shape
    assert B == 1 and n == tm
    act_specs, out_spec = _ffn_specs(tm)
    up_in = pl.BlockSpec((1, D_MODEL, tf), lambda b, i, f: (layer, 0, f))
    down_in = pl.BlockSpec((1, tf, D_MODEL), lambda b, i, f: (layer, f, 0))
    up_out = pl.BlockSpec((D_MODEL, tf), lambda b, i, f: (0, f))
    down_out = pl.BlockSpec((tf, D_MODEL), lambda b, i, f: (f, 0))
    return pl.pallas_call(
        functools.partial(_ffn_cast_kernel, rc=rc, nc=nc),
        grid=(1, 1, D_FF // tf),
        in_specs=act_specs + [up_in, up_in, down_in],
        out_specs=[out_spec, up_out, up_out, down_out],
        out_shape=[jax.ShapeDtypeStruct((1, n, D_MODEL), F32),
                   jax.ShapeDtypeStruct((D_MODEL, D_FF), BF),
                   jax.ShapeDtypeStruct((D_MODEL, D_FF), BF),
                   jax.ShapeDtypeStruct((D_FF, D_MODEL), BF)],
        compiler_params=_ffn_params(),
        name="ffn_cast",
    )(h, x, mod, wg_all, wu_all, wd_all)


def _dft_tables(n):
    n1, n2, s = n // DFT_N2, DFT_N2, SUBLANES
    i32 = jnp.int32
    j = jnp.arange(n2 // s, dtype=i32)[:, None, None]
    r = jnp.arange(2 * n1, dtype=i32)[None, :, None]
    c = jnp.arange(n1 * s, dtype=i32)[None, None, :]
    part, k1, nn1, l = r // n1, r % n1, c // s, c % s
    idx = (k1 * (s * j + l) + n2 * k1 * nn1 + part * (n // 4)) % n
    m1 = jnp.cos(idx.astype(F32) * (2.0 * np.pi / n))
    same_l = (jnp.arange(s, dtype=i32)[:, None] == (jnp.arange(n1 * s, dtype=i32)[None, :] % s)).astype(F32)
    kron1 = (m1[:, :, None, :] * same_l[None, None, :, :]).reshape(n2 // s, 2 * n1 * s, n1 * s)
    r = jnp.arange(2 * n2, dtype=i32)[:, None]
    c = jnp.arange(2 * s * n2, dtype=i32)[None, :]
    qpart, k2, part, nn2 = r // n2, r % n2, c // (s * n2), c % n2
    idx2 = (k2 * nn2 + (n2 // 4) * (qpart - part) + n2) % n2
    m2 = jnp.cos(idx2.astype(F32) * (2.0 * np.pi / n2))
    same_l2 = (jnp.arange(s, dtype=i32)[:, None]
               == ((jnp.arange(2 * s * n2, dtype=i32)[None, :] // n2) % s)).astype(F32)
    kron2 = (m2[:, None, :] * same_l2[None, :, :]).reshape(2 * n2 * s, 2 * s * n2)
    return kron1.astype(BF), kron2.astype(BF)


def _channel_dft_table(n):
    c = np.arange(FOURIER_GROUP_DIM)
    ang = 2.0 * np.pi * ((c[:, None] * c[None, :]) % FOURIER_GROUP_DIM) / FOURIER_GROUP_DIM
    scale = 1.0 / np.sqrt(float(n) * FOURIER_GROUP_DIM)
    return jnp.asarray(np.concatenate([np.cos(ang), np.sin(ang)], axis=0) * scale, BF)


def _channel_mix(zr, zi, cs):
    outs = []
    for g in range(zr.shape[1] // FOURIER_GROUP_DIM):
        sl = slice(g * FOURIER_GROUP_DIM, (g + 1) * FOURIER_GROUP_DIM)
        zg = jnp.concatenate([zr[:, sl], zi[:, sl]], axis=1).astype(BF)
        outs.append(jnp.dot(zg, cs, preferred_element_type=F32))
    return jnp.concatenate(outs, axis=1)


def _dft1_kernel(x_ref, ssq_ref, mod_ref, g_ref, k_ref, t_ref):
    n1, s, tc = x_ref.shape[1], x_ref.shape[2], x_ref.shape[3]
    rstd = lax.rsqrt(ssq_ref[0].reshape(n1 * s, 1) * (1.0 / D_MODEL) + EPS)
    y = x_ref[0].reshape(n1 * s, tc) * rstd
    xb = ((y * g_ref[...]) * (1.0 + mod_ref[0, 1:2, :]) + mod_ref[0, 0:1, :]).astype(BF)
    t = jnp.dot(k_ref[0], xb, preferred_element_type=F32)
    rows = n1 * s
    t_ref[0] = _pack_bf16_pair(t[:rows], t[rows:]).reshape(n1, s, tc)


def _pack_bf16_pair(hi, lo):
    hi_bits = lax.bitcast_convert_type(hi.astype(BF).astype(F32), jnp.uint32)
    lo_bits = lax.bitcast_convert_type(lo.astype(BF).astype(F32), jnp.uint32)
    return hi_bits | lax.shift_right_logical(lo_bits, jnp.uint32(16))


def _unpack_bf16_pair(w):
    hi = lax.bitcast_convert_type(w & jnp.uint32(0xFFFF0000), F32)
    lo = lax.bitcast_convert_type(lax.shift_left(w, jnp.uint32(16)), F32)
    return hi.astype(BF), lo.astype(BF)


def _dft2_kernel(t_ref, k_ref, cs_ref, y_ref):
    s, half = y_ref.shape[2], y_ref.shape[3]
    rows = s * DFT_N2
    tr, ti = _unpack_bf16_pair(t_ref[0, 0])
    tb = jnp.concatenate([tr, ti], axis=0)
    z = jnp.dot(k_ref[...], tb, preferred_element_type=F32)
    y = _channel_mix(z[:rows], z[rows:], cs_ref[...])
    y_ref[0] = _pack_bf16_pair(y[:, :half], y[:, half:]).reshape(DFT_N2, s, half)


def _fourier_positions(x, ssq, mod, g, kron1, kron2, cs, tc):
    B, n, _ = x.shape
    n1, s = n // DFT_N2, SUBLANES
    nj = DFT_N2 // s
    t = pl.pallas_call(
        _dft1_kernel,
        grid=(nj, B, D_MODEL // tc),
        in_specs=[pl.BlockSpec((1, n1, s, tc), lambda j, b, c: (b, 0, j, c)),
                  pl.BlockSpec((1, n1, s, 1), lambda j, b, c: (b, 0, j, 0)),
                  pl.BlockSpec((1, 6, tc), lambda j, b, c: (b, 0, c)),
                  pl.BlockSpec((1, tc), lambda j, b, c: (0, c)),
                  pl.BlockSpec((1, 2 * n1 * s, n1 * s), lambda j, b, c: (j, 0, 0))],
        out_specs=pl.BlockSpec((1, n1, s, tc), lambda j, b, c: (b, 0, j, c)),
        out_shape=jax.ShapeDtypeStruct((B, n1, DFT_N2, D_MODEL), jnp.uint32),
        compiler_params=_params("arbitrary", "arbitrary", "arbitrary"),
        name="dft1",
    )(x.reshape(B, n1, DFT_N2, D_MODEL), ssq.reshape(B, n1, DFT_N2, 1), mod, g, kron1)
    na = n1 // s
    y = pl.pallas_call(
        _dft2_kernel,
        grid=(B, na, D_MODEL // tc),
        in_specs=[pl.BlockSpec((1, 1, s * DFT_N2, tc), lambda b, a, c: (b, a, 0, c)),
                  _resident((2 * DFT_N2 * s, 2 * s * DFT_N2), lambda b, a, c: (0, 0)),
                  _resident((2 * FOURIER_GROUP_DIM, FOURIER_GROUP_DIM), lambda b, a, c: (0, 0))],
        out_specs=pl.BlockSpec((1, DFT_N2, s, tc // 2), lambda b, a, c: (b, 0, a, c)),
        out_shape=jax.ShapeDtypeStruct((B, DFT_N2, n1, D_MODEL // 2), jnp.uint32),
        compiler_params=_params("parallel", "parallel", "parallel"),
        name="dft2",
    )(t.reshape(B, na, s * DFT_N2, D_MODEL), kron2, cs)
    return y.reshape(B, n, D_MODEL // 2)


def _dft_ctx_kernel(x_ref, f_ref, cs_ref, y_ref):
    n_ctx = x_ref.shape[1]
    z = jnp.dot(f_ref[...], x_ref[0].astype(BF), preferred_element_type=F32)
    y_ref[0] = _channel_mix(z[:n_ctx], z[n_ctx:], cs_ref[...])


def _fourier_ctx(hc, cs):
    B, n_ctx, _ = hc.shape
    p = np.arange(n_ctx)
    ang = 2.0 * np.pi * ((p[:, None] * p[None, :]) % n_ctx) / n_ctx
    fmat = jnp.asarray(np.concatenate([np.cos(ang), -np.sin(ang)], axis=0), BF)
    return pl.pallas_call(
        _dft_ctx_kernel,
        grid=(B,),
        in_specs=[pl.BlockSpec((1, n_ctx, D_MODEL), lambda b: (b, 0, 0)),
                  pl.BlockSpec((2 * n_ctx, n_ctx), lambda b: (0, 0)),
                  pl.BlockSpec((2 * FOURIER_GROUP_DIM, FOURIER_GROUP_DIM), lambda b: (0, 0))],
        out_specs=pl.BlockSpec((1, n_ctx, D_MODEL), lambda b: (b, 0, 0)),
        out_shape=jax.ShapeDtypeStruct((B, n_ctx, D_MODEL), F32),
        compiler_params=_params("parallel"),
        name="dft_ctx",
    )(hc, fmat, cs)


def _gelu(z):
    return 0.5 * z * (1.0 + lax.erf(z * (2.0 ** -0.5)))


def _sgu_in_kernel(x_ref, mod_ref, g_ref, win_ref, h_ref, v_ref, ssq_ref, *, cw, rc):
    s = pl.program_id(2)
    tm = x_ref.shape[1]

    def tile(first):
        for r in range(tm // rc):
            rows = slice(r * rc, (r + 1) * rc)
            if first:
                h = _modnorm(x_ref[0, rows], g_ref[...], mod_ref[0, 0:1, :], mod_ref[0, 1:2, :]).astype(BF)
                h_ref[0, rows] = h
            else:
                h = h_ref[0, rows]
            sq = None
            for c in range(win_ref.shape[1] // cw):
                cols = slice(c * cw, (c + 1) * cw)
                z = _gelu(jnp.dot(h, win_ref[:, cols], preferred_element_type=F32))
                v_ref[0, rows, cols] = z.astype(BF)
                part = jnp.sum(z * z, axis=-1, keepdims=True)
                sq = part if sq is None else sq + part
            ssq_ref[0, rows] = sq if first else ssq_ref[0, rows] + sq

    @pl.when(s == 0)
    def _():
        tile(True)

    @pl.when(s > 0)
    def _():
        tile(False)


def _sgu_in_call(x, mod, g, win_bf, tm):
    B, n, _ = x.shape
    gpt = 2
    tw = gpt * SGU_GROUP_DIM
    n_v = SGU_HALF // tw
    row = lambda b, i, s: (b, i, 0)
    return pl.pallas_call(
        functools.partial(_sgu_in_kernel, cw=MXU_DIM, rc=min(tm, FFN_RC)),
        grid=(B, n // tm, n_v),
        in_specs=[pl.BlockSpec((1, tm, D_MODEL), row),
                  pl.BlockSpec((1, 6, D_MODEL), lambda b, i, s: (b, 0, 0)),
                  pl.BlockSpec((1, D_MODEL), lambda b, i, s: (0, 0)),
                  pl.BlockSpec((D_MODEL, tw), lambda b, i, s: (0, n_v + s))],
        out_specs=[pl.BlockSpec((1, tm, D_MODEL), row),
                   pl.BlockSpec((1, tm, tw), lambda b, i, s: (b, i, s)),
                   pl.BlockSpec((1, tm, 1), row)],
        out_shape=[jax.ShapeDtypeStruct((B, n, D_MODEL), BF),
                   jax.ShapeDtypeStruct((B, n, SGU_HALF), BF),
                   jax.ShapeDtypeStruct((B, n, 1), F32)],
        compiler_params=_params("parallel", "parallel", "arbitrary"),
        name="sgu_in",
    )(x, mod, g, win_bf)


def _sgu_out_kernel(h_ref, v_ref, ssq_ref, x_ref, mod_ref, g2_ref, win_ref, gv_ref, ws_ref, bs_ref, wout_ref,
                    o_ref, h2_ref, a0_ref, a1_ref, *, cw, nc, rc, fr):
    t = pl.program_id(2)
    ng = N_SGU_GROUPS
    gd = SGU_GROUP_DIM
    tm = h_ref.shape[1]

    def out_proj(a_ref):
        for r in range(tm // rc):
            rows = slice(r * rc, (r + 1) * rc)
            a = a_ref[rows, :]
            for c in range(D_MODEL // nc):
                cols = slice(c * nc, (c + 1) * nc)
                o_ref[0, rows, cols] += jnp.dot(a, wout_ref[:, cols], preferred_element_type=F32)

    def prep(a_ref):
        for r in range(tm // rc):
            rows = slice(r * rc, (r + 1) * rc)
            h = h_ref[0, rows]
            for c in range(gd // cw):
                cols = slice(c * cw, (c + 1) * cw)
                u = _gelu(jnp.dot(h, win_ref[:, cols], preferred_element_type=F32))
                a_ref[rows, cols] = u.astype(BF)
        rstd = lax.rsqrt(ssq_ref[0] * (1.0 / SGU_HALF) + EPS)
        for c in range(tm // SGU_CHUNK):
            rows = slice(c * SGU_CHUNK, (c + 1) * SGU_CHUNK)
            vn = ((v_ref[0, rows, :].astype(F32) * rstd[rows]) * gv_ref[0]).astype(BF)
            sp = jnp.dot(ws_ref[0], vn, preferred_element_type=F32) + bs_ref[0]
            a_ref[rows, :] = (a_ref[rows, :].astype(F32) * sp).astype(BF)

    @pl.when(t == 0)
    def _():
        o_ref[0] = jnp.zeros(o_ref.shape[1:], F32)
        prep(a0_ref)

    @pl.when((t >= 1) & (t < ng) & ((t & 1) == 1))
    def _():
        out_proj(a0_ref)
        prep(a1_ref)

    @pl.when((t >= 1) & (t < ng) & ((t & 1) == 0))
    def _():
        out_proj(a1_ref)
        prep(a0_ref)

    @pl.when(t == ng)
    def _():
        out_proj(a1_ref if (ng - 1) % 2 == 1 else a0_ref)

    @pl.when(t > ng)
    def _():
        rows = pl.ds(pl.multiple_of((t - ng - 1) * fr, fr), fr)
        xn = x_ref[0] + mod_ref[0, 2:3, :] * o_ref[0, rows, :]
        o_ref[0, rows, :] = xn
        h2_ref[0, rows, :] = _modnorm(xn, g2_ref[...], mod_ref[0, 3:4, :], mod_ref[0, 4:5, :]).astype(BF)


def _sgu_out_call(h, v, ssq, x, mod, g2, win_bf, gv, ws_bf, bs, wout_bf, tm):
    B, n, _ = x.shape
    ng, gd = N_SGU_GROUPS, SGU_GROUP_DIM
    fr = ROW_CHUNK
    nfin = tm // fr
    row = lambda b, i, t: (b, i, 0)
    grp = lambda b, i, t: (jnp.clip(t, 0, ng - 1), 0, 0)
    return pl.pallas_call(
        functools.partial(_sgu_out_kernel, cw=MXU_DIM, nc=FFN_NC, rc=min(tm, FFN_RC), fr=fr),
        grid=(B, n // tm, ng + 1 + nfin),
        in_specs=[pl.BlockSpec((1, tm, D_MODEL), row),
                  pl.BlockSpec((1, tm, gd), lambda b, i, t: (b, i, jnp.clip(t, 0, ng - 1))),
                  pl.BlockSpec((1, tm, 1), row),
                  pl.BlockSpec((1, fr, D_MODEL), lambda b, i, t: (b, i * nfin + jnp.clip(t - ng - 1, 0, nfin - 1), 0)),
                  pl.BlockSpec((1, 6, D_MODEL), lambda b, i, t: (b, 0, 0)),
                  pl.BlockSpec((1, D_MODEL), lambda b, i, t: (0, 0)),
                  pl.BlockSpec((D_MODEL, gd), lambda b, i, t: (0, jnp.clip(t, 0, ng - 1))),
                  pl.BlockSpec((1, 1, gd), grp),
                  pl.BlockSpec((1, SGU_CHUNK, SGU_CHUNK), grp),
                  pl.BlockSpec((1, SGU_CHUNK, 1), grp),
                  pl.BlockSpec((gd, D_MODEL), lambda b, i, t: (jnp.clip(t - 1, 0, ng - 1), 0))],
        out_specs=[pl.BlockSpec((1, tm, D_MODEL), row),
                   pl.BlockSpec((1, tm, D_MODEL), row)],
        out_shape=[jax.ShapeDtypeStruct((B, n, D_MODEL), F32),
                   jax.ShapeDtypeStruct((B, n, D_MODEL), BF)],
        scratch_shapes=[pltpu.VMEM((tm, gd), BF),
                        pltpu.VMEM((tm, gd), BF)],
        compiler_params=pltpu.CompilerParams(dimension_semantics=("parallel", "parallel", "arbitrary"),
                                             vmem_limit_bytes=BIG_VMEM_LIMIT_BYTES),
        name="sgu_out",
    )(h, v, ssq, x, mod, g2, win_bf, gv.reshape(ng, 1, gd), ws_bf, bs.reshape(ng, SGU_CHUNK, 1), wout_bf)


def _sgu_call(x, mod, g, g2, win_bf, gv, ws_bf, bs, wout_bf, tm):
    h, v, ssq = _sgu_in_call(x, mod, g, win_bf, tm)
    return _sgu_out_call(h, v, ssq, x, mod, g2, win_bf, gv, ws_bf, bs, wout_bf, tm)


def _rope_tables(n):
    row, col = jnp.meshgrid(jnp.arange(n // GRID_W), jnp.arange(GRID_W), indexing="ij")
    n_freq = HEAD_DIM // 4
    inv_freq = ROPE_BASE ** (-jnp.arange(n_freq, dtype=F32) / n_freq)
    ang = jnp.concatenate([row.reshape(-1, 1).astype(F32) * inv_freq,
                           col.reshape(-1, 1).astype(F32) * inv_freq], axis=-1)
    ang = jnp.concatenate([ang, ang], axis=-1)
    sign = jnp.where(jnp.arange(HEAD_DIM) < HEAD_DIM // 2, -1.0, 1.0).astype(F32)
    return jnp.cos(ang), jnp.sin(ang) * sign


def kernel(x, c, ctx, c_ctx, w_ada, b_ada, norm_g, w_ffn_gate, w_ffn_up, w_ffn_down, w_attn_qkv,
           w_attn_o, attn_q_g, attn_k_g, attn_sink, w_fourier, w_sgu_in, sgu_v_g, w_sgu_spatial,
           b_sgu_spatial, w_sgu_out):
    B, n, _ = x.shape
    n_ctx = ctx.shape[1]
    tm_x = ROW_TILE
    tm_c = B * n_ctx
    ctx = ctx.reshape(1, tm_c, D_MODEL)

    cond = jnp.zeros((SUBLANES, D_MODEL), F32).at[:B].set(c).at[B].set(c_ctx)
    ada = _ada_all(cond, w_ada, b_ada)
    cos, sin = _rope_tables(n)
    cos_c = jnp.ones((tm_c, HEAD_DIM), F32)
    sin_c = jnp.zeros((tm_c, HEAD_DIM), F32)

    for i in range(DEPTH):
        kind = i % N_MIXERS
        j = i // N_MIXERS
        need_ctx = i < DEPTH - 1
        modx = ada[i, :B].reshape(B, 6, D_MODEL)
        modc = ada[i, B].reshape(1, 6, D_MODEL)
        g1 = norm_g[i, 0].reshape(1, D_MODEL)
        g2 = norm_g[i, 1].reshape(1, D_MODEL)
        if kind == 0:
            wqkv = w_attn_qkv[j].astype(BF)
            wo = w_attn_o[j].astype(BF)
            qg = attn_q_g[j].reshape(1, HEAD_DIM)
            kg = attn_k_g[j].reshape(1, HEAD_DIM)
            sink = attn_sink[j]
            q, k, v = _qkv_call(x, modx, g1, wqkv, qg, kg, cos, sin, QKV_TM, ROW_CHUNK)
            qc, kc, vc = _qkv_call(ctx, modc, g1, wqkv, qg, kg, cos_c, sin_c, tm_c, ROW_CHUNK)
            kc = kc.reshape(B, n_ctx, HKV)
            vc = vc.reshape(B, n_ctx, HKV)
            ox = _attn_call(q, k, v, kc, vc, sink, ATTN_QBLOCKS)
            x, hx2 = _mm_res_call(ox, wo, x, modx, g2, tm_x, ROW_CHUNK)
            if need_ctx:
                oc = _attn_ctx_call(qc.reshape(B, n_ctx, HQ), kc, vc, sink).reshape(1, tm_c, HQ)
                ctx, hc2 = _mm_res_call(oc, wo, ctx, modc, g2, tm_c, ROW_CHUNK)
        elif kind == 1:
            wf = w_fourier[j].astype(BF)
            kron1, kron2 = _dft_tables(n)
            yx = _fourier_positions(x, ssq_x, modx, g1, kron1, kron2, _channel_dft_table(n), DFT_TC)
            x, hx2 = _mm_res_call(yx, wf, x, modx, g2, tm_x, ROW_CHUNK, packed_half=DFT_TC // 2)
            if need_ctx:
                hc = _modnorm_call(ctx, modc, g1, tm_c).reshape(B, n_ctx, D_MODEL)
                yc = _fourier_ctx(hc, _channel_dft_table(n_ctx)).reshape(1, tm_c, D_MODEL)
                ctx, hc2 = _mm_res_call(yc, wf, ctx, modc, g2, tm_c, ROW_CHUNK)
        else:
            win = w_sgu_in[j].astype(BF)
            wout = w_sgu_out[j].astype(BF)
            ws = w_sgu_spatial[j].astype(BF)
            args = (win, sgu_v_g[j], ws, b_sgu_spatial[j], wout)
            x, hx2 = _sgu_call(x, modx, g1, g2, *args, SGU_TM)
            if need_ctx:
                ctx, hc2 = _sgu_call(ctx, modc, g1, g2, *args, tm_c)
        if need_ctx:
            ctx, wg, wu, wd = _ffn_cast_call(hc2, ctx, modc, w_ffn_gate, w_ffn_up, w_ffn_down, i,
                                             tm_c, FFN_CAST_TF, tm_c, FFN_NC)
        else:
            wg = w_ffn_gate[i].astype(BF)
            wu = w_ffn_up[i].astype(BF)
            wd = w_ffn_down[i].astype(BF)
        if (i + 1) % N_MIXERS == 1 and i + 1 < DEPTH:
            x, ssq_x = _ffn_call(hx2, x, modx, wg, wu, wd, FFN_TM, FFN_TF, FFN_RC, FFN_NC, True)
        else:
            x = _ffn_call(hx2, x, modx, wg, wu, wd, FFN_TM, FFN_TF, FFN_RC, FFN_NC, False)
    return x
```

```python
import functools

import numpy as np
import jax
import jax.numpy as jnp
from jax import lax
from jax.experimental import pallas as pl
from jax.experimental.pallas import tpu as pltpu

D_MODEL = 2048
DEPTH = 4
GRID_W = 64
N_MIXERS = 3
EPS = 1e-6
N_HEADS = 16
N_KV_HEADS = 4
HEAD_DIM = D_MODEL // N_HEADS
KV_GROUP = N_HEADS // N_KV_HEADS
HQ = N_HEADS * HEAD_DIM
HKV = N_KV_HEADS * HEAD_DIM
WINDOW = 128
BLOCK = 128
ROPE_BASE = 10000.0
N_FOURIER_GROUPS = 8
FOURIER_GROUP_DIM = D_MODEL // N_FOURIER_GROUPS
SGU_CHUNK = 128
SGU_HALF = 3 * D_MODEL
N_SGU_GROUPS = 8
SGU_GROUP_DIM = SGU_HALF // N_SGU_GROUPS
D_FF = 5632

BF = jnp.bfloat16
F32 = jnp.float32

V7X_VMEM_BYTES = 64 * 1024 * 1024
VMEM_LIMIT_BYTES = V7X_VMEM_BYTES - 8 * 1024 * 1024
BIG_VMEM_LIMIT_BYTES = V7X_VMEM_BYTES - 4 * 1024 * 1024
SUBLANES = 8
MXU_DIM = 256

ROW_TILE = 512
SGU_TM = 1024
ROW_CHUNK = MXU_DIM
QKV_TM = 1024
FFN_TM = 1024
FFN_TF = 512
FFN_RC = 512
FFN_NC = 512
FFN_CAST_TF = MXU_DIM
ATTN_QBLOCKS = 8
DFT_TC = 1024
ADA_TN = 1024
NEG = -1e30
NO_CAP = float(np.finfo(np.float32).max)
LOG2E = float(np.log2(np.e))

DFT_N2 = 64


def _params(*sem):
    return pltpu.CompilerParams(dimension_semantics=sem, vmem_limit_bytes=VMEM_LIMIT_BYTES)


def _resident(shape, index_map):
    return pl.BlockSpec(shape, index_map, pipeline_mode=pl.Buffered(1))


def _modnorm(x, g, shift, scale):
    ms = jnp.mean(x * x, axis=-1, keepdims=True)
    y = x * lax.rsqrt(ms + EPS)
    return (y * g) * (1.0 + scale) + shift


def _silu(t):
    return t * jax.nn.sigmoid(t)


def _ada_kernel(cond_ref, w_ref, b_ref, o_ref):
    s = _silu(cond_ref[...]).astype(BF)
    o_ref[0] = jnp.dot(s, w_ref[0].astype(BF), preferred_element_type=F32) + b_ref[0]


def _ada_all(cond, w_ada, b_ada):
    tn = ADA_TN
    return pl.pallas_call(
        _ada_kernel,
        grid=(DEPTH, 6 * D_MODEL // tn),
        in_specs=[pl.BlockSpec((SUBLANES, D_MODEL), lambda l, j: (0, 0)),
                  pl.BlockSpec((1, D_MODEL, tn), lambda l, j: (l, 0, j)),
                  pl.BlockSpec((1, 1, tn), lambda l, j: (l, 0, j))],
        out_specs=pl.BlockSpec((1, SUBLANES, tn), lambda l, j: (l, 0, j)),
        out_shape=jax.ShapeDtypeStruct((DEPTH, SUBLANES, 6 * D_MODEL), F32),
        compiler_params=_params("arbitrary", "arbitrary"),
        name="ada",
    )(cond, w_ada, b_ada.reshape(DEPTH, 1, 6 * D_MODEL))


def _modnorm_kernel(x_ref, mod_ref, g_ref, o_ref):
    o_ref[0] = _modnorm(x_ref[0], g_ref[...], mod_ref[0, 0:1, :], mod_ref[0, 1:2, :])


def _modnorm_call(x, mod, g, tm):
    B, n, _ = x.shape
    return pl.pallas_call(
        _modnorm_kernel,
        grid=(B, n // tm),
        in_specs=[pl.BlockSpec((1, tm, D_MODEL), lambda b, i: (b, i, 0)),
                  pl.BlockSpec((1, 6, D_MODEL), lambda b, i: (b, 0, 0)),
                  pl.BlockSpec((1, D_MODEL), lambda b, i: (0, 0))],
        out_specs=pl.BlockSpec((1, tm, D_MODEL), lambda b, i: (b, i, 0)),
        out_shape=jax.ShapeDtypeStruct((B, n, D_MODEL), F32),
        compiler_params=_params("parallel", "parallel"),
        name="modnorm",
    )(x, mod, g)


def _qkv_kernel(x_ref, mod_ref, g_ref, w_ref, qg_ref, kg_ref, cos_ref, sin_ref, q_ref, k_ref, v_ref, *, rc):
    tm = x_ref.shape[1]
    qscale = (HEAD_DIM ** -0.5) * LOG2E
    cw = 4 * HEAD_DIM

    for r in range(tm // rc):
        rows = slice(r * rc, (r + 1) * rc)
        h = _modnorm(x_ref[0, rows], g_ref[...], mod_ref[0, 0:1, :], mod_ref[0, 1:2, :]).astype(BF)
        cos = cos_ref[rows]
        sin = sin_ref[rows]

        def head(t, gain):
            ms = jnp.mean(t * t, axis=-1, keepdims=True)
            t = (t * lax.rsqrt(ms + EPS)) * gain
            return t * cos + pltpu.roll(t, HEAD_DIM // 2, 1) * sin

        t = jnp.dot(h, w_ref[:, HQ:HQ + HKV], preferred_element_type=F32)
        for j in range(N_KV_HEADS):
            tj = head(t[:, j * HEAD_DIM:(j + 1) * HEAD_DIM], kg_ref[...])
            k_ref[0, rows, j * HEAD_DIM:(j + 1) * HEAD_DIM] = tj.astype(BF)
        for c in range(HQ // cw):
            t = jnp.dot(h, w_ref[:, c * cw:(c + 1) * cw], preferred_element_type=F32)
            for j in range(4):
                tj = head(t[:, j * HEAD_DIM:(j + 1) * HEAD_DIM], qg_ref[...]) * qscale
                q_ref[0, rows, c * cw + j * HEAD_DIM:c * cw + (j + 1) * HEAD_DIM] = tj.astype(BF)
        t = jnp.dot(h, w_ref[:, HQ + HKV:], preferred_element_type=F32)
        v_ref[0, rows] = t.astype(BF)


def _qkv_call(x, mod, g, w_bf, qg, kg, cos, sin, tm, rc):
    B, n, _ = x.shape
    row = lambda b, i: (b, i, 0)
    return pl.pallas_call(
        functools.partial(_qkv_kernel, rc=rc),
        grid=(B, n // tm),
        in_specs=[pl.BlockSpec((1, tm, D_MODEL), row),
                  pl.BlockSpec((1, 6, D_MODEL), lambda b, i: (b, 0, 0)),
                  pl.BlockSpec((1, D_MODEL), lambda b, i: (0, 0)),
                  _resident((D_MODEL, HQ + 2 * HKV), lambda b, i: (0, 0)),
                  pl.BlockSpec((1, HEAD_DIM), lambda b, i: (0, 0)),
                  pl.BlockSpec((1, HEAD_DIM), lambda b, i: (0, 0)),
                  pl.BlockSpec((tm, HEAD_DIM), lambda b, i: (i, 0)),
                  pl.BlockSpec((tm, HEAD_DIM), lambda b, i: (i, 0))],
        out_specs=[pl.BlockSpec((1, tm, HQ), row),
                   pl.BlockSpec((1, tm, HKV), row),
                   pl.BlockSpec((1, tm, HKV), row)],
        out_shape=[jax.ShapeDtypeStruct((B, n, HQ), BF),
                   jax.ShapeDtypeStruct((B, n, HKV), BF),
                   jax.ShapeDtypeStruct((B, n, HKV), BF)],
        compiler_params=_params("parallel", "parallel"),
        name="qkv",
    )(x, mod, g, w_bf, qg, kg, cos, sin)


def _softmax_pv(s, sink_col, vcat):
    m = jnp.maximum(jnp.max(s, axis=-1, keepdims=True), sink_col)
    p = jnp.exp2(s - m)
    den = jnp.sum(p, axis=-1, keepdims=True) + jnp.exp2(sink_col - m)
    return jnp.dot(p.astype(BF), vcat, preferred_element_type=F32) / den


def _sink_column(sink_ref, h, rows):
    ridx = lax.broadcasted_iota(jnp.int32, (KV_GROUP * rows, 1), 0)
    col = jnp.full((KV_GROUP * rows, 1), sink_ref[h * KV_GROUP] * LOG2E, F32)
    for g in range(1, KV_GROUP):
        col = jnp.where(ridx >= g * rows, sink_ref[h * KV_GROUP + g] * LOG2E, col)
    return col


def _stack_heads(q):
    return jnp.concatenate([q[:, g * HEAD_DIM:(g + 1) * HEAD_DIM] for g in range(KV_GROUP)], axis=0)


def _attn_kernel(sink_ref, band_ref, q_ref, kp_ref, kc_ref, kn_ref, vp_ref, vc_ref, vn_ref, kx_ref, vx_ref,
                 o_ref, *, n_ctx, qblocks):
    h = pl.program_id(1)
    i = pl.program_id(2)
    last = pl.num_programs(2) - 1
    kloc = jnp.concatenate([kp_ref[0], kc_ref[0], kn_ref[0]], axis=0)
    vloc = jnp.concatenate([vp_ref[0], vc_ref[0], vn_ref[0]], axis=0)
    sink_col = _sink_column(sink_ref, h, BLOCK)
    for t in range(qblocks):
        qs = _stack_heads(q_ref[0, t * BLOCK:(t + 1) * BLOCK, :])
        kcat = jnp.concatenate([kx_ref[0], kloc[t * BLOCK:(t + 3) * BLOCK]], axis=0)
        vcat = jnp.concatenate([vx_ref[0], vloc[t * BLOCK:(t + 3) * BLOCK]], axis=0)
        s = lax.dot_general(qs, kcat, (((1,), (1,)), ((), ())), preferred_element_type=F32)
        lo = jnp.minimum(s[:, n_ctx:n_ctx + BLOCK], band_ref[:, :BLOCK])
        hi = jnp.minimum(s[:, n_ctx + 2 * BLOCK:], band_ref[:, BLOCK:])
        if t == 0:
            lo = jnp.minimum(lo, jnp.where(i == 0, NEG, NO_CAP))
        if t == qblocks - 1:
            hi = jnp.minimum(hi, jnp.where(i == last, NEG, NO_CAP))
        s = jnp.concatenate([s[:, :n_ctx], lo, s[:, n_ctx + BLOCK:n_ctx + 2 * BLOCK], hi], axis=1)
        o = _softmax_pv(s, sink_col, vcat)
        for g in range(KV_GROUP):
            o_ref[0, t * BLOCK:(t + 1) * BLOCK, g * HEAD_DIM:(g + 1) * HEAD_DIM] = (
                o[g * BLOCK:(g + 1) * BLOCK].astype(BF))


def _band_bias():
    row = np.arange(KV_GROUP * BLOCK)[:, None] % BLOCK
    col = np.arange(BLOCK)[None, :]
    prev_ok = (row - (col - BLOCK)) <= WINDOW
    next_ok = ((col + BLOCK) - row) <= WINDOW
    return jnp.asarray(np.where(np.concatenate([prev_ok, next_ok], axis=1), NO_CAP, NEG), F32)


def _attn_call(q, k, v, kx, vx, sink, qblocks):
    B, n, _ = q.shape
    n_ctx = kx.shape[1]
    nb = n // BLOCK
    tq = qblocks * BLOCK
    gw = KV_GROUP * HEAD_DIM
    prev = lambda b, h, i: (b, jnp.maximum(i * qblocks - 1, 0), h)
    cur = lambda b, h, i: (b, i, h)
    nxt = lambda b, h, i: (b, jnp.minimum((i + 1) * qblocks, nb - 1), h)
    ctx = lambda b, h, i: (b, 0, h)
    edge = lambda m: pl.BlockSpec((1, BLOCK, HEAD_DIM), m)
    mid = pl.BlockSpec((1, tq, HEAD_DIM), cur)
    return pl.pallas_call(
        functools.partial(_attn_kernel, n_ctx=n_ctx, qblocks=qblocks),
        grid=(B, N_KV_HEADS, n // tq),
        in_specs=[pl.BlockSpec(memory_space=pltpu.SMEM),
                  pl.BlockSpec((KV_GROUP * BLOCK, 2 * BLOCK), lambda b, h, i: (0, 0)),
                  pl.BlockSpec((1, tq, gw), cur),
                  edge(prev), mid, edge(nxt), edge(prev), mid, edge(nxt),
                  pl.BlockSpec((1, n_ctx, HEAD_DIM), ctx),
                  pl.BlockSpec((1, n_ctx, HEAD_DIM), ctx)],
        out_specs=pl.BlockSpec((1, tq, gw), cur),
        out_shape=jax.ShapeDtypeStruct((B, n, HQ), BF),
        compiler_params=_params("parallel", "parallel", "arbitrary"),
        name="attn",
    )(sink, _band_bias(), q, k, k, k, v, v, v, kx, vx)


def _attn_ctx_kernel(sink_ref, q_ref, k_ref, v_ref, o_ref, *, n_ctx):
    h = pl.program_id(1)
    qs = _stack_heads(q_ref[0])
    s = lax.dot_general(qs, k_ref[0], (((1,), (1,)), ((), ())), preferred_element_type=F32)
    o = _softmax_pv(s, _sink_column(sink_ref, h, n_ctx), v_ref[0])
    for g in range(KV_GROUP):
        o_ref[0, :, g * HEAD_DIM:(g + 1) * HEAD_DIM] = o[g * n_ctx:(g + 1) * n_ctx].astype(BF)


def _attn_ctx_call(q, k, v, sink):
    B, n_ctx, _ = q.shape
    gw = KV_GROUP * HEAD_DIM
    idx = lambda b, h: (b, 0, h)
    return pl.pallas_call(
        functools.partial(_attn_ctx_kernel, n_ctx=n_ctx),
        grid=(B, N_KV_HEADS),
        in_specs=[pl.BlockSpec(memory_space=pltpu.SMEM),
                  pl.BlockSpec((1, n_ctx, gw), idx),
                  pl.BlockSpec((1, n_ctx, HEAD_DIM), idx),
                  pl.BlockSpec((1, n_ctx, HEAD_DIM), idx)],
        out_specs=pl.BlockSpec((1, n_ctx, gw), idx),
        out_shape=jax.ShapeDtypeStruct((B, n_ctx, HQ), BF),
        compiler_params=_params("parallel", "parallel"),
        name="attn_ctx",
    )(sink, q, k, v)


def _mm_res_kernel(a_ref, w_ref, x_ref, mod_ref, g2_ref, o_ref, h_ref, *, rc):
    tm = x_ref.shape[1]
    for r in range(tm // rc):
        rows = slice(r * rc, (r + 1) * rc)
        p = jnp.dot(a_ref[0, rows].astype(BF), w_ref[...], preferred_element_type=F32)
        xn = x_ref[0, rows] + mod_ref[0, 2:3, :] * p
        o_ref[0, rows] = xn
        h_ref[0, rows] = _modnorm(xn, g2_ref[...], mod_ref[0, 3:4, :], mod_ref[0, 4:5, :]).astype(BF)


def _mm_res_call(a, w_bf, x, mod, g2, tm, rc):
    B, n, K = a.shape
    row = lambda b, i: (b, i, 0)
    return pl.pallas_call(
        functools.partial(_mm_res_kernel, rc=rc),
        grid=(B, n // tm),
        in_specs=[pl.BlockSpec((1, tm, K), row),
                  _resident((K, D_MODEL), lambda b, i: (0, 0)),
                  pl.BlockSpec((1, tm, D_MODEL), row),
                  pl.BlockSpec((1, 6, D_MODEL), lambda b, i: (b, 0, 0)),
                  pl.BlockSpec((1, D_MODEL), lambda b, i: (0, 0))],
        out_specs=[pl.BlockSpec((1, tm, D_MODEL), row),
                   pl.BlockSpec((1, tm, D_MODEL), row)],
        out_shape=[jax.ShapeDtypeStruct((B, n, D_MODEL), F32),
                   jax.ShapeDtypeStruct((B, n, D_MODEL), BF)],
        compiler_params=_params("parallel", "parallel"),
        name="mm_res",
    )(a, w_bf, x, mod, g2)


def _ffn_step(h_ref, x_ref, mod_ref, wg, wu, wd, o_ref, ssq_ref, rc, nc, first, last):
    tm = h_ref.shape[1]
    for r in range(tm // rc):
        rows = slice(r * rc, (r + 1) * rc)
        h = h_ref[0, rows]
        gate = jnp.dot(h, wg[...], preferred_element_type=F32)
        up = jnp.dot(h, wu[...], preferred_element_type=F32)
        a = (_silu(gate) * up).astype(BF)
        sq = None
        for c in range(D_MODEL // nc):
            cols = slice(c * nc, (c + 1) * nc)
            acc = jnp.dot(a, wd[:, cols], preferred_element_type=F32)
            if not first:
                acc = o_ref[0, rows, cols] + acc
            if last:
                acc = x_ref[0, rows, cols] + mod_ref[0, 5:6, cols] * acc
                if ssq_ref is not None:
                    part = jnp.sum(acc * acc, axis=-1, keepdims=True)
                    sq = part if sq is None else sq + part
            o_ref[0, rows, cols] = acc
        if sq is not None:
            ssq_ref[0, rows] = sq


def _ffn_steps(h_ref, x_ref, mod_ref, wg, wu, wd, o_ref, ssq_ref, rc, nc):
    f = pl.program_id(2)
    nf = pl.num_programs(2)

    @pl.when(f == 0)
    def _():
        _ffn_step(h_ref, x_ref, mod_ref, wg, wu, wd, o_ref, ssq_ref, rc, nc, True, False)

    @pl.when((f > 0) & (f < nf - 1))
    def _():
        _ffn_step(h_ref, x_ref, mod_ref, wg, wu, wd, o_ref, ssq_ref, rc, nc, False, False)

    @pl.when(f == nf - 1)
    def _():
        _ffn_step(h_ref, x_ref, mod_ref, wg, wu, wd, o_ref, ssq_ref, rc, nc, False, True)


def _ffn_kernel(h_ref, x_ref, mod_ref, wg_ref, wu_ref, wd_ref, *rest, rc, nc, emit_ssq, side_cast):
    rest = list(rest)
    side_in = [rest.pop(0) for _ in range(3)] if side_cast else []
    o_ref = rest.pop(0)
    ssq_ref = rest.pop(0) if emit_ssq else None
    for src, dst in zip(side_in, rest):
        dst[...] = src[0].astype(BF)
    _ffn_steps(h_ref, x_ref, mod_ref, wg_ref, wu_ref, wd_ref, o_ref, ssq_ref, rc, nc)


def _ffn_cast_kernel(h_ref, x_ref, mod_ref, wg_ref, wu_ref, wd_ref, o_ref, wgb_ref, wub_ref, wdb_ref, *, rc, nc):
    wg = wg_ref[0].astype(BF)
    wu = wu_ref[0].astype(BF)
    wd = wd_ref[0].astype(BF)
    wgb_ref[...] = wg
    wub_ref[...] = wu
    wdb_ref[...] = wd
    _ffn_steps(h_ref, x_ref, mod_ref, wg, wu, wd, o_ref, None, rc, nc)


def _ffn_specs(tm):
    row = lambda b, i, f: (b, i, 0)
    return [pl.BlockSpec((1, tm, D_MODEL), row),
            pl.BlockSpec((1, tm, D_MODEL), row),
            pl.BlockSpec((1, 6, D_MODEL), lambda b, i, f: (b, 0, 0))], pl.BlockSpec((1, tm, D_MODEL), row)


def _ffn_params():
    return pltpu.CompilerParams(dimension_semantics=("parallel", "parallel", "arbitrary"),
                                vmem_limit_bytes=BIG_VMEM_LIMIT_BYTES)


def _slabs(rows, steps):
    tile = 2 * SUBLANES
    count = max(d for d in range(1, min(steps, rows // tile) + 1) if (rows // tile) % d == 0)
    return count, rows // count


def _ffn_call(h, x, mod, wg_bf, wu_bf, wd_bf, tm, tf, rc, nc, emit_ssq, cast_next=None):
    B, n, _ = x.shape
    nt, nf = n // tm, D_FF // tf
    act_specs, out_spec = _ffn_specs(tm)
    in_specs = act_specs + [pl.BlockSpec((D_MODEL, tf), lambda b, i, f: (0, f)),
                            pl.BlockSpec((D_MODEL, tf), lambda b, i, f: (0, f)),
                            pl.BlockSpec((tf, D_MODEL), lambda b, i, f: (f, 0))]
    args = [h, x, mod, wg_bf, wu_bf, wd_bf]
    out_specs = [out_spec]
    out_shape = [jax.ShapeDtypeStruct((B, n, D_MODEL), F32)]
    if emit_ssq:
        out_specs.append(pl.BlockSpec((1, tm, 1), lambda b, i, f: (b, i, 0)))
        out_shape.append(jax.ShapeDtypeStruct((B, n, 1), F32))
    if cast_next is not None:
        wg_all, wu_all, wd_all, layer = cast_next
        steps = B * nt * nf
        step = lambda b, i, f: (b * nt + i) * nf + f
        up_slabs, up_rows = _slabs(D_MODEL, steps)
        down_slabs, down_rows = _slabs(D_FF, steps)
        up_idx = lambda b, i, f: jnp.minimum(step(b, i, f), up_slabs - 1)
        down_idx = lambda b, i, f: jnp.minimum(step(b, i, f), down_slabs - 1)
        in_specs += [pl.BlockSpec((1, up_rows, D_FF), lambda b, i, f: (layer, up_idx(b, i, f), 0)),
                     pl.BlockSpec((1, up_rows, D_FF), lambda b, i, f: (layer, up_idx(b, i, f), 0)),
                     pl.BlockSpec((1, down_rows, D_MODEL), lambda b, i, f: (layer, down_idx(b, i, f), 0))]
        args += [wg_all, wu_all, wd_all]
        out_specs += [pl.BlockSpec((up_rows, D_FF), lambda b, i, f: (up_idx(b, i, f), 0)),
                      pl.BlockSpec((up_rows, D_FF), lambda b, i, f: (up_idx(b, i, f), 0)),
                      pl.BlockSpec((down_rows, D_MODEL), lambda b, i, f: (down_idx(b, i, f), 0))]
        out_shape += [jax.ShapeDtypeStruct((D_MODEL, D_FF), BF),
                      jax.ShapeDtypeStruct((D_MODEL, D_FF), BF),
                      jax.ShapeDtypeStruct((D_FF, D_MODEL), BF)]
    return pl.pallas_call(
        functools.partial(_ffn_kernel, rc=rc, nc=nc, emit_ssq=emit_ssq, side_cast=cast_next is not None),
        grid=(B, nt, nf),
        in_specs=in_specs,
        out_specs=out_specs,
        out_shape=out_shape,
        compiler_params=pltpu.CompilerParams(dimension_semantics=("arbitrary", "arbitrary", "arbitrary"),
                                             vmem_limit_bytes=BIG_VMEM_LIMIT_BYTES),
        name="ffn",
    )(*args)


def _ffn_cast_call(h, x, mod, wg_all, wu_all, wd_all, layer, tm, tf, rc, nc):
    B, n, _ = x.shape
    assert B == 1 and n == tm
    act_specs, out_spec = _ffn_specs(tm)
    up_in = pl.BlockSpec((1, D_MODEL, tf), lambda b, i, f: (layer, 0, f))
    down_in = pl.BlockSpec((1, tf, D_MODEL), lambda b, i, f: (layer, f, 0))
    up_out = pl.BlockSpec((D_MODEL, tf), lambda b, i, f: (0, f))
    down_out = pl.BlockSpec((tf, D_MODEL), lambda b, i, f: (f, 0))
    return pl.pallas_call(
        functools.partial(_ffn_cast_kernel, rc=rc, nc=nc),
        grid=(1, 1, D_FF // tf),
        in_specs=act_specs + [up_in, up_in, down_in],
        out_specs=[out_spec, up_out, up_out, down_out],
        out_shape=[jax.ShapeDtypeStruct((1, n, D_MODEL), F32),
                   jax.ShapeDtypeStruct((D_MODEL, D_FF), BF),
                   jax.ShapeDtypeStruct((D_MODEL, D_FF), BF),
                   jax.ShapeDtypeStruct((D_FF, D_MODEL), BF)],
        compiler_params=_ffn_params(),
        name="ffn_cast",
    )(h, x, mod, wg_all, wu_all, wd_all)


def _dft_tables(n):
    n1, n2, s = n // DFT_N2, DFT_N2, SUBLANES
    i32 = jnp.int32
    j = jnp.arange(n2 // s, dtype=i32)[:, None, None]
    r = jnp.arange(2 * n1, dtype=i32)[None, :, None]
    c = jnp.arange(n1 * s, dtype=i32)[None, None, :]
    part, k1, nn1, l = r // n1, r % n1, c // s, c % s
    idx = (k1 * (s * j + l) + n2 * k1 * nn1 + part * (n // 4)) % n
    m1 = jnp.cos(idx.astype(F32) * (2.0 * np.pi / n))
    same_l = (jnp.arange(s, dtype=i32)[:, None] == (jnp.arange(n1 * s, dtype=i32)[None, :] % s)).astype(F32)
    kron1 = (m1[:, :, None, :] * same_l[None, None, :, :]).reshape(n2 // s, 2 * n1 * s, n1 * s)
    r = jnp.arange(2 * n2, dtype=i32)[:, None]
    c = jnp.arange(2 * s * n2, dtype=i32)[None, :]
    qpart, k2, part, nn2 = r // n2, r % n2, c // (s * n2), c % n2
    idx2 = (k2 * nn2 + (n2 // 4) * (qpart - part) + n2) % n2
    m2 = jnp.cos(idx2.astype(F32) * (2.0 * np.pi / n2))
    same_l2 = (jnp.arange(s, dtype=i32)[:, None]
               == ((jnp.arange(2 * s * n2, dtype=i32)[None, :] // n2) % s)).astype(F32)
    kron2 = (m2[:, None, :] * same_l2[None, :, :]).reshape(2 * n2 * s, 2 * s * n2)
    return kron1.astype(BF), kron2.astype(BF)


def _channel_dft_table(n):
    c = np.arange(FOURIER_GROUP_DIM)
    ang = 2.0 * np.pi * ((c[:, None] * c[None, :]) % FOURIER_GROUP_DIM) / FOURIER_GROUP_DIM
    scale = 1.0 / np.sqrt(float(n) * FOURIER_GROUP_DIM)
    return jnp.asarray(np.concatenate([np.cos(ang), np.sin(ang)], axis=0) * scale, BF)


def _channel_mix(zr, zi, cs):
    outs = []
    for g in range(zr.shape[1] // FOURIER_GROUP_DIM):
        sl = slice(g * FOURIER_GROUP_DIM, (g + 1) * FOURIER_GROUP_DIM)
        zg = jnp.concatenate([zr[:, sl], zi[:, sl]], axis=1).astype(BF)
        outs.append(jnp.dot(zg, cs, preferred_element_type=F32))
    return jnp.concatenate(outs, axis=1)


def _dft1_kernel(x_ref, ssq_ref, mod_ref, g_ref, k_ref, t_ref):
    n1, s, tc = x_ref.shape[1], x_ref.shape[2], x_ref.shape[3]
    rstd = lax.rsqrt(ssq_ref[0].reshape(n1 * s, 1) * (1.0 / D_MODEL) + EPS)
    y = x_ref[0].reshape(n1 * s, tc) * rstd
    xb = ((y * g_ref[...]) * (1.0 + mod_ref[0, 1:2, :]) + mod_ref[0, 0:1, :]).astype(BF)
    t = jnp.dot(k_ref[0], xb, preferred_element_type=F32)
    t_ref[0] = t.reshape(2, n1, s, tc)


def _dft2_kernel(t_ref, k_ref, cs_ref, y_ref):
    s, tc = y_ref.shape[2], y_ref.shape[3]
    rows = s * DFT_N2
    tb = t_ref[0, :, 0].reshape(2 * rows, tc).astype(BF)
    z = jnp.dot(k_ref[...], tb, preferred_element_type=F32)
    y = _channel_mix(z[:rows], z[rows:], cs_ref[...])
    y_ref[0] = y.reshape(DFT_N2, s, tc)


def _fourier_positions(x, ssq, mod, g, kron1, kron2, cs, tc):
    B, n, _ = x.shape
    n1, s = n // DFT_N2, SUBLANES
    nj = DFT_N2 // s
    t = pl.pallas_call(
        _dft1_kernel,
        grid=(nj, B, D_MODEL // tc),
        in_specs=[pl.BlockSpec((1, n1, s, tc), lambda j, b, c: (b, 0, j, c)),
                  pl.BlockSpec((1, n1, s, 1), lambda j, b, c: (b, 0, j, 0)),
                  pl.BlockSpec((1, 6, tc), lambda j, b, c: (b, 0, c)),
                  pl.BlockSpec((1, tc), lambda j, b, c: (0, c)),
                  pl.BlockSpec((1, 2 * n1 * s, n1 * s), lambda j, b, c: (j, 0, 0))],
        out_specs=pl.BlockSpec((1, 2, n1, s, tc), lambda j, b, c: (b, 0, 0, j, c)),
        out_shape=jax.ShapeDtypeStruct((B, 2, n1, DFT_N2, D_MODEL), F32),
        compiler_params=_params("arbitrary", "arbitrary", "arbitrary"),
        name="dft1",
    )(x.reshape(B, n1, DFT_N2, D_MODEL), ssq.reshape(B, n1, DFT_N2, 1), mod, g, kron1)
    na = n1 // s
    y = pl.pallas_call(
        _dft2_kernel,
        grid=(B, na, D_MODEL // tc),
        in_specs=[pl.BlockSpec((1, 2, 1, s * DFT_N2, tc), lambda b, a, c: (b, 0, a, 0, c)),
                  _resident((2 * DFT_N2 * s, 2 * s * DFT_N2), lambda b, a, c: (0, 0)),
                  _resident((2 * FOURIER_GROUP_DIM, FOURIER_GROUP_DIM), lambda b, a, c: (0, 0))],
        out_specs=pl.BlockSpec((1, DFT_N2, s, tc), lambda b, a, c: (b, 0, a, c)),
        out_shape=jax.ShapeDtypeStruct((B, DFT_N2, n1, D_MODEL), F32),
        compiler_params=_params("parallel", "parallel", "parallel"),
        name="dft2",
    )(t.reshape(B, 2, na, s * DFT_N2, D_MODEL), kron2, cs)
    return y.reshape(B, n, D_MODEL)


def _dft_ctx_kernel(x_ref, f_ref, cs_ref, y_ref):
    n_ctx = x_ref.shape[1]
    z = jnp.dot(f_ref[...], x_ref[0].astype(BF), preferred_element_type=F32)
    y_ref[0] = _channel_mix(z[:n_ctx], z[n_ctx:], cs_ref[...])


def _fourier_ctx(hc, cs):
    B, n_ctx, _ = hc.shape
    p = np.arange(n_ctx)
    ang = 2.0 * np.pi * ((p[:, None] * p[None, :]) % n_ctx) / n_ctx
    fmat = jnp.asarray(np.concatenate([np.cos(ang), -np.sin(ang)], axis=0), BF)
    return pl.pallas_call(
        _dft_ctx_kernel,
        grid=(B,),
        in_specs=[pl.BlockSpec((1, n_ctx, D_MODEL), lambda b: (b, 0, 0)),
                  pl.BlockSpec((2 * n_ctx, n_ctx), lambda b: (0, 0)),
                  pl.BlockSpec((2 * FOURIER_GROUP_DIM, FOURIER_GROUP_DIM), lambda b: (0, 0))],
        out_specs=pl.BlockSpec((1, n_ctx, D_MODEL), lambda b: (b, 0, 0)),
        out_shape=jax.ShapeDtypeStruct((B, n_ctx, D_MODEL), F32),
        compiler_params=_params("parallel"),
        name="dft_ctx",
    )(hc, fmat, cs)


def _gelu(z):
    return 0.5 * z * (1.0 + lax.erf(z * (2.0 ** -0.5)))


def _sgu_in_kernel(x_ref, mod_ref, g_ref, win_ref, h_ref, v_ref, ssq_ref, *, cw, rc):
    s = pl.program_id(2)
    tm = x_ref.shape[1]

    def tile(first):
        for r in range(tm // rc):
            rows = slice(r * rc, (r + 1) * rc)
            if first:
                h = _modnorm(x_ref[0, rows], g_ref[...], mod_ref[0, 0:1, :], mod_ref[0, 1:2, :]).astype(BF)
                h_ref[0, rows] = h
            else:
                h = h_ref[0, rows]
            sq = None
            for c in range(win_ref.shape[1] // cw):
                cols = slice(c * cw, (c + 1) * cw)
                z = _gelu(jnp.dot(h, win_ref[:, cols], preferred_element_type=F32))
                v_ref[0, rows, cols] = z.astype(BF)
                part = jnp.sum(z * z, axis=-1, keepdims=True)
                sq = part if sq is None else sq + part
            ssq_ref[0, rows] = sq if first else ssq_ref[0, rows] + sq

    @pl.when(s == 0)
    def _():
        tile(True)

    @pl.when(s > 0)
    def _():
        tile(False)


def _sgu_in_call(x, mod, g, win_bf, tm):
    B, n, _ = x.shape
    gpt = 2
    tw = gpt * SGU_GROUP_DIM
    n_v = SGU_HALF // tw
    row = lambda b, i, s: (b, i, 0)
    return pl.pallas_call(
        functools.partial(_sgu_in_kernel, cw=MXU_DIM, rc=min(tm, FFN_RC)),
        grid=(B, n // tm, n_v),
        in_specs=[pl.BlockSpec((1, tm, D_MODEL), row),
                  pl.BlockSpec((1, 6, D_MODEL), lambda b, i, s: (b, 0, 0)),
                  pl.BlockSpec((1, D_MODEL), lambda b, i, s: (0, 0)),
                  pl.BlockSpec((D_MODEL, tw), lambda b, i, s: (0, n_v + s))],
        out_specs=[pl.BlockSpec((1, tm, D_MODEL), row),
                   pl.BlockSpec((1, tm, tw), lambda b, i, s: (b, i, s)),
                   pl.BlockSpec((1, tm, 1), row)],
        out_shape=[jax.ShapeDtypeStruct((B, n, D_MODEL), BF),
                   jax.ShapeDtypeStruct((B, n, SGU_HALF), BF),
                   jax.ShapeDtypeStruct((B, n, 1), F32)],
        compiler_params=_params("parallel", "parallel", "arbitrary"),
        name="sgu_in",
    )(x, mod, g, win_bf)


def _sgu_out_kernel(h_ref, v_ref, ssq_ref, x_ref, mod_ref, g2_ref, win_ref, gv_ref, ws_ref, bs_ref, wout_ref,
                    o_ref, h2_ref, a0_ref, a1_ref, *, cw, nc, rc, fr):
    t = pl.program_id(2)
    ng = N_SGU_GROUPS
    gd = SGU_GROUP_DIM
    tm = h_ref.shape[1]

    def out_proj(a_ref):
        for r in range(tm // rc):
            rows = slice(r * rc, (r + 1) * rc)
            a = a_ref[rows, :]
            for c in range(D_MODEL // nc):
                cols = slice(c * nc, (c + 1) * nc)
                o_ref[0, rows, cols] += jnp.dot(a, wout_ref[:, cols], preferred_element_type=F32)

    def prep(a_ref):
        for r in range(tm // rc):
            rows = slice(r * rc, (r + 1) * rc)
            h = h_ref[0, rows]
            for c in range(gd // cw):
                cols = slice(c * cw, (c + 1) * cw)
                u = _gelu(jnp.dot(h, win_ref[:, cols], preferred_element_type=F32))
                a_ref[rows, cols] = u.astype(BF)
        rstd = lax.rsqrt(ssq_ref[0] * (1.0 / SGU_HALF) + EPS)
        for c in range(tm // SGU_CHUNK):
            rows = slice(c * SGU_CHUNK, (c + 1) * SGU_CHUNK)
            vn = ((v_ref[0, rows, :].astype(F32) * rstd[rows]) * gv_ref[0]).astype(BF)
            sp = jnp.dot(ws_ref[0], vn, preferred_element_type=F32) + bs_ref[0]
            a_ref[rows, :] = (a_ref[rows, :].astype(F32) * sp).astype(BF)

    @pl.when(t == 0)
    def _():
        o_ref[0] = jnp.zeros(o_ref.shape[1:], F32)
        prep(a0_ref)

    @pl.when((t >= 1) & (t < ng) & ((t & 1) == 1))
    def _():
        out_proj(a0_ref)
        prep(a1_ref)

    @pl.when((t >= 1) & (t < ng) & ((t & 1) == 0))
    def _():
        out_proj(a1_ref)
        prep(a0_ref)

    @pl.when(t == ng)
    def _():
        out_proj(a1_ref if (ng - 1) % 2 == 1 else a0_ref)

    @pl.when(t > ng)
    def _():
        rows = pl.ds(pl.multiple_of((t - ng - 1) * fr, fr), fr)
        xn = x_ref[0] + mod_ref[0, 2:3, :] * o_ref[0, rows, :]
        o_ref[0, rows, :] = xn
        h2_ref[0, rows, :] = _modnorm(xn, g2_ref[...], mod_ref[0, 3:4, :], mod_ref[0, 4:5, :]).astype(BF)


def _sgu_out_call(h, v, ssq, x, mod, g2, win_bf, gv, ws_bf, bs, wout_bf, tm):
    B, n, _ = x.shape
    ng, gd = N_SGU_GROUPS, SGU_GROUP_DIM
    fr = ROW_CHUNK
    nfin = tm // fr
    row = lambda b, i, t: (b, i, 0)
    grp = lambda b, i, t: (jnp.clip(t, 0, ng - 1), 0, 0)
    return pl.pallas_call(
        functools.partial(_sgu_out_kernel, cw=MXU_DIM, nc=FFN_NC, rc=min(tm, FFN_RC), fr=fr),
        grid=(B, n // tm, ng + 1 + nfin),
        in_specs=[pl.BlockSpec((1, tm, D_MODEL), row),
                  pl.BlockSpec((1, tm, gd), lambda b, i, t: (b, i, jnp.clip(t, 0, ng - 1))),
                  pl.BlockSpec((1, tm, 1), row),
                  pl.BlockSpec((1, fr, D_MODEL), lambda b, i, t: (b, i * nfin + jnp.clip(t - ng - 1, 0, nfin - 1), 0)),
                  pl.BlockSpec((1, 6, D_MODEL), lambda b, i, t: (b, 0, 0)),
                  pl.BlockSpec((1, D_MODEL), lambda b, i, t: (0, 0)),
                  pl.BlockSpec((D_MODEL, gd), lambda b, i, t: (0, jnp.clip(t, 0, ng - 1))),
                  pl.BlockSpec((1, 1, gd), grp),
                  pl.BlockSpec((1, SGU_CHUNK, SGU_CHUNK), grp),
                  pl.BlockSpec((1, SGU_CHUNK, 1), grp),
                  pl.BlockSpec((gd, D_MODEL), lambda b, i, t: (jnp.clip(t - 1, 0, ng - 1), 0))],
        out_specs=[pl.BlockSpec((1, tm, D_MODEL), row),
                   pl.BlockSpec((1, tm, D_MODEL), row)],
        out_shape=[jax.ShapeDtypeStruct((B, n, D_MODEL), F32),
                   jax.ShapeDtypeStruct((B, n, D_MODEL), BF)],
        scratch_shapes=[pltpu.VMEM((tm, gd), BF),
                        pltpu.VMEM((tm, gd), BF)],
        compiler_params=pltpu.CompilerParams(dimension_semantics=("parallel", "parallel", "arbitrary"),
                                             vmem_limit_bytes=BIG_VMEM_LIMIT_BYTES),
        name="sgu_out",
    )(h, v, ssq, x, mod, g2, win_bf, gv.reshape(ng, 1, gd), ws_bf, bs.reshape(ng, SGU_CHUNK, 1), wout_bf)


def _sgu_call(x, mod, g, g2, win_bf, gv, ws_bf, bs, wout_bf, tm):
    h, v, ssq = _sgu_in_call(x, mod, g, win_bf, tm)
    return _sgu_out_call(h, v, ssq, x, mod, g2, win_bf, gv, ws_bf, bs, wout_bf, tm)


def _rope_tables(n):
    row, col = jnp.meshgrid(jnp.arange(n // GRID_W), jnp.arange(GRID_W), indexing="ij")
    n_freq = HEAD_DIM // 4
    inv_freq = ROPE_BASE ** (-jnp.arange(n_freq, dtype=F32) / n_freq)
    ang = jnp.concatenate([row.reshape(-1, 1).astype(F32) * inv_freq,
                           col.reshape(-1, 1).astype(F32) * inv_freq], axis=-1)
    ang = jnp.concatenate([ang, ang], axis=-1)
    sign = jnp.where(jnp.arange(HEAD_DIM) < HEAD_DIM // 2, -1.0, 1.0).astype(F32)
    return jnp.cos(ang), jnp.sin(ang) * sign


def kernel(x, c, ctx, c_ctx, w_ada, b_ada, norm_g, w_ffn_gate, w_ffn_up, w_ffn_down, w_attn_qkv,
           w_attn_o, attn_q_g, attn_k_g, attn_sink, w_fourier, w_sgu_in, sgu_v_g, w_sgu_spatial,
           b_sgu_spatial, w_sgu_out):
    B, n, _ = x.shape
    n_ctx = ctx.shape[1]
    tm_x = ROW_TILE
    tm_c = B * n_ctx
    ctx = ctx.reshape(1, tm_c, D_MODEL)

    cond = jnp.zeros((SUBLANES, D_MODEL), F32).at[:B].set(c).at[B].set(c_ctx)
    ada = _ada_all(cond, w_ada, b_ada)
    cos, sin = _rope_tables(n)
    cos_c = jnp.ones((tm_c, HEAD_DIM), F32)
    sin_c = jnp.zeros((tm_c, HEAD_DIM), F32)

    for i in range(DEPTH):
        kind = i % N_MIXERS
        j = i // N_MIXERS
        need_ctx = i < DEPTH - 1
        modx = ada[i, :B].reshape(B, 6, D_MODEL)
        modc = ada[i, B].reshape(1, 6, D_MODEL)
        g1 = norm_g[i, 0].reshape(1, D_MODEL)
        g2 = norm_g[i, 1].reshape(1, D_MODEL)
        if kind == 0:
            wqkv = w_attn_qkv[j].astype(BF)
            wo = w_attn_o[j].astype(BF)
            qg = attn_q_g[j].reshape(1, HEAD_DIM)
            kg = attn_k_g[j].reshape(1, HEAD_DIM)
            sink = attn_sink[j]
            q, k, v = _qkv_call(x, modx, g1, wqkv, qg, kg, cos, sin, QKV_TM, ROW_CHUNK)
            qc, kc, vc = _qkv_call(ctx, modc, g1, wqkv, qg, kg, cos_c, sin_c, tm_c, ROW_CHUNK)
            kc = kc.reshape(B, n_ctx, HKV)
            vc = vc.reshape(B, n_ctx, HKV)
            ox = _attn_call(q, k, v, kc, vc, sink, ATTN_QBLOCKS)
            x, hx2 = _mm_res_call(ox, wo, x, modx, g2, tm_x, ROW_CHUNK)
            if need_ctx:
                oc = _attn_ctx_call(qc.reshape(B, n_ctx, HQ), kc, vc, sink).reshape(1, tm_c, HQ)
                ctx, hc2 = _mm_res_call(oc, wo, ctx, modc, g2, tm_c, ROW_CHUNK)
        elif kind == 1:
            wf = w_fourier[j].astype(BF)
            kron1, kron2 = _dft_tables(n)
            yx = _fourier_positions(x, ssq_x, modx, g1, kron1, kron2, _channel_dft_table(n), DFT_TC)
            x, hx2 = _mm_res_call(yx, wf, x, modx, g2, tm_x, ROW_CHUNK)
            if need_ctx:
                hc = _modnorm_call(ctx, modc, g1, tm_c).reshape(B, n_ctx, D_MODEL)
                yc = _fourier_ctx(hc, _channel_dft_table(n_ctx)).reshape(1, tm_c, D_MODEL)
                ctx, hc2 = _mm_res_call(yc, wf, ctx, modc, g2, tm_c, ROW_CHUNK)
        else:
            win = w_sgu_in[j].astype(BF)
            wout = w_sgu_out[j].astype(BF)
            ws = w_sgu_spatial[j].astype(BF)
            args = (win, sgu_v_g[j], ws, b_sgu_spatial[j], wout)
            x, hx2 = _sgu_call(x, modx, g1, g2, *args, SGU_TM)
            if need_ctx:
                ctx, hc2 = _sgu_call(ctx, modc, g1, g2, *args, tm_c)
        if i == 0:
            ctx, wg, wu, wd = _ffn_cast_call(hc2, ctx, modc, w_ffn_gate, w_ffn_up, w_ffn_down, i,
                                             tm_c, FFN_CAST_TF, tm_c, FFN_NC)
        elif need_ctx:
            (ctx,) = _ffn_call(hc2, ctx, modc, wg, wu, wd, tm_c, FFN_TF, tm_c, FFN_NC, False)
        cast_next = (w_ffn_gate, w_ffn_up, w_ffn_down, i + 1) if i + 1 < DEPTH else None
        emit_ssq = (i + 1) % N_MIXERS == 1 and i + 1 < DEPTH
        outs = list(_ffn_call(hx2, x, modx, wg, wu, wd, FFN_TM, FFN_TF, FFN_RC, FFN_NC, emit_ssq, cast_next))
        x = outs.pop(0)
        if emit_ssq:
            ssq_x = outs.pop(0)
        if cast_next is not None:
            wg, wu, wd = outs
    return x
```

```python
import functools

import numpy as np
import jax
import jax.numpy as jnp
from jax import lax
from jax.experimental import pallas as pl
from jax.experimental.pallas import tpu as pltpu

D_MODEL = 2048
DEPTH = 4
GRID_W = 64
N_MIXERS = 3
EPS = 1e-6
N_HEADS = 16
N_KV_HEADS = 4
HEAD_DIM = D_MODEL // N_HEADS
KV_GROUP = N_HEADS // N_KV_HEADS
HQ = N_HEADS * HEAD_DIM
HKV = N_KV_HEADS * HEAD_DIM
WINDOW = 128
BLOCK = 128
ROPE_BASE = 10000.0
N_FOURIER_GROUPS = 8
FOURIER_GROUP_DIM = D_MODEL // N_FOURIER_GROUPS
SGU_CHUNK = 128
SGU_HALF = 3 * D_MODEL
N_SGU_GROUPS = 8
SGU_GROUP_DIM = SGU_HALF // N_SGU_GROUPS
D_FF = 5632

BF = jnp.bfloat16
F32 = jnp.float32

V7X_VMEM_BYTES = 64 * 1024 * 1024
VMEM_LIMIT_BYTES = V7X_VMEM_BYTES - 8 * 1024 * 1024
BIG_VMEM_LIMIT_BYTES = V7X_VMEM_BYTES - 4 * 1024 * 1024
SUBLANES = 8
MXU_DIM = 256

ROW_TILE = 512
SGU_TM = 1024
ROW_CHUNK = MXU_DIM
QKV_TM = 1024
FFN_TM = 1024
FFN_TF = 512
FFN_RC = 512
FFN_NC = 512
FFN_CAST_TF = MXU_DIM
ATTN_QBLOCKS = 16
DFT_TC = 1024
ADA_TN = 1024
NEG = -1e30
NO_CAP = float(np.finfo(np.float32).max)
LOG2E = float(np.log2(np.e))

DFT_N2 = 64


def _params(*sem):
    return pltpu.CompilerParams(dimension_semantics=sem, vmem_limit_bytes=VMEM_LIMIT_BYTES)


def _resident(shape, index_map):
    return pl.BlockSpec(shape, index_map, pipeline_mode=pl.Buffered(1))


def _modnorm(x, g, shift, scale):
    ms = jnp.mean(x * x, axis=-1, keepdims=True)
    y = x * lax.rsqrt(ms + EPS)
    return (y * g) * (1.0 + scale) + shift


def _silu(t):
    return t * jax.nn.sigmoid(t)


def _ada_kernel(cond_ref, w_ref, b_ref, o_ref):
    s = _silu(cond_ref[...]).astype(BF)
    o_ref[0] = jnp.dot(s, w_ref[0].astype(BF), preferred_element_type=F32) + b_ref[0]


def _ada_all(cond, w_ada, b_ada):
    tn = ADA_TN
    return pl.pallas_call(
        _ada_kernel,
        grid=(DEPTH, 6 * D_MODEL // tn),
        in_specs=[pl.BlockSpec((SUBLANES, D_MODEL), lambda l, j: (0, 0)),
                  pl.BlockSpec((1, D_MODEL, tn), lambda l, j: (l, 0, j)),
                  pl.BlockSpec((1, 1, tn), lambda l, j: (l, 0, j))],
        out_specs=pl.BlockSpec((1, SUBLANES, tn), lambda l, j: (l, 0, j)),
        out_shape=jax.ShapeDtypeStruct((DEPTH, SUBLANES, 6 * D_MODEL), F32),
        compiler_params=_params("arbitrary", "arbitrary"),
        name="ada",
    )(cond, w_ada, b_ada.reshape(DEPTH, 1, 6 * D_MODEL))


def _modnorm_kernel(x_ref, mod_ref, g_ref, o_ref):
    o_ref[0] = _modnorm(x_ref[0], g_ref[...], mod_ref[0, 0:1, :], mod_ref[0, 1:2, :])


def _modnorm_call(x, mod, g, tm):
    B, n, _ = x.shape
    return pl.pallas_call(
        _modnorm_kernel,
        grid=(B, n // tm),
        in_specs=[pl.BlockSpec((1, tm, D_MODEL), lambda b, i: (b, i, 0)),
                  pl.BlockSpec((1, 6, D_MODEL), lambda b, i: (b, 0, 0)),
                  pl.BlockSpec((1, D_MODEL), lambda b, i: (0, 0))],
        out_specs=pl.BlockSpec((1, tm, D_MODEL), lambda b, i: (b, i, 0)),
        out_shape=jax.ShapeDtypeStruct((B, n, D_MODEL), F32),
        compiler_params=_params("parallel", "parallel"),
        name="modnorm",
    )(x, mod, g)


def _qkv_kernel(x_ref, mod_ref, g_ref, w_ref, qg_ref, kg_ref, cos_ref, sin_ref, q_ref, k_ref, v_ref, *, rc):
    tm = x_ref.shape[1]
    qscale = (HEAD_DIM ** -0.5) * LOG2E
    cw = 4 * HEAD_DIM

    for r in range(tm // rc):
        rows = slice(r * rc, (r + 1) * rc)
        h = _modnorm(x_ref[0, rows], g_ref[...], mod_ref[0, 0:1, :], mod_ref[0, 1:2, :]).astype(BF)
        cos = cos_ref[rows]
        sin = sin_ref[rows]

        def head(t, gain):
            ms = jnp.mean(t * t, axis=-1, keepdims=True)
            t = (t * lax.rsqrt(ms + EPS)) * gain
            return t * cos + pltpu.roll(t, HEAD_DIM // 2, 1) * sin

        t = jnp.dot(h, w_ref[:, HQ:HQ + HKV], preferred_element_type=F32)
        for j in range(N_KV_HEADS):
            tj = head(t[:, j * HEAD_DIM:(j + 1) * HEAD_DIM], kg_ref[...])
            k_ref[0, rows, j * HEAD_DIM:(j + 1) * HEAD_DIM] = tj.astype(BF)
        for c in range(HQ // cw):
            t = jnp.dot(h, w_ref[:, c * cw:(c + 1) * cw], preferred_element_type=F32)
            for j in range(4):
                tj = head(t[:, j * HEAD_DIM:(j + 1) * HEAD_DIM], qg_ref[...]) * qscale
                q_ref[0, rows, c * cw + j * HEAD_DIM:c * cw + (j + 1) * HEAD_DIM] = tj.astype(BF)
        t = jnp.dot(h, w_ref[:, HQ + HKV:], preferred_element_type=F32)
        v_ref[0, rows] = t.astype(BF)


def _qkv_call(x, mod, g, w_bf, qg, kg, cos, sin, tm, rc):
    B, n, _ = x.shape
    row = lambda b, i: (b, i, 0)
    return pl.pallas_call(
        functools.partial(_qkv_kernel, rc=rc),
        grid=(B, n // tm),
        in_specs=[pl.BlockSpec((1, tm, D_MODEL), row),
                  pl.BlockSpec((1, 6, D_MODEL), lambda b, i: (b, 0, 0)),
                  pl.BlockSpec((1, D_MODEL), lambda b, i: (0, 0)),
                  _resident((D_MODEL, HQ + 2 * HKV), lambda b, i: (0, 0)),
                  pl.BlockSpec((1, HEAD_DIM), lambda b, i: (0, 0)),
                  pl.BlockSpec((1, HEAD_DIM), lambda b, i: (0, 0)),
                  pl.BlockSpec((tm, HEAD_DIM), lambda b, i: (i, 0)),
                  pl.BlockSpec((tm, HEAD_DIM), lambda b, i: (i, 0))],
        out_specs=[pl.BlockSpec((1, tm, HQ), row),
                   pl.BlockSpec((1, tm, HKV), row),
                   pl.BlockSpec((1, tm, HKV), row)],
        out_shape=[jax.ShapeDtypeStruct((B, n, HQ), BF),
                   jax.ShapeDtypeStruct((B, n, HKV), BF),
                   jax.ShapeDtypeStruct((B, n, HKV), BF)],
        compiler_params=_params("parallel", "parallel"),
        name="qkv",
    )(x, mod, g, w_bf, qg, kg, cos, sin)


def _softmax_pv(s, sink_col, vcat):
    m = jnp.maximum(jnp.max(s, axis=-1, keepdims=True), sink_col)
    p = jnp.exp2(s - m)
    den = jnp.sum(p, axis=-1, keepdims=True) + jnp.exp2(sink_col - m)
    return jnp.dot(p.astype(BF), vcat, preferred_element_type=F32) / den


def _sink_column(sink_ref, h, rows):
    ridx = lax.broadcasted_iota(jnp.int32, (KV_GROUP * rows, 1), 0)
    col = jnp.full((KV_GROUP * rows, 1), sink_ref[h * KV_GROUP] * LOG2E, F32)
    for g in range(1, KV_GROUP):
        col = jnp.where(ridx >= g * rows, sink_ref[h * KV_GROUP + g] * LOG2E, col)
    return col


def _stack_heads(q):
    return jnp.concatenate([q[:, g * HEAD_DIM:(g + 1) * HEAD_DIM] for g in range(KV_GROUP)], axis=0)


def _attn_kernel(sink_ref, band_ref, q_ref, kp_ref, kc_ref, kn_ref, vp_ref, vc_ref, vn_ref, kx_ref, vx_ref,
                 o_ref, *, n_ctx, qblocks):
    h = pl.program_id(1)
    i = pl.program_id(2)
    last = pl.num_programs(2) - 1
    kloc = jnp.concatenate([kp_ref[0], kc_ref[0], kn_ref[0]], axis=0)
    vloc = jnp.concatenate([vp_ref[0], vc_ref[0], vn_ref[0]], axis=0)
    sink_col = _sink_column(sink_ref, h, BLOCK)
    for t in range(qblocks):
        qs = _stack_heads(q_ref[0, t * BLOCK:(t + 1) * BLOCK, :])
        kcat = jnp.concatenate([kx_ref[0], kloc[t * BLOCK:(t + 3) * BLOCK]], axis=0)
        vcat = jnp.concatenate([vx_ref[0], vloc[t * BLOCK:(t + 3) * BLOCK]], axis=0)
        s = lax.dot_general(qs, kcat, (((1,), (1,)), ((), ())), preferred_element_type=F32)
        lo = jnp.minimum(s[:, n_ctx:n_ctx + BLOCK], band_ref[:, :BLOCK])
        hi = jnp.minimum(s[:, n_ctx + 2 * BLOCK:], band_ref[:, BLOCK:])
        if t == 0:
            lo = jnp.minimum(lo, jnp.where(i == 0, NEG, NO_CAP))
        if t == qblocks - 1:
            hi = jnp.minimum(hi, jnp.where(i == last, NEG, NO_CAP))
        s = jnp.concatenate([s[:, :n_ctx], lo, s[:, n_ctx + BLOCK:n_ctx + 2 * BLOCK], hi], axis=1)
        o = _softmax_pv(s, sink_col, vcat)
        for g in range(KV_GROUP):
            o_ref[0, t * BLOCK:(t + 1) * BLOCK, g * HEAD_DIM:(g + 1) * HEAD_DIM] = (
                o[g * BLOCK:(g + 1) * BLOCK].astype(BF))


def _band_bias():
    row = np.arange(KV_GROUP * BLOCK)[:, None] % BLOCK
    col = np.arange(BLOCK)[None, :]
    prev_ok = (row - (col - BLOCK)) <= WINDOW
    next_ok = ((col + BLOCK) - row) <= WINDOW
    return jnp.asarray(np.where(np.concatenate([prev_ok, next_ok], axis=1), NO_CAP, NEG), F32)


def _attn_call(q, k, v, kx, vx, sink, qblocks):
    B, n, _ = q.shape
    n_ctx = kx.shape[1]
    nb = n // BLOCK
    tq = qblocks * BLOCK
    gw = KV_GROUP * HEAD_DIM
    prev = lambda b, h, i: (b, jnp.maximum(i * qblocks - 1, 0), h)
    cur = lambda b, h, i: (b, i, h)
    nxt = lambda b, h, i: (b, jnp.minimum((i + 1) * qblocks, nb - 1), h)
    ctx = lambda b, h, i: (b, 0, h)
    edge = lambda m: pl.BlockSpec((1, BLOCK, HEAD_DIM), m)
    mid = pl.BlockSpec((1, tq, HEAD_DIM), cur)
    return pl.pallas_call(
        functools.partial(_attn_kernel, n_ctx=n_ctx, qblocks=qblocks),
        grid=(B, N_KV_HEADS, n // tq),
        in_specs=[pl.BlockSpec(memory_space=pltpu.SMEM),
                  pl.BlockSpec((KV_GROUP * BLOCK, 2 * BLOCK), lambda b, h, i: (0, 0)),
                  pl.BlockSpec((1, tq, gw), cur),
                  edge(prev), mid, edge(nxt), edge(prev), mid, edge(nxt),
                  pl.BlockSpec((1, n_ctx, HEAD_DIM), ctx),
                  pl.BlockSpec((1, n_ctx, HEAD_DIM), ctx)],
        out_specs=pl.BlockSpec((1, tq, gw), cur),
        out_shape=jax.ShapeDtypeStruct((B, n, HQ), BF),
        compiler_params=_params("parallel", "parallel", "arbitrary"),
        name="attn",
    )(sink, _band_bias(), q, k, k, k, v, v, v, kx, vx)


def _attn_ctx_kernel(sink_ref, q_ref, k_ref, v_ref, o_ref, *, n_ctx):
    h = pl.program_id(1)
    qs = _stack_heads(q_ref[0])
    s = lax.dot_general(qs, k_ref[0], (((1,), (1,)), ((), ())), preferred_element_type=F32)
    o = _softmax_pv(s, _sink_column(sink_ref, h, n_ctx), v_ref[0])
    for g in range(KV_GROUP):
        o_ref[0, :, g * HEAD_DIM:(g + 1) * HEAD_DIM] = o[g * n_ctx:(g + 1) * n_ctx].astype(BF)


def _attn_ctx_call(q, k, v, sink):
    B, n_ctx, _ = q.shape
    gw = KV_GROUP * HEAD_DIM
    idx = lambda b, h: (b, 0, h)
    return pl.pallas_call(
        functools.partial(_attn_ctx_kernel, n_ctx=n_ctx),
        grid=(B, N_KV_HEADS),
        in_specs=[pl.BlockSpec(memory_space=pltpu.SMEM),
                  pl.BlockSpec((1, n_ctx, gw), idx),
                  pl.BlockSpec((1, n_ctx, HEAD_DIM), idx),
                  pl.BlockSpec((1, n_ctx, HEAD_DIM), idx)],
        out_specs=pl.BlockSpec((1, n_ctx, gw), idx),
        out_shape=jax.ShapeDtypeStruct((B, n_ctx, HQ), BF),
        compiler_params=_params("parallel", "parallel"),
        name="attn_ctx",
    )(sink, q, k, v)


def _mm_res_kernel(a_ref, w_ref, x_ref, mod_ref, g2_ref, o_ref, h_ref, *, rc):
    tm = x_ref.shape[1]
    for r in range(tm // rc):
        rows = slice(r * rc, (r + 1) * rc)
        p = jnp.dot(a_ref[0, rows].astype(BF), w_ref[...], preferred_element_type=F32)
        xn = x_ref[0, rows] + mod_ref[0, 2:3, :] * p
        o_ref[0, rows] = xn
        h_ref[0, rows] = _modnorm(xn, g2_ref[...], mod_ref[0, 3:4, :], mod_ref[0, 4:5, :]).astype(BF)


def _mm_res_call(a, w_bf, x, mod, g2, tm, rc):
    B, n, K = a.shape
    row = lambda b, i: (b, i, 0)
    return pl.pallas_call(
        functools.partial(_mm_res_kernel, rc=rc),
        grid=(B, n // tm),
        in_specs=[pl.BlockSpec((1, tm, K), row),
                  _resident((K, D_MODEL), lambda b, i: (0, 0)),
                  pl.BlockSpec((1, tm, D_MODEL), row),
                  pl.BlockSpec((1, 6, D_MODEL), lambda b, i: (b, 0, 0)),
                  pl.BlockSpec((1, D_MODEL), lambda b, i: (0, 0))],
        out_specs=[pl.BlockSpec((1, tm, D_MODEL), row),
                   pl.BlockSpec((1, tm, D_MODEL), row)],
        out_shape=[jax.ShapeDtypeStruct((B, n, D_MODEL), F32),
                   jax.ShapeDtypeStruct((B, n, D_MODEL), BF)],
        compiler_params=_params("parallel", "parallel"),
        name="mm_res",
    )(a, w_bf, x, mod, g2)


def _ffn_step(h_ref, x_ref, mod_ref, wg, wu, wd, o_ref, ssq_ref, rc, nc, first, last):
    tm = h_ref.shape[1]
    for r in range(tm // rc):
        rows = slice(r * rc, (r + 1) * rc)
        h = h_ref[0, rows]
        gate = jnp.dot(h, wg[...], preferred_element_type=F32)
        up = jnp.dot(h, wu[...], preferred_element_type=F32)
        a = (_silu(gate) * up).astype(BF)
        sq = None
        for c in range(D_MODEL // nc):
            cols = slice(c * nc, (c + 1) * nc)
            acc = jnp.dot(a, wd[:, cols], preferred_element_type=F32)
            if not first:
                acc = o_ref[0, rows, cols] + acc
            if last:
                acc = x_ref[0, rows, cols] + mod_ref[0, 5:6, cols] * acc
                if ssq_ref is not None:
                    part = jnp.sum(acc * acc, axis=-1, keepdims=True)
                    sq = part if sq is None else sq + part
            o_ref[0, rows, cols] = acc
        if sq is not None:
            ssq_ref[0, rows] = sq


def _ffn_steps(h_ref, x_ref, mod_ref, wg, wu, wd, o_ref, ssq_ref, rc, nc):
    f = pl.program_id(2)
    nf = pl.num_programs(2)

    @pl.when(f == 0)
    def _():
        _ffn_step(h_ref, x_ref, mod_ref, wg, wu, wd, o_ref, ssq_ref, rc, nc, True, False)

    @pl.when((f > 0) & (f < nf - 1))
    def _():
        _ffn_step(h_ref, x_ref, mod_ref, wg, wu, wd, o_ref, ssq_ref, rc, nc, False, False)

    @pl.when(f == nf - 1)
    def _():
        _ffn_step(h_ref, x_ref, mod_ref, wg, wu, wd, o_ref, ssq_ref, rc, nc, False, True)


def _ffn_kernel(h_ref, x_ref, mod_ref, wg_ref, wu_ref, wd_ref, *rest, rc, nc, emit_ssq, side_cast):
    rest = list(rest)
    side_in = [rest.pop(0) for _ in range(3)] if side_cast else []
    o_ref = rest.pop(0)
    ssq_ref = rest.pop(0) if emit_ssq else None
    for src, dst in zip(side_in, rest):
        dst[...] = src[0].astype(BF)
    _ffn_steps(h_ref, x_ref, mod_ref, wg_ref, wu_ref, wd_ref, o_ref, ssq_ref, rc, nc)


def _ffn_cast_kernel(h_ref, x_ref, mod_ref, wg_ref, wu_ref, wd_ref, o_ref, wgb_ref, wub_ref, wdb_ref, *, rc, nc):
    wg = wg_ref[0].astype(BF)
    wu = wu_ref[0].astype(BF)
    wd = wd_ref[0].astype(BF)
    wgb_ref[...] = wg
    wub_ref[...] = wu
    wdb_ref[...] = wd
    _ffn_steps(h_ref, x_ref, mod_ref, wg, wu, wd, o_ref, None, rc, nc)


def _ffn_specs(tm):
    row = lambda b, i, f: (b, i, 0)
    return [pl.BlockSpec((1, tm, D_MODEL), row),
            pl.BlockSpec((1, tm, D_MODEL), row),
            pl.BlockSpec((1, 6, D_MODEL), lambda b, i, f: (b, 0, 0))], pl.BlockSpec((1, tm, D_MODEL), row)


def _ffn_params():
    return pltpu.CompilerParams(dimension_semantics=("parallel", "parallel", "arbitrary"),
                                vmem_limit_bytes=BIG_VMEM_LIMIT_BYTES)


def _slabs(rows, steps):
    tile = 2 * SUBLANES
    count = max(d for d in range(1, min(steps, rows // tile) + 1) if (rows // tile) % d == 0)
    return count, rows // count


def _ffn_call(h, x, mod, wg_bf, wu_bf, wd_bf, tm, tf, rc, nc, emit_ssq, cast_next=None):
    B, n, _ = x.shape
    nt, nf = n // tm, D_FF // tf
    act_specs, out_spec = _ffn_specs(tm)
    in_specs = act_specs + [pl.BlockSpec((D_MODEL, tf), lambda b, i, f: (0, f)),
                            pl.BlockSpec((D_MODEL, tf), lambda b, i, f: (0, f)),
                            pl.BlockSpec((tf, D_MODEL), lambda b, i, f: (f, 0))]
    args = [h, x, mod, wg_bf, wu_bf, wd_bf]
    out_specs = [out_spec]
    out_shape = [jax.ShapeDtypeStruct((B, n, D_MODEL), F32)]
    if emit_ssq:
        out_specs.append(pl.BlockSpec((1, tm, 1), lambda b, i, f: (b, i, 0)))
        out_shape.append(jax.ShapeDtypeStruct((B, n, 1), F32))
    if cast_next is not None:
        wg_all, wu_all, wd_all, layer = cast_next
        steps = B * nt * nf
        step = lambda b, i, f: (b * nt + i) * nf + f
        up_slabs, up_rows = _slabs(D_MODEL, steps)
        down_slabs, down_rows = _slabs(D_FF, steps)
        up_idx = lambda b, i, f: jnp.minimum(step(b, i, f), up_slabs - 1)
        down_idx = lambda b, i, f: jnp.minimum(step(b, i, f), down_slabs - 1)
        in_specs += [pl.BlockSpec((1, up_rows, D_FF), lambda b, i, f: (layer, up_idx(b, i, f), 0)),
                     pl.BlockSpec((1, up_rows, D_FF), lambda b, i, f: (layer, up_idx(b, i, f), 0)),
                     pl.BlockSpec((1, down_rows, D_MODEL), lambda b, i, f: (layer, down_idx(b, i, f), 0))]
        args += [wg_all, wu_all, wd_all]
        out_specs += [pl.BlockSpec((up_rows, D_FF), lambda b, i, f: (up_idx(b, i, f), 0)),
                      pl.BlockSpec((up_rows, D_FF), lambda b, i, f: (up_idx(b, i, f), 0)),
                      pl.BlockSpec((down_rows, D_MODEL), lambda b, i, f: (down_idx(b, i, f), 0))]
        out_shape += [jax.ShapeDtypeStruct((D_MODEL, D_FF), BF),
                      jax.ShapeDtypeStruct((D_MODEL, D_FF), BF),
                      jax.ShapeDtypeStruct((D_FF, D_MODEL), BF)]
    return pl.pallas_call(
        functools.partial(_ffn_kernel, rc=rc, nc=nc, emit_ssq=emit_ssq, side_cast=cast_next is not None),
        grid=(B, nt, nf),
        in_specs=in_specs,
        out_specs=out_specs,
        out_shape=out_shape,
        compiler_params=pltpu.CompilerParams(dimension_semantics=("arbitrary", "arbitrary", "arbitrary"),
                                             vmem_limit_bytes=BIG_VMEM_LIMIT_BYTES),
        name="ffn",
    )(*args)


def _ffn_cast_call(h, x, mod, wg_all, wu_all, wd_all, layer, tm, tf, rc, nc):
    B, n, _ = x.shape
    assert B == 1 and n == tm
    act_specs, out_spec = _ffn_specs(tm)
    up_in = pl.BlockSpec((1, D_MODEL, tf), lambda b, i, f: (layer, 0, f))
    down_in = pl.BlockSpec((1, tf, D_MODEL), lambda b, i, f: (layer, f, 0))
    up_out = pl.BlockSpec((D_MODEL, tf), lambda b, i, f: (0, f))
    down_out = pl.BlockSpec((tf, D_MODEL), lambda b, i, f: (f, 0))
    return pl.pallas_call(
        functools.partial(_ffn_cast_kernel, rc=rc, nc=nc),
        grid=(1, 1, D_FF // tf),
        in_specs=act_specs + [up_in, up_in, down_in],
        out_specs=[out_spec, up_out, up_out, down_out],
        out_shape=[jax.ShapeDtypeStruct((1, n, D_MODEL), F32),
                   jax.ShapeDtypeStruct((D_MODEL, D_FF), BF),
                   jax.ShapeDtypeStruct((D_MODEL, D_FF), BF),
                   jax.ShapeDtypeStruct((D_FF, D_MODEL), BF)],
        compiler_params=_ffn_params(),
        name="ffn_cast",
    )(h, x, mod, wg_all, wu_all, wd_all)


def _dft_tables(n):
    n1, n2, s = n // DFT_N2, DFT_N2, SUBLANES
    i32 = jnp.int32
    j = jnp.arange(n2 // s, dtype=i32)[:, None, None]
    r = jnp.arange(2 * n1, dtype=i32)[None, :, None]
    c = jnp.arange(n1 * s, dtype=i32)[None, None, :]
    part, k1, nn1, l = r // n1, r % n1, c // s, c % s
    idx = (k1 * (s * j + l) + n2 * k1 * nn1 + part * (n // 4)) % n
    m1 = jnp.cos(idx.astype(F32) * (2.0 * np.pi / n))
    same_l = (jnp.arange(s, dtype=i32)[:, None] == (jnp.arange(n1 * s, dtype=i32)[None, :] % s)).astype(F32)
    kron1 = (m1[:, :, None, :] * same_l[None, None, :, :]).reshape(n2 // s, 2 * n1 * s, n1 * s)
    r = jnp.arange(2 * n2, dtype=i32)[:, None]
    c = jnp.arange(2 * s * n2, dtype=i32)[None, :]
    qpart, k2, part, nn2 = r // n2, r % n2, c // (s * n2), c % n2
    idx2 = (k2 * nn2 + (n2 // 4) * (qpart - part) + n2) % n2
    m2 = jnp.cos(idx2.astype(F32) * (2.0 * np.pi / n2))
    same_l2 = (jnp.arange(s, dtype=i32)[:, None]
               == ((jnp.arange(2 * s * n2, dtype=i32)[None, :] // n2) % s)).astype(F32)
    kron2 = (m2[:, None, :] * same_l2[None, :, :]).reshape(2 * n2 * s, 2 * s * n2)
    return kron1.astype(BF), kron2.astype(BF)


def _channel_dft_table(n):
    c = np.arange(FOURIER_GROUP_DIM)
    ang = 2.0 * np.pi * ((c[:, None] * c[None, :]) % FOURIER_GROUP_DIM) / FOURIER_GROUP_DIM
    scale = 1.0 / np.sqrt(float(n) * FOURIER_GROUP_DIM)
    return jnp.asarray(np.concatenate([np.cos(ang), np.sin(ang)], axis=0) * scale, BF)


def _channel_mix(zr, zi, cs):
    outs = []
    for g in range(zr.shape[1] // FOURIER_GROUP_DIM):
        sl = slice(g * FOURIER_GROUP_DIM, (g + 1) * FOURIER_GROUP_DIM)
        zg = jnp.concatenate([zr[:, sl], zi[:, sl]], axis=1).astype(BF)
        outs.append(jnp.dot(zg, cs, preferred_element_type=F32))
    return jnp.concatenate(outs, axis=1)


def _dft1_kernel(x_ref, ssq_ref, mod_ref, g_ref, k_ref, t_ref):
    n1, s, tc = x_ref.shape[1], x_ref.shape[2], x_ref.shape[3]
    rstd = lax.rsqrt(ssq_ref[0].reshape(n1 * s, 1) * (1.0 / D_MODEL) + EPS)
    y = x_ref[0].reshape(n1 * s, tc) * rstd
    xb = ((y * g_ref[...]) * (1.0 + mod_ref[0, 1:2, :]) + mod_ref[0, 0:1, :]).astype(BF)
    t = jnp.dot(k_ref[0], xb, preferred_element_type=F32)
    t_ref[0] = t.reshape(2, n1, s, tc)


def _dft2_kernel(t_ref, k_ref, cs_ref, y_ref):
    s, tc = y_ref.shape[2], y_ref.shape[3]
    rows = s * DFT_N2
    tb = t_ref[0, :, 0].reshape(2 * rows, tc).astype(BF)
    z = jnp.dot(k_ref[...], tb, preferred_element_type=F32)
    y = _channel_mix(z[:rows], z[rows:], cs_ref[...])
    y_ref[0] = y.reshape(DFT_N2, s, tc)


def _fourier_positions(x, ssq, mod, g, kron1, kron2, cs, tc):
    B, n, _ = x.shape
    n1, s = n // DFT_N2, SUBLANES
    nj = DFT_N2 // s
    t = pl.pallas_call(
        _dft1_kernel,
        grid=(nj, B, D_MODEL // tc),
        in_specs=[pl.BlockSpec((1, n1, s, tc), lambda j, b, c: (b, 0, j, c)),
                  pl.BlockSpec((1, n1, s, 1), lambda j, b, c: (b, 0, j, 0)),
                  pl.BlockSpec((1, 6, tc), lambda j, b, c: (b, 0, c)),
                  pl.BlockSpec((1, tc), lambda j, b, c: (0, c)),
                  pl.BlockSpec((1, 2 * n1 * s, n1 * s), lambda j, b, c: (j, 0, 0))],
        out_specs=pl.BlockSpec((1, 2, n1, s, tc), lambda j, b, c: (b, 0, 0, j, c)),
        out_shape=jax.ShapeDtypeStruct((B, 2, n1, DFT_N2, D_MODEL), F32),
        compiler_params=_params("arbitrary", "arbitrary", "arbitrary"),
        name="dft1",
    )(x.reshape(B, n1, DFT_N2, D_MODEL), ssq.reshape(B, n1, DFT_N2, 1), mod, g, kron1)
    na = n1 // s
    y = pl.pallas_call(
        _dft2_kernel,
        grid=(B, na, D_MODEL // tc),
        in_specs=[pl.BlockSpec((1, 2, 1, s * DFT_N2, tc), lambda b, a, c: (b, 0, a, 0, c)),
                  _resident((2 * DFT_N2 * s, 2 * s * DFT_N2), lambda b, a, c: (0, 0)),
                  _resident((2 * FOURIER_GROUP_DIM, FOURIER_GROUP_DIM), lambda b, a, c: (0, 0))],
        out_specs=pl.BlockSpec((1, DFT_N2, s, tc), lambda b, a, c: (b, 0, a, c)),
        out_shape=jax.ShapeDtypeStruct((B, DFT_N2, n1, D_MODEL), F32),
        compiler_params=_params("parallel", "parallel", "parallel"),
        name="dft2",
    )(t.reshape(B, 2, na, s * DFT_N2, D_MODEL), kron2, cs)
    return y.reshape(B, n, D_MODEL)


def _dft_ctx_kernel(x_ref, f_ref, cs_ref, y_ref):
    n_ctx = x_ref.shape[1]
    z = jnp.dot(f_ref[...], x_ref[0].astype(BF), preferred_element_type=F32)
    y_ref[0] = _channel_mix(z[:n_ctx], z[n_ctx:], cs_ref[...])


def _fourier_ctx(hc, cs):
    B, n_ctx, _ = hc.shape
    p = np.arange(n_ctx)
    ang = 2.0 * np.pi * ((p[:, None] * p[None, :]) % n_ctx) / n_ctx
    fmat = jnp.asarray(np.concatenate([np.cos(ang), -np.sin(ang)], axis=0), BF)
    return pl.pallas_call(
        _dft_ctx_kernel,
        grid=(B,),
        in_specs=[pl.BlockSpec((1, n_ctx, D_MODEL), lambda b: (b, 0, 0)),
                  pl.BlockSpec((2 * n_ctx, n_ctx), lambda b: (0, 0)),
                  pl.BlockSpec((2 * FOURIER_GROUP_DIM, FOURIER_GROUP_DIM), lambda b: (0, 0))],
        out_specs=pl.BlockSpec((1, n_ctx, D_MODEL), lambda b: (b, 0, 0)),
        out_shape=jax.ShapeDtypeStruct((B, n_ctx, D_MODEL), F32),
        compiler_params=_params("parallel"),
        name="dft_ctx",
    )(hc, fmat, cs)


def _gelu(z):
    return 0.5 * z * (1.0 + lax.erf(z * (2.0 ** -0.5)))


def _sgu_in_kernel(x_ref, mod_ref, g_ref, win_ref, h_ref, v_ref, ssq_ref, *, cw, rc):
    s = pl.program_id(2)
    tm = x_ref.shape[1]

    def tile(first):
        for r in range(tm // rc):
            rows = slice(r * rc, (r + 1) * rc)
            if first:
                h = _modnorm(x_ref[0, rows], g_ref[...], mod_ref[0, 0:1, :], mod_ref[0, 1:2, :]).astype(BF)
                h_ref[0, rows] = h
            else:
                h = h_ref[0, rows]
            sq = None
            for c in range(win_ref.shape[1] // cw):
                cols = slice(c * cw, (c + 1) * cw)
                z = _gelu(jnp.dot(h, win_ref[:, cols], preferred_element_type=F32))
                v_ref[0, rows, cols] = z.astype(BF)
                part = jnp.sum(z * z, axis=-1, keepdims=True)
                sq = part if sq is None else sq + part
            ssq_ref[0, rows] = sq if first else ssq_ref[0, rows] + sq

    @pl.when(s == 0)
    def _():
        tile(True)

    @pl.when(s > 0)
    def _():
        tile(False)


def _sgu_in_call(x, mod, g, win_bf, tm):
    B, n, _ = x.shape
    gpt = 2
    tw = gpt * SGU_GROUP_DIM
    n_v = SGU_HALF // tw
    row = lambda b, i, s: (b, i, 0)
    return pl.pallas_call(
        functools.partial(_sgu_in_kernel, cw=MXU_DIM, rc=min(tm, FFN_RC)),
        grid=(B, n // tm, n_v),
        in_specs=[pl.BlockSpec((1, tm, D_MODEL), row),
                  pl.BlockSpec((1, 6, D_MODEL), lambda b, i, s: (b, 0, 0)),
                  pl.BlockSpec((1, D_MODEL), lambda b, i, s: (0, 0)),
                  pl.BlockSpec((D_MODEL, tw), lambda b, i, s: (0, n_v + s))],
        out_specs=[pl.BlockSpec((1, tm, D_MODEL), row),
                   pl.BlockSpec((1, tm, tw), lambda b, i, s: (b, i, s)),
                   pl.BlockSpec((1, tm, 1), row)],
        out_shape=[jax.ShapeDtypeStruct((B, n, D_MODEL), BF),
                   jax.ShapeDtypeStruct((B, n, SGU_HALF), BF),
                   jax.ShapeDtypeStruct((B, n, 1), F32)],
        compiler_params=_params("parallel", "parallel", "arbitrary"),
        name="sgu_in",
    )(x, mod, g, win_bf)


def _sgu_out_kernel(h_ref, v_ref, ssq_ref, x_ref, mod_ref, g2_ref, win_ref, gv_ref, ws_ref, bs_ref, wout_ref,
                    o_ref, h2_ref, a0_ref, a1_ref, *, cw, nc, rc, fr):
    t = pl.program_id(2)
    ng = N_SGU_GROUPS
    gd = SGU_GROUP_DIM
    tm = h_ref.shape[1]

    def out_proj(a_ref):
        for r in range(tm // rc):
            rows = slice(r * rc, (r + 1) * rc)
            a = a_ref[rows, :]
            for c in range(D_MODEL // nc):
                cols = slice(c * nc, (c + 1) * nc)
                o_ref[0, rows, cols] += jnp.dot(a, wout_ref[:, cols], preferred_element_type=F32)

    def prep(a_ref):
        for r in range(tm // rc):
            rows = slice(r * rc, (r + 1) * rc)
            h = h_ref[0, rows]
            for c in range(gd // cw):
                cols = slice(c * cw, (c + 1) * cw)
                u = _gelu(jnp.dot(h, win_ref[:, cols], preferred_element_type=F32))
                a_ref[rows, cols] = u.astype(BF)
        rstd = lax.rsqrt(ssq_ref[0] * (1.0 / SGU_HALF) + EPS)
        for c in range(tm // SGU_CHUNK):
            rows = slice(c * SGU_CHUNK, (c + 1) * SGU_CHUNK)
            vn = ((v_ref[0, rows, :].astype(F32) * rstd[rows]) * gv_ref[0]).astype(BF)
            sp = jnp.dot(ws_ref[0], vn, preferred_element_type=F32) + bs_ref[0]
            a_ref[rows, :] = (a_ref[rows, :].astype(F32) * sp).astype(BF)

    @pl.when(t == 0)
    def _():
        o_ref[0] = jnp.zeros(o_ref.shape[1:], F32)
        prep(a0_ref)

    @pl.when((t >= 1) & (t < ng) & ((t & 1) == 1))
    def _():
        out_proj(a0_ref)
        prep(a1_ref)

    @pl.when((t >= 1) & (t < ng) & ((t & 1) == 0))
    def _():
        out_proj(a1_ref)
        prep(a0_ref)

    @pl.when(t == ng)
    def _():
        out_proj(a1_ref if (ng - 1) % 2 == 1 else a0_ref)

    @pl.when(t > ng)
    def _():
        rows = pl.ds(pl.multiple_of((t - ng - 1) * fr, fr), fr)
        xn = x_ref[0] + mod_ref[0, 2:3, :] * o_ref[0, rows, :]
        o_ref[0, rows, :] = xn
        h2_ref[0, rows, :] = _modnorm(xn, g2_ref[...], mod_ref[0, 3:4, :], mod_ref[0, 4:5, :]).astype(BF)


def _sgu_out_call(h, v, ssq, x, mod, g2, win_bf, gv, ws_bf, bs, wout_bf, tm):
    B, n, _ = x.shape
    ng, gd = N_SGU_GROUPS, SGU_GROUP_DIM
    fr = ROW_CHUNK
    nfin = tm // fr
    row = lambda b, i, t: (b, i, 0)
    grp = lambda b, i, t: (jnp.clip(t, 0, ng - 1), 0, 0)
    return pl.pallas_call(
        functools.partial(_sgu_out_kernel, cw=MXU_DIM, nc=FFN_NC, rc=min(tm, FFN_RC), fr=fr),
        grid=(B, n // tm, ng + 1 + nfin),
        in_specs=[pl.BlockSpec((1, tm, D_MODEL), row),
                  pl.BlockSpec((1, tm, gd), lambda b, i, t: (b, i, jnp.clip(t, 0, ng - 1))),
                  pl.BlockSpec((1, tm, 1), row),
                  pl.BlockSpec((1, fr, D_MODEL), lambda b, i, t: (b, i * nfin + jnp.clip(t - ng - 1, 0, nfin - 1), 0)),
                  pl.BlockSpec((1, 6, D_MODEL), lambda b, i, t: (b, 0, 0)),
                  pl.BlockSpec((1, D_MODEL), lambda b, i, t: (0, 0)),
                  pl.BlockSpec((D_MODEL, gd), lambda b, i, t: (0, jnp.clip(t, 0, ng - 1))),
                  pl.BlockSpec((1, 1, gd), grp),
                  pl.BlockSpec((1, SGU_CHUNK, SGU_CHUNK), grp),
                  pl.BlockSpec((1, SGU_CHUNK, 1), grp),
                  pl.BlockSpec((gd, D_MODEL), lambda b, i, t: (jnp.clip(t - 1, 0, ng - 1), 0))],
        out_specs=[pl.BlockSpec((1, tm, D_MODEL), row),
                   pl.BlockSpec((1, tm, D_MODEL), row)],
        out_shape=[jax.ShapeDtypeStruct((B, n, D_MODEL), F32),
                   jax.ShapeDtypeStruct((B, n, D_MODEL), BF)],
        scratch_shapes=[pltpu.VMEM((tm, gd), BF),
                        pltpu.VMEM((tm, gd), BF)],
        compiler_params=pltpu.CompilerParams(dimension_semantics=("parallel", "parallel", "arbitrary"),
                                             vmem_limit_bytes=BIG_VMEM_LIMIT_BYTES),
        name="sgu_out",
    )(h, v, ssq, x, mod, g2, win_bf, gv.reshape(ng, 1, gd), ws_bf, bs.reshape(ng, SGU_CHUNK, 1), wout_bf)


def _sgu_call(x, mod, g, g2, win_bf, gv, ws_bf, bs, wout_bf, tm):
    h, v, ssq = _sgu_in_call(x, mod, g, win_bf, tm)
    return _sgu_out_call(h, v, ssq, x, mod, g2, win_bf, gv, ws_bf, bs, wout_bf, tm)


def _rope_tables(n):
    row, col = jnp.meshgrid(jnp.arange(n // GRID_W), jnp.arange(GRID_W), indexing="ij")
    n_freq = HEAD_DIM // 4
    inv_freq = ROPE_BASE ** (-jnp.arange(n_freq, dtype=F32) / n_freq)
    ang = jnp.concatenate([row.reshape(-1, 1).astype(F32) * inv_freq,
                           col.reshape(-1, 1).astype(F32) * inv_freq], axis=-1)
    ang = jnp.concatenate([ang, ang], axis=-1)
    sign = jnp.where(jnp.arange(HEAD_DIM) < HEAD_DIM // 2, -1.0, 1.0).astype(F32)
    return jnp.cos(ang), jnp.sin(ang) * sign


def kernel(x, c, ctx, c_ctx, w_ada, b_ada, norm_g, w_ffn_gate, w_ffn_up, w_ffn_down, w_attn_qkv,
           w_attn_o, attn_q_g, attn_k_g, attn_sink, w_fourier, w_sgu_in, sgu_v_g, w_sgu_spatial,
           b_sgu_spatial, w_sgu_out):
    B, n, _ = x.shape
    n_ctx = ctx.shape[1]
    tm_x = ROW_TILE
    tm_c = B * n_ctx
    ctx = ctx.reshape(1, tm_c, D_MODEL)

    cond = jnp.zeros((SUBLANES, D_MODEL), F32).at[:B].set(c).at[B].set(c_ctx)
    ada = _ada_all(cond, w_ada, b_ada)
    cos, sin = _rope_tables(n)
    cos_c = jnp.ones((tm_c, HEAD_DIM), F32)
    sin_c = jnp.zeros((tm_c, HEAD_DIM), F32)

    for i in range(DEPTH):
        kind = i % N_MIXERS
        j = i // N_MIXERS
        need_ctx = i < DEPTH - 1
        modx = ada[i, :B].reshape(B, 6, D_MODEL)
        modc = ada[i, B].reshape(1, 6, D_MODEL)
        g1 = norm_g[i, 0].reshape(1, D_MODEL)
        g2 = norm_g[i, 1].reshape(1, D_MODEL)
        if kind == 0:
            wqkv = w_attn_qkv[j].astype(BF)
            wo = w_attn_o[j].astype(BF)
            qg = attn_q_g[j].reshape(1, HEAD_DIM)
            kg = attn_k_g[j].reshape(1, HEAD_DIM)
            sink = attn_sink[j]
            q, k, v = _qkv_call(x, modx, g1, wqkv, qg, kg, cos, sin, QKV_TM, ROW_CHUNK)
            qc, kc, vc = _qkv_call(ctx, modc, g1, wqkv, qg, kg, cos_c, sin_c, tm_c, ROW_CHUNK)
            kc = kc.reshape(B, n_ctx, HKV)
            vc = vc.reshape(B, n_ctx, HKV)
            ox = _attn_call(q, k, v, kc, vc, sink, ATTN_QBLOCKS)
            x, hx2 = _mm_res_call(ox, wo, x, modx, g2, tm_x, ROW_CHUNK)
            if need_ctx:
                oc = _attn_ctx_call(qc.reshape(B, n_ctx, HQ), kc, vc, sink).reshape(1, tm_c, HQ)
                ctx, hc2 = _mm_res_call(oc, wo, ctx, modc, g2, tm_c, ROW_CHUNK)
        elif kind == 1:
            wf = w_fourier[j].astype(BF)
            kron1, kron2 = _dft_tables(n)
            yx = _fourier_positions(x, ssq_x, modx, g1, kron1, kron2, _channel_dft_table(n), DFT_TC)
            x, hx2 = _mm_res_call(yx, wf, x, modx, g2, tm_x, ROW_CHUNK)
            if need_ctx:
                hc = _modnorm_call(ctx, modc, g1, tm_c).reshape(B, n_ctx, D_MODEL)
                yc = _fourier_ctx(hc, _channel_dft_table(n_ctx)).reshape(1, tm_c, D_MODEL)
                ctx, hc2 = _mm_res_call(yc, wf, ctx, modc, g2, tm_c, ROW_CHUNK)
        else:
            win = w_sgu_in[j].astype(BF)
            wout = w_sgu_out[j].astype(BF)
            ws = w_sgu_spatial[j].astype(BF)
            args = (win, sgu_v_g[j], ws, b_sgu_spatial[j], wout)
            x, hx2 = _sgu_call(x, modx, g1, g2, *args, SGU_TM)
            if need_ctx:
                ctx, hc2 = _sgu_call(ctx, modc, g1, g2, *args, tm_c)
        if i == 0:
            ctx, wg, wu, wd = _ffn_cast_call(hc2, ctx, modc, w_ffn_gate, w_ffn_up, w_ffn_down, i,
                                             tm_c, FFN_CAST_TF, tm_c, FFN_NC)
        elif need_ctx:
            (ctx,) = _ffn_call(hc2, ctx, modc, wg, wu, wd, tm_c, FFN_TF, tm_c, FFN_NC, False)
        cast_next = (w_ffn_gate, w_ffn_up, w_ffn_down, i + 1) if i + 1 < DEPTH else None
        emit_ssq = (i + 1) % N_MIXERS == 1 and i + 1 < DEPTH
        outs = list(_ffn_call(hx2, x, modx, wg, wu, wd, FFN_TM, FFN_TF, FFN_RC, FFN_NC, emit_ssq, cast_next))
        x = outs.pop(0)
        if emit_ssq:
            ssq_x = outs.pop(0)
        if cast_next is not None:
            wg, wu, wd = outs
    return x
```

```python
import functools

import numpy as np
import jax
import jax.numpy as jnp
from jax import lax
from jax.experimental import pallas as pl
from jax.experimental.pallas import tpu as pltpu

D_MODEL = 2048
DEPTH = 4
GRID_W = 64
N_MIXERS = 3
EPS = 1e-6
N_HEADS = 16
N_KV_HEADS = 4
HEAD_DIM = D_MODEL // N_HEADS
KV_GROUP = N_HEADS // N_KV_HEADS
HQ = N_HEADS * HEAD_DIM
HKV = N_KV_HEADS * HEAD_DIM
WINDOW = 128
BLOCK = 128
ROPE_BASE = 10000.0
N_FOURIER_GROUPS = 8
FOURIER_GROUP_DIM = D_MODEL // N_FOURIER_GROUPS
SGU_CHUNK = 128
SGU_HALF = 3 * D_MODEL
N_SGU_GROUPS = 8
SGU_GROUP_DIM = SGU_HALF // N_SGU_GROUPS
D_FF = 5632

BF = jnp.bfloat16
F32 = jnp.float32

V7X_VMEM_BYTES = 64 * 1024 * 1024
VMEM_LIMIT_BYTES = V7X_VMEM_BYTES - 8 * 1024 * 1024
BIG_VMEM_LIMIT_BYTES = V7X_VMEM_BYTES - 4 * 1024 * 1024
SUBLANES = 8
MXU_DIM = 256

ROW_TILE = 512
SGU_TM = 1024
ROW_CHUNK = MXU_DIM
QKV_TM = 1024
FFN_TM = 1024
FFN_TF = 512
FFN_RC = 512
FFN_NC = 512
FFN_CAST_TF = MXU_DIM
ATTN_QBLOCKS = 16
DFT_TC = 1024
ADA_TN = 1024
NEG = -1e30
NO_CAP = float(np.finfo(np.float32).max)
LOG2E = float(np.log2(np.e))

DFT_N2 = 64


def _params(*sem):
    return pltpu.CompilerParams(dimension_semantics=sem, vmem_limit_bytes=VMEM_LIMIT_BYTES)


def _big_params(*sem):
    return pltpu.CompilerParams(dimension_semantics=sem, vmem_limit_bytes=BIG_VMEM_LIMIT_BYTES)


def _resident(shape, index_map):
    return pl.BlockSpec(shape, index_map, pipeline_mode=pl.Buffered(1))


def _modnorm(x, g, shift, scale):
    ms = jnp.mean(x * x, axis=-1, keepdims=True)
    y = x * lax.rsqrt(ms + EPS)
    return (y * g) * (1.0 + scale) + shift


def _silu(t):
    return t * jax.nn.sigmoid(t)


def _ada_kernel(cond_ref, w_ref, b_ref, o_ref):
    s = _silu(cond_ref[...]).astype(BF)
    o_ref[0] = jnp.dot(s, w_ref[0].astype(BF), preferred_element_type=F32) + b_ref[0]


def _ada_all(cond, w_ada, b_ada):
    tn = ADA_TN
    return pl.pallas_call(
        _ada_kernel,
        grid=(DEPTH, 6 * D_MODEL // tn),
        in_specs=[pl.BlockSpec((SUBLANES, D_MODEL), lambda l, j: (0, 0)),
                  pl.BlockSpec((1, D_MODEL, tn), lambda l, j: (l, 0, j)),
                  pl.BlockSpec((1, 1, tn), lambda l, j: (l, 0, j))],
        out_specs=pl.BlockSpec((1, SUBLANES, tn), lambda l, j: (l, 0, j)),
        out_shape=jax.ShapeDtypeStruct((DEPTH, SUBLANES, 6 * D_MODEL), F32),
        compiler_params=_params("arbitrary", "arbitrary"),
        name="ada",
    )(cond, w_ada, b_ada.reshape(DEPTH, 1, 6 * D_MODEL))


def _modnorm_kernel(x_ref, mod_ref, g_ref, o_ref):
    o_ref[0] = _modnorm(x_ref[0], g_ref[...], mod_ref[0, 0:1, :], mod_ref[0, 1:2, :])


def _modnorm_call(x, mod, g, tm):
    B, n, _ = x.shape
    return pl.pallas_call(
        _modnorm_kernel,
        grid=(B, n // tm),
        in_specs=[pl.BlockSpec((1, tm, D_MODEL), lambda b, i: (b, i, 0)),
                  pl.BlockSpec((1, 6, D_MODEL), lambda b, i: (b, 0, 0)),
                  pl.BlockSpec((1, D_MODEL), lambda b, i: (0, 0))],
        out_specs=pl.BlockSpec((1, tm, D_MODEL), lambda b, i: (b, i, 0)),
        out_shape=jax.ShapeDtypeStruct((B, n, D_MODEL), F32),
        compiler_params=_params("parallel", "parallel"),
        name="modnorm",
    )(x, mod, g)


def _qkv_kernel(x_ref, mod_ref, g_ref, w_ref, qg_ref, kg_ref, cos_ref, sin_ref, q_ref, k_ref, v_ref, *, rc):
    tm = x_ref.shape[1]
    qscale = (HEAD_DIM ** -0.5) * LOG2E
    cw = 4 * HEAD_DIM

    for r in range(tm // rc):
        rows = slice(r * rc, (r + 1) * rc)
        h = _modnorm(x_ref[0, rows], g_ref[...], mod_ref[0, 0:1, :], mod_ref[0, 1:2, :]).astype(BF)
        cos = cos_ref[rows]
        sin = sin_ref[rows]

        def head(t, gain):
            ms = jnp.mean(t * t, axis=-1, keepdims=True)
            t = (t * lax.rsqrt(ms + EPS)) * gain
            return t * cos + pltpu.roll(t, HEAD_DIM // 2, 1) * sin

        t = jnp.dot(h, w_ref[:, HQ:HQ + HKV], preferred_element_type=F32)
        for j in range(N_KV_HEADS):
            tj = head(t[:, j * HEAD_DIM:(j + 1) * HEAD_DIM], kg_ref[...])
            k_ref[0, rows, j * HEAD_DIM:(j + 1) * HEAD_DIM] = tj.astype(BF)
        for c in range(HQ // cw):
            t = jnp.dot(h, w_ref[:, c * cw:(c + 1) * cw], preferred_element_type=F32)
            for j in range(4):
                tj = head(t[:, j * HEAD_DIM:(j + 1) * HEAD_DIM], qg_ref[...]) * qscale
                q_ref[0, rows, c * cw + j * HEAD_DIM:c * cw + (j + 1) * HEAD_DIM] = tj.astype(BF)
        t = jnp.dot(h, w_ref[:, HQ + HKV:], preferred_element_type=F32)
        v_ref[0, rows] = t.astype(BF)


def _qkv_call(x, mod, g, w_bf, qg, kg, cos, sin, tm, rc):
    B, n, _ = x.shape
    row = lambda b, i: (b, i, 0)
    return pl.pallas_call(
        functools.partial(_qkv_kernel, rc=rc),
        grid=(B, n // tm),
        in_specs=[pl.BlockSpec((1, tm, D_MODEL), row),
                  pl.BlockSpec((1, 6, D_MODEL), lambda b, i: (b, 0, 0)),
                  pl.BlockSpec((1, D_MODEL), lambda b, i: (0, 0)),
                  _resident((D_MODEL, HQ + 2 * HKV), lambda b, i: (0, 0)),
                  pl.BlockSpec((1, HEAD_DIM), lambda b, i: (0, 0)),
                  pl.BlockSpec((1, HEAD_DIM), lambda b, i: (0, 0)),
                  pl.BlockSpec((tm, HEAD_DIM), lambda b, i: (i, 0)),
                  pl.BlockSpec((tm, HEAD_DIM), lambda b, i: (i, 0))],
        out_specs=[pl.BlockSpec((1, tm, HQ), row),
                   pl.BlockSpec((1, tm, HKV), row),
                   pl.BlockSpec((1, tm, HKV), row)],
        out_shape=[jax.ShapeDtypeStruct((B, n, HQ), BF),
                   jax.ShapeDtypeStruct((B, n, HKV), BF),
                   jax.ShapeDtypeStruct((B, n, HKV), BF)],
        compiler_params=_params("parallel", "parallel"),
        name="qkv",
    )(x, mod, g, w_bf, qg, kg, cos, sin)


def _softmax_pv(s, sink_col, vcat):
    m = jnp.maximum(jnp.max(s, axis=-1, keepdims=True), sink_col)
    p = jnp.exp2(s - m)
    den = jnp.sum(p, axis=-1, keepdims=True) + jnp.exp2(sink_col - m)
    return jnp.dot(p.astype(BF), vcat, preferred_element_type=F32) / den


def _sink_column(sink_ref, h, rows):
    ridx = lax.broadcasted_iota(jnp.int32, (KV_GROUP * rows, 1), 0)
    col = jnp.full((KV_GROUP * rows, 1), sink_ref[h * KV_GROUP] * LOG2E, F32)
    for g in range(1, KV_GROUP):
        col = jnp.where(ridx >= g * rows, sink_ref[h * KV_GROUP + g] * LOG2E, col)
    return col


def _stack_heads(q):
    return jnp.concatenate([q[:, g * HEAD_DIM:(g + 1) * HEAD_DIM] for g in range(KV_GROUP)], axis=0)


def _attn_kernel(sink_ref, band_ref, q_ref, kp_ref, kc_ref, kn_ref, vp_ref, vc_ref, vn_ref, kx_ref, vx_ref,
                 o_ref, *, n_ctx, qblocks):
    h = pl.program_id(1)
    i = pl.program_id(2)
    last = pl.num_programs(2) - 1
    kloc = jnp.concatenate([kp_ref[0], kc_ref[0], kn_ref[0]], axis=0)
    vloc = jnp.concatenate([vp_ref[0], vc_ref[0], vn_ref[0]], axis=0)
    sink_col = _sink_column(sink_ref, h, BLOCK)
    for t in range(qblocks):
        qs = _stack_heads(q_ref[0, t * BLOCK:(t + 1) * BLOCK, :])
        kcat = jnp.concatenate([kx_ref[0], kloc[t * BLOCK:(t + 3) * BLOCK]], axis=0)
        vcat = jnp.concatenate([vx_ref[0], vloc[t * BLOCK:(t + 3) * BLOCK]], axis=0)
        s = lax.dot_general(qs, kcat, (((1,), (1,)), ((), ())), preferred_element_type=F32)
        lo = jnp.minimum(s[:, n_ctx:n_ctx + BLOCK], band_ref[:, :BLOCK])
        hi = jnp.minimum(s[:, n_ctx + 2 * BLOCK:], band_ref[:, BLOCK:])
        if t == 0:
            lo = jnp.minimum(lo, jnp.where(i == 0, NEG, NO_CAP))
        if t == qblocks - 1:
            hi = jnp.minimum(hi, jnp.where(i == last, NEG, NO_CAP))
        s = jnp.concatenate([s[:, :n_ctx], lo, s[:, n_ctx + BLOCK:n_ctx + 2 * BLOCK], hi], axis=1)
        o = _softmax_pv(s, sink_col, vcat)
        for g in range(KV_GROUP):
            o_ref[0, t * BLOCK:(t + 1) * BLOCK, g * HEAD_DIM:(g + 1) * HEAD_DIM] = (
                o[g * BLOCK:(g + 1) * BLOCK].astype(BF))


def _band_bias():
    row = np.arange(KV_GROUP * BLOCK)[:, None] % BLOCK
    col = np.arange(BLOCK)[None, :]
    prev_ok = (row - (col - BLOCK)) <= WINDOW
    next_ok = ((col + BLOCK) - row) <= WINDOW
    return jnp.asarray(np.where(np.concatenate([prev_ok, next_ok], axis=1), NO_CAP, NEG), F32)


def _attn_call(q, k, v, kx, vx, sink, qblocks):
    B, n, _ = q.shape
    n_ctx = kx.shape[1]
    nb = n // BLOCK
    tq = qblocks * BLOCK
    gw = KV_GROUP * HEAD_DIM
    prev = lambda b, h, i: (b, jnp.maximum(i * qblocks - 1, 0), h)
    cur = lambda b, h, i: (b, i, h)
    nxt = lambda b, h, i: (b, jnp.minimum((i + 1) * qblocks, nb - 1), h)
    ctx = lambda b, h, i: (b, 0, h)
    edge = lambda m: pl.BlockSpec((1, BLOCK, HEAD_DIM), m)
    mid = pl.BlockSpec((1, tq, HEAD_DIM), cur)
    return pl.pallas_call(
        functools.partial(_attn_kernel, n_ctx=n_ctx, qblocks=qblocks),
        grid=(B, N_KV_HEADS, n // tq),
        in_specs=[pl.BlockSpec(memory_space=pltpu.SMEM),
                  pl.BlockSpec((KV_GROUP * BLOCK, 2 * BLOCK), lambda b, h, i: (0, 0)),
                  pl.BlockSpec((1, tq, gw), cur),
                  edge(prev), mid, edge(nxt), edge(prev), mid, edge(nxt),
                  pl.BlockSpec((1, n_ctx, HEAD_DIM), ctx),
                  pl.BlockSpec((1, n_ctx, HEAD_DIM), ctx)],
        out_specs=pl.BlockSpec((1, tq, gw), cur),
        out_shape=jax.ShapeDtypeStruct((B, n, HQ), BF),
        compiler_params=_params("parallel", "parallel", "arbitrary"),
        name="attn",
    )(sink, _band_bias(), q, k, k, k, v, v, v, kx, vx)


def _attn_ctx_kernel(sink_ref, q_ref, k_ref, v_ref, o_ref, *, n_ctx):
    h = pl.program_id(1)
    qs = _stack_heads(q_ref[0])
    s = lax.dot_general(qs, k_ref[0], (((1,), (1,)), ((), ())), preferred_element_type=F32)
    o = _softmax_pv(s, _sink_column(sink_ref, h, n_ctx), v_ref[0])
    for g in range(KV_GROUP):
        o_ref[0, :, g * HEAD_DIM:(g + 1) * HEAD_DIM] = o[g * n_ctx:(g + 1) * n_ctx].astype(BF)


def _attn_ctx_call(q, k, v, sink):
    B, n_ctx, _ = q.shape
    gw = KV_GROUP * HEAD_DIM
    idx = lambda b, h: (b, 0, h)
    return pl.pallas_call(
        functools.partial(_attn_ctx_kernel, n_ctx=n_ctx),
        grid=(B, N_KV_HEADS),
        in_specs=[pl.BlockSpec(memory_space=pltpu.SMEM),
                  pl.BlockSpec((1, n_ctx, gw), idx),
                  pl.BlockSpec((1, n_ctx, HEAD_DIM), idx),
                  pl.BlockSpec((1, n_ctx, HEAD_DIM), idx)],
        out_specs=pl.BlockSpec((1, n_ctx, gw), idx),
        out_shape=jax.ShapeDtypeStruct((B, n_ctx, HQ), BF),
        compiler_params=_params("parallel", "parallel"),
        name="attn_ctx",
    )(sink, q, k, v)


def _mm_res_kernel(a_ref, w_ref, x_ref, mod_ref, g2_ref, o_ref, h_ref, *, rc):
    tm = x_ref.shape[1]
    for r in range(tm // rc):
        rows = slice(r * rc, (r + 1) * rc)
        p = jnp.dot(a_ref[0, rows].astype(BF), w_ref[...], preferred_element_type=F32)
        xn = x_ref[0, rows] + mod_ref[0, 2:3, :] * p
        o_ref[0, rows] = xn
        h_ref[0, rows] = _modnorm(xn, g2_ref[...], mod_ref[0, 3:4, :], mod_ref[0, 4:5, :]).astype(BF)


def _mm_res_call(a, w_bf, x, mod, g2, tm, rc):
    B, n, K = a.shape
    row = lambda b, i: (b, i, 0)
    return pl.pallas_call(
        functools.partial(_mm_res_kernel, rc=rc),
        grid=(B, n // tm),
        in_specs=[pl.BlockSpec((1, tm, K), row),
                  _resident((K, D_MODEL), lambda b, i: (0, 0)),
                  pl.BlockSpec((1, tm, D_MODEL), row),
                  pl.BlockSpec((1, 6, D_MODEL), lambda b, i: (b, 0, 0)),
                  pl.BlockSpec((1, D_MODEL), lambda b, i: (0, 0))],
        out_specs=[pl.BlockSpec((1, tm, D_MODEL), row),
                   pl.BlockSpec((1, tm, D_MODEL), row)],
        out_shape=[jax.ShapeDtypeStruct((B, n, D_MODEL), F32),
                   jax.ShapeDtypeStruct((B, n, D_MODEL), BF)],
        compiler_params=_params("parallel", "parallel"),
        name="mm_res",
    )(a, w_bf, x, mod, g2)


def _ffn_step(h_ref, x_ref, mod_ref, wg, wu, wd, o_ref, ssq_ref, rc, nc, first, last):
    tm = h_ref.shape[1]
    for r in range(tm // rc):
        rows = slice(r * rc, (r + 1) * rc)
        h = h_ref[0, rows]
        gate = jnp.dot(h, wg[...], preferred_element_type=F32)
        up = jnp.dot(h, wu[...], preferred_element_type=F32)
        a = (_silu(gate) * up).astype(BF)
        sq = None
        for c in range(D_MODEL // nc):
            cols = slice(c * nc, (c + 1) * nc)
            acc = jnp.dot(a, wd[:, cols], preferred_element_type=F32)
            if not first:
                acc = o_ref[0, rows, cols] + acc
            if last:
                acc = x_ref[0, rows, cols] + mod_ref[0, 5:6, cols] * acc
                if ssq_ref is not None:
                    part = jnp.sum(acc * acc, axis=-1, keepdims=True)
                    sq = part if sq is None else sq + part
            o_ref[0, rows, cols] = acc
        if sq is not None:
            ssq_ref[0, rows] = sq


def _ffn_steps(h_ref, x_ref, mod_ref, wg, wu, wd, o_ref, ssq_ref, rc, nc):
    f = pl.program_id(2)
    nf = pl.num_programs(2)

    @pl.when(f == 0)
    def _():
        _ffn_step(h_ref, x_ref, mod_ref, wg, wu, wd, o_ref, ssq_ref, rc, nc, True, False)

    @pl.when((f > 0) & (f < nf - 1))
    def _():
        _ffn_step(h_ref, x_ref, mod_ref, wg, wu, wd, o_ref, ssq_ref, rc, nc, False, False)

    @pl.when(f == nf - 1)
    def _():
        _ffn_step(h_ref, x_ref, mod_ref, wg, wu, wd, o_ref, ssq_ref, rc, nc, False, True)


def _ffn_kernel(h_ref, x_ref, mod_ref, wg_ref, wu_ref, wd_ref, *rest, rc, nc, emit_ssq, side_cast):
    rest = list(rest)
    side_in = [rest.pop(0) for _ in range(3)] if side_cast else []
    o_ref = rest.pop(0)
    ssq_ref = rest.pop(0) if emit_ssq else None
    for src, dst in zip(side_in, rest):
        dst[...] = src[0].astype(BF)
    _ffn_steps(h_ref, x_ref, mod_ref, wg_ref, wu_ref, wd_ref, o_ref, ssq_ref, rc, nc)


def _ffn_cast_kernel(h_ref, x_ref, mod_ref, wg_ref, wu_ref, wd_ref, o_ref, wgb_ref, wub_ref, wdb_ref, *, rc, nc):
    wg = wg_ref[0].astype(BF)
    wu = wu_ref[0].astype(BF)
    wd = wd_ref[0].astype(BF)
    wgb_ref[...] = wg
    wub_ref[...] = wu
    wdb_ref[...] = wd
    _ffn_steps(h_ref, x_ref, mod_ref, wg, wu, wd, o_ref, None, rc, nc)


def _ffn_specs(tm):
    row = lambda b, i, f: (b, i, 0)
    return [pl.BlockSpec((1, tm, D_MODEL), row),
            pl.BlockSpec((1, tm, D_MODEL), row),
            pl.BlockSpec((1, 6, D_MODEL), lambda b, i, f: (b, 0, 0))], pl.BlockSpec((1, tm, D_MODEL), row)


def _slabs(rows, steps):
    tile = 2 * SUBLANES
    count = max(d for d in range(1, min(steps, rows // tile) + 1) if (rows // tile) % d == 0)
    return count, rows // count


def _ffn_call(h, x, mod, wg_bf, wu_bf, wd_bf, tm, tf, rc, nc, emit_ssq, cast_next=None):
    B, n, _ = x.shape
    nt, nf = n // tm, D_FF // tf
    act_specs, out_spec = _ffn_specs(tm)
    in_specs = act_specs + [pl.BlockSpec((D_MODEL, tf), lambda b, i, f: (0, f)),
                            pl.BlockSpec((D_MODEL, tf), lambda b, i, f: (0, f)),
                            pl.BlockSpec((tf, D_MODEL), lambda b, i, f: (f, 0))]
    args = [h, x, mod, wg_bf, wu_bf, wd_bf]
    out_specs = [out_spec]
    out_shape = [jax.ShapeDtypeStruct((B, n, D_MODEL), F32)]
    if emit_ssq:
        out_specs.append(pl.BlockSpec((1, tm, 1), lambda b, i, f: (b, i, 0)))
        out_shape.append(jax.ShapeDtypeStruct((B, n, 1), F32))
    if cast_next is not None:
        wg_all, wu_all, wd_all, layer = cast_next
        steps = B * nt * nf
        step = lambda b, i, f: (b * nt + i) * nf + f
        up_slabs, up_rows = _slabs(D_MODEL, steps)
        down_slabs, down_rows = _slabs(D_FF, steps)
        up_idx = lambda b, i, f: jnp.minimum(step(b, i, f), up_slabs - 1)
        down_idx = lambda b, i, f: jnp.minimum(step(b, i, f), down_slabs - 1)
        in_specs += [pl.BlockSpec((1, up_rows, D_FF), lambda b, i, f: (layer, up_idx(b, i, f), 0)),
                     pl.BlockSpec((1, up_rows, D_FF), lambda b, i, f: (layer, up_idx(b, i, f), 0)),
                     pl.BlockSpec((1, down_rows, D_MODEL), lambda b, i, f: (layer, down_idx(b, i, f), 0))]
        args += [wg_all, wu_all, wd_all]
        out_specs += [pl.BlockSpec((up_rows, D_FF), lambda b, i, f: (up_idx(b, i, f), 0)),
                      pl.BlockSpec((up_rows, D_FF), lambda b, i, f: (up_idx(b, i, f), 0)),
                      pl.BlockSpec((down_rows, D_MODEL), lambda b, i, f: (down_idx(b, i, f), 0))]
        out_shape += [jax.ShapeDtypeStruct((D_MODEL, D_FF), BF),
                      jax.ShapeDtypeStruct((D_MODEL, D_FF), BF),
                      jax.ShapeDtypeStruct((D_FF, D_MODEL), BF)]
    return pl.pallas_call(
        functools.partial(_ffn_kernel, rc=rc, nc=nc, emit_ssq=emit_ssq, side_cast=cast_next is not None),
        grid=(B, nt, nf),
        in_specs=in_specs,
        out_specs=out_specs,
        out_shape=out_shape,
        compiler_params=_big_params("arbitrary", "arbitrary", "arbitrary"),
        name="ffn",
    )(*args)


def _ffn_cast_call(h, x, mod, wg_all, wu_all, wd_all, layer, tm, tf, rc, nc):
    B, n, _ = x.shape
    assert B == 1 and n == tm
    act_specs, out_spec = _ffn_specs(tm)
    up_in = pl.BlockSpec((1, D_MODEL, tf), lambda b, i, f: (layer, 0, f))
    down_in = pl.BlockSpec((1, tf, D_MODEL), lambda b, i, f: (layer, f, 0))
    up_out = pl.BlockSpec((D_MODEL, tf), lambda b, i, f: (0, f))
    down_out = pl.BlockSpec((tf, D_MODEL), lambda b, i, f: (f, 0))
    return pl.pallas_call(
        functools.partial(_ffn_cast_kernel, rc=rc, nc=nc),
        grid=(1, 1, D_FF // tf),
        in_specs=act_specs + [up_in, up_in, down_in],
        out_specs=[out_spec, up_out, up_out, down_out],
        out_shape=[jax.ShapeDtypeStruct((1, n, D_MODEL), F32),
                   jax.ShapeDtypeStruct((D_MODEL, D_FF), BF),
                   jax.ShapeDtypeStruct((D_MODEL, D_FF), BF),
                   jax.ShapeDtypeStruct((D_FF, D_MODEL), BF)],
        compiler_params=_big_params("arbitrary", "arbitrary", "arbitrary"),
        name="ffn_cast",
    )(h, x, mod, wg_all, wu_all, wd_all)


def _dft_tables(n):
    n1, n2, s = n // DFT_N2, DFT_N2, SUBLANES
    j = np.arange(n2 // s)[:, None, None]
    r = np.arange(2 * n1)[None, :, None]
    c = np.arange(n1 * s)[None, None, :]
    part, k1, nn1, l = r // n1, r % n1, c // s, c % s
    idx = (k1 * (s * j + l) + n2 * k1 * nn1 + part * (n // 4)) % n
    m1 = jnp.asarray(np.cos(idx * (2.0 * np.pi / n)), F32)
    same_l = jnp.asarray(np.arange(s)[:, None] == (np.arange(n1 * s)[None, :] % s), F32)
    kron1 = (m1[:, :, None, :] * same_l[None, None, :, :]).reshape(n2 // s, 2 * n1 * s, n1 * s)
    r = np.arange(2 * n2)[:, None]
    c = np.arange(2 * s * n2)[None, :]
    qpart, k2, part, nn2 = r // n2, r % n2, c // (s * n2), c % n2
    idx2 = (k2 * nn2 + (n2 // 4) * (qpart - part) + n2) % n2
    m2 = jnp.asarray(np.cos(idx2 * (2.0 * np.pi / n2)), F32)
    same_l2 = jnp.asarray(np.arange(s)[:, None] == ((np.arange(2 * s * n2)[None, :] // n2) % s), F32)
    kron2 = (m2[:, None, :] * same_l2[None, :, :]).reshape(2 * n2 * s, 2 * s * n2)
    return kron1.astype(BF), kron2.astype(BF)


def _channel_dft_table(n):
    c = np.arange(FOURIER_GROUP_DIM)
    ang = 2.0 * np.pi * ((c[:, None] * c[None, :]) % FOURIER_GROUP_DIM) / FOURIER_GROUP_DIM
    scale = 1.0 / np.sqrt(float(n) * FOURIER_GROUP_DIM)
    return jnp.asarray(np.concatenate([np.cos(ang), np.sin(ang)], axis=0) * scale, BF)


def _channel_mix(zr, zi, cs):
    outs = []
    for g in range(zr.shape[1] // FOURIER_GROUP_DIM):
        sl = slice(g * FOURIER_GROUP_DIM, (g + 1) * FOURIER_GROUP_DIM)
        zg = jnp.concatenate([zr[:, sl], zi[:, sl]], axis=1).astype(BF)
        outs.append(jnp.dot(zg, cs, preferred_element_type=F32))
    return jnp.concatenate(outs, axis=1)


def _dft1_kernel(x_ref, ssq_ref, mod_ref, g_ref, k_ref, t_ref):
    n1, s, tc = x_ref.shape[1], x_ref.shape[2], x_ref.shape[3]
    rstd = lax.rsqrt(ssq_ref[0].reshape(n1 * s, 1) * (1.0 / D_MODEL) + EPS)
    y = x_ref[0].reshape(n1 * s, tc) * rstd
    xb = ((y * g_ref[...]) * (1.0 + mod_ref[0, 1:2, :]) + mod_ref[0, 0:1, :]).astype(BF)
    t = jnp.dot(k_ref[0], xb, preferred_element_type=F32)
    t_ref[0] = t.reshape(2, n1, s, tc)


def _dft2_kernel(t_ref, k_ref, cs_ref, y_ref):
    s, tc = y_ref.shape[2], y_ref.shape[3]
    rows = s * DFT_N2
    tb = t_ref[0, :, 0].reshape(2 * rows, tc).astype(BF)
    z = jnp.dot(k_ref[...], tb, preferred_element_type=F32)
    y = _channel_mix(z[:rows], z[rows:], cs_ref[...])
    y_ref[0] = y.reshape(DFT_N2, s, tc)


def _fourier_positions(x, ssq, mod, g, kron1, kron2, cs, tc):
    B, n, _ = x.shape
    n1, s = n // DFT_N2, SUBLANES
    nj = DFT_N2 // s
    t = pl.pallas_call(
        _dft1_kernel,
        grid=(nj, B, D_MODEL // tc),
        in_specs=[pl.BlockSpec((1, n1, s, tc), lambda j, b, c: (b, 0, j, c)),
                  pl.BlockSpec((1, n1, s, 1), lambda j, b, c: (b, 0, j, 0)),
                  pl.BlockSpec((1, 6, tc), lambda j, b, c: (b, 0, c)),
                  pl.BlockSpec((1, tc), lambda j, b, c: (0, c)),
                  pl.BlockSpec((1, 2 * n1 * s, n1 * s), lambda j, b, c: (j, 0, 0))],
        out_specs=pl.BlockSpec((1, 2, n1, s, tc), lambda j, b, c: (b, 0, 0, j, c)),
        out_shape=jax.ShapeDtypeStruct((B, 2, n1, DFT_N2, D_MODEL), F32),
        compiler_params=_params("arbitrary", "arbitrary", "arbitrary"),
        name="dft1",
    )(x.reshape(B, n1, DFT_N2, D_MODEL), ssq.reshape(B, n1, DFT_N2, 1), mod, g, kron1)
    na = n1 // s
    y = pl.pallas_call(
        _dft2_kernel,
        grid=(B, na, D_MODEL // tc),
        in_specs=[pl.BlockSpec((1, 2, 1, s * DFT_N2, tc), lambda b, a, c: (b, 0, a, 0, c)),
                  _resident((2 * DFT_N2 * s, 2 * s * DFT_N2), lambda b, a, c: (0, 0)),
                  _resident((2 * FOURIER_GROUP_DIM, FOURIER_GROUP_DIM), lambda b, a, c: (0, 0))],
        out_specs=pl.BlockSpec((1, DFT_N2, s, tc), lambda b, a, c: (b, 0, a, c)),
        out_shape=jax.ShapeDtypeStruct((B, DFT_N2, n1, D_MODEL), F32),
        compiler_params=_params("parallel", "parallel", "parallel"),
        name="dft2",
    )(t.reshape(B, 2, na, s * DFT_N2, D_MODEL), kron2, cs)
    return y.reshape(B, n, D_MODEL)


def _dft_ctx_kernel(x_ref, f_ref, cs_ref, y_ref):
    n_ctx = x_ref.shape[1]
    z = jnp.dot(f_ref[...], x_ref[0].astype(BF), preferred_element_type=F32)
    y_ref[0] = _channel_mix(z[:n_ctx], z[n_ctx:], cs_ref[...])


def _fourier_ctx(hc, cs):
    B, n_ctx, _ = hc.shape
    p = np.arange(n_ctx)
    ang = 2.0 * np.pi * ((p[:, None] * p[None, :]) % n_ctx) / n_ctx
    fmat = jnp.asarray(np.concatenate([np.cos(ang), -np.sin(ang)], axis=0), BF)
    return pl.pallas_call(
        _dft_ctx_kernel,
        grid=(B,),
        in_specs=[pl.BlockSpec((1, n_ctx, D_MODEL), lambda b: (b, 0, 0)),
                  pl.BlockSpec((2 * n_ctx, n_ctx), lambda b: (0, 0)),
                  pl.BlockSpec((2 * FOURIER_GROUP_DIM, FOURIER_GROUP_DIM), lambda b: (0, 0))],
        out_specs=pl.BlockSpec((1, n_ctx, D_MODEL), lambda b: (b, 0, 0)),
        out_shape=jax.ShapeDtypeStruct((B, n_ctx, D_MODEL), F32),
        compiler_params=_params("parallel"),
        name="dft_ctx",
    )(hc, fmat, cs)


def _gelu(z):
    return 0.5 * z * (1.0 + lax.erf(z * (2.0 ** -0.5)))


def _sgu_in_kernel(x_ref, mod_ref, g_ref, win_ref, h_ref, v_ref, ssq_ref, *, cw, rc):
    s = pl.program_id(2)
    tm = x_ref.shape[1]

    def tile(first):
        for r in range(tm // rc):
            rows = slice(r * rc, (r + 1) * rc)
            if first:
                h = _modnorm(x_ref[0, rows], g_ref[...], mod_ref[0, 0:1, :], mod_ref[0, 1:2, :]).astype(BF)
                h_ref[0, rows] = h
            else:
                h = h_ref[0, rows]
            sq = None
            for c in range(win_ref.shape[1] // cw):
                cols = slice(c * cw, (c + 1) * cw)
                z = _gelu(jnp.dot(h, win_ref[:, cols], preferred_element_type=F32))
                v_ref[0, rows, cols] = z.astype(BF)
                part = jnp.sum(z * z, axis=-1, keepdims=True)
                sq = part if sq is None else sq + part
            ssq_ref[0, rows] = sq if first else ssq_ref[0, rows] + sq

    @pl.when(s == 0)
    def _():
        tile(True)

    @pl.when(s > 0)
    def _():
        tile(False)


def _sgu_in_call(x, mod, g, win_bf, tm):
    B, n, _ = x.shape
    gpt = 2
    tw = gpt * SGU_GROUP_DIM
    n_v = SGU_HALF // tw
    row = lambda b, i, s: (b, i, 0)
    return pl.pallas_call(
        functools.partial(_sgu_in_kernel, cw=MXU_DIM, rc=min(tm, FFN_RC)),
        grid=(B, n // tm, n_v),
        in_specs=[pl.BlockSpec((1, tm, D_MODEL), row),
                  pl.BlockSpec((1, 6, D_MODEL), lambda b, i, s: (b, 0, 0)),
                  pl.BlockSpec((1, D_MODEL), lambda b, i, s: (0, 0)),
                  pl.BlockSpec((D_MODEL, tw), lambda b, i, s: (0, n_v + s))],
        out_specs=[pl.BlockSpec((1, tm, D_MODEL), row),
                   pl.BlockSpec((1, tm, tw), lambda b, i, s: (b, i, s)),
                   pl.BlockSpec((1, tm, 1), row)],
        out_shape=[jax.ShapeDtypeStruct((B, n, D_MODEL), BF),
                   jax.ShapeDtypeStruct((B, n, SGU_HALF), BF),
                   jax.ShapeDtypeStruct((B, n, 1), F32)],
        compiler_params=_params("parallel", "parallel", "arbitrary"),
        name="sgu_in",
    )(x, mod, g, win_bf)


def _sgu_out_kernel(h_ref, v_ref, ssq_ref, x_ref, mod_ref, g2_ref, win_ref, gv_ref, ws_ref, bs_ref, wout_ref,
                    o_ref, h2_ref, a0_ref, a1_ref, *, cw, nc, rc, fr):
    t = pl.program_id(2)
    ng = N_SGU_GROUPS
    gd = SGU_GROUP_DIM
    tm = h_ref.shape[1]

    def out_proj(a_ref):
        for r in range(tm // rc):
            rows = slice(r * rc, (r + 1) * rc)
            a = a_ref[rows, :]
            for c in range(D_MODEL // nc):
                cols = slice(c * nc, (c + 1) * nc)
                o_ref[0, rows, cols] += jnp.dot(a, wout_ref[:, cols], preferred_element_type=F32)

    def prep(a_ref):
        for r in range(tm // rc):
            rows = slice(r * rc, (r + 1) * rc)
            h = h_ref[0, rows]
            for c in range(gd // cw):
                cols = slice(c * cw, (c + 1) * cw)
                u = _gelu(jnp.dot(h, win_ref[:, cols], preferred_element_type=F32))
                a_ref[rows, cols] = u.astype(BF)
        rstd = lax.rsqrt(ssq_ref[0] * (1.0 / SGU_HALF) + EPS)
        for c in range(tm // SGU_CHUNK):
            rows = slice(c * SGU_CHUNK, (c + 1) * SGU_CHUNK)
            vn = ((v_ref[0, rows, :].astype(F32) * rstd[rows]) * gv_ref[0]).astype(BF)
            sp = jnp.dot(ws_ref[0], vn, preferred_element_type=F32) + bs_ref[0]
            a_ref[rows, :] = (a_ref[rows, :].astype(F32) * sp).astype(BF)

    @pl.when(t == 0)
    def _():
        o_ref[0] = jnp.zeros(o_ref.shape[1:], F32)
        prep(a0_ref)

    @pl.when((t >= 1) & (t < ng) & ((t & 1) == 1))
    def _():
        out_proj(a0_ref)
        prep(a1_ref)

    @pl.when((t >= 1) & (t < ng) & ((t & 1) == 0))
    def _():
        out_proj(a1_ref)
        prep(a0_ref)

    @pl.when(t == ng)
    def _():
        out_proj(a1_ref if (ng - 1) % 2 == 1 else a0_ref)

    @pl.when(t > ng)
    def _():
        rows = pl.ds(pl.multiple_of((t - ng - 1) * fr, fr), fr)
        xn = x_ref[0] + mod_ref[0, 2:3, :] * o_ref[0, rows, :]
        o_ref[0, rows, :] = xn
        h2_ref[0, rows, :] = _modnorm(xn, g2_ref[...], mod_ref[0, 3:4, :], mod_ref[0, 4:5, :]).astype(BF)


def _sgu_out_call(h, v, ssq, x, mod, g2, win_bf, gv, ws_bf, bs, wout_bf, tm):
    B, n, _ = x.shape
    ng, gd = N_SGU_GROUPS, SGU_GROUP_DIM
    fr = ROW_CHUNK
    nfin = tm // fr
    row = lambda b, i, t: (b, i, 0)
    grp = lambda b, i, t: (jnp.clip(t, 0, ng - 1), 0, 0)
    return pl.pallas_call(
        functools.partial(_sgu_out_kernel, cw=MXU_DIM, nc=FFN_NC, rc=min(tm, FFN_RC), fr=fr),
        grid=(B, n // tm, ng + 1 + nfin),
        in_specs=[pl.BlockSpec((1, tm, D_MODEL), row),
                  pl.BlockSpec((1, tm, gd), lambda b, i, t: (b, i, jnp.clip(t, 0, ng - 1))),
                  pl.BlockSpec((1, tm, 1), row),
                  pl.BlockSpec((1, fr, D_MODEL), lambda b, i, t: (b, i * nfin + jnp.clip(t - ng - 1, 0, nfin - 1), 0)),
                  pl.BlockSpec((1, 6, D_MODEL), lambda b, i, t: (b, 0, 0)),
                  pl.BlockSpec((1, D_MODEL), lambda b, i, t: (0, 0)),
                  pl.BlockSpec((D_MODEL, gd), lambda b, i, t: (0, jnp.clip(t, 0, ng - 1))),
                  pl.BlockSpec((1, 1, gd), grp),
                  pl.BlockSpec((1, SGU_CHUNK, SGU_CHUNK), grp),
                  pl.BlockSpec((1, SGU_CHUNK, 1), grp),
                  pl.BlockSpec((gd, D_MODEL), lambda b, i, t: (jnp.clip(t - 1, 0, ng - 1), 0))],
        out_specs=[pl.BlockSpec((1, tm, D_MODEL), row),
                   pl.BlockSpec((1, tm, D_MODEL), row)],
        out_shape=[jax.ShapeDtypeStruct((B, n, D_MODEL), F32),
                   jax.ShapeDtypeStruct((B, n, D_MODEL), BF)],
        scratch_shapes=[pltpu.VMEM((tm, gd), BF),
                        pltpu.VMEM((tm, gd), BF)],
        compiler_params=_big_params("parallel", "parallel", "arbitrary"),
        name="sgu_out",
    )(h, v, ssq, x, mod, g2, win_bf, gv.reshape(ng, 1, gd), ws_bf, bs.reshape(ng, SGU_CHUNK, 1), wout_bf)


def _sgu_call(x, mod, g, g2, win_bf, gv, ws_bf, bs, wout_bf, tm):
    h, v, ssq = _sgu_in_call(x, mod, g, win_bf, tm)
    return _sgu_out_call(h, v, ssq, x, mod, g2, win_bf, gv, ws_bf, bs, wout_bf, tm)


def _rope_tables(n):
    f32 = np.float32
    row, col = np.meshgrid(np.arange(n // GRID_W), np.arange(GRID_W), indexing="ij")
    n_freq = HEAD_DIM // 4
    inv_freq = (f32(ROPE_BASE) ** (-np.arange(n_freq, dtype=f32) / f32(n_freq))).astype(f32)
    ang = np.concatenate([row.reshape(-1, 1).astype(f32) * inv_freq,
                          col.reshape(-1, 1).astype(f32) * inv_freq], axis=-1)
    ang = np.concatenate([ang, ang], axis=-1)
    sign = np.where(np.arange(HEAD_DIM) < HEAD_DIM // 2, -1.0, 1.0).astype(f32)
    return jnp.asarray(np.cos(ang), F32), jnp.asarray(np.sin(ang) * sign, F32)


def kernel(x, c, ctx, c_ctx, w_ada, b_ada, norm_g, w_ffn_gate, w_ffn_up, w_ffn_down, w_attn_qkv,
           w_attn_o, attn_q_g, attn_k_g, attn_sink, w_fourier, w_sgu_in, sgu_v_g, w_sgu_spatial,
           b_sgu_spatial, w_sgu_out):
    B, n, _ = x.shape
    n_ctx = ctx.shape[1]
    tm_x = ROW_TILE
    tm_c = B * n_ctx
    ctx = ctx.reshape(1, tm_c, D_MODEL)

    cond = jnp.zeros((SUBLANES, D_MODEL), F32).at[:B].set(c).at[B].set(c_ctx)
    ada = _ada_all(cond, w_ada, b_ada)
    cos, sin = _rope_tables(n)
    cos_c = jnp.ones((tm_c, HEAD_DIM), F32)
    sin_c = jnp.zeros((tm_c, HEAD_DIM), F32)

    for i in range(DEPTH):
        kind = i % N_MIXERS
        j = i // N_MIXERS
        need_ctx = i < DEPTH - 1
        modx = ada[i, :B].reshape(B, 6, D_MODEL)
        modc = ada[i, B].reshape(1, 6, D_MODEL)
        g1 = norm_g[i, 0].reshape(1, D_MODEL)
        g2 = norm_g[i, 1].reshape(1, D_MODEL)
        if kind == 0:
            wqkv = w_attn_qkv[j].astype(BF)
            wo = w_attn_o[j].astype(BF)
            qg = attn_q_g[j].reshape(1, HEAD_DIM)
            kg = attn_k_g[j].reshape(1, HEAD_DIM)
            sink = attn_sink[j]
            q, k, v = _qkv_call(x, modx, g1, wqkv, qg, kg, cos, sin, QKV_TM, ROW_CHUNK)
            qc, kc, vc = _qkv_call(ctx, modc, g1, wqkv, qg, kg, cos_c, sin_c, tm_c, ROW_CHUNK)
            kc = kc.reshape(B, n_ctx, HKV)
            vc = vc.reshape(B, n_ctx, HKV)
            ox = _attn_call(q, k, v, kc, vc, sink, ATTN_QBLOCKS)
            x, hx2 = _mm_res_call(ox, wo, x, modx, g2, tm_x, ROW_CHUNK)
            if need_ctx:
                oc = _attn_ctx_call(qc.reshape(B, n_ctx, HQ), kc, vc, sink).reshape(1, tm_c, HQ)
                ctx, hc2 = _mm_res_call(oc, wo, ctx, modc, g2, tm_c, ROW_CHUNK)
        elif kind == 1:
            wf = w_fourier[j].astype(BF)
            kron1, kron2 = _dft_tables(n)
            yx = _fourier_positions(x, ssq_x, modx, g1, kron1, kron2, _channel_dft_table(n), DFT_TC)
            x, hx2 = _mm_res_call(yx, wf, x, modx, g2, tm_x, ROW_CHUNK)
            if need_ctx:
                hc = _modnorm_call(ctx, modc, g1, tm_c).reshape(B, n_ctx, D_MODEL)
                yc = _fourier_ctx(hc, _channel_dft_table(n_ctx)).reshape(1, tm_c, D_MODEL)
                ctx, hc2 = _mm_res_call(yc, wf, ctx, modc, g2, tm_c, ROW_CHUNK)
        else:
            win = w_sgu_in[j].astype(BF)
            wout = w_sgu_out[j].astype(BF)
            ws = w_sgu_spatial[j].astype(BF)
            args = (win, sgu_v_g[j], ws, b_sgu_spatial[j], wout)
            x, hx2 = _sgu_call(x, modx, g1, g2, *args, SGU_TM)
            if need_ctx:
                ctx, hc2 = _sgu_call(ctx, modc, g1, g2, *args, tm_c)
        if i == 0:
            ctx, wg, wu, wd = _ffn_cast_call(hc2, ctx, modc, w_ffn_gate, w_ffn_up, w_ffn_down, i,
                                             tm_c, FFN_CAST_TF, tm_c, FFN_NC)
        elif need_ctx:
            (ctx,) = _ffn_call(hc2, ctx, modc, wg, wu, wd, tm_c, FFN_TF, tm_c, FFN_NC, False)
        cast_next = (w_ffn_gate, w_ffn_up, w_ffn_down, i + 1) if i + 1 < DEPTH else None
        emit_ssq = (i + 1) % N_MIXERS == 1 and i + 1 < DEPTH
        outs = list(_ffn_call(hx2, x, modx, wg, wu, wd, FFN_TM, FFN_TF, FFN_RC, FFN_NC, emit_ssq, cast_next))
        x = outs.pop(0)
        if emit_ssq:
            ssq_x = outs.pop(0)
        if cast_next is not None:
            wg, wu, wd = outs
    return x
```

```python
import functools

import numpy as np
import jax
import jax.numpy as jnp
from jax import lax
from jax.experimental import pallas as pl
from jax.experimental.pallas import tpu as pltpu

D_MODEL = 2048
DEPTH = 4
GRID_W = 64
N_MIXERS = 3
EPS = 1e-6
N_HEADS = 16
N_KV_HEADS = 4
HEAD_DIM = D_MODEL // N_HEADS
KV_GROUP = N_HEADS // N_KV_HEADS
HQ = N_HEADS * HEAD_DIM
HKV = N_KV_HEADS * HEAD_DIM
WINDOW = 128
BLOCK = 128
ROPE_BASE = 10000.0
N_FOURIER_GROUPS = 8
FOURIER_GROUP_DIM = D_MODEL // N_FOURIER_GROUPS
SGU_CHUNK = 128
SGU_HALF = 3 * D_MODEL
N_SGU_GROUPS = 8
SGU_GROUP_DIM = SGU_HALF // N_SGU_GROUPS
D_FF = 5632

BF = jnp.bfloat16
F32 = jnp.float32

V7X_VMEM_BYTES = 64 * 1024 * 1024
VMEM_LIMIT_BYTES = V7X_VMEM_BYTES - 8 * 1024 * 1024
BIG_VMEM_LIMIT_BYTES = V7X_VMEM_BYTES - 4 * 1024 * 1024
SUBLANES = 8
MXU_DIM = 256

ROW_TILE = 512
SGU_TM = 1024
ROW_CHUNK = MXU_DIM
QKV_TM = 1024
FFN_TM = 1024
FFN_TF = 512
FFN_RC = 512
FFN_NC = 512
FFN_CAST_TF = MXU_DIM
ATTN_QBLOCKS = 16
DFT_TC = 1024
ADA_TN = 1024
NEG = -1e30
NO_CAP = float(np.finfo(np.float32).max)
LOG2E = float(np.log2(np.e))

DFT_N2 = 64


def _params(*sem):
    return pltpu.CompilerParams(dimension_semantics=sem, vmem_limit_bytes=VMEM_LIMIT_BYTES)


def _big_params(*sem):
    return pltpu.CompilerParams(dimension_semantics=sem, vmem_limit_bytes=BIG_VMEM_LIMIT_BYTES)


def _resident(shape, index_map):
    return pl.BlockSpec(shape, index_map, pipeline_mode=pl.Buffered(1))


def _slabs(rows, steps):
    tile = 2 * SUBLANES
    count = max(d for d in range(1, min(steps, rows // tile) + 1) if (rows // tile) % d == 0)
    return count, rows // count


def _side_cast_specs(weights, steps, step):
    in_specs, args, out_specs, out_shape = [], [], [], []
    for w, layer in weights:
        _, rows, cols = w.shape
        slabs, slab_rows = _slabs(rows, steps)
        idx = lambda *g, slabs=slabs: jnp.minimum(step(*g), slabs - 1)
        in_specs.append(pl.BlockSpec((1, slab_rows, cols), lambda *g, idx=idx, layer=layer: (layer, idx(*g), 0)))
        out_specs.append(pl.BlockSpec((slab_rows, cols), lambda *g, idx=idx: (idx(*g), 0)))
        args.append(w)
        out_shape.append(jax.ShapeDtypeStruct((rows, cols), BF))
    return in_specs, args, out_specs, out_shape


def _side_cast(srcs, dsts):
    for src, dst in zip(srcs, dsts):
        dst[...] = src[0].astype(BF)


def _modnorm(x, g, shift, scale):
    ms = jnp.mean(x * x, axis=-1, keepdims=True)
    y = x * lax.rsqrt(ms + EPS)
    return (y * g) * (1.0 + scale) + shift


def _silu(t):
    return t * jax.nn.sigmoid(t)


def _ada_kernel(cond_ref, w_ref, b_ref, o_ref):
    s = _silu(cond_ref[...]).astype(BF)
    o_ref[0] = jnp.dot(s, w_ref[0].astype(BF), preferred_element_type=F32) + b_ref[0]


def _ada_all(cond, w_ada, b_ada):
    tn = ADA_TN
    return pl.pallas_call(
        _ada_kernel,
        grid=(DEPTH, 6 * D_MODEL // tn),
        in_specs=[pl.BlockSpec((SUBLANES, D_MODEL), lambda l, j: (0, 0)),
                  pl.BlockSpec((1, D_MODEL, tn), lambda l, j: (l, 0, j)),
                  pl.BlockSpec((1, 1, tn), lambda l, j: (l, 0, j))],
        out_specs=pl.BlockSpec((1, SUBLANES, tn), lambda l, j: (l, 0, j)),
        out_shape=jax.ShapeDtypeStruct((DEPTH, SUBLANES, 6 * D_MODEL), F32),
        compiler_params=_params("arbitrary", "arbitrary"),
        name="ada",
    )(cond, w_ada, b_ada.reshape(DEPTH, 1, 6 * D_MODEL))


def _modnorm_kernel(x_ref, mod_ref, g_ref, o_ref):
    o_ref[0] = _modnorm(x_ref[0], g_ref[...], mod_ref[0, 0:1, :], mod_ref[0, 1:2, :])


def _modnorm_call(x, mod, g, tm):
    B, n, _ = x.shape
    return pl.pallas_call(
        _modnorm_kernel,
        grid=(B, n // tm),
        in_specs=[pl.BlockSpec((1, tm, D_MODEL), lambda b, i: (b, i, 0)),
                  pl.BlockSpec((1, 6, D_MODEL), lambda b, i: (b, 0, 0)),
                  pl.BlockSpec((1, D_MODEL), lambda b, i: (0, 0))],
        out_specs=pl.BlockSpec((1, tm, D_MODEL), lambda b, i: (b, i, 0)),
        out_shape=jax.ShapeDtypeStruct((B, n, D_MODEL), F32),
        compiler_params=_params("parallel", "parallel"),
        name="modnorm",
    )(x, mod, g)


def _qkv_kernel(x_ref, mod_ref, g_ref, w_ref, qg_ref, kg_ref, cos_ref, sin_ref, q_ref, k_ref, v_ref, *, rc):
    tm = x_ref.shape[1]
    qscale = (HEAD_DIM ** -0.5) * LOG2E
    cw = 4 * HEAD_DIM

    for r in range(tm // rc):
        rows = slice(r * rc, (r + 1) * rc)
        h = _modnorm(x_ref[0, rows], g_ref[...], mod_ref[0, 0:1, :], mod_ref[0, 1:2, :]).astype(BF)
        cos = cos_ref[rows]
        sin = sin_ref[rows]

        def head(t, gain):
            ms = jnp.mean(t * t, axis=-1, keepdims=True)
            t = (t * lax.rsqrt(ms + EPS)) * gain
            return t * cos + pltpu.roll(t, HEAD_DIM // 2, 1) * sin

        t = jnp.dot(h, w_ref[:, HQ:HQ + HKV], preferred_element_type=F32)
        for j in range(N_KV_HEADS):
            tj = head(t[:, j * HEAD_DIM:(j + 1) * HEAD_DIM], kg_ref[...])
            k_ref[0, rows, j * HEAD_DIM:(j + 1) * HEAD_DIM] = tj.astype(BF)
        for c in range(HQ // cw):
            t = jnp.dot(h, w_ref[:, c * cw:(c + 1) * cw], preferred_element_type=F32)
            for j in range(4):
                tj = head(t[:, j * HEAD_DIM:(j + 1) * HEAD_DIM], qg_ref[...]) * qscale
                q_ref[0, rows, c * cw + j * HEAD_DIM:c * cw + (j + 1) * HEAD_DIM] = tj.astype(BF)
        t = jnp.dot(h, w_ref[:, HQ + HKV:], preferred_element_type=F32)
        v_ref[0, rows] = t.astype(BF)


def _qkv_call(x, mod, g, w_bf, qg, kg, cos, sin, tm, rc):
    B, n, _ = x.shape
    row = lambda b, i: (b, i, 0)
    return pl.pallas_call(
        functools.partial(_qkv_kernel, rc=rc),
        grid=(B, n // tm),
        in_specs=[pl.BlockSpec((1, tm, D_MODEL), row),
                  pl.BlockSpec((1, 6, D_MODEL), lambda b, i: (b, 0, 0)),
                  pl.BlockSpec((1, D_MODEL), lambda b, i: (0, 0)),
                  _resident((D_MODEL, HQ + 2 * HKV), lambda b, i: (0, 0)),
                  pl.BlockSpec((1, HEAD_DIM), lambda b, i: (0, 0)),
                  pl.BlockSpec((1, HEAD_DIM), lambda b, i: (0, 0)),
                  pl.BlockSpec((tm, HEAD_DIM), lambda b, i: (i, 0)),
                  pl.BlockSpec((tm, HEAD_DIM), lambda b, i: (i, 0))],
        out_specs=[pl.BlockSpec((1, tm, HQ), row),
                   pl.BlockSpec((1, tm, HKV), row),
                   pl.BlockSpec((1, tm, HKV), row)],
        out_shape=[jax.ShapeDtypeStruct((B, n, HQ), BF),
                   jax.ShapeDtypeStruct((B, n, HKV), BF),
                   jax.ShapeDtypeStruct((B, n, HKV), BF)],
        compiler_params=_params("parallel", "parallel"),
        name="qkv",
    )(x, mod, g, w_bf, qg, kg, cos, sin)


def _softmax_pv(s, sink_col, vcat):
    m = jnp.maximum(jnp.max(s, axis=-1, keepdims=True), sink_col)
    p = jnp.exp2(s - m)
    den = jnp.sum(p, axis=-1, keepdims=True) + jnp.exp2(sink_col - m)
    return jnp.dot(p.astype(BF), vcat, preferred_element_type=F32) / den


def _sink_column(sink_ref, h, rows):
    ridx = lax.broadcasted_iota(jnp.int32, (KV_GROUP * rows, 1), 0)
    col = jnp.full((KV_GROUP * rows, 1), sink_ref[h * KV_GROUP] * LOG2E, F32)
    for g in range(1, KV_GROUP):
        col = jnp.where(ridx >= g * rows, sink_ref[h * KV_GROUP + g] * LOG2E, col)
    return col


def _stack_heads(q):
    return jnp.concatenate([q[:, g * HEAD_DIM:(g + 1) * HEAD_DIM] for g in range(KV_GROUP)], axis=0)


def _attn_kernel(sink_ref, band_ref, q_ref, kp_ref, kc_ref, kn_ref, vp_ref, vc_ref, vn_ref, kx_ref, vx_ref,
                 *rest, n_ctx, qblocks, n_side):
    rest = list(rest)
    side_in = [rest.pop(0) for _ in range(n_side)]
    o_ref = rest.pop(0)
    _side_cast(side_in, rest)
    h = pl.program_id(1)
    i = pl.program_id(2)
    last = pl.num_programs(2) - 1
    kloc = jnp.concatenate([kp_ref[0], kc_ref[0], kn_ref[0]], axis=0)
    vloc = jnp.concatenate([vp_ref[0], vc_ref[0], vn_ref[0]], axis=0)
    sink_col = _sink_column(sink_ref, h, BLOCK)
    for t in range(qblocks):
        qs = _stack_heads(q_ref[0, t * BLOCK:(t + 1) * BLOCK, :])
        kcat = jnp.concatenate([kx_ref[0], kloc[t * BLOCK:(t + 3) * BLOCK]], axis=0)
        vcat = jnp.concatenate([vx_ref[0], vloc[t * BLOCK:(t + 3) * BLOCK]], axis=0)
        s = lax.dot_general(qs, kcat, (((1,), (1,)), ((), ())), preferred_element_type=F32)
        lo = jnp.minimum(s[:, n_ctx:n_ctx + BLOCK], band_ref[:, :BLOCK])
        hi = jnp.minimum(s[:, n_ctx + 2 * BLOCK:], band_ref[:, BLOCK:])
        if t == 0:
            lo = jnp.minimum(lo, jnp.where(i == 0, NEG, NO_CAP))
        if t == qblocks - 1:
            hi = jnp.minimum(hi, jnp.where(i == last, NEG, NO_CAP))
        s = jnp.concatenate([s[:, :n_ctx], lo, s[:, n_ctx + BLOCK:n_ctx + 2 * BLOCK], hi], axis=1)
        o = _softmax_pv(s, sink_col, vcat)
        for g in range(KV_GROUP):
            o_ref[0, t * BLOCK:(t + 1) * BLOCK, g * HEAD_DIM:(g + 1) * HEAD_DIM] = (
                o[g * BLOCK:(g + 1) * BLOCK].astype(BF))


def _band_bias():
    row = np.arange(KV_GROUP * BLOCK)[:, None] % BLOCK
    col = np.arange(BLOCK)[None, :]
    prev_ok = (row - (col - BLOCK)) <= WINDOW
    next_ok = ((col + BLOCK) - row) <= WINDOW
    return jnp.asarray(np.where(np.concatenate([prev_ok, next_ok], axis=1), NO_CAP, NEG), F32)


def _attn_call(q, k, v, kx, vx, sink, qblocks, cast_weights=()):
    B, n, _ = q.shape
    n_ctx = kx.shape[1]
    nb = n // BLOCK
    tq = qblocks * BLOCK
    nq = n // tq
    gw = KV_GROUP * HEAD_DIM
    side = _side_cast_specs(cast_weights, B * N_KV_HEADS * nq, lambda b, h, i: (b * N_KV_HEADS + h) * nq + i)
    prev = lambda b, h, i: (b, jnp.maximum(i * qblocks - 1, 0), h)
    cur = lambda b, h, i: (b, i, h)
    nxt = lambda b, h, i: (b, jnp.minimum((i + 1) * qblocks, nb - 1), h)
    ctx = lambda b, h, i: (b, 0, h)
    edge = lambda m: pl.BlockSpec((1, BLOCK, HEAD_DIM), m)
    mid = pl.BlockSpec((1, tq, HEAD_DIM), cur)
    return pl.pallas_call(
        functools.partial(_attn_kernel, n_ctx=n_ctx, qblocks=qblocks, n_side=len(side[1])),
        grid=(B, N_KV_HEADS, nq),
        in_specs=[pl.BlockSpec(memory_space=pltpu.SMEM),
                  pl.BlockSpec((KV_GROUP * BLOCK, 2 * BLOCK), lambda b, h, i: (0, 0)),
                  pl.BlockSpec((1, tq, gw), cur),
                  edge(prev), mid, edge(nxt), edge(prev), mid, edge(nxt),
                  pl.BlockSpec((1, n_ctx, HEAD_DIM), ctx),
                  pl.BlockSpec((1, n_ctx, HEAD_DIM), ctx)] + side[0],
        out_specs=[pl.BlockSpec((1, tq, gw), cur)] + side[2],
        out_shape=[jax.ShapeDtypeStruct((B, n, HQ), BF)] + side[3],
        compiler_params=_params("arbitrary", "arbitrary", "arbitrary"),
        name="attn",
    )(sink, _band_bias(), q, k, k, k, v, v, v, kx, vx, *side[1])


def _attn_ctx_kernel(sink_ref, q_ref, k_ref, v_ref, o_ref, *, n_ctx):
    h = pl.program_id(1)
    qs = _stack_heads(q_ref[0])
    s = lax.dot_general(qs, k_ref[0], (((1,), (1,)), ((), ())), preferred_element_type=F32)
    o = _softmax_pv(s, _sink_column(sink_ref, h, n_ctx), v_ref[0])
    for g in range(KV_GROUP):
        o_ref[0, :, g * HEAD_DIM:(g + 1) * HEAD_DIM] = o[g * n_ctx:(g + 1) * n_ctx].astype(BF)


def _attn_ctx_call(q, k, v, sink):
    B, n_ctx, _ = q.shape
    gw = KV_GROUP * HEAD_DIM
    idx = lambda b, h: (b, 0, h)
    return pl.pallas_call(
        functools.partial(_attn_ctx_kernel, n_ctx=n_ctx),
        grid=(B, N_KV_HEADS),
        in_specs=[pl.BlockSpec(memory_space=pltpu.SMEM),
                  pl.BlockSpec((1, n_ctx, gw), idx),
                  pl.BlockSpec((1, n_ctx, HEAD_DIM), idx),
                  pl.BlockSpec((1, n_ctx, HEAD_DIM), idx)],
        out_specs=pl.BlockSpec((1, n_ctx, gw), idx),
        out_shape=jax.ShapeDtypeStruct((B, n_ctx, HQ), BF),
        compiler_params=_params("parallel", "parallel"),
        name="attn_ctx",
    )(sink, q, k, v)


def _mm_res_kernel(a_ref, w_ref, x_ref, mod_ref, g2_ref, o_ref, h_ref, *, rc):
    tm = x_ref.shape[1]
    for r in range(tm // rc):
        rows = slice(r * rc, (r + 1) * rc)
        p = jnp.dot(a_ref[0, rows].astype(BF), w_ref[...], preferred_element_type=F32)
        xn = x_ref[0, rows] + mod_ref[0, 2:3, :] * p
        o_ref[0, rows] = xn
        h_ref[0, rows] = _modnorm(xn, g2_ref[...], mod_ref[0, 3:4, :], mod_ref[0, 4:5, :]).astype(BF)


def _mm_res_call(a, w_bf, x, mod, g2, tm, rc):
    B, n, K = a.shape
    row = lambda b, i: (b, i, 0)
    return pl.pallas_call(
        functools.partial(_mm_res_kernel, rc=rc),
        grid=(B, n // tm),
        in_specs=[pl.BlockSpec((1, tm, K), row),
                  _resident((K, D_MODEL), lambda b, i: (0, 0)),
                  pl.BlockSpec((1, tm, D_MODEL), row),
                  pl.BlockSpec((1, 6, D_MODEL), lambda b, i: (b, 0, 0)),
                  pl.BlockSpec((1, D_MODEL), lambda b, i: (0, 0))],
        out_specs=[pl.BlockSpec((1, tm, D_MODEL), row),
                   pl.BlockSpec((1, tm, D_MODEL), row)],
        out_shape=[jax.ShapeDtypeStruct((B, n, D_MODEL), F32),
                   jax.ShapeDtypeStruct((B, n, D_MODEL), BF)],
        compiler_params=_params("parallel", "parallel"),
        name="mm_res",
    )(a, w_bf, x, mod, g2)


def _ffn_step(h_ref, x_ref, mod_ref, wg, wu, wd, o_ref, ssq_ref, rc, nc, first, last):
    tm = h_ref.shape[1]
    for r in range(tm // rc):
        rows = slice(r * rc, (r + 1) * rc)
        h = h_ref[0, rows]
        gate = jnp.dot(h, wg[...], preferred_element_type=F32)
        up = jnp.dot(h, wu[...], preferred_element_type=F32)
        a = (_silu(gate) * up).astype(BF)
        sq = None
        for c in range(D_MODEL // nc):
            cols = slice(c * nc, (c + 1) * nc)
            acc = jnp.dot(a, wd[:, cols], preferred_element_type=F32)
            if not first:
                acc = o_ref[0, rows, cols] + acc
            if last:
                acc = x_ref[0, rows, cols] + mod_ref[0, 5:6, cols] * acc
                if ssq_ref is not None:
                    part = jnp.sum(acc * acc, axis=-1, keepdims=True)
                    sq = part if sq is None else sq + part
            o_ref[0, rows, cols] = acc
        if sq is not None:
            ssq_ref[0, rows] = sq


def _ffn_steps(h_ref, x_ref, mod_ref, wg, wu, wd, o_ref, ssq_ref, rc, nc):
    f = pl.program_id(2)
    nf = pl.num_programs(2)

    @pl.when(f == 0)
    def _():
        _ffn_step(h_ref, x_ref, mod_ref, wg, wu, wd, o_ref, ssq_ref, rc, nc, True, False)

    @pl.when((f > 0) & (f < nf - 1))
    def _():
        _ffn_step(h_ref, x_ref, mod_ref, wg, wu, wd, o_ref, ssq_ref, rc, nc, False, False)

    @pl.when(f == nf - 1)
    def _():
        _ffn_step(h_ref, x_ref, mod_ref, wg, wu, wd, o_ref, ssq_ref, rc, nc, False, True)


def _ffn_kernel(h_ref, x_ref, mod_ref, wg_ref, wu_ref, wd_ref, *rest, rc, nc, emit_ssq, n_side):
    rest = list(rest)
    side_in = [rest.pop(0) for _ in range(n_side)]
    o_ref = rest.pop(0)
    ssq_ref = rest.pop(0) if emit_ssq else None
    _side_cast(side_in, rest)
    _ffn_steps(h_ref, x_ref, mod_ref, wg_ref, wu_ref, wd_ref, o_ref, ssq_ref, rc, nc)


def _ffn_cast_kernel(h_ref, x_ref, mod_ref, wg_ref, wu_ref, wd_ref, o_ref, wgb_ref, wub_ref, wdb_ref, *, rc, nc):
    wg = wg_ref[0].astype(BF)
    wu = wu_ref[0].astype(BF)
    wd = wd_ref[0].astype(BF)
    wgb_ref[...] = wg
    wub_ref[...] = wu
    wdb_ref[...] = wd
    _ffn_steps(h_ref, x_ref, mod_ref, wg, wu, wd, o_ref, None, rc, nc)


def _ffn_specs(tm):
    row = lambda b, i, f: (b, i, 0)
    return [pl.BlockSpec((1, tm, D_MODEL), row),
            pl.BlockSpec((1, tm, D_MODEL), row),
            pl.BlockSpec((1, 6, D_MODEL), lambda b, i, f: (b, 0, 0))], pl.BlockSpec((1, tm, D_MODEL), row)


def _ffn_call(h, x, mod, wg_bf, wu_bf, wd_bf, tm, tf, rc, nc, emit_ssq, cast_next=None):
    B, n, _ = x.shape
    nt, nf = n // tm, D_FF // tf
    act_specs, out_spec = _ffn_specs(tm)
    in_specs = act_specs + [pl.BlockSpec((D_MODEL, tf), lambda b, i, f: (0, f)),
                            pl.BlockSpec((D_MODEL, tf), lambda b, i, f: (0, f)),
                            pl.BlockSpec((tf, D_MODEL), lambda b, i, f: (f, 0))]
    args = [h, x, mod, wg_bf, wu_bf, wd_bf]
    out_specs = [out_spec]
    out_shape = [jax.ShapeDtypeStruct((B, n, D_MODEL), F32)]
    if emit_ssq:
        out_specs.append(pl.BlockSpec((1, tm, 1), lambda b, i, f: (b, i, 0)))
        out_shape.append(jax.ShapeDtypeStruct((B, n, 1), F32))
    n_side = 0
    if cast_next is not None:
        wg_all, wu_all, wd_all, layer = cast_next
        side = _side_cast_specs([(wg_all, layer), (wu_all, layer), (wd_all, layer)], B * nt * nf,
                                lambda b, i, f: (b * nt + i) * nf + f)
        n_side = len(side[1])
        in_specs += side[0]
        args += side[1]
        out_specs += side[2]
        out_shape += side[3]
    return pl.pallas_call(
        functools.partial(_ffn_kernel, rc=rc, nc=nc, emit_ssq=emit_ssq, n_side=n_side),
        grid=(B, nt, nf),
        in_specs=in_specs,
        out_specs=out_specs,
        out_shape=out_shape,
        compiler_params=_big_params("arbitrary", "arbitrary", "arbitrary"),
        name="ffn",
    )(*args)


def _ffn_cast_call(h, x, mod, wg_all, wu_all, wd_all, layer, tm, tf, rc, nc):
    B, n, _ = x.shape
    assert B == 1 and n == tm
    act_specs, out_spec = _ffn_specs(tm)
    up_in = pl.BlockSpec((1, D_MODEL, tf), lambda b, i, f: (layer, 0, f))
    down_in = pl.BlockSpec((1, tf, D_MODEL), lambda b, i, f: (layer, f, 0))
    up_out = pl.BlockSpec((D_MODEL, tf), lambda b, i, f: (0, f))
    down_out = pl.BlockSpec((tf, D_MODEL), lambda b, i, f: (f, 0))
    return pl.pallas_call(
        functools.partial(_ffn_cast_kernel, rc=rc, nc=nc),
        grid=(1, 1, D_FF // tf),
        in_specs=act_specs + [up_in, up_in, down_in],
        out_specs=[out_spec, up_out, up_out, down_out],
        out_shape=[jax.ShapeDtypeStruct((1, n, D_MODEL), F32),
                   jax.ShapeDtypeStruct((D_MODEL, D_FF), BF),
                   jax.ShapeDtypeStruct((D_MODEL, D_FF), BF),
                   jax.ShapeDtypeStruct((D_FF, D_MODEL), BF)],
        compiler_params=_big_params("arbitrary", "arbitrary", "arbitrary"),
        name="ffn_cast",
    )(h, x, mod, wg_all, wu_all, wd_all)


def _dft_tables(n):
    n1, n2, s = n // DFT_N2, DFT_N2, SUBLANES
    j = np.arange(n2 // s)[:, None, None]
    r = np.arange(2 * n1)[None, :, None]
    c = np.arange(n1 * s)[None, None, :]
    part, k1, nn1, l = r // n1, r % n1, c // s, c % s
    idx = (k1 * (s * j + l) + n2 * k1 * nn1 + part * (n // 4)) % n
    m1 = jnp.asarray(np.cos(idx * (2.0 * np.pi / n)), F32)
    same_l = jnp.asarray(np.arange(s)[:, None] == (np.arange(n1 * s)[None, :] % s), F32)
    kron1 = (m1[:, :, None, :] * same_l[None, None, :, :]).reshape(n2 // s, 2 * n1 * s, n1 * s)
    r = np.arange(2 * n2)[:, None]
    c = np.arange(2 * s * n2)[None, :]
    qpart, k2, part, nn2 = r // n2, r % n2, c // (s * n2), c % n2
    idx2 = (k2 * nn2 + (n2 // 4) * (qpart - part) + n2) % n2
    m2 = jnp.asarray(np.cos(idx2 * (2.0 * np.pi / n2)), F32)
    same_l2 = jnp.asarray(np.arange(s)[:, None] == ((np.arange(2 * s * n2)[None, :] // n2) % s), F32)
    kron2 = (m2[:, None, :] * same_l2[None, :, :]).reshape(2 * n2 * s, 2 * s * n2)
    return kron1.astype(BF), kron2.astype(BF)


def _channel_dft_table(n):
    c = np.arange(FOURIER_GROUP_DIM)
    ang = 2.0 * np.pi * ((c[:, None] * c[None, :]) % FOURIER_GROUP_DIM) / FOURIER_GROUP_DIM
    scale = 1.0 / np.sqrt(float(n) * FOURIER_GROUP_DIM)
    return jnp.asarray(np.concatenate([np.cos(ang), np.sin(ang)], axis=0) * scale, BF)


def _channel_mix(zr, zi, cs):
    outs = []
    for g in range(zr.shape[1] // FOURIER_GROUP_DIM):
        sl = slice(g * FOURIER_GROUP_DIM, (g + 1) * FOURIER_GROUP_DIM)
        zg = jnp.concatenate([zr[:, sl], zi[:, sl]], axis=1).astype(BF)
        outs.append(jnp.dot(zg, cs, preferred_element_type=F32))
    return jnp.concatenate(outs, axis=1)


def _dft1_kernel(x_ref, ssq_ref, mod_ref, g_ref, k_ref, t_ref):
    n1, s, tc = x_ref.shape[1], x_ref.shape[2], x_ref.shape[3]
    rstd = lax.rsqrt(ssq_ref[0].reshape(n1 * s, 1) * (1.0 / D_MODEL) + EPS)
    y = x_ref[0].reshape(n1 * s, tc) * rstd
    xb = ((y * g_ref[...]) * (1.0 + mod_ref[0, 1:2, :]) + mod_ref[0, 0:1, :]).astype(BF)
    t = jnp.dot(k_ref[0], xb, preferred_element_type=F32)
    t_ref[0] = t.reshape(2, n1, s, tc)


def _dft2_kernel(t_ref, k_ref, cs_ref, y_ref):
    s, tc = y_ref.shape[2], y_ref.shape[3]
    rows = s * DFT_N2
    tb = t_ref[0, :, 0].reshape(2 * rows, tc).astype(BF)
    z = jnp.dot(k_ref[...], tb, preferred_element_type=F32)
    y = _channel_mix(z[:rows], z[rows:], cs_ref[...])
    y_ref[0] = y.reshape(DFT_N2, s, tc)


def _fourier_positions(x, ssq, mod, g, kron1, kron2, cs, tc):
    B, n, _ = x.shape
    n1, s = n // DFT_N2, SUBLANES
    nj = DFT_N2 // s
    t = pl.pallas_call(
        _dft1_kernel,
        grid=(nj, B, D_MODEL // tc),
        in_specs=[pl.BlockSpec((1, n1, s, tc), lambda j, b, c: (b, 0, j, c)),
                  pl.BlockSpec((1, n1, s, 1), lambda j, b, c: (b, 0, j, 0)),
                  pl.BlockSpec((1, 6, tc), lambda j, b, c: (b, 0, c)),
                  pl.BlockSpec((1, tc), lambda j, b, c: (0, c)),
                  pl.BlockSpec((1, 2 * n1 * s, n1 * s), lambda j, b, c: (j, 0, 0))],
        out_specs=pl.BlockSpec((1, 2, n1, s, tc), lambda j, b, c: (b, 0, 0, j, c)),
        out_shape=jax.ShapeDtypeStruct((B, 2, n1, DFT_N2, D_MODEL), F32),
        compiler_params=_params("arbitrary", "arbitrary", "arbitrary"),
        name="dft1",
    )(x.reshape(B, n1, DFT_N2, D_MODEL), ssq.reshape(B, n1, DFT_N2, 1), mod, g, kron1)
    na = n1 // s
    y = pl.pallas_call(
        _dft2_kernel,
        grid=(B, na, D_MODEL // tc),
        in_specs=[pl.BlockSpec((1, 2, 1, s * DFT_N2, tc), lambda b, a, c: (b, 0, a, 0, c)),
                  _resident((2 * DFT_N2 * s, 2 * s * DFT_N2), lambda b, a, c: (0, 0)),
                  _resident((2 * FOURIER_GROUP_DIM, FOURIER_GROUP_DIM), lambda b, a, c: (0, 0))],
        out_specs=pl.BlockSpec((1, DFT_N2, s, tc), lambda b, a, c: (b, 0, a, c)),
        out_shape=jax.ShapeDtypeStruct((B, DFT_N2, n1, D_MODEL), F32),
        compiler_params=_params("parallel", "parallel", "parallel"),
        name="dft2",
    )(t.reshape(B, 2, na, s * DFT_N2, D_MODEL), kron2, cs)
    return y.reshape(B, n, D_MODEL)


def _dft_ctx_kernel(x_ref, f_ref, cs_ref, y_ref):
    n_ctx = x_ref.shape[1]
    z = jnp.dot(f_ref[...], x_ref[0].astype(BF), preferred_element_type=F32)
    y_ref[0] = _channel_mix(z[:n_ctx], z[n_ctx:], cs_ref[...])


def _fourier_ctx(hc, cs):
    B, n_ctx, _ = hc.shape
    p = np.arange(n_ctx)
    ang = 2.0 * np.pi * ((p[:, None] * p[None, :]) % n_ctx) / n_ctx
    fmat = jnp.asarray(np.concatenate([np.cos(ang), -np.sin(ang)], axis=0), BF)
    return pl.pallas_call(
        _dft_ctx_kernel,
        grid=(B,),
        in_specs=[pl.BlockSpec((1, n_ctx, D_MODEL), lambda b: (b, 0, 0)),
                  pl.BlockSpec((2 * n_ctx, n_ctx), lambda b: (0, 0)),
                  pl.BlockSpec((2 * FOURIER_GROUP_DIM, FOURIER_GROUP_DIM), lambda b: (0, 0))],
        out_specs=pl.BlockSpec((1, n_ctx, D_MODEL), lambda b: (b, 0, 0)),
        out_shape=jax.ShapeDtypeStruct((B, n_ctx, D_MODEL), F32),
        compiler_params=_params("parallel"),
        name="dft_ctx",
    )(hc, fmat, cs)


def _gelu(z):
    return 0.5 * z * (1.0 + lax.erf(z * (2.0 ** -0.5)))


def _sgu_in_kernel(x_ref, mod_ref, g_ref, win_ref, h_ref, v_ref, ssq_ref, *, cw, rc):
    s = pl.program_id(2)
    tm = x_ref.shape[1]

    def tile(first):
        for r in range(tm // rc):
            rows = slice(r * rc, (r + 1) * rc)
            if first:
                h = _modnorm(x_ref[0, rows], g_ref[...], mod_ref[0, 0:1, :], mod_ref[0, 1:2, :]).astype(BF)
                h_ref[0, rows] = h
            else:
                h = h_ref[0, rows]
            sq = None
            for c in range(win_ref.shape[1] // cw):
                cols = slice(c * cw, (c + 1) * cw)
                z = _gelu(jnp.dot(h, win_ref[:, cols], preferred_element_type=F32))
                v_ref[0, rows, cols] = z.astype(BF)
                part = jnp.sum(z * z, axis=-1, keepdims=True)
                sq = part if sq is None else sq + part
            ssq_ref[0, rows] = sq if first else ssq_ref[0, rows] + sq

    @pl.when(s == 0)
    def _():
        tile(True)

    @pl.when(s > 0)
    def _():
        tile(False)


def _sgu_in_call(x, mod, g, win_bf, tm):
    B, n, _ = x.shape
    gpt = 2
    tw = gpt * SGU_GROUP_DIM
    n_v = SGU_HALF // tw
    row = lambda b, i, s: (b, i, 0)
    return pl.pallas_call(
        functools.partial(_sgu_in_kernel, cw=MXU_DIM, rc=min(tm, FFN_RC)),
        grid=(B, n // tm, n_v),
        in_specs=[pl.BlockSpec((1, tm, D_MODEL), row),
                  pl.BlockSpec((1, 6, D_MODEL), lambda b, i, s: (b, 0, 0)),
                  pl.BlockSpec((1, D_MODEL), lambda b, i, s: (0, 0)),
                  pl.BlockSpec((D_MODEL, tw), lambda b, i, s: (0, n_v + s))],
        out_specs=[pl.BlockSpec((1, tm, D_MODEL), row),
                   pl.BlockSpec((1, tm, tw), lambda b, i, s: (b, i, s)),
                   pl.BlockSpec((1, tm, 1), row)],
        out_shape=[jax.ShapeDtypeStruct((B, n, D_MODEL), BF),
                   jax.ShapeDtypeStruct((B, n, SGU_HALF), BF),
                   jax.ShapeDtypeStruct((B, n, 1), F32)],
        compiler_params=_params("parallel", "parallel", "arbitrary"),
        name="sgu_in",
    )(x, mod, g, win_bf)


def _sgu_out_kernel(h_ref, v_ref, ssq_ref, x_ref, mod_ref, g2_ref, win_ref, gv_ref, ws_ref, bs_ref, wout_ref,
                    o_ref, h2_ref, a_ref, *, cw, nc, rc, fr):
    t = pl.program_id(2)
    ng = N_SGU_GROUPS
    gd = SGU_GROUP_DIM
    tm = h_ref.shape[1]

    def out_proj(slot):
        for r in range(tm // rc):
            rows = slice(r * rc, (r + 1) * rc)
            a = a_ref[slot, rows, :]
            for c in range(D_MODEL // nc):
                cols = slice(c * nc, (c + 1) * nc)
                o_ref[0, rows, cols] += jnp.dot(a, wout_ref[:, cols], preferred_element_type=F32)

    def prep(slot):
        for r in range(tm // rc):
            rows = slice(r * rc, (r + 1) * rc)
            h = h_ref[0, rows]
            for c in range(gd // cw):
                cols = slice(c * cw, (c + 1) * cw)
                u = _gelu(jnp.dot(h, win_ref[:, cols], preferred_element_type=F32))
                a_ref[slot, rows, cols] = u.astype(BF)
        rstd = lax.rsqrt(ssq_ref[0] * (1.0 / SGU_HALF) + EPS)
        for c in range(tm // SGU_CHUNK):
            rows = slice(c * SGU_CHUNK, (c + 1) * SGU_CHUNK)
            vn = ((v_ref[0, rows, :].astype(F32) * rstd[rows]) * gv_ref[0]).astype(BF)
            sp = jnp.dot(ws_ref[0], vn, preferred_element_type=F32) + bs_ref[0]
            a_ref[slot, rows, :] = (a_ref[slot, rows, :].astype(F32) * sp).astype(BF)

    @pl.when(t == 0)
    def _():
        o_ref[0] = jnp.zeros(o_ref.shape[1:], F32)
        prep(0)

    @pl.when((t >= 1) & (t < ng))
    def _():
        out_proj((t - 1) & 1)
        prep(t & 1)

    @pl.when(t == ng)
    def _():
        out_proj((ng - 1) % 2)

    @pl.when(t > ng)
    def _():
        rows = pl.ds(pl.multiple_of((t - ng - 1) * fr, fr), fr)
        xn = x_ref[0] + mod_ref[0, 2:3, :] * o_ref[0, rows, :]
        o_ref[0, rows, :] = xn
        h2_ref[0, rows, :] = _modnorm(xn, g2_ref[...], mod_ref[0, 3:4, :], mod_ref[0, 4:5, :]).astype(BF)


def _sgu_out_call(h, v, ssq, x, mod, g2, win_bf, gv, ws_bf, bs, wout_bf, tm):
    B, n, _ = x.shape
    ng, gd = N_SGU_GROUPS, SGU_GROUP_DIM
    fr = ROW_CHUNK
    nfin = tm // fr
    row = lambda b, i, t: (b, i, 0)
    grp = lambda b, i, t: (jnp.clip(t, 0, ng - 1), 0, 0)
    return pl.pallas_call(
        functools.partial(_sgu_out_kernel, cw=MXU_DIM, nc=FFN_NC, rc=min(tm, FFN_RC), fr=fr),
        grid=(B, n // tm, ng + 1 + nfin),
        in_specs=[pl.BlockSpec((1, tm, D_MODEL), row),
                  pl.BlockSpec((1, tm, gd), lambda b, i, t: (b, i, jnp.clip(t, 0, ng - 1))),
                  pl.BlockSpec((1, tm, 1), row),
                  pl.BlockSpec((1, fr, D_MODEL), lambda b, i, t: (b, i * nfin + jnp.clip(t - ng - 1, 0, nfin - 1), 0)),
                  pl.BlockSpec((1, 6, D_MODEL), lambda b, i, t: (b, 0, 0)),
                  pl.BlockSpec((1, D_MODEL), lambda b, i, t: (0, 0)),
                  pl.BlockSpec((D_MODEL, gd), lambda b, i, t: (0, jnp.clip(t, 0, ng - 1))),
                  pl.BlockSpec((1, 1, gd), grp),
                  pl.BlockSpec((1, SGU_CHUNK, SGU_CHUNK), grp),
                  pl.BlockSpec((1, SGU_CHUNK, 1), grp),
                  pl.BlockSpec((gd, D_MODEL), lambda b, i, t: (jnp.clip(t - 1, 0, ng - 1), 0))],
        out_specs=[pl.BlockSpec((1, tm, D_MODEL), row),
                   pl.BlockSpec((1, tm, D_MODEL), row)],
        out_shape=[jax.ShapeDtypeStruct((B, n, D_MODEL), F32),
                   jax.ShapeDtypeStruct((B, n, D_MODEL), BF)],
        scratch_shapes=[pltpu.VMEM((2, tm, gd), BF)],
        compiler_params=_big_params("parallel", "parallel", "arbitrary"),
        name="sgu_out",
    )(h, v, ssq, x, mod, g2, win_bf, gv.reshape(ng, 1, gd), ws_bf, bs.reshape(ng, SGU_CHUNK, 1), wout_bf)


def _sgu_call(x, mod, g, g2, win_bf, gv, ws_bf, bs, wout_bf, tm):
    h, v, ssq = _sgu_in_call(x, mod, g, win_bf, tm)
    return _sgu_out_call(h, v, ssq, x, mod, g2, win_bf, gv, ws_bf, bs, wout_bf, tm)


def _rope_tables(n):
    f32 = np.float32
    row, col = np.meshgrid(np.arange(n // GRID_W), np.arange(GRID_W), indexing="ij")
    n_freq = HEAD_DIM // 4
    inv_freq = (f32(ROPE_BASE) ** (-np.arange(n_freq, dtype=f32) / f32(n_freq))).astype(f32)
    ang = np.concatenate([row.reshape(-1, 1).astype(f32) * inv_freq,
                          col.reshape(-1, 1).astype(f32) * inv_freq], axis=-1)
    ang = np.concatenate([ang, ang], axis=-1)
    sign = np.where(np.arange(HEAD_DIM) < HEAD_DIM // 2, -1.0, 1.0).astype(f32)
    return jnp.asarray(np.cos(ang), F32), jnp.asarray(np.sin(ang) * sign, F32)


def kernel(x, c, ctx, c_ctx, w_ada, b_ada, norm_g, w_ffn_gate, w_ffn_up, w_ffn_down, w_attn_qkv,
           w_attn_o, attn_q_g, attn_k_g, attn_sink, w_fourier, w_sgu_in, sgu_v_g, w_sgu_spatial,
           b_sgu_spatial, w_sgu_out):
    B, n, _ = x.shape
    n_ctx = ctx.shape[1]
    tm_x = ROW_TILE
    tm_c = B * n_ctx
    ctx = ctx.reshape(1, tm_c, D_MODEL)

    cond = jnp.zeros((SUBLANES, D_MODEL), F32).at[:B].set(c).at[B].set(c_ctx)
    ada = _ada_all(cond, w_ada, b_ada)
    cos, sin = _rope_tables(n)
    cos_c = jnp.ones((tm_c, HEAD_DIM), F32)
    sin_c = jnp.zeros((tm_c, HEAD_DIM), F32)

    bf16_w = {}
    for i in range(DEPTH):
        kind = i % N_MIXERS
        j = i // N_MIXERS
        need_ctx = i < DEPTH - 1
        modx = ada[i, :B].reshape(B, 6, D_MODEL)
        modc = ada[i, B].reshape(1, 6, D_MODEL)
        g1 = norm_g[i, 0].reshape(1, D_MODEL)
        g2 = norm_g[i, 1].reshape(1, D_MODEL)
        if kind == 0:
            wqkv = bf16_w[("qkv", j)] if ("qkv", j) in bf16_w else w_attn_qkv[j].astype(BF)
            wo = bf16_w[("o", j)] if ("o", j) in bf16_w else w_attn_o[j].astype(BF)
            qg = attn_q_g[j].reshape(1, HEAD_DIM)
            kg = attn_k_g[j].reshape(1, HEAD_DIM)
            sink = attn_sink[j]
            q, k, v = _qkv_call(x, modx, g1, wqkv, qg, kg, cos, sin, QKV_TM, ROW_CHUNK)
            qc, kc, vc = _qkv_call(ctx, modc, g1, wqkv, qg, kg, cos_c, sin_c, tm_c, ROW_CHUNK)
            kc = kc.reshape(B, n_ctx, HKV)
            vc = vc.reshape(B, n_ctx, HKV)
            later = []
            if i == 0:
                for i2 in range(1, DEPTH):
                    j2 = i2 // N_MIXERS
                    later += [[("qkv", w_attn_qkv, j2), ("o", w_attn_o, j2)], [("fourier", w_fourier, j2)],
                              [("sgu_in", w_sgu_in, j2), ("sgu_out", w_sgu_out, j2)]][i2 % N_MIXERS]
            outs = _attn_call(q, k, v, kc, vc, sink, ATTN_QBLOCKS, [(w, j2) for _, w, j2 in later])
            ox = outs[0]
            for (name, _, j2), w_bf in zip(later, outs[1:]):
                bf16_w[(name, j2)] = w_bf
            x, hx2 = _mm_res_call(ox, wo, x, modx, g2, tm_x, ROW_CHUNK)
            if need_ctx:
                oc = _attn_ctx_call(qc.reshape(B, n_ctx, HQ), kc, vc, sink).reshape(1, tm_c, HQ)
                ctx, hc2 = _mm_res_call(oc, wo, ctx, modc, g2, tm_c, ROW_CHUNK)
        elif kind == 1:
            wf = bf16_w[("fourier", j)] if ("fourier", j) in bf16_w else w_fourier[j].astype(BF)
            kron1, kron2 = _dft_tables(n)
            yx = _fourier_positions(x, ssq_x, modx, g1, kron1, kron2, _channel_dft_table(n), DFT_TC)
            x, hx2 = _mm_res_call(yx, wf, x, modx, g2, tm_x, ROW_CHUNK)
            if need_ctx:
                hc = _modnorm_call(ctx, modc, g1, tm_c).reshape(B, n_ctx, D_MODEL)
                yc = _fourier_ctx(hc, _channel_dft_table(n_ctx)).reshape(1, tm_c, D_MODEL)
                ctx, hc2 = _mm_res_call(yc, wf, ctx, modc, g2, tm_c, ROW_CHUNK)
        else:
            win = bf16_w[("sgu_in", j)] if ("sgu_in", j) in bf16_w else w_sgu_in[j].astype(BF)
            wout = bf16_w[("sgu_out", j)] if ("sgu_out", j) in bf16_w else w_sgu_out[j].astype(BF)
            ws = w_sgu_spatial[j].astype(BF)
            args = (win, sgu_v_g[j], ws, b_sgu_spatial[j], wout)
            x, hx2 = _sgu_call(x, modx, g1, g2, *args, SGU_TM)
            if need_ctx:
                ctx, hc2 = _sgu_call(ctx, modc, g1, g2, *args, tm_c)
        if i == 0:
            ctx, wg, wu, wd = _ffn_cast_call(hc2, ctx, modc, w_ffn_gate, w_ffn_up, w_ffn_down, i,
                                             tm_c, FFN_CAST_TF, tm_c, FFN_NC)
        elif need_ctx:
            (ctx,) = _ffn_call(hc2, ctx, modc, wg, wu, wd, tm_c, FFN_TF, tm_c, FFN_NC, False)
        cast_next = (w_ffn_gate, w_ffn_up, w_ffn_down, i + 1) if i + 1 < DEPTH else None
        emit_ssq = (i + 1) % N_MIXERS == 1 and i + 1 < DEPTH
        outs = list(_ffn_call(hx2, x, modx, wg, wu, wd, FFN_TM, FFN_TF, FFN_RC, FFN_NC, emit_ssq, cast_next))
        x = outs.pop(0)
        if emit_ssq:
            ssq_x = outs.pop(0)
        if cast_next is not None:
            wg, wu, wd = outs
    return x
```

```python
import functools

import numpy as np
import jax
import jax.numpy as jnp
from jax import lax
from jax.experimental import pallas as pl
from jax.experimental.pallas import tpu as pltpu

D_MODEL = 2048
DEPTH = 4
GRID_W = 64
N_MIXERS = 3
EPS = 1e-6
N_HEADS = 16
N_KV_HEADS = 4
HEAD_DIM = D_MODEL // N_HEADS
KV_GROUP = N_HEADS // N_KV_HEADS
HQ = N_HEADS * HEAD_DIM
HKV = N_KV_HEADS * HEAD_DIM
WINDOW = 128
BLOCK = 128
ROPE_BASE = 10000.0
N_FOURIER_GROUPS = 8
FOURIER_GROUP_DIM = D_MODEL // N_FOURIER_GROUPS
SGU_CHUNK = 128
SGU_HALF = 3 * D_MODEL
N_SGU_GROUPS = 8
SGU_GROUP_DIM = SGU_HALF // N_SGU_GROUPS
D_FF = 5632

BF = jnp.bfloat16
F32 = jnp.float32

V7X_VMEM_BYTES = 64 * 1024 * 1024
VMEM_LIMIT_BYTES = V7X_VMEM_BYTES - 8 * 1024 * 1024
BIG_VMEM_LIMIT_BYTES = V7X_VMEM_BYTES - 4 * 1024 * 1024
SUBLANES = 8
MXU_DIM = 256

ROW_TILE = 512
SGU_TM = 1024
ROW_CHUNK = MXU_DIM
QKV_TM = 1024
FFN_TM = 1024
FFN_TF = 512
FFN_RC = 512
FFN_NC = 512
FFN_CAST_TF = MXU_DIM
ATTN_QBLOCKS = 16
DFT_TC = 1024
ADA_TN = 1024
NEG = -1e30
NO_CAP = float(np.finfo(np.float32).max)
LOG2E = float(np.log2(np.e))

DFT_N2 = 64


def _params(*sem):
    return pltpu.CompilerParams(dimension_semantics=sem, vmem_limit_bytes=VMEM_LIMIT_BYTES)


def _big_params(*sem):
    return pltpu.CompilerParams(dimension_semantics=sem, vmem_limit_bytes=BIG_VMEM_LIMIT_BYTES)


def _resident(shape, index_map):
    return pl.BlockSpec(shape, index_map, pipeline_mode=pl.Buffered(1))


def _slabs(rows, steps):
    tile = 2 * SUBLANES
    count = max(d for d in range(1, min(steps, rows // tile) + 1) if (rows // tile) % d == 0)
    return count, rows // count


def _side_cast_specs(weights, steps, step):
    in_specs, args, out_specs, out_shape = [], [], [], []
    for w, layer in weights:
        _, rows, cols = w.shape
        slabs, slab_rows = _slabs(rows, steps)
        idx = lambda *g, slabs=slabs: jnp.minimum(step(*g), slabs - 1)
        in_specs.append(pl.BlockSpec((1, slab_rows, cols), lambda *g, idx=idx, layer=layer: (layer, idx(*g), 0)))
        out_specs.append(pl.BlockSpec((slab_rows, cols), lambda *g, idx=idx: (idx(*g), 0)))
        args.append(w)
        out_shape.append(jax.ShapeDtypeStruct((rows, cols), BF))
    return in_specs, args, out_specs, out_shape


def _side_cast(srcs, dsts):
    for src, dst in zip(srcs, dsts):
        dst[...] = src[0].astype(BF)


def _modnorm(x, g, shift, scale):
    ms = jnp.mean(x * x, axis=-1, keepdims=True)
    y = x * lax.rsqrt(ms + EPS)
    return (y * g) * (1.0 + scale) + shift


def _silu(t):
    return t * jax.nn.sigmoid(t)


def _ada_kernel(cond_ref, w_ref, b_ref, o_ref):
    s = _silu(cond_ref[...]).astype(BF)
    o_ref[0] = jnp.dot(s, w_ref[0].astype(BF), preferred_element_type=F32) + b_ref[0]


def _ada_all(cond, w_ada, b_ada):
    tn = ADA_TN
    return pl.pallas_call(
        _ada_kernel,
        grid=(DEPTH, 6 * D_MODEL // tn),
        in_specs=[pl.BlockSpec((SUBLANES, D_MODEL), lambda l, j: (0, 0)),
                  pl.BlockSpec((1, D_MODEL, tn), lambda l, j: (l, 0, j)),
                  pl.BlockSpec((1, 1, tn), lambda l, j: (l, 0, j))],
        out_specs=pl.BlockSpec((1, SUBLANES, tn), lambda l, j: (l, 0, j)),
        out_shape=jax.ShapeDtypeStruct((DEPTH, SUBLANES, 6 * D_MODEL), F32),
        compiler_params=_params("arbitrary", "arbitrary"),
        name="ada",
    )(cond, w_ada, b_ada.reshape(DEPTH, 1, 6 * D_MODEL))


def _modnorm_kernel(x_ref, mod_ref, g_ref, o_ref):
    o_ref[0] = _modnorm(x_ref[0], g_ref[...], mod_ref[0, 0:1, :], mod_ref[0, 1:2, :])


def _modnorm_call(x, mod, g, tm):
    B, n, _ = x.shape
    return pl.pallas_call(
        _modnorm_kernel,
        grid=(B, n // tm),
        in_specs=[pl.BlockSpec((1, tm, D_MODEL), lambda b, i: (b, i, 0)),
                  pl.BlockSpec((1, 6, D_MODEL), lambda b, i: (b, 0, 0)),
                  pl.BlockSpec((1, D_MODEL), lambda b, i: (0, 0))],
        out_specs=pl.BlockSpec((1, tm, D_MODEL), lambda b, i: (b, i, 0)),
        out_shape=jax.ShapeDtypeStruct((B, n, D_MODEL), F32),
        compiler_params=_params("parallel", "parallel"),
        name="modnorm",
    )(x, mod, g)


def _qkv_kernel(x_ref, mod_ref, g_ref, w_ref, qg_ref, kg_ref, cos_ref, sin_ref, q_ref, k_ref, v_ref, *, rc):
    tm = x_ref.shape[1]
    qscale = (HEAD_DIM ** -0.5) * LOG2E
    cw = 4 * HEAD_DIM

    for r in range(tm // rc):
        rows = slice(r * rc, (r + 1) * rc)
        h = _modnorm(x_ref[0, rows], g_ref[...], mod_ref[0, 0:1, :], mod_ref[0, 1:2, :]).astype(BF)
        cos = cos_ref[rows]
        sin = sin_ref[rows]

        def head(t, gain):
            ms = jnp.mean(t * t, axis=-1, keepdims=True)
            t = (t * lax.rsqrt(ms + EPS)) * gain
            return t * cos + pltpu.roll(t, HEAD_DIM // 2, 1) * sin

        t = jnp.dot(h, w_ref[:, HQ:HQ + HKV], preferred_element_type=F32)
        for j in range(N_KV_HEADS):
            tj = head(t[:, j * HEAD_DIM:(j + 1) * HEAD_DIM], kg_ref[...])
            k_ref[0, rows, j * HEAD_DIM:(j + 1) * HEAD_DIM] = tj.astype(BF)
        for c in range(HQ // cw):
            t = jnp.dot(h, w_ref[:, c * cw:(c + 1) * cw], preferred_element_type=F32)
            for j in range(4):
                tj = head(t[:, j * HEAD_DIM:(j + 1) * HEAD_DIM], qg_ref[...]) * qscale
                q_ref[0, rows, c * cw + j * HEAD_DIM:c * cw + (j + 1) * HEAD_DIM] = tj.astype(BF)
        t = jnp.dot(h, w_ref[:, HQ + HKV:], preferred_element_type=F32)
        v_ref[0, rows] = t.astype(BF)


def _qkv_call(x, mod, g, w_bf, qg, kg, cos, sin, tm, rc):
    B, n, _ = x.shape
    row = lambda b, i: (b, i, 0)
    return pl.pallas_call(
        functools.partial(_qkv_kernel, rc=rc),
        grid=(B, n // tm),
        in_specs=[pl.BlockSpec((1, tm, D_MODEL), row),
                  pl.BlockSpec((1, 6, D_MODEL), lambda b, i: (b, 0, 0)),
                  pl.BlockSpec((1, D_MODEL), lambda b, i: (0, 0)),
                  _resident((D_MODEL, HQ + 2 * HKV), lambda b, i: (0, 0)),
                  pl.BlockSpec((1, HEAD_DIM), lambda b, i: (0, 0)),
                  pl.BlockSpec((1, HEAD_DIM), lambda b, i: (0, 0)),
                  pl.BlockSpec((tm, HEAD_DIM), lambda b, i: (i, 0)),
                  pl.BlockSpec((tm, HEAD_DIM), lambda b, i: (i, 0))],
        out_specs=[pl.BlockSpec((1, tm, HQ), row),
                   pl.BlockSpec((1, tm, HKV), row),
                   pl.BlockSpec((1, tm, HKV), row)],
        out_shape=[jax.ShapeDtypeStruct((B, n, HQ), BF),
                   jax.ShapeDtypeStruct((B, n, HKV), BF),
                   jax.ShapeDtypeStruct((B, n, HKV), BF)],
        compiler_params=_params("parallel", "parallel"),
        name="qkv",
    )(x, mod, g, w_bf, qg, kg, cos, sin)


def _softmax_pv(s, sink_col, vcat):
    m = jnp.maximum(jnp.max(s, axis=-1, keepdims=True), sink_col)
    p = jnp.exp2(s - m)
    den = jnp.sum(p, axis=-1, keepdims=True) + jnp.exp2(sink_col - m)
    return jnp.dot(p.astype(BF), vcat, preferred_element_type=F32) / den


def _sink_column(sink_ref, h, rows):
    ridx = lax.broadcasted_iota(jnp.int32, (KV_GROUP * rows, 1), 0)
    col = jnp.full((KV_GROUP * rows, 1), sink_ref[h * KV_GROUP] * LOG2E, F32)
    for g in range(1, KV_GROUP):
        col = jnp.where(ridx >= g * rows, sink_ref[h * KV_GROUP + g] * LOG2E, col)
    return col


def _stack_heads(q):
    return jnp.concatenate([q[:, g * HEAD_DIM:(g + 1) * HEAD_DIM] for g in range(KV_GROUP)], axis=0)


def _attn_kernel(sink_ref, band_ref, q_ref, kp_ref, kc_ref, kn_ref, vp_ref, vc_ref, vn_ref, kx_ref, vx_ref,
                 *rest, n_ctx, qblocks, n_side):
    rest = list(rest)
    side_in = [rest.pop(0) for _ in range(n_side)]
    o_ref = rest.pop(0)
    _side_cast(side_in, rest)
    h = pl.program_id(1)
    i = pl.program_id(2)
    last = pl.num_programs(2) - 1
    kloc = jnp.concatenate([kp_ref[0], kc_ref[0], kn_ref[0]], axis=0)
    vloc = jnp.concatenate([vp_ref[0], vc_ref[0], vn_ref[0]], axis=0)
    sink_col = _sink_column(sink_ref, h, BLOCK)
    for t in range(qblocks):
        qs = _stack_heads(q_ref[0, t * BLOCK:(t + 1) * BLOCK, :])
        kcat = jnp.concatenate([kx_ref[0], kloc[t * BLOCK:(t + 3) * BLOCK]], axis=0)
        vcat = jnp.concatenate([vx_ref[0], vloc[t * BLOCK:(t + 3) * BLOCK]], axis=0)
        s = lax.dot_general(qs, kcat, (((1,), (1,)), ((), ())), preferred_element_type=F32)
        lo = jnp.minimum(s[:, n_ctx:n_ctx + BLOCK], band_ref[:, :BLOCK])
        hi = jnp.minimum(s[:, n_ctx + 2 * BLOCK:], band_ref[:, BLOCK:])
        if t == 0:
            lo = jnp.minimum(lo, jnp.where(i == 0, NEG, NO_CAP))
        if t == qblocks - 1:
            hi = jnp.minimum(hi, jnp.where(i == last, NEG, NO_CAP))
        s = jnp.concatenate([s[:, :n_ctx], lo, s[:, n_ctx + BLOCK:n_ctx + 2 * BLOCK], hi], axis=1)
        o = _softmax_pv(s, sink_col, vcat)
        for g in range(KV_GROUP):
            o_ref[0, t * BLOCK:(t + 1) * BLOCK, g * HEAD_DIM:(g + 1) * HEAD_DIM] = (
                o[g * BLOCK:(g + 1) * BLOCK].astype(BF))


def _band_bias():
    row = np.arange(KV_GROUP * BLOCK)[:, None] % BLOCK
    col = np.arange(BLOCK)[None, :]
    prev_ok = (row - (col - BLOCK)) <= WINDOW
    next_ok = ((col + BLOCK) - row) <= WINDOW
    return jnp.asarray(np.where(np.concatenate([prev_ok, next_ok], axis=1), NO_CAP, NEG), F32)


def _attn_call(q, k, v, kx, vx, sink, qblocks, cast_weights=()):
    B, n, _ = q.shape
    n_ctx = kx.shape[1]
    nb = n // BLOCK
    tq = qblocks * BLOCK
    nq = n // tq
    gw = KV_GROUP * HEAD_DIM
    side = _side_cast_specs(cast_weights, B * N_KV_HEADS * nq, lambda b, h, i: (b * N_KV_HEADS + h) * nq + i)
    prev = lambda b, h, i: (b, jnp.maximum(i * qblocks - 1, 0), h)
    cur = lambda b, h, i: (b, i, h)
    nxt = lambda b, h, i: (b, jnp.minimum((i + 1) * qblocks, nb - 1), h)
    ctx = lambda b, h, i: (b, 0, h)
    edge = lambda m: pl.BlockSpec((1, BLOCK, HEAD_DIM), m)
    mid = pl.BlockSpec((1, tq, HEAD_DIM), cur)
    return pl.pallas_call(
        functools.partial(_attn_kernel, n_ctx=n_ctx, qblocks=qblocks, n_side=len(side[1])),
        grid=(B, N_KV_HEADS, nq),
        in_specs=[pl.BlockSpec(memory_space=pltpu.SMEM),
                  pl.BlockSpec((KV_GROUP * BLOCK, 2 * BLOCK), lambda b, h, i: (0, 0)),
                  pl.BlockSpec((1, tq, gw), cur),
                  edge(prev), mid, edge(nxt), edge(prev), mid, edge(nxt),
                  pl.BlockSpec((1, n_ctx, HEAD_DIM), ctx),
                  pl.BlockSpec((1, n_ctx, HEAD_DIM), ctx)] + side[0],
        out_specs=[pl.BlockSpec((1, tq, gw), cur)] + side[2],
        out_shape=[jax.ShapeDtypeStruct((B, n, HQ), BF)] + side[3],
        compiler_params=_params("arbitrary", "arbitrary", "arbitrary"),
        name="attn",
    )(sink, _band_bias(), q, k, k, k, v, v, v, kx, vx, *side[1])


def _attn_ctx_kernel(sink_ref, q_ref, k_ref, v_ref, o_ref, *, n_ctx):
    h = pl.program_id(1)
    qs = _stack_heads(q_ref[0])
    s = lax.dot_general(qs, k_ref[0], (((1,), (1,)), ((), ())), preferred_element_type=F32)
    o = _softmax_pv(s, _sink_column(sink_ref, h, n_ctx), v_ref[0])
    for g in range(KV_GROUP):
        o_ref[0, :, g * HEAD_DIM:(g + 1) * HEAD_DIM] = o[g * n_ctx:(g + 1) * n_ctx].astype(BF)


def _attn_ctx_call(q, k, v, sink):
    B, n_ctx, _ = q.shape
    gw = KV_GROUP * HEAD_DIM
    idx = lambda b, h: (b, 0, h)
    return pl.pallas_call(
        functools.partial(_attn_ctx_kernel, n_ctx=n_ctx),
        grid=(B, N_KV_HEADS),
        in_specs=[pl.BlockSpec(memory_space=pltpu.SMEM),
                  pl.BlockSpec((1, n_ctx, gw), idx),
                  pl.BlockSpec((1, n_ctx, HEAD_DIM), idx),
                  pl.BlockSpec((1, n_ctx, HEAD_DIM), idx)],
        out_specs=pl.BlockSpec((1, n_ctx, gw), idx),
        out_shape=jax.ShapeDtypeStruct((B, n_ctx, HQ), BF),
        compiler_params=_params("parallel", "parallel"),
        name="attn_ctx",
    )(sink, q, k, v)


def _mm_res_kernel(a_ref, w_ref, x_ref, mod_ref, g2_ref, o_ref, h_ref, *, rc):
    tm = x_ref.shape[1]
    for r in range(tm // rc):
        rows = slice(r * rc, (r + 1) * rc)
        p = jnp.dot(a_ref[0, rows].astype(BF), w_ref[...], preferred_element_type=F32)
        xn = x_ref[0, rows] + mod_ref[0, 2:3, :] * p
        o_ref[0, rows] = xn
        h_ref[0, rows] = _modnorm(xn, g2_ref[...], mod_ref[0, 3:4, :], mod_ref[0, 4:5, :]).astype(BF)


def _mm_res_call(a, w_bf, x, mod, g2, tm, rc):
    B, n, K = a.shape
    row = lambda b, i: (b, i, 0)
    return pl.pallas_call(
        functools.partial(_mm_res_kernel, rc=rc),
        grid=(B, n // tm),
        in_specs=[pl.BlockSpec((1, tm, K), row),
                  _resident((K, D_MODEL), lambda b, i: (0, 0)),
                  pl.BlockSpec((1, tm, D_MODEL), row),
                  pl.BlockSpec((1, 6, D_MODEL), lambda b, i: (b, 0, 0)),
                  pl.BlockSpec((1, D_MODEL), lambda b, i: (0, 0))],
        out_specs=[pl.BlockSpec((1, tm, D_MODEL), row),
                   pl.BlockSpec((1, tm, D_MODEL), row)],
        out_shape=[jax.ShapeDtypeStruct((B, n, D_MODEL), F32),
                   jax.ShapeDtypeStruct((B, n, D_MODEL), BF)],
        compiler_params=_params("parallel", "parallel"),
        name="mm_res",
    )(a, w_bf, x, mod, g2)


def _ffn_step(h_ref, x_ref, mod_ref, wg, wu, wd, o_ref, ssq_ref, rc, nc, first, last):
    tm = h_ref.shape[1]
    for r in range(tm // rc):
        rows = slice(r * rc, (r + 1) * rc)
        h = h_ref[0, rows]
        gate = jnp.dot(h, wg[...], preferred_element_type=F32)
        up = jnp.dot(h, wu[...], preferred_element_type=F32)
        a = (_silu(gate) * up).astype(BF)
        sq = None
        for c in range(D_MODEL // nc):
            cols = slice(c * nc, (c + 1) * nc)
            acc = jnp.dot(a, wd[:, cols], preferred_element_type=F32)
            if not first:
                acc = o_ref[0, rows, cols] + acc
            if last:
                acc = x_ref[0, rows, cols] + mod_ref[0, 5:6, cols] * acc
                if ssq_ref is not None:
                    part = jnp.sum(acc * acc, axis=-1, keepdims=True)
                    sq = part if sq is None else sq + part
            o_ref[0, rows, cols] = acc
        if sq is not None:
            ssq_ref[0, rows] = sq


def _ffn_steps(h_ref, x_ref, mod_ref, wg, wu, wd, o_ref, ssq_ref, rc, nc):
    f = pl.program_id(2)
    nf = pl.num_programs(2)

    @pl.when(f == 0)
    def _():
        _ffn_step(h_ref, x_ref, mod_ref, wg, wu, wd, o_ref, ssq_ref, rc, nc, True, False)

    @pl.when((f > 0) & (f < nf - 1))
    def _():
        _ffn_step(h_ref, x_ref, mod_ref, wg, wu, wd, o_ref, ssq_ref, rc, nc, False, False)

    @pl.when(f == nf - 1)
    def _():
        _ffn_step(h_ref, x_ref, mod_ref, wg, wu, wd, o_ref, ssq_ref, rc, nc, False, True)


def _ffn_kernel(h_ref, x_ref, mod_ref, wg_ref, wu_ref, wd_ref, *rest, rc, nc, emit_ssq, n_side):
    rest = list(rest)
    side_in = [rest.pop(0) for _ in range(n_side)]
    o_ref = rest.pop(0)
    ssq_ref = rest.pop(0) if emit_ssq else None
    _side_cast(side_in, rest)
    _ffn_steps(h_ref, x_ref, mod_ref, wg_ref, wu_ref, wd_ref, o_ref, ssq_ref, rc, nc)


def _ffn_cast_kernel(h_ref, x_ref, mod_ref, wg_ref, wu_ref, wd_ref, o_ref, wgb_ref, wub_ref, wdb_ref, *, rc, nc):
    wg = wg_ref[0].astype(BF)
    wu = wu_ref[0].astype(BF)
    wd = wd_ref[0].astype(BF)
    wgb_ref[...] = wg
    wub_ref[...] = wu
    wdb_ref[...] = wd
    _ffn_steps(h_ref, x_ref, mod_ref, wg, wu, wd, o_ref, None, rc, nc)


def _ffn_specs(tm):
    row = lambda b, i, f: (b, i, 0)
    return [pl.BlockSpec((1, tm, D_MODEL), row),
            pl.BlockSpec((1, tm, D_MODEL), row),
            pl.BlockSpec((1, 6, D_MODEL), lambda b, i, f: (b, 0, 0))], pl.BlockSpec((1, tm, D_MODEL), row)


def _ffn_call(h, x, mod, wg_bf, wu_bf, wd_bf, tm, tf, rc, nc, emit_ssq, cast_next=None):
    B, n, _ = x.shape
    nt, nf = n // tm, D_FF // tf
    act_specs, out_spec = _ffn_specs(tm)
    in_specs = act_specs + [pl.BlockSpec((D_MODEL, tf), lambda b, i, f: (0, f)),
                            pl.BlockSpec((D_MODEL, tf), lambda b, i, f: (0, f)),
                            pl.BlockSpec((tf, D_MODEL), lambda b, i, f: (f, 0))]
    args = [h, x, mod, wg_bf, wu_bf, wd_bf]
    out_specs = [out_spec]
    out_shape = [jax.ShapeDtypeStruct((B, n, D_MODEL), F32)]
    if emit_ssq:
        out_specs.append(pl.BlockSpec((1, tm, 1), lambda b, i, f: (b, i, 0)))
        out_shape.append(jax.ShapeDtypeStruct((B, n, 1), F32))
    n_side = 0
    if cast_next is not None:
        wg_all, wu_all, wd_all, layer = cast_next
        side = _side_cast_specs([(wg_all, layer), (wu_all, layer), (wd_all, layer)], B * nt * nf,
                                lambda b, i, f: (b * nt + i) * nf + f)
        n_side = len(side[1])
        in_specs += side[0]
        args += side[1]
        out_specs += side[2]
        out_shape += side[3]
    return pl.pallas_call(
        functools.partial(_ffn_kernel, rc=rc, nc=nc, emit_ssq=emit_ssq, n_side=n_side),
        grid=(B, nt, nf),
        in_specs=in_specs,
        out_specs=out_specs,
        out_shape=out_shape,
        compiler_params=_big_params("arbitrary", "arbitrary", "arbitrary"),
        name="ffn",
    )(*args)


def _ffn_cast_call(h, x, mod, wg_all, wu_all, wd_all, layer, tm, tf, rc, nc):
    B, n, _ = x.shape
    assert B == 1 and n == tm
    act_specs, out_spec = _ffn_specs(tm)
    up_in = pl.BlockSpec((1, D_MODEL, tf), lambda b, i, f: (layer, 0, f))
    down_in = pl.BlockSpec((1, tf, D_MODEL), lambda b, i, f: (layer, f, 0))
    up_out = pl.BlockSpec((D_MODEL, tf), lambda b, i, f: (0, f))
    down_out = pl.BlockSpec((tf, D_MODEL), lambda b, i, f: (f, 0))
    return pl.pallas_call(
        functools.partial(_ffn_cast_kernel, rc=rc, nc=nc),
        grid=(1, 1, D_FF // tf),
        in_specs=act_specs + [up_in, up_in, down_in],
        out_specs=[out_spec, up_out, up_out, down_out],
        out_shape=[jax.ShapeDtypeStruct((1, n, D_MODEL), F32),
                   jax.ShapeDtypeStruct((D_MODEL, D_FF), BF),
                   jax.ShapeDtypeStruct((D_MODEL, D_FF), BF),
                   jax.ShapeDtypeStruct((D_FF, D_MODEL), BF)],
        compiler_params=_big_params("arbitrary", "arbitrary", "arbitrary"),
        name="ffn_cast",
    )(h, x, mod, wg_all, wu_all, wd_all)


def _dft_tables(n):
    n1, n2, s = n // DFT_N2, DFT_N2, SUBLANES
    j = np.arange(n2 // s)[:, None, None]
    r = np.arange(2 * n1)[None, :, None]
    c = np.arange(n1 * s)[None, None, :]
    part, k1, nn1, l = r // n1, r % n1, c // s, c % s
    idx = (k1 * (s * j + l) + n2 * k1 * nn1 + part * (n // 4)) % n
    m1 = jnp.asarray(np.cos(idx * (2.0 * np.pi / n)), F32)
    same_l = jnp.asarray(np.arange(s)[:, None] == (np.arange(n1 * s)[None, :] % s), F32)
    kron1 = (m1[:, :, None, :] * same_l[None, None, :, :]).reshape(n2 // s, 2 * n1 * s, n1 * s)
    r = np.arange(2 * n2)[:, None]
    c = np.arange(2 * s * n2)[None, :]
    qpart, k2, part, nn2 = r // n2, r % n2, c // (s * n2), c % n2
    idx2 = (k2 * nn2 + (n2 // 4) * (qpart - part) + n2) % n2
    m2 = jnp.asarray(np.cos(idx2 * (2.0 * np.pi / n2)), F32)
    same_l2 = jnp.asarray(np.arange(s)[:, None] == ((np.arange(2 * s * n2)[None, :] // n2) % s), F32)
    kron2 = (m2[:, None, :] * same_l2[None, :, :]).reshape(2 * n2 * s, 2 * s * n2)
    return kron1.astype(BF), kron2.astype(BF)


def _channel_dft_table(n):
    c = np.arange(FOURIER_GROUP_DIM)
    ang = 2.0 * np.pi * ((c[:, None] * c[None, :]) % FOURIER_GROUP_DIM) / FOURIER_GROUP_DIM
    scale = 1.0 / np.sqrt(float(n) * FOURIER_GROUP_DIM)
    return jnp.asarray(np.concatenate([np.cos(ang), np.sin(ang)], axis=0) * scale, BF)


def _channel_mix(zr, zi, cs):
    outs = []
    for g in range(zr.shape[1] // FOURIER_GROUP_DIM):
        sl = slice(g * FOURIER_GROUP_DIM, (g + 1) * FOURIER_GROUP_DIM)
        zg = jnp.concatenate([zr[:, sl], zi[:, sl]], axis=1).astype(BF)
        outs.append(jnp.dot(zg, cs, preferred_element_type=F32))
    return jnp.concatenate(outs, axis=1)


def _dft1_kernel(x_ref, ssq_ref, mod_ref, g_ref, k_ref, t_ref):
    n1, s, tc = x_ref.shape[1], x_ref.shape[2], x_ref.shape[3]
    rstd = lax.rsqrt(ssq_ref[0].reshape(n1 * s, 1) * (1.0 / D_MODEL) + EPS)
    y = x_ref[0].reshape(n1 * s, tc) * rstd
    xb = ((y * g_ref[...]) * (1.0 + mod_ref[0, 1:2, :]) + mod_ref[0, 0:1, :]).astype(BF)
    t = jnp.dot(k_ref[0], xb, preferred_element_type=F32)
    t_ref[0] = t.reshape(2, n1, s, tc)


def _dft2_kernel(t_ref, k_ref, cs_ref, y_ref):
    s, tc = y_ref.shape[2], y_ref.shape[3]
    rows = s * DFT_N2
    tb = t_ref[0, :, 0].reshape(2 * rows, tc).astype(BF)
    z = jnp.dot(k_ref[...], tb, preferred_element_type=F32)
    y = _channel_mix(z[:rows], z[rows:], cs_ref[...])
    y_ref[0] = y.reshape(DFT_N2, s, tc)


def _fourier_positions(x, ssq, mod, g, kron1, kron2, cs, tc):
    B, n, _ = x.shape
    n1, s = n // DFT_N2, SUBLANES
    nj = DFT_N2 // s
    t = pl.pallas_call(
        _dft1_kernel,
        grid=(nj, B, D_MODEL // tc),
        in_specs=[pl.BlockSpec((1, n1, s, tc), lambda j, b, c: (b, 0, j, c)),
                  pl.BlockSpec((1, n1, s, 1), lambda j, b, c: (b, 0, j, 0)),
                  pl.BlockSpec((1, 6, tc), lambda j, b, c: (b, 0, c)),
                  pl.BlockSpec((1, tc), lambda j, b, c: (0, c)),
                  pl.BlockSpec((1, 2 * n1 * s, n1 * s), lambda j, b, c: (j, 0, 0))],
        out_specs=pl.BlockSpec((1, 2, n1, s, tc), lambda j, b, c: (b, 0, 0, j, c)),
        out_shape=jax.ShapeDtypeStruct((B, 2, n1, DFT_N2, D_MODEL), F32),
        compiler_params=_params("arbitrary", "arbitrary", "arbitrary"),
        name="dft1",
    )(x.reshape(B, n1, DFT_N2, D_MODEL), ssq.reshape(B, n1, DFT_N2, 1), mod, g, kron1)
    na = n1 // s
    y = pl.pallas_call(
        _dft2_kernel,
        grid=(B, na, D_MODEL // tc),
        in_specs=[pl.BlockSpec((1, 2, 1, s * DFT_N2, tc), lambda b, a, c: (b, 0, a, 0, c)),
                  _resident((2 * DFT_N2 * s, 2 * s * DFT_N2), lambda b, a, c: (0, 0)),
                  _resident((2 * FOURIER_GROUP_DIM, FOURIER_GROUP_DIM), lambda b, a, c: (0, 0))],
        out_specs=pl.BlockSpec((1, DFT_N2, s, tc), lambda b, a, c: (b, 0, a, c)),
        out_shape=jax.ShapeDtypeStruct((B, DFT_N2, n1, D_MODEL), F32),
        compiler_params=_params("parallel", "parallel", "parallel"),
        name="dft2",
    )(t.reshape(B, 2, na, s * DFT_N2, D_MODEL), kron2, cs)
    return y.reshape(B, n, D_MODEL)


def _dft_ctx_kernel(x_ref, f_ref, cs_ref, y_ref):
    n_ctx = x_ref.shape[1]
    z = jnp.dot(f_ref[...], x_ref[0].astype(BF), preferred_element_type=F32)
    y_ref[0] = _channel_mix(z[:n_ctx], z[n_ctx:], cs_ref[...])


def _fourier_ctx(hc, cs):
    B, n_ctx, _ = hc.shape
    p = np.arange(n_ctx)
    ang = 2.0 * np.pi * ((p[:, None] * p[None, :]) % n_ctx) / n_ctx
    fmat = jnp.asarray(np.concatenate([np.cos(ang), -np.sin(ang)], axis=0), BF)
    return pl.pallas_call(
        _dft_ctx_kernel,
        grid=(B,),
        in_specs=[pl.BlockSpec((1, n_ctx, D_MODEL), lambda b: (b, 0, 0)),
                  pl.BlockSpec((2 * n_ctx, n_ctx), lambda b: (0, 0)),
                  pl.BlockSpec((2 * FOURIER_GROUP_DIM, FOURIER_GROUP_DIM), lambda b: (0, 0))],
        out_specs=pl.BlockSpec((1, n_ctx, D_MODEL), lambda b: (b, 0, 0)),
        out_shape=jax.ShapeDtypeStruct((B, n_ctx, D_MODEL), F32),
        compiler_params=_params("parallel"),
        name="dft_ctx",
    )(hc, fmat, cs)


def _gelu(z):
    return 0.5 * z * (1.0 + lax.erf(z * (2.0 ** -0.5)))


def _sgu_in_kernel(x_ref, mod_ref, g_ref, win_ref, h_ref, v_ref, ssq_ref, *, cw, rc):
    s = pl.program_id(2)
    tm = x_ref.shape[1]

    def tile(first):
        for r in range(tm // rc):
            rows = slice(r * rc, (r + 1) * rc)
            if first:
                h = _modnorm(x_ref[0, rows], g_ref[...], mod_ref[0, 0:1, :], mod_ref[0, 1:2, :]).astype(BF)
                h_ref[0, rows] = h
            else:
                h = h_ref[0, rows]
            sq = None
            for c in range(win_ref.shape[1] // cw):
                cols = slice(c * cw, (c + 1) * cw)
                z = _gelu(jnp.dot(h, win_ref[:, cols], preferred_element_type=F32))
                v_ref[0, rows, cols] = z.astype(BF)
                part = jnp.sum(z * z, axis=-1, keepdims=True)
                sq = part if sq is None else sq + part
            ssq_ref[0, rows] = sq if first else ssq_ref[0, rows] + sq

    @pl.when(s == 0)
    def _():
        tile(True)

    @pl.when(s > 0)
    def _():
        tile(False)


def _sgu_in_call(x, mod, g, win_bf, tm):
    B, n, _ = x.shape
    gpt = 2
    tw = gpt * SGU_GROUP_DIM
    n_v = SGU_HALF // tw
    row = lambda b, i, s: (b, i, 0)
    return pl.pallas_call(
        functools.partial(_sgu_in_kernel, cw=MXU_DIM, rc=min(tm, FFN_RC)),
        grid=(B, n // tm, n_v),
        in_specs=[pl.BlockSpec((1, tm, D_MODEL), row),
                  pl.BlockSpec((1, 6, D_MODEL), lambda b, i, s: (b, 0, 0)),
                  pl.BlockSpec((1, D_MODEL), lambda b, i, s: (0, 0)),
                  pl.BlockSpec((D_MODEL, tw), lambda b, i, s: (0, n_v + s))],
        out_specs=[pl.BlockSpec((1, tm, D_MODEL), row),
                   pl.BlockSpec((1, tm, tw), lambda b, i, s: (b, i, s)),
                   pl.BlockSpec((1, tm, 1), row)],
        out_shape=[jax.ShapeDtypeStruct((B, n, D_MODEL), BF),
                   jax.ShapeDtypeStruct((B, n, SGU_HALF), BF),
                   jax.ShapeDtypeStruct((B, n, 1), F32)],
        compiler_params=_params("parallel", "parallel", "arbitrary"),
        name="sgu_in",
    )(x, mod, g, win_bf)


def _sgu_out_kernel(h_ref, v_ref, ssq_ref, x_ref, mod_ref, g2_ref, win_ref, gv_ref, ws_ref, bs_ref, wout_ref,
                    o_ref, h2_ref, a0_ref, a1_ref, *, cw, nc, rc, fr):
    t = pl.program_id(2)
    ng = N_SGU_GROUPS
    gd = SGU_GROUP_DIM
    tm = h_ref.shape[1]
    nfin = tm // fr

    def out_proj(a_ref, last):
        gate = mod_ref[0, 2:3, :]
        if not last:
            k = jnp.minimum(t - 1, nfin - 1)
            xrows = pl.ds(pl.multiple_of(k * fr, fr), fr)
            o_ref[0, xrows, :] += jnp.where(t <= nfin, x_ref[0], 0.0)
        for r in range(tm // rc):
            rows = slice(r * rc, (r + 1) * rc)
            a = a_ref[rows, :]
            for c in range(D_MODEL // nc):
                cols = slice(c * nc, (c + 1) * nc)
                o_ref[0, rows, cols] += gate[:, cols] * jnp.dot(a, wout_ref[:, cols], preferred_element_type=F32)
            if last:
                h2_ref[0, rows, :] = _modnorm(o_ref[0, rows, :], g2_ref[...],
                                              mod_ref[0, 3:4, :], mod_ref[0, 4:5, :]).astype(BF)

    def prep(a_ref):
        for r in range(tm // rc):
            rows = slice(r * rc, (r + 1) * rc)
            h = h_ref[0, rows]
            for c in range(gd // cw):
                cols = slice(c * cw, (c + 1) * cw)
                u = _gelu(jnp.dot(h, win_ref[:, cols], preferred_element_type=F32))
                a_ref[rows, cols] = u.astype(BF)
        rstd = lax.rsqrt(ssq_ref[0] * (1.0 / SGU_HALF) + EPS)
        for c in range(tm // SGU_CHUNK):
            rows = slice(c * SGU_CHUNK, (c + 1) * SGU_CHUNK)
            vn = ((v_ref[0, rows, :].astype(F32) * rstd[rows]) * gv_ref[0]).astype(BF)
            sp = jnp.dot(ws_ref[0], vn, preferred_element_type=F32) + bs_ref[0]
            a_ref[rows, :] = (a_ref[rows, :].astype(F32) * sp).astype(BF)

    @pl.when(t == 0)
    def _():
        o_ref[0] = jnp.zeros(o_ref.shape[1:], F32)
        prep(a0_ref)

    @pl.when((t >= 1) & (t < ng) & ((t & 1) == 1))
    def _():
        out_proj(a0_ref, False)
        prep(a1_ref)

    @pl.when((t >= 1) & (t < ng) & ((t & 1) == 0))
    def _():
        out_proj(a1_ref, False)
        prep(a0_ref)

    @pl.when(t == ng)
    def _():
        out_proj(a1_ref if (ng - 1) % 2 == 1 else a0_ref, True)


def _sgu_out_call(h, v, ssq, x, mod, g2, win_bf, gv, ws_bf, bs, wout_bf, tm):
    B, n, _ = x.shape
    ng, gd = N_SGU_GROUPS, SGU_GROUP_DIM
    fr = ROW_CHUNK
    nfin = tm // fr
    assert nfin < ng
    row = lambda b, i, t: (b, i, 0)
    grp = lambda b, i, t: (jnp.clip(t, 0, ng - 1), 0, 0)
    return pl.pallas_call(
        functools.partial(_sgu_out_kernel, cw=MXU_DIM, nc=FFN_NC, rc=min(tm, FFN_RC), fr=fr),
        grid=(B, n // tm, ng + 1),
        in_specs=[pl.BlockSpec((1, tm, D_MODEL), row),
                  pl.BlockSpec((1, tm, gd), lambda b, i, t: (b, i, jnp.clip(t, 0, ng - 1))),
                  pl.BlockSpec((1, tm, 1), row),
                  pl.BlockSpec((1, fr, D_MODEL), lambda b, i, t: (b, i * nfin + jnp.clip(t - 1, 0, nfin - 1), 0)),
                  pl.BlockSpec((1, 6, D_MODEL), lambda b, i, t: (b, 0, 0)),
                  pl.BlockSpec((1, D_MODEL), lambda b, i, t: (0, 0)),
                  pl.BlockSpec((D_MODEL, gd), lambda b, i, t: (0, jnp.clip(t, 0, ng - 1))),
                  pl.BlockSpec((1, 1, gd), grp),
                  pl.BlockSpec((1, SGU_CHUNK, SGU_CHUNK), grp),
                  pl.BlockSpec((1, SGU_CHUNK, 1), grp),
                  pl.BlockSpec((gd, D_MODEL), lambda b, i, t: (jnp.clip(t - 1, 0, ng - 1), 0))],
        out_specs=[pl.BlockSpec((1, tm, D_MODEL), row),
                   pl.BlockSpec((1, tm, D_MODEL), row)],
        out_shape=[jax.ShapeDtypeStruct((B, n, D_MODEL), F32),
                   jax.ShapeDtypeStruct((B, n, D_MODEL), BF)],
        scratch_shapes=[pltpu.VMEM((tm, gd), BF),
                        pltpu.VMEM((tm, gd), BF)],
        compiler_params=_big_params("parallel", "parallel", "arbitrary"),
        name="sgu_out",
    )(h, v, ssq, x, mod, g2, win_bf, gv.reshape(ng, 1, gd), ws_bf, bs.reshape(ng, SGU_CHUNK, 1), wout_bf)


def _sgu_call(x, mod, g, g2, win_bf, gv, ws_bf, bs, wout_bf, tm):
    h, v, ssq = _sgu_in_call(x, mod, g, win_bf, tm)
    return _sgu_out_call(h, v, ssq, x, mod, g2, win_bf, gv, ws_bf, bs, wout_bf, tm)


def _rope_tables(n):
    f32 = np.float32
    row, col = np.meshgrid(np.arange(n // GRID_W), np.arange(GRID_W), indexing="ij")
    n_freq = HEAD_DIM // 4
    inv_freq = (f32(ROPE_BASE) ** (-np.arange(n_freq, dtype=f32) / f32(n_freq))).astype(f32)
    ang = np.concatenate([row.reshape(-1, 1).astype(f32) * inv_freq,
                          col.reshape(-1, 1).astype(f32) * inv_freq], axis=-1)
    ang = np.concatenate([ang, ang], axis=-1)
    sign = np.where(np.arange(HEAD_DIM) < HEAD_DIM // 2, -1.0, 1.0).astype(f32)
    return jnp.asarray(np.cos(ang), F32), jnp.asarray(np.sin(ang) * sign, F32)


def kernel(x, c, ctx, c_ctx, w_ada, b_ada, norm_g, w_ffn_gate, w_ffn_up, w_ffn_down, w_attn_qkv,
           w_attn_o, attn_q_g, attn_k_g, attn_sink, w_fourier, w_sgu_in, sgu_v_g, w_sgu_spatial,
           b_sgu_spatial, w_sgu_out):
    B, n, _ = x.shape
    n_ctx = ctx.shape[1]
    tm_x = ROW_TILE
    tm_c = B * n_ctx
    ctx = ctx.reshape(1, tm_c, D_MODEL)

    cond = jnp.zeros((SUBLANES, D_MODEL), F32).at[:B].set(c).at[B].set(c_ctx)
    ada = _ada_all(cond, w_ada, b_ada)
    cos, sin = _rope_tables(n)
    cos_c = jnp.ones((tm_c, HEAD_DIM), F32)
    sin_c = jnp.zeros((tm_c, HEAD_DIM), F32)

    bf16_w = {}
    for i in range(DEPTH):
        kind = i % N_MIXERS
        j = i // N_MIXERS
        need_ctx = i < DEPTH - 1
        modx = ada[i, :B].reshape(B, 6, D_MODEL)
        modc = ada[i, B].reshape(1, 6, D_MODEL)
        g1 = norm_g[i, 0].reshape(1, D_MODEL)
        g2 = norm_g[i, 1].reshape(1, D_MODEL)
        if kind == 0:
            wqkv = bf16_w[("qkv", j)] if ("qkv", j) in bf16_w else w_attn_qkv[j].astype(BF)
            wo = bf16_w[("o", j)] if ("o", j) in bf16_w else w_attn_o[j].astype(BF)
            qg = attn_q_g[j].reshape(1, HEAD_DIM)
            kg = attn_k_g[j].reshape(1, HEAD_DIM)
            sink = attn_sink[j]
            q, k, v = _qkv_call(x, modx, g1, wqkv, qg, kg, cos, sin, QKV_TM, ROW_CHUNK)
            qc, kc, vc = _qkv_call(ctx, modc, g1, wqkv, qg, kg, cos_c, sin_c, tm_c, ROW_CHUNK)
            kc = kc.reshape(B, n_ctx, HKV)
            vc = vc.reshape(B, n_ctx, HKV)
            later = []
            if i == 0:
                for i2 in range(1, DEPTH):
                    j2 = i2 // N_MIXERS
                    later += [[("qkv", w_attn_qkv, j2), ("o", w_attn_o, j2)], [("fourier", w_fourier, j2)],
                              [("sgu_in", w_sgu_in, j2), ("sgu_out", w_sgu_out, j2)]][i2 % N_MIXERS]
            outs = _attn_call(q, k, v, kc, vc, sink, ATTN_QBLOCKS, [(w, j2) for _, w, j2 in later])
            ox = outs[0]
            for (name, _, j2), w_bf in zip(later, outs[1:]):
                bf16_w[(name, j2)] = w_bf
            x, hx2 = _mm_res_call(ox, wo, x, modx, g2, tm_x, ROW_CHUNK)
            if need_ctx:
                oc = _attn_ctx_call(qc.reshape(B, n_ctx, HQ), kc, vc, sink).reshape(1, tm_c, HQ)
                ctx, hc2 = _mm_res_call(oc, wo, ctx, modc, g2, tm_c, ROW_CHUNK)
        elif kind == 1:
            wf = bf16_w[("fourier", j)] if ("fourier", j) in bf16_w else w_fourier[j].astype(BF)
            kron1, kron2 = _dft_tables(n)
            yx = _fourier_positions(x, ssq_x, modx, g1, kron1, kron2, _channel_dft_table(n), DFT_TC)
            x, hx2 = _mm_res_call(yx, wf, x, modx, g2, tm_x, ROW_CHUNK)
            if need_ctx:
                hc = _modnorm_call(ctx, modc, g1, tm_c).reshape(B, n_ctx, D_MODEL)
                yc = _fourier_ctx(hc, _channel_dft_table(n_ctx)).reshape(1, tm_c, D_MODEL)
                ctx, hc2 = _mm_res_call(yc, wf, ctx, modc, g2, tm_c, ROW_CHUNK)
        else:
            win = bf16_w[("sgu_in", j)] if ("sgu_in", j) in bf16_w else w_sgu_in[j].astype(BF)
            wout = bf16_w[("sgu_out", j)] if ("sgu_out", j) in bf16_w else w_sgu_out[j].astype(BF)
            ws = w_sgu_spatial[j].astype(BF)
            args = (win, sgu_v_g[j], ws, b_sgu_spatial[j], wout)
            x, hx2 = _sgu_call(x, modx, g1, g2, *args, SGU_TM)
            if need_ctx:
                ctx, hc2 = _sgu_call(ctx, modc, g1, g2, *args, tm_c)
        if i == 0:
            ctx, wg, wu, wd = _ffn_cast_call(hc2, ctx, modc, w_ffn_gate, w_ffn_up, w_ffn_down, i,
                                             tm_c, FFN_CAST_TF, tm_c, FFN_NC)
        elif need_ctx:
            (ctx,) = _ffn_call(hc2, ctx, modc, wg, wu, wd, tm_c, FFN_TF, tm_c, FFN_NC, False)
        cast_next = (w_ffn_gate, w_ffn_up, w_ffn_down, i + 1) if i + 1 < DEPTH else None
        emit_ssq = (i + 1) % N_MIXERS == 1 and i + 1 < DEPTH
        outs = list(_ffn_call(hx2, x, modx, wg, wu, wd, FFN_TM, FFN_TF, FFN_RC, FFN_NC, emit_ssq, cast_next))
        x = outs.pop(0)
        if emit_ssq:
            ssq_x = outs.pop(0)
        if cast_next is not None:
            wg, wu, wd = outs
    return x
```

```python
import functools

import numpy as np
import jax
import jax.numpy as jnp
from jax import lax
from jax.experimental import pallas as pl
from jax.experimental.pallas import tpu as pltpu

D_MODEL = 2048
DEPTH = 4
GRID_W = 64
N_MIXERS = 3
EPS = 1e-6
N_HEADS = 16
N_KV_HEADS = 4
HEAD_DIM = D_MODEL // N_HEADS
KV_GROUP = N_HEADS // N_KV_HEADS
HQ = N_HEADS * HEAD_DIM
HKV = N_KV_HEADS * HEAD_DIM
WINDOW = 128
BLOCK = 128
ROPE_BASE = 10000.0
N_FOURIER_GROUPS = 8
FOURIER_GROUP_DIM = D_MODEL // N_FOURIER_GROUPS
SGU_CHUNK = 128
SGU_HALF = 3 * D_MODEL
N_SGU_GROUPS = 8
SGU_GROUP_DIM = SGU_HALF // N_SGU_GROUPS
D_FF = 5632

BF = jnp.bfloat16
F32 = jnp.float32

V7X_VMEM_BYTES = 64 * 1024 * 1024
VMEM_LIMIT_BYTES = V7X_VMEM_BYTES - 8 * 1024 * 1024
BIG_VMEM_LIMIT_BYTES = V7X_VMEM_BYTES - 4 * 1024 * 1024
SUBLANES = 8
MXU_DIM = 256

ROW_TILE = 512
SGU_TM = 1024
ROW_CHUNK = MXU_DIM
QKV_TM = 1024
FFN_TM = 1024
FFN_TF = 512
FFN_RC = 512
FFN_NC = 512
FFN_CAST_TF = MXU_DIM
ATTN_QBLOCKS = 16
DFT_TC = 1024
ADA_TN = 1024
NEG = -1e30
NO_CAP = float(np.finfo(np.float32).max)
LOG2E = float(np.log2(np.e))

DFT_N2 = 64


def _params(*sem):
    return pltpu.CompilerParams(dimension_semantics=sem, vmem_limit_bytes=VMEM_LIMIT_BYTES)


def _big_params(*sem):
    return pltpu.CompilerParams(dimension_semantics=sem, vmem_limit_bytes=BIG_VMEM_LIMIT_BYTES)


def _resident(shape, index_map):
    return pl.BlockSpec(shape, index_map, pipeline_mode=pl.Buffered(1))


def _slabs(rows, steps):
    tile = 2 * SUBLANES
    count = max(d for d in range(1, min(steps, rows // tile) + 1) if (rows // tile) % d == 0)
    return count, rows // count


def _side_cast_specs(weights, steps, step):
    in_specs, args, out_specs, out_shape = [], [], [], []
    for w, layer in weights:
        _, rows, cols = w.shape
        slabs, slab_rows = _slabs(rows, steps)
        idx = lambda *g, slabs=slabs: jnp.minimum(step(*g), slabs - 1)
        in_specs.append(pl.BlockSpec((1, slab_rows, cols), lambda *g, idx=idx, layer=layer: (layer, idx(*g), 0)))
        out_specs.append(pl.BlockSpec((slab_rows, cols), lambda *g, idx=idx: (idx(*g), 0)))
        args.append(w)
        out_shape.append(jax.ShapeDtypeStruct((rows, cols), BF))
    return in_specs, args, out_specs, out_shape


def _side_cast(srcs, dsts):
    for src, dst in zip(srcs, dsts):
        dst[...] = src[0].astype(BF)


def _modnorm(x, g, shift, scale):
    ms = jnp.mean(x * x, axis=-1, keepdims=True)
    y = x * lax.rsqrt(ms + EPS)
    return (y * g) * (1.0 + scale) + shift


def _silu(t):
    return t * jax.nn.sigmoid(t)


def _ada_kernel(cond_ref, w_ref, b_ref, o_ref):
    s = _silu(cond_ref[...]).astype(BF)
    o_ref[0] = jnp.dot(s, w_ref[0].astype(BF), preferred_element_type=F32) + b_ref[0]


def _ada_all(cond, w_ada, b_ada):
    tn = ADA_TN
    return pl.pallas_call(
        _ada_kernel,
        grid=(DEPTH, 6 * D_MODEL // tn),
        in_specs=[pl.BlockSpec((SUBLANES, D_MODEL), lambda l, j: (0, 0)),
                  pl.BlockSpec((1, D_MODEL, tn), lambda l, j: (l, 0, j)),
                  pl.BlockSpec((1, 1, tn), lambda l, j: (l, 0, j))],
        out_specs=pl.BlockSpec((1, SUBLANES, tn), lambda l, j: (l, 0, j)),
        out_shape=jax.ShapeDtypeStruct((DEPTH, SUBLANES, 6 * D_MODEL), F32),
        compiler_params=_params("arbitrary", "arbitrary"),
        name="ada",
    )(cond, w_ada, b_ada.reshape(DEPTH, 1, 6 * D_MODEL))


def _modnorm_kernel(x_ref, mod_ref, g_ref, o_ref):
    o_ref[0] = _modnorm(x_ref[0], g_ref[...], mod_ref[0, 0:1, :], mod_ref[0, 1:2, :])


def _modnorm_call(x, mod, g, tm):
    B, n, _ = x.shape
    return pl.pallas_call(
        _modnorm_kernel,
        grid=(B, n // tm),
        in_specs=[pl.BlockSpec((1, tm, D_MODEL), lambda b, i: (b, i, 0)),
                  pl.BlockSpec((1, 6, D_MODEL), lambda b, i: (b, 0, 0)),
                  pl.BlockSpec((1, D_MODEL), lambda b, i: (0, 0))],
        out_specs=pl.BlockSpec((1, tm, D_MODEL), lambda b, i: (b, i, 0)),
        out_shape=jax.ShapeDtypeStruct((B, n, D_MODEL), F32),
        compiler_params=_params("parallel", "parallel"),
        name="modnorm",
    )(x, mod, g)


def _qkv_kernel(x_ref, mod_ref, g_ref, w_ref, qg_ref, kg_ref, cos_ref, sin_ref, q_ref, k_ref, v_ref, *, rc):
    tm = x_ref.shape[1]
    qscale = (HEAD_DIM ** -0.5) * LOG2E
    cw = 4 * HEAD_DIM

    for r in range(tm // rc):
        rows = slice(r * rc, (r + 1) * rc)
        h = _modnorm(x_ref[0, rows], g_ref[...], mod_ref[0, 0:1, :], mod_ref[0, 1:2, :]).astype(BF)
        cos = cos_ref[rows]
        sin = sin_ref[rows]

        def head(t, gain):
            ms = jnp.mean(t * t, axis=-1, keepdims=True)
            t = (t * lax.rsqrt(ms + EPS)) * gain
            return t * cos + pltpu.roll(t, HEAD_DIM // 2, 1) * sin

        t = jnp.dot(h, w_ref[:, HQ:HQ + HKV], preferred_element_type=F32)
        for j in range(N_KV_HEADS):
            tj = head(t[:, j * HEAD_DIM:(j + 1) * HEAD_DIM], kg_ref[...])
            k_ref[0, rows, j * HEAD_DIM:(j + 1) * HEAD_DIM] = tj.astype(BF)
        for c in range(HQ // cw):
            t = jnp.dot(h, w_ref[:, c * cw:(c + 1) * cw], preferred_element_type=F32)
            for j in range(4):
                tj = head(t[:, j * HEAD_DIM:(j + 1) * HEAD_DIM], qg_ref[...]) * qscale
                q_ref[0, rows, c * cw + j * HEAD_DIM:c * cw + (j + 1) * HEAD_DIM] = tj.astype(BF)
        t = jnp.dot(h, w_ref[:, HQ + HKV:], preferred_element_type=F32)
        v_ref[0, rows] = t.astype(BF)


def _qkv_call(x, mod, g, w_bf, qg, kg, cos, sin, tm, rc):
    B, n, _ = x.shape
    row = lambda b, i: (b, i, 0)
    return pl.pallas_call(
        functools.partial(_qkv_kernel, rc=rc),
        grid=(B, n // tm),
        in_specs=[pl.BlockSpec((1, tm, D_MODEL), row),
                  pl.BlockSpec((1, 6, D_MODEL), lambda b, i: (b, 0, 0)),
                  pl.BlockSpec((1, D_MODEL), lambda b, i: (0, 0)),
                  _resident((D_MODEL, HQ + 2 * HKV), lambda b, i: (0, 0)),
                  pl.BlockSpec((1, HEAD_DIM), lambda b, i: (0, 0)),
                  pl.BlockSpec((1, HEAD_DIM), lambda b, i: (0, 0)),
                  pl.BlockSpec((tm, HEAD_DIM), lambda b, i: (i, 0)),
                  pl.BlockSpec((tm, HEAD_DIM), lambda b, i: (i, 0))],
        out_specs=[pl.BlockSpec((1, tm, HQ), row),
                   pl.BlockSpec((1, tm, HKV), row),
                   pl.BlockSpec((1, tm, HKV), row)],
        out_shape=[jax.ShapeDtypeStruct((B, n, HQ), BF),
                   jax.ShapeDtypeStruct((B, n, HKV), BF),
                   jax.ShapeDtypeStruct((B, n, HKV), BF)],
        compiler_params=_params("parallel", "parallel"),
        name="qkv",
    )(x, mod, g, w_bf, qg, kg, cos, sin)


def _softmax_pv(s, sink_col, vcat):
    m = jnp.maximum(jnp.max(s, axis=-1, keepdims=True), sink_col)
    p = jnp.exp2(s - m)
    den = jnp.sum(p, axis=-1, keepdims=True) + jnp.exp2(sink_col - m)
    return jnp.dot(p.astype(BF), vcat, preferred_element_type=F32) / den


def _sink_column(sink_ref, h, rows):
    ridx = lax.broadcasted_iota(jnp.int32, (KV_GROUP * rows, 1), 0)
    col = jnp.full((KV_GROUP * rows, 1), sink_ref[h * KV_GROUP] * LOG2E, F32)
    for g in range(1, KV_GROUP):
        col = jnp.where(ridx >= g * rows, sink_ref[h * KV_GROUP + g] * LOG2E, col)
    return col


def _stack_heads(q):
    return jnp.concatenate([q[:, g * HEAD_DIM:(g + 1) * HEAD_DIM] for g in range(KV_GROUP)], axis=0)


def _attn_kernel(sink_ref, band_ref, q_ref, kp_ref, kc_ref, kn_ref, vp_ref, vc_ref, vn_ref, kx_ref, vx_ref,
                 *rest, n_ctx, qblocks, n_side):
    rest = list(rest)
    side_in = [rest.pop(0) for _ in range(n_side)]
    o_ref = rest.pop(0)
    _side_cast(side_in, rest)
    h = pl.program_id(1)
    i = pl.program_id(2)
    last = pl.num_programs(2) - 1
    kloc = jnp.concatenate([kp_ref[0], kc_ref[0], kn_ref[0]], axis=0)
    vloc = jnp.concatenate([vp_ref[0], vc_ref[0], vn_ref[0]], axis=0)
    sink_col = _sink_column(sink_ref, h, BLOCK)
    for t in range(qblocks):
        qs = _stack_heads(q_ref[0, t * BLOCK:(t + 1) * BLOCK, :])
        kcat = jnp.concatenate([kx_ref[0], kloc[t * BLOCK:(t + 3) * BLOCK]], axis=0)
        vcat = jnp.concatenate([vx_ref[0], vloc[t * BLOCK:(t + 3) * BLOCK]], axis=0)
        s = lax.dot_general(qs, kcat, (((1,), (1,)), ((), ())), preferred_element_type=F32)
        lo = jnp.minimum(s[:, n_ctx:n_ctx + BLOCK], band_ref[:, :BLOCK])
        hi = jnp.minimum(s[:, n_ctx + 2 * BLOCK:], band_ref[:, BLOCK:])
        if t == 0:
            lo = jnp.minimum(lo, jnp.where(i == 0, NEG, NO_CAP))
        if t == qblocks - 1:
            hi = jnp.minimum(hi, jnp.where(i == last, NEG, NO_CAP))
        s = jnp.concatenate([s[:, :n_ctx], lo, s[:, n_ctx + BLOCK:n_ctx + 2 * BLOCK], hi], axis=1)
        o = _softmax_pv(s, sink_col, vcat)
        for g in range(KV_GROUP):
            o_ref[0, t * BLOCK:(t + 1) * BLOCK, g * HEAD_DIM:(g + 1) * HEAD_DIM] = (
                o[g * BLOCK:(g + 1) * BLOCK].astype(BF))


def _band_bias():
    row = np.arange(KV_GROUP * BLOCK)[:, None] % BLOCK
    col = np.arange(BLOCK)[None, :]
    prev_ok = (row - (col - BLOCK)) <= WINDOW
    next_ok = ((col + BLOCK) - row) <= WINDOW
    return jnp.asarray(np.where(np.concatenate([prev_ok, next_ok], axis=1), NO_CAP, NEG), F32)


def _attn_call(q, k, v, kx, vx, sink, qblocks, cast_weights=()):
    B, n, _ = q.shape
    n_ctx = kx.shape[1]
    nb = n // BLOCK
    tq = qblocks * BLOCK
    nq = n // tq
    gw = KV_GROUP * HEAD_DIM
    side = _side_cast_specs(cast_weights, B * N_KV_HEADS * nq, lambda b, h, i: (b * N_KV_HEADS + h) * nq + i)
    prev = lambda b, h, i: (b, jnp.maximum(i * qblocks - 1, 0), h)
    cur = lambda b, h, i: (b, i, h)
    nxt = lambda b, h, i: (b, jnp.minimum((i + 1) * qblocks, nb - 1), h)
    ctx = lambda b, h, i: (b, 0, h)
    edge = lambda m: pl.BlockSpec((1, BLOCK, HEAD_DIM), m)
    mid = pl.BlockSpec((1, tq, HEAD_DIM), cur)
    return pl.pallas_call(
        functools.partial(_attn_kernel, n_ctx=n_ctx, qblocks=qblocks, n_side=len(side[1])),
        grid=(B, N_KV_HEADS, nq),
        in_specs=[pl.BlockSpec(memory_space=pltpu.SMEM),
                  pl.BlockSpec((KV_GROUP * BLOCK, 2 * BLOCK), lambda b, h, i: (0, 0)),
                  pl.BlockSpec((1, tq, gw), cur),
                  edge(prev), mid, edge(nxt), edge(prev), mid, edge(nxt),
                  pl.BlockSpec((1, n_ctx, HEAD_DIM), ctx),
                  pl.BlockSpec((1, n_ctx, HEAD_DIM), ctx)] + side[0],
        out_specs=[pl.BlockSpec((1, tq, gw), cur)] + side[2],
        out_shape=[jax.ShapeDtypeStruct((B, n, HQ), BF)] + side[3],
        compiler_params=_params("arbitrary", "arbitrary", "arbitrary"),
        name="attn",
    )(sink, _band_bias(), q, k, k, k, v, v, v, kx, vx, *side[1])


def _attn_ctx_kernel(sink_ref, q_ref, k_ref, v_ref, o_ref, *, n_ctx):
    h = pl.program_id(1)
    qs = _stack_heads(q_ref[0])
    s = lax.dot_general(qs, k_ref[0], (((1,), (1,)), ((), ())), preferred_element_type=F32)
    o = _softmax_pv(s, _sink_column(sink_ref, h, n_ctx), v_ref[0])
    for g in range(KV_GROUP):
        o_ref[0, :, g * HEAD_DIM:(g + 1) * HEAD_DIM] = o[g * n_ctx:(g + 1) * n_ctx].astype(BF)


def _attn_ctx_call(q, k, v, sink):
    B, n_ctx, _ = q.shape
    gw = KV_GROUP * HEAD_DIM
    idx = lambda b, h: (b, 0, h)
    return pl.pallas_call(
        functools.partial(_attn_ctx_kernel, n_ctx=n_ctx),
        grid=(B, N_KV_HEADS),
        in_specs=[pl.BlockSpec(memory_space=pltpu.SMEM),
                  pl.BlockSpec((1, n_ctx, gw), idx),
                  pl.BlockSpec((1, n_ctx, HEAD_DIM), idx),
                  pl.BlockSpec((1, n_ctx, HEAD_DIM), idx)],
        out_specs=pl.BlockSpec((1, n_ctx, gw), idx),
        out_shape=jax.ShapeDtypeStruct((B, n_ctx, HQ), BF),
        compiler_params=_params("parallel", "parallel"),
        name="attn_ctx",
    )(sink, q, k, v)


def _mm_res_kernel(a_ref, w_ref, x_ref, mod_ref, g2_ref, o_ref, h_ref, p_ref, *, rc):
    i = pl.program_id(1)
    nt = pl.num_programs(1) - 1
    tm = x_ref.shape[1]

    def epilogue(p, rows):
        xn = x_ref[0, rows] + mod_ref[0, 2:3, :] * p
        o_ref[0, rows] = xn
        h_ref[0, rows] = _modnorm(xn, g2_ref[...], mod_ref[0, 3:4, :], mod_ref[0, 4:5, :]).astype(BF)

    def step(do_matmul, do_epilogue):
        for r in range(tm // rc):
            rows = slice(r * rc, (r + 1) * rc)
            p_old = p_ref[rows] if do_epilogue else None
            if do_matmul:
                p_ref[rows] = jnp.dot(a_ref[0, rows].astype(BF), w_ref[...], preferred_element_type=F32)
            if do_epilogue:
                epilogue(p_old, rows)

    @pl.when(i == 0)
    def _():
        step(True, False)

    @pl.when((i > 0) & (i < nt))
    def _():
        step(True, True)

    @pl.when(i == nt)
    def _():
        step(False, True)


def _mm_res_call(a, w_bf, x, mod, g2, tm, rc):
    B, n, K = a.shape
    nt = n // tm
    cur = lambda b, i: (b, jnp.minimum(i, nt - 1), 0)
    prev = lambda b, i: (b, jnp.maximum(i - 1, 0), 0)
    return pl.pallas_call(
        functools.partial(_mm_res_kernel, rc=rc),
        grid=(B, nt + 1),
        in_specs=[pl.BlockSpec((1, tm, K), cur),
                  _resident((K, D_MODEL), lambda b, i: (0, 0)),
                  pl.BlockSpec((1, tm, D_MODEL), prev),
                  pl.BlockSpec((1, 6, D_MODEL), lambda b, i: (b, 0, 0)),
                  pl.BlockSpec((1, D_MODEL), lambda b, i: (0, 0))],
        out_specs=[pl.BlockSpec((1, tm, D_MODEL), prev),
                   pl.BlockSpec((1, tm, D_MODEL), prev)],
        out_shape=[jax.ShapeDtypeStruct((B, n, D_MODEL), F32),
                   jax.ShapeDtypeStruct((B, n, D_MODEL), BF)],
        scratch_shapes=[pltpu.VMEM((tm, D_MODEL), F32)],
        compiler_params=_params("arbitrary", "arbitrary"),
        name="mm_res",
    )(a, w_bf, x, mod, g2)


def _ffn_step(h_ref, x_ref, mod_ref, wg, wu, wd, o_ref, ssq_ref, rc, nc, first, last):
    tm = h_ref.shape[1]
    for r in range(tm // rc):
        rows = slice(r * rc, (r + 1) * rc)
        h = h_ref[0, rows]
        gate = jnp.dot(h, wg[...], preferred_element_type=F32)
        up = jnp.dot(h, wu[...], preferred_element_type=F32)
        a = (_silu(gate) * up).astype(BF)
        sq = None
        for c in range(D_MODEL // nc):
            cols = slice(c * nc, (c + 1) * nc)
            acc = jnp.dot(a, wd[:, cols], preferred_element_type=F32)
            if not first:
                acc = o_ref[0, rows, cols] + acc
            if last:
                acc = x_ref[0, rows, cols] + mod_ref[0, 5:6, cols] * acc
                if ssq_ref is not None:
                    part = jnp.sum(acc * acc, axis=-1, keepdims=True)
                    sq = part if sq is None else sq + part
            o_ref[0, rows, cols] = acc
        if sq is not None:
            ssq_ref[0, rows] = sq


def _ffn_steps(h_ref, x_ref, mod_ref, wg, wu, wd, o_ref, ssq_ref, rc, nc):
    f = pl.program_id(2)
    nf = pl.num_programs(2)

    @pl.when(f == 0)
    def _():
        _ffn_step(h_ref, x_ref, mod_ref, wg, wu, wd, o_ref, ssq_ref, rc, nc, True, False)

    @pl.when((f > 0) & (f < nf - 1))
    def _():
        _ffn_step(h_ref, x_ref, mod_ref, wg, wu, wd, o_ref, ssq_ref, rc, nc, False, False)

    @pl.when(f == nf - 1)
    def _():
        _ffn_step(h_ref, x_ref, mod_ref, wg, wu, wd, o_ref, ssq_ref, rc, nc, False, True)


def _ffn_kernel(h_ref, x_ref, mod_ref, wg_ref, wu_ref, wd_ref, *rest, rc, nc, emit_ssq, n_side):
    rest = list(rest)
    side_in = [rest.pop(0) for _ in range(n_side)]
    o_ref = rest.pop(0)
    ssq_ref = rest.pop(0) if emit_ssq else None
    _side_cast(side_in, rest)
    _ffn_steps(h_ref, x_ref, mod_ref, wg_ref, wu_ref, wd_ref, o_ref, ssq_ref, rc, nc)


def _ffn_cast_kernel(h_ref, x_ref, mod_ref, wg_ref, wu_ref, wd_ref, o_ref, wgb_ref, wub_ref, wdb_ref, *, rc, nc):
    wg = wg_ref[0].astype(BF)
    wu = wu_ref[0].astype(BF)
    wd = wd_ref[0].astype(BF)
    wgb_ref[...] = wg
    wub_ref[...] = wu
    wdb_ref[...] = wd
    _ffn_steps(h_ref, x_ref, mod_ref, wg, wu, wd, o_ref, None, rc, nc)


def _ffn_specs(tm):
    row = lambda b, i, f: (b, i, 0)
    return [pl.BlockSpec((1, tm, D_MODEL), row),
            pl.BlockSpec((1, tm, D_MODEL), row),
            pl.BlockSpec((1, 6, D_MODEL), lambda b, i, f: (b, 0, 0))], pl.BlockSpec((1, tm, D_MODEL), row)


def _ffn_call(h, x, mod, wg_bf, wu_bf, wd_bf, tm, tf, rc, nc, emit_ssq, cast_next=None):
    B, n, _ = x.shape
    nt, nf = n // tm, D_FF // tf
    act_specs, out_spec = _ffn_specs(tm)
    in_specs = act_specs + [pl.BlockSpec((D_MODEL, tf), lambda b, i, f: (0, f)),
                            pl.BlockSpec((D_MODEL, tf), lambda b, i, f: (0, f)),
                            pl.BlockSpec((tf, D_MODEL), lambda b, i, f: (f, 0))]
    args = [h, x, mod, wg_bf, wu_bf, wd_bf]
    out_specs = [out_spec]
    out_shape = [jax.ShapeDtypeStruct((B, n, D_MODEL), F32)]
    if emit_ssq:
        out_specs.append(pl.BlockSpec((1, tm, 1), lambda b, i, f: (b, i, 0)))
        out_shape.append(jax.ShapeDtypeStruct((B, n, 1), F32))
    n_side = 0
    if cast_next is not None:
        wg_all, wu_all, wd_all, layer = cast_next
        side = _side_cast_specs([(wg_all, layer), (wu_all, layer), (wd_all, layer)], B * nt * nf,
                                lambda b, i, f: (b * nt + i) * nf + f)
        n_side = len(side[1])
        in_specs += side[0]
        args += side[1]
        out_specs += side[2]
        out_shape += side[3]
    return pl.pallas_call(
        functools.partial(_ffn_kernel, rc=rc, nc=nc, emit_ssq=emit_ssq, n_side=n_side),
        grid=(B, nt, nf),
        in_specs=in_specs,
        out_specs=out_specs,
        out_shape=out_shape,
        compiler_params=_big_params("arbitrary", "arbitrary", "arbitrary"),
        name="ffn",
    )(*args)


def _ffn_cast_call(h, x, mod, wg_all, wu_all, wd_all, layer, tm, tf, rc, nc):
    B, n, _ = x.shape
    assert B == 1 and n == tm
    act_specs, out_spec = _ffn_specs(tm)
    up_in = pl.BlockSpec((1, D_MODEL, tf), lambda b, i, f: (layer, 0, f))
    down_in = pl.BlockSpec((1, tf, D_MODEL), lambda b, i, f: (layer, f, 0))
    up_out = pl.BlockSpec((D_MODEL, tf), lambda b, i, f: (0, f))
    down_out = pl.BlockSpec((tf, D_MODEL), lambda b, i, f: (f, 0))
    return pl.pallas_call(
        functools.partial(_ffn_cast_kernel, rc=rc, nc=nc),
        grid=(1, 1, D_FF // tf),
        in_specs=act_specs + [up_in, up_in, down_in],
        out_specs=[out_spec, up_out, up_out, down_out],
        out_shape=[jax.ShapeDtypeStruct((1, n, D_MODEL), F32),
                   jax.ShapeDtypeStruct((D_MODEL, D_FF), BF),
                   jax.ShapeDtypeStruct((D_MODEL, D_FF), BF),
                   jax.ShapeDtypeStruct((D_FF, D_MODEL), BF)],
        compiler_params=_big_params("arbitrary", "arbitrary", "arbitrary"),
        name="ffn_cast",
    )(h, x, mod, wg_all, wu_all, wd_all)


def _dft_tables(n):
    n1, n2, s = n // DFT_N2, DFT_N2, SUBLANES
    j = np.arange(n2 // s)[:, None, None]
    r = np.arange(2 * n1)[None, :, None]
    c = np.arange(n1 * s)[None, None, :]
    part, k1, nn1, l = r // n1, r % n1, c // s, c % s
    idx = (k1 * (s * j + l) + n2 * k1 * nn1 + part * (n // 4)) % n
    m1 = jnp.asarray(np.cos(idx * (2.0 * np.pi / n)), F32)
    same_l = jnp.asarray(np.arange(s)[:, None] == (np.arange(n1 * s)[None, :] % s), F32)
    kron1 = (m1[:, :, None, :] * same_l[None, None, :, :]).reshape(n2 // s, 2 * n1 * s, n1 * s)
    r = np.arange(2 * n2)[:, None]
    c = np.arange(2 * s * n2)[None, :]
    qpart, k2, part, nn2 = r // n2, r % n2, c // (s * n2), c % n2
    idx2 = (k2 * nn2 + (n2 // 4) * (qpart - part) + n2) % n2
    m2 = jnp.asarray(np.cos(idx2 * (2.0 * np.pi / n2)), F32)
    same_l2 = jnp.asarray(np.arange(s)[:, None] == ((np.arange(2 * s * n2)[None, :] // n2) % s), F32)
    kron2 = (m2[:, None, :] * same_l2[None, :, :]).reshape(2 * n2 * s, 2 * s * n2)
    return kron1.astype(BF), kron2.astype(BF)


def _channel_dft_table(n):
    c = np.arange(FOURIER_GROUP_DIM)
    ang = 2.0 * np.pi * ((c[:, None] * c[None, :]) % FOURIER_GROUP_DIM) / FOURIER_GROUP_DIM
    scale = 1.0 / np.sqrt(float(n) * FOURIER_GROUP_DIM)
    return jnp.asarray(np.concatenate([np.cos(ang), np.sin(ang)], axis=0) * scale, BF)


def _channel_mix(zr, zi, cs):
    outs = []
    for g in range(zr.shape[1] // FOURIER_GROUP_DIM):
        sl = slice(g * FOURIER_GROUP_DIM, (g + 1) * FOURIER_GROUP_DIM)
        zg = jnp.concatenate([zr[:, sl], zi[:, sl]], axis=1).astype(BF)
        outs.append(jnp.dot(zg, cs, preferred_element_type=F32))
    return jnp.concatenate(outs, axis=1)


def _dft1_kernel(x_ref, ssq_ref, mod_ref, g_ref, k_ref, t_ref):
    n1, s, tc = x_ref.shape[1], x_ref.shape[2], x_ref.shape[3]
    rstd = lax.rsqrt(ssq_ref[0].reshape(n1 * s, 1) * (1.0 / D_MODEL) + EPS)
    y = x_ref[0].reshape(n1 * s, tc) * rstd
    xb = ((y * g_ref[...]) * (1.0 + mod_ref[0, 1:2, :]) + mod_ref[0, 0:1, :]).astype(BF)
    t = jnp.dot(k_ref[0], xb, preferred_element_type=F32)
    t_ref[0] = t.reshape(2, n1, s, tc)


def _dft2_kernel(t_ref, k_ref, cs_ref, y_ref):
    s, tc = y_ref.shape[2], y_ref.shape[3]
    rows = s * DFT_N2
    tb = t_ref[0, :, 0].reshape(2 * rows, tc).astype(BF)
    z = jnp.dot(k_ref[...], tb, preferred_element_type=F32)
    y = _channel_mix(z[:rows], z[rows:], cs_ref[...])
    y_ref[0] = y.reshape(DFT_N2, s, tc)


def _fourier_positions(x, ssq, mod, g, kron1, kron2, cs, tc):
    B, n, _ = x.shape
    n1, s = n // DFT_N2, SUBLANES
    nj = DFT_N2 // s
    t = pl.pallas_call(
        _dft1_kernel,
        grid=(nj, B, D_MODEL // tc),
        in_specs=[pl.BlockSpec((1, n1, s, tc), lambda j, b, c: (b, 0, j, c)),
                  pl.BlockSpec((1, n1, s, 1), lambda j, b, c: (b, 0, j, 0)),
                  pl.BlockSpec((1, 6, tc), lambda j, b, c: (b, 0, c)),
                  pl.BlockSpec((1, tc), lambda j, b, c: (0, c)),
                  pl.BlockSpec((1, 2 * n1 * s, n1 * s), lambda j, b, c: (j, 0, 0))],
        out_specs=pl.BlockSpec((1, 2, n1, s, tc), lambda j, b, c: (b, 0, 0, j, c)),
        out_shape=jax.ShapeDtypeStruct((B, 2, n1, DFT_N2, D_MODEL), F32),
        compiler_params=_params("arbitrary", "arbitrary", "arbitrary"),
        name="dft1",
    )(x.reshape(B, n1, DFT_N2, D_MODEL), ssq.reshape(B, n1, DFT_N2, 1), mod, g, kron1)
    na = n1 // s
    y = pl.pallas_call(
        _dft2_kernel,
        grid=(B, na, D_MODEL // tc),
        in_specs=[pl.BlockSpec((1, 2, 1, s * DFT_N2, tc), lambda b, a, c: (b, 0, a, 0, c)),
                  _resident((2 * DFT_N2 * s, 2 * s * DFT_N2), lambda b, a, c: (0, 0)),
                  _resident((2 * FOURIER_GROUP_DIM, FOURIER_GROUP_DIM), lambda b, a, c: (0, 0))],
        out_specs=pl.BlockSpec((1, DFT_N2, s, tc), lambda b, a, c: (b, 0, a, c)),
        out_shape=jax.ShapeDtypeStruct((B, DFT_N2, n1, D_MODEL), F32),
        compiler_params=_params("parallel", "parallel", "parallel"),
        name="dft2",
    )(t.reshape(B, 2, na, s * DFT_N2, D_MODEL), kron2, cs)
    return y.reshape(B, n, D_MODEL)


def _dft_ctx_kernel(x_ref, f_ref, cs_ref, y_ref):
    n_ctx = x_ref.shape[1]
    z = jnp.dot(f_ref[...], x_ref[0].astype(BF), preferred_element_type=F32)
    y_ref[0] = _channel_mix(z[:n_ctx], z[n_ctx:], cs_ref[...])


def _fourier_ctx(hc, cs):
    B, n_ctx, _ = hc.shape
    p = np.arange(n_ctx)
    ang = 2.0 * np.pi * ((p[:, None] * p[None, :]) % n_ctx) / n_ctx
    fmat = jnp.asarray(np.concatenate([np.cos(ang), -np.sin(ang)], axis=0), BF)
    return pl.pallas_call(
        _dft_ctx_kernel,
        grid=(B,),
        in_specs=[pl.BlockSpec((1, n_ctx, D_MODEL), lambda b: (b, 0, 0)),
                  pl.BlockSpec((2 * n_ctx, n_ctx), lambda b: (0, 0)),
                  pl.BlockSpec((2 * FOURIER_GROUP_DIM, FOURIER_GROUP_DIM), lambda b: (0, 0))],
        out_specs=pl.BlockSpec((1, n_ctx, D_MODEL), lambda b: (b, 0, 0)),
        out_shape=jax.ShapeDtypeStruct((B, n_ctx, D_MODEL), F32),
        compiler_params=_params("parallel"),
        name="dft_ctx",
    )(hc, fmat, cs)


def _gelu(z):
    return 0.5 * z * (1.0 + lax.erf(z * (2.0 ** -0.5)))


def _sgu_in_kernel(x_ref, mod_ref, g_ref, win_ref, h_ref, v_ref, ssq_ref, *, cw, rc):
    s = pl.program_id(2)
    tm = x_ref.shape[1]

    def tile(first):
        for r in range(tm // rc):
            rows = slice(r * rc, (r + 1) * rc)
            if first:
                h = _modnorm(x_ref[0, rows], g_ref[...], mod_ref[0, 0:1, :], mod_ref[0, 1:2, :]).astype(BF)
                h_ref[0, rows] = h
            else:
                h = h_ref[0, rows]
            sq = None
            for c in range(win_ref.shape[1] // cw):
                cols = slice(c * cw, (c + 1) * cw)
                z = _gelu(jnp.dot(h, win_ref[:, cols], preferred_element_type=F32))
                v_ref[0, rows, cols] = z.astype(BF)
                part = jnp.sum(z * z, axis=-1, keepdims=True)
                sq = part if sq is None else sq + part
            ssq_ref[0, rows] = sq if first else ssq_ref[0, rows] + sq

    @pl.when(s == 0)
    def _():
        tile(True)

    @pl.when(s > 0)
    def _():
        tile(False)


def _sgu_in_call(x, mod, g, win_bf, tm):
    B, n, _ = x.shape
    gpt = 2
    tw = gpt * SGU_GROUP_DIM
    n_v = SGU_HALF // tw
    row = lambda b, i, s: (b, i, 0)
    return pl.pallas_call(
        functools.partial(_sgu_in_kernel, cw=MXU_DIM, rc=min(tm, FFN_RC)),
        grid=(B, n // tm, n_v),
        in_specs=[pl.BlockSpec((1, tm, D_MODEL), row),
                  pl.BlockSpec((1, 6, D_MODEL), lambda b, i, s: (b, 0, 0)),
                  pl.BlockSpec((1, D_MODEL), lambda b, i, s: (0, 0)),
                  pl.BlockSpec((D_MODEL, tw), lambda b, i, s: (0, n_v + s))],
        out_specs=[pl.BlockSpec((1, tm, D_MODEL), row),
                   pl.BlockSpec((1, tm, tw), lambda b, i, s: (b, i, s)),
                   pl.BlockSpec((1, tm, 1), row)],
        out_shape=[jax.ShapeDtypeStruct((B, n, D_MODEL), BF),
                   jax.ShapeDtypeStruct((B, n, SGU_HALF), BF),
                   jax.ShapeDtypeStruct((B, n, 1), F32)],
        compiler_params=_params("parallel", "parallel", "arbitrary"),
        name="sgu_in",
    )(x, mod, g, win_bf)


def _sgu_out_kernel(h_ref, v_ref, ssq_ref, x_ref, mod_ref, g2_ref, win_ref, gv_ref, ws_ref, bs_ref, wout_ref,
                    o_ref, h2_ref, a0_ref, a1_ref, *, cw, nc, rc, fr):
    t = pl.program_id(2)
    ng = N_SGU_GROUPS
    gd = SGU_GROUP_DIM
    tm = h_ref.shape[1]
    nfin = tm // fr

    def out_proj(a_ref, last):
        gate = mod_ref[0, 2:3, :]
        if not last:
            k = jnp.minimum(t - 1, nfin - 1)
            xrows = pl.ds(pl.multiple_of(k * fr, fr), fr)
            o_ref[0, xrows, :] += jnp.where(t <= nfin, x_ref[0], 0.0)
        for r in range(tm // rc):
            rows = slice(r * rc, (r + 1) * rc)
            a = a_ref[rows, :]
            for c in range(D_MODEL // nc):
                cols = slice(c * nc, (c + 1) * nc)
                o_ref[0, rows, cols] += gate[:, cols] * jnp.dot(a, wout_ref[:, cols], preferred_element_type=F32)
            if last:
                h2_ref[0, rows, :] = _modnorm(o_ref[0, rows, :], g2_ref[...],
                                              mod_ref[0, 3:4, :], mod_ref[0, 4:5, :]).astype(BF)

    def prep(a_ref):
        for r in range(tm // rc):
            rows = slice(r * rc, (r + 1) * rc)
            h = h_ref[0, rows]
            for c in range(gd // cw):
                cols = slice(c * cw, (c + 1) * cw)
                u = _gelu(jnp.dot(h, win_ref[:, cols], preferred_element_type=F32))
                a_ref[rows, cols] = u.astype(BF)
        rstd = lax.rsqrt(ssq_ref[0] * (1.0 / SGU_HALF) + EPS)
        for c in range(tm // SGU_CHUNK):
            rows = slice(c * SGU_CHUNK, (c + 1) * SGU_CHUNK)
            vn = ((v_ref[0, rows, :].astype(F32) * rstd[rows]) * gv_ref[0]).astype(BF)
            sp = jnp.dot(ws_ref[0], vn, preferred_element_type=F32) + bs_ref[0]
            a_ref[rows, :] = (a_ref[rows, :].astype(F32) * sp).astype(BF)

    @pl.when(t == 0)
    def _():
        o_ref[0] = jnp.zeros(o_ref.shape[1:], F32)
        prep(a0_ref)

    @pl.when((t >= 1) & (t < ng) & ((t & 1) == 1))
    def _():
        out_proj(a0_ref, False)
        prep(a1_ref)

    @pl.when((t >= 1) & (t < ng) & ((t & 1) == 0))
    def _():
        out_proj(a1_ref, False)
        prep(a0_ref)

    @pl.when(t == ng)
    def _():
        out_proj(a1_ref if (ng - 1) % 2 == 1 else a0_ref, True)


def _sgu_out_call(h, v, ssq, x, mod, g2, win_bf, gv, ws_bf, bs, wout_bf, tm):
    B, n, _ = x.shape
    ng, gd = N_SGU_GROUPS, SGU_GROUP_DIM
    fr = ROW_CHUNK
    nfin = tm // fr
    assert nfin < ng
    row = lambda b, i, t: (b, i, 0)
    grp = lambda b, i, t: (jnp.clip(t, 0, ng - 1), 0, 0)
    return pl.pallas_call(
        functools.partial(_sgu_out_kernel, cw=MXU_DIM, nc=FFN_NC, rc=min(tm, FFN_RC), fr=fr),
        grid=(B, n // tm, ng + 1),
        in_specs=[pl.BlockSpec((1, tm, D_MODEL), row),
                  pl.BlockSpec((1, tm, gd), lambda b, i, t: (b, i, jnp.clip(t, 0, ng - 1))),
                  pl.BlockSpec((1, tm, 1), row),
                  pl.BlockSpec((1, fr, D_MODEL), lambda b, i, t: (b, i * nfin + jnp.clip(t - 1, 0, nfin - 1), 0)),
                  pl.BlockSpec((1, 6, D_MODEL), lambda b, i, t: (b, 0, 0)),
                  pl.BlockSpec((1, D_MODEL), lambda b, i, t: (0, 0)),
                  pl.BlockSpec((D_MODEL, gd), lambda b, i, t: (0, jnp.clip(t, 0, ng - 1))),
                  pl.BlockSpec((1, 1, gd), grp),
                  pl.BlockSpec((1, SGU_CHUNK, SGU_CHUNK), grp),
                  pl.BlockSpec((1, SGU_CHUNK, 1), grp),
                  pl.BlockSpec((gd, D_MODEL), lambda b, i, t: (jnp.clip(t - 1, 0, ng - 1), 0))],
        out_specs=[pl.BlockSpec((1, tm, D_MODEL), row),
                   pl.BlockSpec((1, tm, D_MODEL), row)],
        out_shape=[jax.ShapeDtypeStruct((B, n, D_MODEL), F32),
                   jax.ShapeDtypeStruct((B, n, D_MODEL), BF)],
        scratch_shapes=[pltpu.VMEM((tm, gd), BF),
                        pltpu.VMEM((tm, gd), BF)],
        compiler_params=_big_params("parallel", "parallel", "arbitrary"),
        name="sgu_out",
    )(h, v, ssq, x, mod, g2, win_bf, gv.reshape(ng, 1, gd), ws_bf, bs.reshape(ng, SGU_CHUNK, 1), wout_bf)


def _sgu_call(x, mod, g, g2, win_bf, gv, ws_bf, bs, wout_bf, tm):
    h, v, ssq = _sgu_in_call(x, mod, g, win_bf, tm)
    return _sgu_out_call(h, v, ssq, x, mod, g2, win_bf, gv, ws_bf, bs, wout_bf, tm)


def _rope_tables(n):
    f32 = np.float32
    row, col = np.meshgrid(np.arange(n // GRID_W), np.arange(GRID_W), indexing="ij")
    n_freq = HEAD_DIM // 4
    inv_freq = (f32(ROPE_BASE) ** (-np.arange(n_freq, dtype=f32) / f32(n_freq))).astype(f32)
    ang = np.concatenate([row.reshape(-1, 1).astype(f32) * inv_freq,
                          col.reshape(-1, 1).astype(f32) * inv_freq], axis=-1)
    ang = np.concatenate([ang, ang], axis=-1)
    sign = np.where(np.arange(HEAD_DIM) < HEAD_DIM // 2, -1.0, 1.0).astype(f32)
    return jnp.asarray(np.cos(ang), F32), jnp.asarray(np.sin(ang) * sign, F32)


def kernel(x, c, ctx, c_ctx, w_ada, b_ada, norm_g, w_ffn_gate, w_ffn_up, w_ffn_down, w_attn_qkv,
           w_attn_o, attn_q_g, attn_k_g, attn_sink, w_fourier, w_sgu_in, sgu_v_g, w_sgu_spatial,
           b_sgu_spatial, w_sgu_out):
    B, n, _ = x.shape
    n_ctx = ctx.shape[1]
    tm_x = ROW_TILE
    tm_c = B * n_ctx
    ctx = ctx.reshape(1, tm_c, D_MODEL)

    cond = jnp.zeros((SUBLANES, D_MODEL), F32).at[:B].set(c).at[B].set(c_ctx)
    ada = _ada_all(cond, w_ada, b_ada)
    cos, sin = _rope_tables(n)
    cos_c = jnp.ones((tm_c, HEAD_DIM), F32)
    sin_c = jnp.zeros((tm_c, HEAD_DIM), F32)

    bf16_w = {}
    for i in range(DEPTH):
        kind = i % N_MIXERS
        j = i // N_MIXERS
        need_ctx = i < DEPTH - 1
        modx = ada[i, :B].reshape(B, 6, D_MODEL)
        modc = ada[i, B].reshape(1, 6, D_MODEL)
        g1 = norm_g[i, 0].reshape(1, D_MODEL)
        g2 = norm_g[i, 1].reshape(1, D_MODEL)
        if kind == 0:
            wqkv = bf16_w[("qkv", j)] if ("qkv", j) in bf16_w else w_attn_qkv[j].astype(BF)
            wo = bf16_w[("o", j)] if ("o", j) in bf16_w else w_attn_o[j].astype(BF)
            qg = attn_q_g[j].reshape(1, HEAD_DIM)
            kg = attn_k_g[j].reshape(1, HEAD_DIM)
            sink = attn_sink[j]
            q, k, v = _qkv_call(x, modx, g1, wqkv, qg, kg, cos, sin, QKV_TM, ROW_CHUNK)
            qc, kc, vc = _qkv_call(ctx, modc, g1, wqkv, qg, kg, cos_c, sin_c, tm_c, ROW_CHUNK)
            kc = kc.reshape(B, n_ctx, HKV)
            vc = vc.reshape(B, n_ctx, HKV)
            later = []
            if i == 0:
                for i2 in range(1, DEPTH):
                    j2 = i2 // N_MIXERS
                    later += [[("qkv", w_attn_qkv, j2), ("o", w_attn_o, j2)], [("fourier", w_fourier, j2)],
                              [("sgu_in", w_sgu_in, j2), ("sgu_out", w_sgu_out, j2)]][i2 % N_MIXERS]
            outs = _attn_call(q, k, v, kc, vc, sink, ATTN_QBLOCKS, [(w, j2) for _, w, j2 in later])
            ox = outs[0]
            for (name, _, j2), w_bf in zip(later, outs[1:]):
                bf16_w[(name, j2)] = w_bf
            x, hx2 = _mm_res_call(ox, wo, x, modx, g2, tm_x, ROW_CHUNK)
            if need_ctx:
                oc = _attn_ctx_call(qc.reshape(B, n_ctx, HQ), kc, vc, sink).reshape(1, tm_c, HQ)
                ctx, hc2 = _mm_res_call(oc, wo, ctx, modc, g2, tm_c, ROW_CHUNK)
        elif kind == 1:
            wf = bf16_w[("fourier", j)] if ("fourier", j) in bf16_w else w_fourier[j].astype(BF)
            kron1, kron2 = _dft_tables(n)
            yx = _fourier_positions(x, ssq_x, modx, g1, kron1, kron2, _channel_dft_table(n), DFT_TC)
            x, hx2 = _mm_res_call(yx, wf, x, modx, g2, tm_x, ROW_CHUNK)
            if need_ctx:
                hc = _modnorm_call(ctx, modc, g1, tm_c).reshape(B, n_ctx, D_MODEL)
                yc = _fourier_ctx(hc, _channel_dft_table(n_ctx)).reshape(1, tm_c, D_MODEL)
                ctx, hc2 = _mm_res_call(yc, wf, ctx, modc, g2, tm_c, ROW_CHUNK)
        else:
            win = bf16_w[("sgu_in", j)] if ("sgu_in", j) in bf16_w else w_sgu_in[j].astype(BF)
            wout = bf16_w[("sgu_out", j)] if ("sgu_out", j) in bf16_w else w_sgu_out[j].astype(BF)
            ws = w_sgu_spatial[j].astype(BF)
            args = (win, sgu_v_g[j], ws, b_sgu_spatial[j], wout)
            x, hx2 = _sgu_call(x, modx, g1, g2, *args, SGU_TM)
            if need_ctx:
                ctx, hc2 = _sgu_call(ctx, modc, g1, g2, *args, tm_c)
        if i == 0:
            ctx, wg, wu, wd = _ffn_cast_call(hc2, ctx, modc, w_ffn_gate, w_ffn_up, w_ffn_down, i,
                                             tm_c, FFN_CAST_TF, tm_c, FFN_NC)
        elif need_ctx:
            (ctx,) = _ffn_call(hc2, ctx, modc, wg, wu, wd, tm_c, FFN_TF, tm_c, FFN_NC, False)
        cast_next = (w_ffn_gate, w_ffn_up, w_ffn_down, i + 1) if i + 1 < DEPTH else None
        emit_ssq = (i + 1) % N_MIXERS == 1 and i + 1 < DEPTH
        outs = list(_ffn_call(hx2, x, modx, wg, wu, wd, FFN_TM, FFN_TF, FFN_RC, FFN_NC, emit_ssq, cast_next))
        x = outs.pop(0)
        if emit_ssq:
            ssq_x = outs.pop(0)
        if cast_next is not None:
            wg, wu, wd = outs
    return x
```
